```python
import math
import jax
import jax.numpy as jnp
from jax import lax
import numpy as np

D_MODEL = 2048
BATCH = 8
SEQ = 2048
DEPTH = 1

MIX_WIDTH = D_MODEL
S5_WIDTH = MIX_WIDTH // 2
S5_GROUP = 16
S5_GROUPS = S5_WIDTH // S5_GROUP
S5_STATE = 64
RWKV_WIDTH = MIX_WIDTH - S5_WIDTH
RWKV_HEAD = 64
RWKV_HEADS = RWKV_WIDTH // RWKV_HEAD
DECAY_LORA = max(32, int(round(1.8 * math.sqrt(RWKV_WIDTH) / 32)) * 32)
ICLR_LORA = DECAY_LORA
GATE_LORA = max(32, int(round(0.6 * RWKV_WIDTH ** 0.8 / 32)) * 32)
N_DIR = 2
RWKV_SPLITS = (RWKV_WIDTH, RWKV_WIDTH, RWKV_WIDTH, N_DIR * DECAY_LORA, N_DIR * ICLR_LORA, GATE_LORA)
RWKV_IN = sum(RWKV_SPLITS)
PROJ_WIDTH = S5_WIDTH + RWKV_IN
FFN_HIDDEN = 4 * D_MODEL
N_MOD = 6
NORM_EPS = 1e-6
GN_EPS = 64e-5
L2_EPS = 1e-12

kernel_name = 'hymba_s5_rwkv7_adaln_encoder_block'


def rms_norm(x, gain):
    xf = x.astype(jnp.float32)
    y = xf * lax.rsqrt(jnp.mean(xf * xf, axis=-1, keepdims=True) + NORM_EPS)
    return (y * gain).astype(x.dtype)


def modulate(h, shift, scale):
    return h * (1.0 + scale[:, None, :]) + shift[:, None, :]


def centred_token_shift(p, mu_prev, mu_next):
    prev = jnp.pad(p[:, :-1], ((0, 0), (1, 0), (0, 0)))
    nxt = jnp.pad(p[:, 1:], ((0, 0), (0, 1), (0, 0)))
    return p + mu_prev * (prev - p) + mu_next * (nxt - p)


def _complex_scan_combine(earlier, later):
    a1r, a1i, b1r, b1i = earlier
    a2r, a2i, b2r, b2i = later
    return (a1r * a2r - a1i * a2i,
            a1r * a2i + a1i * a2r,
            a2r * b1r - a2i * b1i + b2r,
            a2r * b1i + a2i * b1r + b2i)


def s5_mixer(u, lambda_re, lambda_im, log_step, b_re, b_im, c_re, c_im, d_skip, w_glu, b_glu):
    bsz, seq, _ = u.shape
    ug = u.reshape(bsz, seq, S5_GROUPS, S5_GROUP)
    states_re = []
    states_im = []
    for d in range(N_DIR):
        step = jnp.exp(log_step[d])[:, None]
        lr, li = lambda_re[d], lambda_im[d]
        mag = jnp.exp(lr * step)
        lbar_re = mag * jnp.cos(li * step)
        lbar_im = mag * jnp.sin(li * step)
        den = lr * lr + li * li
        nr = lbar_re - 1.0
        ni = lbar_im
        coef_re = ((nr * lr + ni * li) / den)[..., None]
        coef_im = ((ni * lr - nr * li) / den)[..., None]
        bbar_re = coef_re * b_re - coef_im * b_im
        bbar_im = coef_re * b_im + coef_im * b_re
        bu_re = jnp.einsum('bsgh,gph->bsgp', ug, bbar_re)
        bu_im = jnp.einsum('bsgh,gph->bsgp', ug, bbar_im)
        a_re = jnp.broadcast_to(lbar_re, (1, seq, S5_GROUPS, S5_STATE))
        a_im = jnp.broadcast_to(lbar_im, (1, seq, S5_GROUPS, S5_STATE))
        _, _, s_re, s_im = lax.associative_scan(
            _complex_scan_combine, (a_re, a_im, bu_re, bu_im), reverse=(d == 1), axis=1)
        states_re.append(s_re)
        states_im.append(s_im)
    x_re = states_re[0] + states_re[1]
    x_im = states_im[0] + states_im[1]
    y = jnp.einsum('bsgp,ghp->bsgh', x_re, c_re) - jnp.einsum('bsgp,ghp->bsgh', x_im, c_im)
    y = y.reshape(bsz, seq, S5_WIDTH) + d_skip * u
    y = jax.nn.gelu(y)
    return y * jax.nn.sigmoid(y @ w_glu + b_glu)


def _heads(t):
    return t.reshape(*t.shape[:-1], RWKV_HEADS, RWKV_HEAD)


def _shared_dir_time_major(t):
    t = jnp.stack([t, jnp.flip(t, axis=1)], axis=2)
    return jnp.transpose(t, (1, 2, 0, 3, 4))


def _dir_time_major(t):
    t = jnp.stack([t[:, :, 0], jnp.flip(t[:, :, 1], axis=1)], axis=2)
    return jnp.transpose(t, (1, 2, 0, 3, 4))


def _rwkv7_step(state, inp):
    r_t, w_t, k_t, v_t, kk_t, a_t = inp
    sa = jnp.einsum('dbhij,dbhj->dbhi', state, -kk_t)
    state = (state * w_t[..., None, :]
             + sa[..., :, None] * (kk_t * a_t)[..., None, :]
             + v_t[..., :, None] * k_t[..., None, :])
    y = jnp.einsum('dbhij,dbhj->dbhi', state, r_t)
    return state, y


def rwkv7_mixer(p, mu_prev, mu_next, w0, w_up, a0, a_up, g_up, k_k, k_a, r_k, ln_gain, ln_bias):
    bsz, seq, _ = p.shape
    p = centred_token_shift(p, mu_prev, mu_next)
    cut = [sum(RWKV_SPLITS[:i + 1]) for i in range(len(RWKV_SPLITS) - 1)]
    r, k, v, w_dn, a_dn, g_dn = jnp.split(p, cut, axis=-1)
    w_dn = w_dn.reshape(bsz, seq, N_DIR, DECAY_LORA)
    a_dn = a_dn.reshape(bsz, seq, N_DIR, ICLR_LORA)
    w = -jax.nn.softplus(-(w0 + jnp.einsum('bsdl,dlc->bsdc', jnp.tanh(w_dn), w_up))) - 0.5
    decay = jnp.exp(-jnp.exp(w))
    a = jax.nn.sigmoid(a0 + jnp.einsum('bsdl,dlc->bsdc', a_dn, a_up))
    g = jax.nn.sigmoid(g_dn) @ g_up
    kk = _heads(k * k_k).astype(jnp.float32)
    kk = kk / jnp.maximum(jnp.sqrt(jnp.sum(kk * kk, axis=-1, keepdims=True)), L2_EPS)
    k_dir = k[:, :, None, :] * (1.0 + (a - 1.0) * k_a)
    r_h, v_h = _heads(r), _heads(v)
    k_dir_h, decay_h, a_h = _heads(k_dir), _heads(decay), _heads(a)
    xs = (_shared_dir_time_major(r_h), _dir_time_major(decay_h), _dir_time_major(k_dir_h),
          _shared_dir_time_major(v_h), _shared_dir_time_major(kk), _dir_time_major(a_h))
    state0 = jnp.zeros((N_DIR, bsz, RWKV_HEADS, RWKV_HEAD, RWKV_HEAD), jnp.float32)
    _, ys = lax.scan(_rwkv7_step, state0, xs)
    ys = jnp.transpose(ys, (2, 0, 1, 3, 4))
    y = ys[:, :, 0] + jnp.flip(ys[:, :, 1], axis=1)
    mu = jnp.mean(y, axis=-1, keepdims=True)
    var = jnp.mean(jnp.square(y - mu), axis=-1, keepdims=True)
    y = ((y - mu) * lax.rsqrt(var + GN_EPS)).reshape(bsz, seq, RWKV_WIDTH) * ln_gain + ln_bias
    bonus_coef = jnp.sum(r_h[:, :, None] * k_dir_h * r_k, axis=(2, 4))[..., None]
    y = y + (bonus_coef * v_h).reshape(bsz, seq, RWKV_WIDTH)
    return y * g


def _fwd_setup_inputs(seed: int = 0) -> dict:
    key = jax.random.key(seed)
    ks = iter(jax.random.split(key, 48))

    def nrm(shape, scale):
        return scale * jax.random.normal(next(ks), shape, jnp.float32)

    def unif(shape, lo, hi):
        return jax.random.uniform(next(ks), shape, jnp.float32, lo, hi)

    L, C, G, P, Hg = DEPTH, RWKV_WIDTH, S5_GROUPS, S5_STATE, S5_GROUP
    w0_base = -6.0 + 5.0 * jnp.linspace(0.0, 1.0, C) ** 0.85
    lam_im_base = jnp.pi * jnp.arange(P, dtype=jnp.float32)
    return {
        'x': nrm((BATCH, SEQ, D_MODEL), 1.0),
        'c': nrm((BATCH, D_MODEL), 1.0),
        'ada_w': nrm((L, D_MODEL, N_MOD * D_MODEL), 0.5 * D_MODEL ** -0.5),
        'ada_b': nrm((L, N_MOD * D_MODEL), 0.02),
        'norm1_gain': 1.0 + nrm((L, D_MODEL), 0.05),
        'norm2_gain': 1.0 + nrm((L, D_MODEL), 0.05),
        'final_gain': 1.0 + nrm((D_MODEL,), 0.05),
        'w_in': nrm((L, D_MODEL, PROJ_WIDTH), D_MODEL ** -0.5),
        'w_out': nrm((L, MIX_WIDTH, D_MODEL), MIX_WIDTH ** -0.5),
        's5_lambda_re': -0.5 + nrm((L, N_DIR, G, P), 0.01),
        's5_lambda_im': lam_im_base + nrm((L, N_DIR, G, P), 0.01),
        's5_log_step': unif((L, N_DIR, G), math.log(1e-3), math.log(1e-1)),
        's5_b_re': nrm((L, G, P, Hg), (2 * Hg) ** -0.5),
        's5_b_im': nrm((L, G, P, Hg), (2 * Hg) ** -0.5),
        's5_c_re': nrm((L, G, Hg, P), 0.5),
        's5_c_im': nrm((L, G, Hg, P), 0.5),
        's5_d': nrm((L, S5_WIDTH), 0.5),
        's5_w_glu': nrm((L, S5_WIDTH, S5_WIDTH), S5_WIDTH ** -0.5),
        's5_b_glu': nrm((L, S5_WIDTH), 0.02),
        'rk_shift_prev': unif((L, RWKV_IN), 0.1, 0.5),
        'rk_shift_next': unif((L, RWKV_IN), 0.1, 0.5),
        'rk_w0': w0_base + nrm((L, N_DIR, C), 0.1),
        'rk_w_up': nrm((L, N_DIR, DECAY_LORA, C), 0.1),
        'rk_a0': nrm((L, N_DIR, C), 0.1),
        'rk_a_up': nrm((L, N_DIR, ICLR_LORA, C), 0.5 * ICLR_LORA ** -0.5),
        'rk_g_up': nrm((L, GATE_LORA, C), GATE_LORA ** -0.5),
        'rk_k_k': 0.85 + nrm((L, C), 0.05),
        'rk_k_a': 1.0 + nrm((L, C), 0.05),
        'rk_r_k': nrm((L, RWKV_HEADS, RWKV_HEAD), 0.1),
        'rk_ln_gain': 1.0 + nrm((L, C), 0.05),
        'rk_ln_bias': nrm((L, C), 0.02),
        'ffn_w1': nrm((L, D_MODEL, FFN_HIDDEN), D_MODEL ** -0.5),
        'ffn_w2': nrm((L, FFN_HIDDEN, D_MODEL), FFN_HIDDEN ** -0.5),
    }


def _fwd_reference(x, c, ada_w, ada_b, norm1_gain, norm2_gain, final_gain, w_in, w_out,
              s5_lambda_re, s5_lambda_im, s5_log_step, s5_b_re, s5_b_im, s5_c_re, s5_c_im,
              s5_d, s5_w_glu, s5_b_glu, rk_shift_prev, rk_shift_next, rk_w0, rk_w_up,
              rk_a0, rk_a_up, rk_g_up, rk_k_k, rk_k_a, rk_r_k, rk_ln_gain, rk_ln_bias,
              ffn_w1, ffn_w2):
    c_act = jax.nn.silu(c)
    for l in range(DEPTH):
        mod = c_act @ ada_w[l] + ada_b[l]
        shift1, scale1, gate1, shift2, scale2, gate2 = jnp.split(mod, N_MOD, axis=-1)
        h = modulate(rms_norm(x, norm1_gain[l]), shift1, scale1)
        proj = h @ w_in[l]
        u_s5, p_rwkv = jnp.split(proj, [S5_WIDTH], axis=-1)
        y_s5 = s5_mixer(u_s5, s5_lambda_re[l], s5_lambda_im[l], s5_log_step[l],
                        s5_b_re[l], s5_b_im[l], s5_c_re[l], s5_c_im[l], s5_d[l],
                        s5_w_glu[l], s5_b_glu[l])
        y_rk = rwkv7_mixer(p_rwkv, rk_shift_prev[l], rk_shift_next[l], rk_w0[l], rk_w_up[l],
                           rk_a0[l], rk_a_up[l], rk_g_up[l], rk_k_k[l], rk_k_a[l], rk_r_k[l],
                           rk_ln_gain[l], rk_ln_bias[l])
        mixed = jnp.concatenate([y_s5, y_rk], axis=-1) @ w_out[l]
        x = x + gate1[:, None, :] * mixed
        h = modulate(rms_norm(x, norm2_gain[l]), shift2, scale2)
        ffn = jnp.square(jax.nn.relu(h @ ffn_w1[l])) @ ffn_w2[l]
        x = x + gate2[:, None, :] * ffn
    return rms_norm(x, final_gain)


import jax as _jax
import jax.numpy as _jnp

TWIN_FORMAT = 'train_step'
FWD_PARAMS = ['x', 'c', 'ada_w', 'ada_b', 'norm1_gain', 'norm2_gain', 'final_gain', 'w_in', 'w_out', 's5_lambda_re', 's5_lambda_im', 's5_log_step', 's5_b_re', 's5_b_im', 's5_c_re', 's5_c_im', 's5_d', 's5_w_glu', 's5_b_glu', 'rk_shift_prev', 'rk_shift_next', 'rk_w0', 'rk_w_up', 'rk_a0', 'rk_a_up', 'rk_g_up', 'rk_k_k', 'rk_k_a', 'rk_r_k', 'rk_ln_gain', 'rk_ln_bias', 'ffn_w1', 'ffn_w2']
TWIN_WEIGHTS = ['ada_w', 'ada_b', 'norm1_gain', 'norm2_gain', 'final_gain', 'w_in', 'w_out', 's5_lambda_re', 's5_lambda_im', 's5_log_step', 's5_b_re', 's5_b_im', 's5_c_re', 's5_c_im', 's5_d', 's5_w_glu', 's5_b_glu', 'rk_shift_prev', 'rk_shift_next', 'rk_w0', 'rk_w_up', 'rk_a0', 'rk_a_up', 'rk_g_up', 'rk_k_k', 'rk_k_a', 'rk_r_k', 'rk_ln_gain', 'rk_ln_bias', 'ffn_w1', 'ffn_w2']
TWIN_DIFF_INPUT = 'x'
TWIN_INPUTS = ['x', 'c', 'ada_w', 'ada_b', 'norm1_gain', 'norm2_gain', 'final_gain', 'w_in', 'w_out', 's5_lambda_re', 's5_lambda_im', 's5_log_step', 's5_b_re', 's5_b_im', 's5_c_re', 's5_c_im', 's5_d', 's5_w_glu', 's5_b_glu', 'rk_shift_prev', 'rk_shift_next', 'rk_w0', 'rk_w_up', 'rk_a0', 'rk_a_up', 'rk_g_up', 'rk_k_k', 'rk_k_a', 'rk_r_k', 'rk_ln_gain', 'rk_ln_bias', 'ffn_w1', 'ffn_w2', 'loss_target', 'm_ada_w', 'm_ada_b', 'm_norm1_gain', 'm_norm2_gain', 'm_final_gain', 'm_w_in', 'm_w_out', 'm_s5_lambda_re', 'm_s5_lambda_im', 'm_s5_log_step', 'm_s5_b_re', 'm_s5_b_im', 'm_s5_c_re', 'm_s5_c_im', 'm_s5_d', 'm_s5_w_glu', 'm_s5_b_glu', 'm_rk_shift_prev', 'm_rk_shift_next', 'm_rk_w0', 'm_rk_w_up', 'm_rk_a0', 'm_rk_a_up', 'm_rk_g_up', 'm_rk_k_k', 'm_rk_k_a', 'm_rk_r_k', 'm_rk_ln_gain', 'm_rk_ln_bias', 'm_ffn_w1', 'm_ffn_w2', 'v_ada_w', 'v_ada_b', 'v_norm1_gain', 'v_norm2_gain', 'v_final_gain', 'v_w_in', 'v_w_out', 'v_s5_lambda_re', 'v_s5_lambda_im', 'v_s5_log_step', 'v_s5_b_re', 'v_s5_b_im', 'v_s5_c_re', 'v_s5_c_im', 'v_s5_d', 'v_s5_w_glu', 'v_s5_b_glu', 'v_rk_shift_prev', 'v_rk_shift_next', 'v_rk_w0', 'v_rk_w_up', 'v_rk_a0', 'v_rk_a_up', 'v_rk_g_up', 'v_rk_k_k', 'v_rk_k_a', 'v_rk_r_k', 'v_rk_ln_gain', 'v_rk_ln_bias', 'v_ffn_w1', 'v_ffn_w2']
TWIN_OUTPUTS = ['loss', 'grad_x', 'grad_ada_w', 'grad_ada_b', 'grad_norm1_gain', 'grad_norm2_gain', 'grad_final_gain', 'grad_w_in', 'grad_w_out', 'grad_s5_lambda_re', 'grad_s5_lambda_im', 'grad_s5_log_step', 'grad_s5_b_re', 'grad_s5_b_im', 'grad_s5_c_re', 'grad_s5_c_im', 'grad_s5_d', 'grad_s5_w_glu', 'grad_s5_b_glu', 'grad_rk_shift_prev', 'grad_rk_shift_next', 'grad_rk_w0', 'grad_rk_w_up', 'grad_rk_a0', 'grad_rk_a_up', 'grad_rk_g_up', 'grad_rk_k_k', 'grad_rk_k_a', 'grad_rk_r_k', 'grad_rk_ln_gain', 'grad_rk_ln_bias', 'grad_ffn_w1', 'grad_ffn_w2', 'delta_ada_w', 'delta_ada_b', 'delta_norm1_gain', 'delta_norm2_gain', 'delta_final_gain', 'delta_w_in', 'delta_w_out', 'delta_s5_lambda_re', 'delta_s5_lambda_im', 'delta_s5_log_step', 'delta_s5_b_re', 'delta_s5_b_im', 'delta_s5_c_re', 'delta_s5_c_im', 'delta_s5_d', 'delta_s5_w_glu', 'delta_s5_b_glu', 'delta_rk_shift_prev', 'delta_rk_shift_next', 'delta_rk_w0', 'delta_rk_w_up', 'delta_rk_a0', 'delta_rk_a_up', 'delta_rk_g_up', 'delta_rk_k_k', 'delta_rk_k_a', 'delta_rk_r_k', 'delta_rk_ln_gain', 'delta_rk_ln_bias', 'delta_ffn_w1', 'delta_ffn_w2', 'new_m_ada_w', 'new_m_ada_b', 'new_m_norm1_gain', 'new_m_norm2_gain', 'new_m_final_gain', 'new_m_w_in', 'new_m_w_out', 'new_m_s5_lambda_re', 'new_m_s5_lambda_im', 'new_m_s5_log_step', 'new_m_s5_b_re', 'new_m_s5_b_im', 'new_m_s5_c_re', 'new_m_s5_c_im', 'new_m_s5_d', 'new_m_s5_w_glu', 'new_m_s5_b_glu', 'new_m_rk_shift_prev', 'new_m_rk_shift_next', 'new_m_rk_w0', 'new_m_rk_w_up', 'new_m_rk_a0', 'new_m_rk_a_up', 'new_m_rk_g_up', 'new_m_rk_k_k', 'new_m_rk_k_a', 'new_m_rk_r_k', 'new_m_rk_ln_gain', 'new_m_rk_ln_bias', 'new_m_ffn_w1', 'new_m_ffn_w2', 'new_v_ada_w', 'new_v_ada_b', 'new_v_norm1_gain', 'new_v_norm2_gain', 'new_v_final_gain', 'new_v_w_in', 'new_v_w_out', 'new_v_s5_lambda_re', 'new_v_s5_lambda_im', 'new_v_s5_log_step', 'new_v_s5_b_re', 'new_v_s5_b_im', 'new_v_s5_c_re', 'new_v_s5_c_im', 'new_v_s5_d', 'new_v_s5_w_glu', 'new_v_s5_b_glu', 'new_v_rk_shift_prev', 'new_v_rk_shift_next', 'new_v_rk_w0', 'new_v_rk_w_up', 'new_v_rk_a0', 'new_v_rk_a_up', 'new_v_rk_g_up', 'new_v_rk_k_k', 'new_v_rk_k_a', 'new_v_rk_r_k', 'new_v_rk_ln_gain', 'new_v_rk_ln_bias', 'new_v_ffn_w1', 'new_v_ffn_w2']
TWIN_LEAF_KINDS = {'loss': 'loss', 'grad_x': 'grad_x', 'grad_ada_w': 'grad_w', 'grad_ada_b': 'grad_w', 'grad_norm1_gain': 'grad_w', 'grad_norm2_gain': 'grad_w', 'grad_final_gain': 'grad_w', 'grad_w_in': 'grad_w', 'grad_w_out': 'grad_w', 'grad_s5_lambda_re': 'grad_w', 'grad_s5_lambda_im': 'grad_w', 'grad_s5_log_step': 'grad_w', 'grad_s5_b_re': 'grad_w', 'grad_s5_b_im': 'grad_w', 'grad_s5_c_re': 'grad_w', 'grad_s5_c_im': 'grad_w', 'grad_s5_d': 'grad_w', 'grad_s5_w_glu': 'grad_w', 'grad_s5_b_glu': 'grad_w', 'grad_rk_shift_prev': 'grad_w', 'grad_rk_shift_next': 'grad_w', 'grad_rk_w0': 'grad_w', 'grad_rk_w_up': 'grad_w', 'grad_rk_a0': 'grad_w', 'grad_rk_a_up': 'grad_w', 'grad_rk_g_up': 'grad_w', 'grad_rk_k_k': 'grad_w', 'grad_rk_k_a': 'grad_w', 'grad_rk_r_k': 'grad_w', 'grad_rk_ln_gain': 'grad_w', 'grad_rk_ln_bias': 'grad_w', 'grad_ffn_w1': 'grad_w', 'grad_ffn_w2': 'grad_w', 'delta_ada_w': 'delta_w', 'delta_ada_b': 'delta_w', 'delta_norm1_gain': 'delta_w', 'delta_norm2_gain': 'delta_w', 'delta_final_gain': 'delta_w', 'delta_w_in': 'delta_w', 'delta_w_out': 'delta_w', 'delta_s5_lambda_re': 'delta_w', 'delta_s5_lambda_im': 'delta_w', 'delta_s5_log_step': 'delta_w', 'delta_s5_b_re': 'delta_w', 'delta_s5_b_im': 'delta_w', 'delta_s5_c_re': 'delta_w', 'delta_s5_c_im': 'delta_w', 'delta_s5_d': 'delta_w', 'delta_s5_w_glu': 'delta_w', 'delta_s5_b_glu': 'delta_w', 'delta_rk_shift_prev': 'delta_w', 'delta_rk_shift_next': 'delta_w', 'delta_rk_w0': 'delta_w', 'delta_rk_w_up': 'delta_w', 'delta_rk_a0': 'delta_w', 'delta_rk_a_up': 'delta_w', 'delta_rk_g_up': 'delta_w', 'delta_rk_k_k': 'delta_w', 'delta_rk_k_a': 'delta_w', 'delta_rk_r_k': 'delta_w', 'delta_rk_ln_gain': 'delta_w', 'delta_rk_ln_bias': 'delta_w', 'delta_ffn_w1': 'delta_w', 'delta_ffn_w2': 'delta_w', 'new_m_ada_w': 'new_m', 'new_m_ada_b': 'new_m', 'new_m_norm1_gain': 'new_m', 'new_m_norm2_gain': 'new_m', 'new_m_final_gain': 'new_m', 'new_m_w_in': 'new_m', 'new_m_w_out': 'new_m', 'new_m_s5_lambda_re': 'new_m', 'new_m_s5_lambda_im': 'new_m', 'new_m_s5_log_step': 'new_m', 'new_m_s5_b_re': 'new_m', 'new_m_s5_b_im': 'new_m', 'new_m_s5_c_re': 'new_m', 'new_m_s5_c_im': 'new_m', 'new_m_s5_d': 'new_m', 'new_m_s5_w_glu': 'new_m', 'new_m_s5_b_glu': 'new_m', 'new_m_rk_shift_prev': 'new_m', 'new_m_rk_shift_next': 'new_m', 'new_m_rk_w0': 'new_m', 'new_m_rk_w_up': 'new_m', 'new_m_rk_a0': 'new_m', 'new_m_rk_a_up': 'new_m', 'new_m_rk_g_up': 'new_m', 'new_m_rk_k_k': 'new_m', 'new_m_rk_k_a': 'new_m', 'new_m_rk_r_k': 'new_m', 'new_m_rk_ln_gain': 'new_m', 'new_m_rk_ln_bias': 'new_m', 'new_m_ffn_w1': 'new_m', 'new_m_ffn_w2': 'new_m', 'new_v_ada_w': 'new_v', 'new_v_ada_b': 'new_v', 'new_v_norm1_gain': 'new_v', 'new_v_norm2_gain': 'new_v', 'new_v_final_gain': 'new_v', 'new_v_w_in': 'new_v', 'new_v_w_out': 'new_v', 'new_v_s5_lambda_re': 'new_v', 'new_v_s5_lambda_im': 'new_v', 'new_v_s5_log_step': 'new_v', 'new_v_s5_b_re': 'new_v', 'new_v_s5_b_im': 'new_v', 'new_v_s5_c_re': 'new_v', 'new_v_s5_c_im': 'new_v', 'new_v_s5_d': 'new_v', 'new_v_s5_w_glu': 'new_v', 'new_v_s5_b_glu': 'new_v', 'new_v_rk_shift_prev': 'new_v', 'new_v_rk_shift_next': 'new_v', 'new_v_rk_w0': 'new_v', 'new_v_rk_w_up': 'new_v', 'new_v_rk_a0': 'new_v', 'new_v_rk_a_up': 'new_v', 'new_v_rk_g_up': 'new_v', 'new_v_rk_k_k': 'new_v', 'new_v_rk_k_a': 'new_v', 'new_v_rk_r_k': 'new_v', 'new_v_rk_ln_gain': 'new_v', 'new_v_rk_ln_bias': 'new_v', 'new_v_ffn_w1': 'new_v', 'new_v_ffn_w2': 'new_v'}


def _forward(args):
    return _fwd_reference(*[args[k] for k in FWD_PARAMS])


def _output_shape():
    out = _jax.eval_shape(lambda: _forward(_fwd_setup_inputs(0)))
    return out.shape, out.dtype

N_MICROBATCH = 1
ADAM_LR = 0.001
ADAM_B1 = 0.9
ADAM_B2 = 0.999
ADAM_EPS = 1e-08
ADAM_WD = 0.01
ADAM_STEP = 10
PER_EXAMPLE_BATCH_AXIS = {'x': 0, 'c': 0, 'loss_target': 0}
SHARED_INPUTS = []
_WEIGHT_DTYPES = {'ada_w': _jnp.float32, 'ada_b': _jnp.float32, 'norm1_gain': _jnp.float32, 'norm2_gain': _jnp.float32, 'final_gain': _jnp.float32, 'w_in': _jnp.float32, 'w_out': _jnp.float32, 's5_lambda_re': _jnp.float32, 's5_lambda_im': _jnp.float32, 's5_log_step': _jnp.float32, 's5_b_re': _jnp.float32, 's5_b_im': _jnp.float32, 's5_c_re': _jnp.float32, 's5_c_im': _jnp.float32, 's5_d': _jnp.float32, 's5_w_glu': _jnp.float32, 's5_b_glu': _jnp.float32, 'rk_shift_prev': _jnp.float32, 'rk_shift_next': _jnp.float32, 'rk_w0': _jnp.float32, 'rk_w_up': _jnp.float32, 'rk_a0': _jnp.float32, 'rk_a_up': _jnp.float32, 'rk_g_up': _jnp.float32, 'rk_k_k': _jnp.float32, 'rk_k_a': _jnp.float32, 'rk_r_k': _jnp.float32, 'rk_ln_gain': _jnp.float32, 'rk_ln_bias': _jnp.float32, 'ffn_w1': _jnp.float32, 'ffn_w2': _jnp.float32}
MOMENT_SCALE = {'ada_w': 5.203731e-02, 'ada_b': 9.831520e-02, 'norm1_gain': 1.339870e-02, 'norm2_gain': 2.798198e-02, 'final_gain': 8.069260e+00, 'w_in': 1.016000e-02, 'w_out': 9.072462e-03, 's5_lambda_re': 2.847286e-03, 's5_lambda_im': 3.239922e-03, 's5_log_step': 1.559338e+00, 's5_b_re': 3.264620e-03, 's5_b_im': 3.221637e-03, 's5_c_re': 1.199277e-03, 's5_c_im': 1.164415e-03, 's5_d': 8.063278e-03, 's5_w_glu': 1.952169e-03, 's5_b_glu': 3.051318e-03, 'rk_shift_prev': 1.877453e-02, 'rk_shift_next': 1.848352e-02, 'rk_w0': 3.817236e-03, 'rk_w_up': 1.001425e-03, 'rk_a0': 3.083277e-03, 'rk_a_up': 2.418228e-03, 'rk_g_up': 1.104697e-02, 'rk_k_k': 3.657524e-02, 'rk_k_a': 3.438869e-02, 'rk_r_k': 3.053976e-02, 'rk_ln_gain': 1.030607e-02, 'rk_ln_bias': 1.344467e-02, 'ffn_w1': 1.505893e-02, 'ffn_w2': 3.335499e-02}


def _to_microbatches(a, axis):
    t = _jnp.moveaxis(a, axis, 0)
    t = t.reshape((N_MICROBATCH, t.shape[0] // N_MICROBATCH) + t.shape[1:])
    return _jnp.moveaxis(t, 1, axis + 1)


def setup_inputs(seed: int = 0) -> dict:
    inp = _fwd_setup_inputs(seed)
    key = _jax.random.fold_in(_jax.random.key(seed), 7919)
    shape, _ = _output_shape()
    out = dict(inp)
    out["loss_target"] = _jax.random.normal(_jax.random.fold_in(key, 0), shape, _jnp.float32)
    for i, name in enumerate(TWIN_WEIGHTS):
        w = inp[name].astype(_jnp.float32)
        if MOMENT_SCALE is None:
            s = _jnp.sqrt(_jnp.mean(_jnp.square(w)) + 1e-30)
        else:
            s = MOMENT_SCALE[name]
        km, kv = _jax.random.split(_jax.random.fold_in(key, i + 1))
        out[name] = w
        out["m_" + name] = s * _jax.random.normal(km, w.shape, _jnp.float32)
        out["v_" + name] = (s * s) * _jax.random.uniform(kv, w.shape, _jnp.float32, 0.5, 1.5)
    if N_MICROBATCH > 1:
        for name, axis in PER_EXAMPLE_BATCH_AXIS.items():
            out[name] = _to_microbatches(out[name], axis)
    return {'x': out['x'], 'c': out['c'], 'ada_w': out['ada_w'], 'ada_b': out['ada_b'], 'norm1_gain': out['norm1_gain'], 'norm2_gain': out['norm2_gain'], 'final_gain': out['final_gain'], 'w_in': out['w_in'], 'w_out': out['w_out'], 's5_lambda_re': out['s5_lambda_re'], 's5_lambda_im': out['s5_lambda_im'], 's5_log_step': out['s5_log_step'], 's5_b_re': out['s5_b_re'], 's5_b_im': out['s5_b_im'], 's5_c_re': out['s5_c_re'], 's5_c_im': out['s5_c_im'], 's5_d': out['s5_d'], 's5_w_glu': out['s5_w_glu'], 's5_b_glu': out['s5_b_glu'], 'rk_shift_prev': out['rk_shift_prev'], 'rk_shift_next': out['rk_shift_next'], 'rk_w0': out['rk_w0'], 'rk_w_up': out['rk_w_up'], 'rk_a0': out['rk_a0'], 'rk_a_up': out['rk_a_up'], 'rk_g_up': out['rk_g_up'], 'rk_k_k': out['rk_k_k'], 'rk_k_a': out['rk_k_a'], 'rk_r_k': out['rk_r_k'], 'rk_ln_gain': out['rk_ln_gain'], 'rk_ln_bias': out['rk_ln_bias'], 'ffn_w1': out['ffn_w1'], 'ffn_w2': out['ffn_w2'], 'loss_target': out['loss_target'], 'm_ada_w': out['m_ada_w'], 'm_ada_b': out['m_ada_b'], 'm_norm1_gain': out['m_norm1_gain'], 'm_norm2_gain': out['m_norm2_gain'], 'm_final_gain': out['m_final_gain'], 'm_w_in': out['m_w_in'], 'm_w_out': out['m_w_out'], 'm_s5_lambda_re': out['m_s5_lambda_re'], 'm_s5_lambda_im': out['m_s5_lambda_im'], 'm_s5_log_step': out['m_s5_log_step'], 'm_s5_b_re': out['m_s5_b_re'], 'm_s5_b_im': out['m_s5_b_im'], 'm_s5_c_re': out['m_s5_c_re'], 'm_s5_c_im': out['m_s5_c_im'], 'm_s5_d': out['m_s5_d'], 'm_s5_w_glu': out['m_s5_w_glu'], 'm_s5_b_glu': out['m_s5_b_glu'], 'm_rk_shift_prev': out['m_rk_shift_prev'], 'm_rk_shift_next': out['m_rk_shift_next'], 'm_rk_w0': out['m_rk_w0'], 'm_rk_w_up': out['m_rk_w_up'], 'm_rk_a0': out['m_rk_a0'], 'm_rk_a_up': out['m_rk_a_up'], 'm_rk_g_up': out['m_rk_g_up'], 'm_rk_k_k': out['m_rk_k_k'], 'm_rk_k_a': out['m_rk_k_a'], 'm_rk_r_k': out['m_rk_r_k'], 'm_rk_ln_gain': out['m_rk_ln_gain'], 'm_rk_ln_bias': out['m_rk_ln_bias'], 'm_ffn_w1': out['m_ffn_w1'], 'm_ffn_w2': out['m_ffn_w2'], 'v_ada_w': out['v_ada_w'], 'v_ada_b': out['v_ada_b'], 'v_norm1_gain': out['v_norm1_gain'], 'v_norm2_gain': out['v_norm2_gain'], 'v_final_gain': out['v_final_gain'], 'v_w_in': out['v_w_in'], 'v_w_out': out['v_w_out'], 'v_s5_lambda_re': out['v_s5_lambda_re'], 'v_s5_lambda_im': out['v_s5_lambda_im'], 'v_s5_log_step': out['v_s5_log_step'], 'v_s5_b_re': out['v_s5_b_re'], 'v_s5_b_im': out['v_s5_b_im'], 'v_s5_c_re': out['v_s5_c_re'], 'v_s5_c_im': out['v_s5_c_im'], 'v_s5_d': out['v_s5_d'], 'v_s5_w_glu': out['v_s5_w_glu'], 'v_s5_b_glu': out['v_s5_b_glu'], 'v_rk_shift_prev': out['v_rk_shift_prev'], 'v_rk_shift_next': out['v_rk_shift_next'], 'v_rk_w0': out['v_rk_w0'], 'v_rk_w_up': out['v_rk_w_up'], 'v_rk_a0': out['v_rk_a0'], 'v_rk_a_up': out['v_rk_a_up'], 'v_rk_g_up': out['v_rk_g_up'], 'v_rk_k_k': out['v_rk_k_k'], 'v_rk_k_a': out['v_rk_k_a'], 'v_rk_r_k': out['v_rk_r_k'], 'v_rk_ln_gain': out['v_rk_ln_gain'], 'v_rk_ln_bias': out['v_rk_ln_bias'], 'v_ffn_w1': out['v_ffn_w1'], 'v_ffn_w2': out['v_ffn_w2']}


def _loss(weights, diff, rest, loss_target):
    with _jax.named_scope("forward"):
        args = {**rest, TWIN_DIFF_INPUT: diff, **{k: w.astype(_WEIGHT_DTYPES[k]) for k, w in weights.items()}}
        y = _forward(args)
    with _jax.named_scope("loss_head"):
        err = _jnp.square(y.astype(_jnp.float32) - loss_target)
        return 0.5 * _jnp.sum(_jnp.mean(err, axis=-1)) if err.ndim else 0.5 * err


def _adamw(w, g, m, v):
    m = ADAM_B1 * m + (1.0 - ADAM_B1) * g
    v = ADAM_B2 * v + (1.0 - ADAM_B2) * _jnp.square(g)
    m_hat = m / (1.0 - ADAM_B1 ** ADAM_STEP)
    v_hat = v / (1.0 - ADAM_B2 ** ADAM_STEP)
    delta = -ADAM_LR * (m_hat / (_jnp.sqrt(v_hat) + ADAM_EPS) + ADAM_WD * w)
    return delta, m, v


def reference(x, c, ada_w, ada_b, norm1_gain, norm2_gain, final_gain, w_in, w_out, s5_lambda_re, s5_lambda_im, s5_log_step, s5_b_re, s5_b_im, s5_c_re, s5_c_im, s5_d, s5_w_glu, s5_b_glu, rk_shift_prev, rk_shift_next, rk_w0, rk_w_up, rk_a0, rk_a_up, rk_g_up, rk_k_k, rk_k_a, rk_r_k, rk_ln_gain, rk_ln_bias, ffn_w1, ffn_w2, loss_target, m_ada_w, m_ada_b, m_norm1_gain, m_norm2_gain, m_final_gain, m_w_in, m_w_out, m_s5_lambda_re, m_s5_lambda_im, m_s5_log_step, m_s5_b_re, m_s5_b_im, m_s5_c_re, m_s5_c_im, m_s5_d, m_s5_w_glu, m_s5_b_glu, m_rk_shift_prev, m_rk_shift_next, m_rk_w0, m_rk_w_up, m_rk_a0, m_rk_a_up, m_rk_g_up, m_rk_k_k, m_rk_k_a, m_rk_r_k, m_rk_ln_gain, m_rk_ln_bias, m_ffn_w1, m_ffn_w2, v_ada_w, v_ada_b, v_norm1_gain, v_norm2_gain, v_final_gain, v_w_in, v_w_out, v_s5_lambda_re, v_s5_lambda_im, v_s5_log_step, v_s5_b_re, v_s5_b_im, v_s5_c_re, v_s5_c_im, v_s5_d, v_s5_w_glu, v_s5_b_glu, v_rk_shift_prev, v_rk_shift_next, v_rk_w0, v_rk_w_up, v_rk_a0, v_rk_a_up, v_rk_g_up, v_rk_k_k, v_rk_k_a, v_rk_r_k, v_rk_ln_gain, v_rk_ln_bias, v_ffn_w1, v_ffn_w2):
    given = dict(x=x, c=c, ada_w=ada_w, ada_b=ada_b, norm1_gain=norm1_gain, norm2_gain=norm2_gain, final_gain=final_gain, w_in=w_in, w_out=w_out, s5_lambda_re=s5_lambda_re, s5_lambda_im=s5_lambda_im, s5_log_step=s5_log_step, s5_b_re=s5_b_re, s5_b_im=s5_b_im, s5_c_re=s5_c_re, s5_c_im=s5_c_im, s5_d=s5_d, s5_w_glu=s5_w_glu, s5_b_glu=s5_b_glu, rk_shift_prev=rk_shift_prev, rk_shift_next=rk_shift_next, rk_w0=rk_w0, rk_w_up=rk_w_up, rk_a0=rk_a0, rk_a_up=rk_a_up, rk_g_up=rk_g_up, rk_k_k=rk_k_k, rk_k_a=rk_k_a, rk_r_k=rk_r_k, rk_ln_gain=rk_ln_gain, rk_ln_bias=rk_ln_bias, ffn_w1=ffn_w1, ffn_w2=ffn_w2, loss_target=loss_target, m_ada_w=m_ada_w, m_ada_b=m_ada_b, m_norm1_gain=m_norm1_gain, m_norm2_gain=m_norm2_gain, m_final_gain=m_final_gain, m_w_in=m_w_in, m_w_out=m_w_out, m_s5_lambda_re=m_s5_lambda_re, m_s5_lambda_im=m_s5_lambda_im, m_s5_log_step=m_s5_log_step, m_s5_b_re=m_s5_b_re, m_s5_b_im=m_s5_b_im, m_s5_c_re=m_s5_c_re, m_s5_c_im=m_s5_c_im, m_s5_d=m_s5_d, m_s5_w_glu=m_s5_w_glu, m_s5_b_glu=m_s5_b_glu, m_rk_shift_prev=m_rk_shift_prev, m_rk_shift_next=m_rk_shift_next, m_rk_w0=m_rk_w0, m_rk_w_up=m_rk_w_up, m_rk_a0=m_rk_a0, m_rk_a_up=m_rk_a_up, m_rk_g_up=m_rk_g_up, m_rk_k_k=m_rk_k_k, m_rk_k_a=m_rk_k_a, m_rk_r_k=m_rk_r_k, m_rk_ln_gain=m_rk_ln_gain, m_rk_ln_bias=m_rk_ln_bias, m_ffn_w1=m_ffn_w1, m_ffn_w2=m_ffn_w2, v_ada_w=v_ada_w, v_ada_b=v_ada_b, v_norm1_gain=v_norm1_gain, v_norm2_gain=v_norm2_gain, v_final_gain=v_final_gain, v_w_in=v_w_in, v_w_out=v_w_out, v_s5_lambda_re=v_s5_lambda_re, v_s5_lambda_im=v_s5_lambda_im, v_s5_log_step=v_s5_log_step, v_s5_b_re=v_s5_b_re, v_s5_b_im=v_s5_b_im, v_s5_c_re=v_s5_c_re, v_s5_c_im=v_s5_c_im, v_s5_d=v_s5_d, v_s5_w_glu=v_s5_w_glu, v_s5_b_glu=v_s5_b_glu, v_rk_shift_prev=v_rk_shift_prev, v_rk_shift_next=v_rk_shift_next, v_rk_w0=v_rk_w0, v_rk_w_up=v_rk_w_up, v_rk_a0=v_rk_a0, v_rk_a_up=v_rk_a_up, v_rk_g_up=v_rk_g_up, v_rk_k_k=v_rk_k_k, v_rk_k_a=v_rk_k_a, v_rk_r_k=v_rk_r_k, v_rk_ln_gain=v_rk_ln_gain, v_rk_ln_bias=v_rk_ln_bias, v_ffn_w1=v_ffn_w1, v_ffn_w2=v_ffn_w2)
    weights = {n: given[n] for n in TWIN_WEIGHTS}
    shared = {n: given[n] for n in SHARED_INPUTS}
    per_example = {n: given[n] for n in ['x', 'c']}
    grad_fn = _jax.value_and_grad(_loss, argnums=(0, 1))

    def one_microbatch(ex, loss_target):
        ex = dict(ex)
        diff = ex.pop(TWIN_DIFF_INPUT)
        return grad_fn(weights, diff, {**shared, **ex}, loss_target)

    if N_MICROBATCH == 1:
        loss, (grad_w, grad_x) = one_microbatch(per_example, given["loss_target"])
    else:
        def body(carry, xs):
            loss_sum, grad_sum = carry
            l_k, (gw_k, gx_k) = one_microbatch(xs[0], xs[1])
            with _jax.named_scope("update"):
                return (loss_sum + l_k, _jax.tree.map(_jnp.add, grad_sum, gw_k)), gx_k

        init = (_jnp.zeros((), _jnp.float32), _jax.tree.map(_jnp.zeros_like, weights))
        (loss, grad_w), grad_x = _jax.lax.scan(body, init, (per_example, given["loss_target"]))
    with _jax.named_scope("update"):
        delta_w, new_m, new_v = {}, {}, {}
        for n in TWIN_WEIGHTS:
            delta_w[n], new_m[n], new_v[n] = _adamw(weights[n], grad_w[n], given["m_" + n], given["v_" + n])
    return (loss, grad_x, *[grad_w[n] for n in TWIN_WEIGHTS], *[delta_w[n] for n in TWIN_WEIGHTS],
            *[new_m[n] for n in TWIN_WEIGHTS], *[new_v[n] for n in TWIN_WEIGHTS])
```

```python
import functools
import math

import jax
import jax.numpy as jnp
from jax import lax
from jax.experimental import pallas as pl
from jax.experimental.pallas import tpu as pltpu

F32 = jnp.float32
BF16 = jnp.bfloat16
HI = lax.Precision.HIGHEST
MXU_DTYPE = jnp.bfloat16

N_DEV = 8
MESH_AXES = ("x", "y", "c")
D_MODEL = 2048
SEQ = 2048
S5_GROUP = 16
S5_STATE = 64
RWKV_HEAD = 64
DECAY_LORA = 64
GATE_LORA = 160
N_DIR = 2
N_MOD = 6
NORM_EPS = 1e-6
GN_EPS = 64e-5
L2_EPS = 1e-12
ADAM_LR = 0.001
ADAM_B1 = 0.9
ADAM_B2 = 0.999
ADAM_EPS = 1e-08
ADAM_WD = 0.01
ADAM_STEP = 10
LANE = 128
SUBLANE = 8
S5_CHUNK_GROUPS = 8
S5_SCAN_ROWS = 256
VMEM_LIMIT = 56 * 1024 * 1024

WEIGHTS = ['ada_w', 'ada_b', 'norm1_gain', 'norm2_gain', 'final_gain', 'w_in', 'w_out', 's5_lambda_re',
           's5_lambda_im', 's5_log_step', 's5_b_re', 's5_b_im', 's5_c_re', 's5_c_im', 's5_d', 's5_w_glu',
           's5_b_glu', 'rk_shift_prev', 'rk_shift_next', 'rk_w0', 'rk_w_up', 'rk_a0', 'rk_a_up', 'rk_g_up',
           'rk_k_k', 'rk_k_a', 'rk_r_k', 'rk_ln_gain', 'rk_ln_bias', 'ffn_w1', 'ffn_w2']
SMALL = ['ada_b', 'norm1_gain', 'norm2_gain', 'final_gain', 's5_lambda_re', 's5_lambda_im', 's5_log_step',
         's5_b_re', 's5_b_im', 's5_c_re', 's5_c_im', 's5_d', 's5_b_glu', 'rk_shift_prev', 'rk_shift_next',
         'rk_k_k', 'rk_k_a', 'rk_r_k', 'rk_ln_gain', 'rk_ln_bias']
RKPACK = ['rk_w0', 'rk_a0', 'rk_w_up', 'rk_a_up', 'rk_g_up']


def _round_up(n, m):
    return (n + m - 1) // m * m


def _tile(dim, pref, unit=LANE):
    t = min(pref, dim) // unit * unit
    while t >= unit:
        if dim % t == 0:
            return t
        t -= unit
    return dim


def _cparams(sem=None):
    return pltpu.CompilerParams(dimension_semantics=sem, vmem_limit_bytes=VMEM_LIMIT)


def _full_spec(a):
    nd = a.ndim
    return pl.BlockSpec(a.shape, lambda *_: (0,) * nd)


def exchange(arrs, modes, name):
    n = len(arrs)
    out_shape = [jax.ShapeDtypeStruct((N_DEV,) + a.shape if m == 'gather' else a.shape, a.dtype)
                 for a, m in zip(arrs, modes)]

    def body(*refs):
        ins, outs = refs[:n], refs[n:2 * n]
        send_sems, recv_sems, local_sems = refs[2 * n:]
        x, y, c = (lax.axis_index(a) for a in MESH_AXES)
        me = 4 * x + 2 * y + c
        copies = []
        for i in range(n):
            gather = modes[i] == 'gather'
            mine = pltpu.make_async_copy(ins[i] if gather else ins[i].at[me], outs[i].at[me], local_sems.at[i])
            mine.start()
            copies.append(mine)
        remote = []
        for k in range(1, N_DEV):
            px = 1 - x if (k >> 2) & 1 else x
            py = 1 - y if (k >> 1) & 1 else y
            pc = 1 - c if k & 1 else c
            peer = 4 * px + 2 * py + pc
            for i in range(n):
                src = ins[i] if modes[i] == 'gather' else ins[i].at[peer]
                cp = pltpu.make_async_remote_copy(
                    src_ref=src, dst_ref=outs[i].at[me], send_sem=send_sems.at[i, k - 1],
                    recv_sem=recv_sems.at[i, k - 1], device_id=(px, py, pc), device_id_type=pl.DeviceIdType.MESH)
                cp.start()
                remote.append(cp)
        for cp in remote:
            cp.wait_recv()
        for cp in remote:
            cp.wait_send()
        for cp in copies:
            cp.wait()

    any_spec = pl.BlockSpec(memory_space=pl.ANY)
    return pl.pallas_call(
        body, name=name, out_shape=out_shape,
        in_specs=[any_spec] * n, out_specs=[any_spec] * n,
        scratch_shapes=[pltpu.SemaphoreType.DMA((n, N_DEV - 1)), pltpu.SemaphoreType.DMA((n, N_DEV - 1)),
                        pltpu.SemaphoreType.DMA((n,))],
        compiler_params=pltpu.CompilerParams(has_side_effects=True),
    )(*arrs)


def matmul(a, b, *, name, ta=False, tb=False, b_slots=False, out_slots=0, out_dtypes=(F32,), epi=None,
           extras=(), precise=False, tm=512, tn=512, tk=512):
    if ta:
        K, M = a.shape
    else:
        M, K = a.shape
    if b_slots:
        ns, br, bc = b.shape
        bshape = (br, ns * bc)
    else:
        bshape = b.shape
    N = bshape[0] if tb else bshape[1]
    assert (bshape[1] if tb else bshape[0]) == K, (a.shape, b.shape, ta, tb)
    tm, tn, tk = _tile(M, tm, SUBLANE), _tile(N, tn), _tile(K, tk, SUBLANE if K < LANE else LANE)
    if b_slots and tb:
        tk = _tile(b.shape[2], tk)
    elif b_slots:
        tn = _tile(b.shape[2], tn)
    if out_slots:
        tn = _tile(N // out_slots, tn)
    if b_slots:
        cs = b.shape[2]
        tcol = tk if tb else tn
        assert cs % tcol == 0
        per = cs // tcol
    if out_slots:
        ncs = N // out_slots
        assert ncs % tn == 0
        operc = ncs // tn
    nk = K // tk
    a_spec = pl.BlockSpec((tk, tm), lambda i, j, k: (k, i)) if ta else pl.BlockSpec((tm, tk), lambda i, j, k: (i, k))
    if b_slots:
        if tb:
            b_spec = pl.BlockSpec((None, tn, tk), lambda i, j, k: (k // per, j, k % per))
        else:
            b_spec = pl.BlockSpec((None, tk, tn), lambda i, j, k: (j // per, k, j % per))
    else:
        b_spec = pl.BlockSpec((tn, tk), lambda i, j, k: (j, k)) if tb else pl.BlockSpec((tk, tn), lambda i, j, k: (k, j))
    ex_specs = []
    for arr, kind in extras:
        if kind == 'mn':
            ex_specs.append(pl.BlockSpec((tm, tn), lambda i, j, k: (i, j)))
        else:
            ex_specs.append(pl.BlockSpec((1, tn), lambda i, j, k: (0, j)))
    if out_slots:
        o_spec = pl.BlockSpec((None, tm, tn), lambda i, j, k: (j // operc, i, j % operc))
        o_shape = (out_slots, M, ncs)
    else:
        o_spec = pl.BlockSpec((tm, tn), lambda i, j, k: (i, j))
        o_shape = (M, N)
    ne, no = len(extras), len(out_dtypes)
    dims = (((0 if ta else 1,), (1 if tb else 0,)), ((), ()))
    op_dtype = F32 if precise else MXU_DTYPE

    def body(a_ref, b_ref, *rest):
        ex_refs, out_refs, acc = rest[:ne], rest[ne:ne + no], rest[-1]
        k = pl.program_id(2)

        @pl.when(k == 0)
        def _():
            acc[...] = jnp.zeros_like(acc)

        acc[...] += lax.dot_general(a_ref[...].astype(op_dtype), b_ref[...].astype(op_dtype), dims,
                                    precision=HI if precise else None, preferred_element_type=F32)

        @pl.when(k == nk - 1)
        def _():
            res = epi(acc[...], *[e[...] for e in ex_refs]) if epi is not None else (acc[...],)
            for o, r in zip(out_refs, res):
                o[...] = r.astype(o.dtype)

    outs = pl.pallas_call(
        body, name=name, grid=(M // tm, N // tn, nk),
        in_specs=[a_spec, b_spec] + ex_specs, out_specs=[o_spec] * no,
        out_shape=[jax.ShapeDtypeStruct(o_shape, dt) for dt in out_dtypes],
        scratch_shapes=[pltpu.VMEM((tm, tn), F32)],
        compiler_params=_cparams(("parallel", "parallel", "arbitrary")),
    )(a, b, *[e[0] for e in extras])
    return outs[0] if no == 1 else outs


def rowcall(fn, name, tiled, full, tiled_out, acc_out, tt):
    views = [(t, t.shape[1], 0) if not isinstance(t, tuple) else t for t in tiled]
    T = views[0][0].shape[0]
    tt = _tile(T, tt, SUBLANE)
    nt, nf, nto, nao = len(views), len(full), len(tiled_out), len(acc_out)

    def view_spec(w, cb):
        return pl.BlockSpec((tt, w), lambda i: (i, cb))

    in_specs = [view_spec(w, cb) for _, w, cb in views] + [_full_spec(f) for f in full]
    out_specs = [pl.BlockSpec((tt, w), lambda i: (i, 0)) for w, _ in tiled_out]
    out_specs += [pl.BlockSpec(s, lambda i, nd=len(s): (0,) * nd) for s in acc_out]
    out_shape = [jax.ShapeDtypeStruct((T, w), dt) for w, dt in tiled_out]
    out_shape += [jax.ShapeDtypeStruct(s, F32) for s in acc_out]

    def body(*refs):
        tin, fin = refs[:nt], refs[nt:nt + nf]
        tout, aout = refs[nt + nf:nt + nf + nto], refs[nt + nf + nto:]
        touts, aouts = fn(*[r[...] for r in tin], *[r[...] for r in fin])
        for r, v in zip(tout, touts):
            r[...] = v.astype(r.dtype)
        if nao:
            @pl.when(pl.program_id(0) == 0)
            def _():
                for r in aout:
                    r[...] = jnp.zeros_like(r)

            for r, v in zip(aout, aouts):
                r[...] += v.astype(F32)

    outs = pl.pallas_call(
        body, name=name, grid=(T // tt,), in_specs=in_specs, out_specs=out_specs, out_shape=out_shape,
        compiler_params=_cparams(("arbitrary",) if nao else ("parallel",)),
    )(*[v[0] for v in views], *full)
    return outs[:nto], outs[nto:]


def _mm(a, b):
    return jnp.dot(a.astype(MXU_DTYPE), b.astype(MXU_DTYPE), preferred_element_type=F32)


def _rms(x, gain):
    ms = jnp.mean(x * x, axis=-1, keepdims=True)
    return x * lax.rsqrt(ms + NORM_EPS) * gain


def _normmod(x, gain, scale, shift):
    return _rms(x, gain) * (1.0 + scale) + shift


def _gelu_tanh(y):
    return 0.5 * y * (1.0 + jnp.tanh(math.sqrt(2.0 / math.pi) * (y + 0.044715 * (y * y * y))))


def _sigmoid(x):
    return 1.0 / (1.0 + jnp.exp(-x))


def _softplus(x):
    return jnp.maximum(x, 0.0) + jnp.log(1.0 + jnp.exp(-jnp.abs(x)))


def _seg_mats(width, seg):
    r = lax.broadcasted_iota(jnp.int32, (width, LANE), 0) // seg
    c = lax.broadcasted_iota(jnp.int32, (width, LANE), 1)
    s = (r == c).astype(F32)
    rt = lax.broadcasted_iota(jnp.int32, (LANE, width), 0)
    ct = lax.broadcasted_iota(jnp.int32, (LANE, width), 1) // seg
    st = (rt == ct).astype(F32)
    return s, st


def _segsum(x, s):
    return jnp.dot(x, s, precision=HI, preferred_element_type=F32)


def _s5_prep_fn(lre, lim, lstep, bre, bim):
    step = jnp.exp(lstep)
    mag = jnp.exp(lre * step)
    lbr = mag * jnp.cos(lim * step)
    lbi = mag * jnp.sin(lim * step)
    den = lre * lre + lim * lim
    nr = lbr - 1.0
    ni = lbi
    cre = (nr * lre + ni * lim) / den
    cim = (ni * lre - nr * lim) / den
    bbr = jnp.stack([cre[d:d + 1] * bre - cim[d:d + 1] * bim for d in range(N_DIR)])
    bbi = jnp.stack([cre[d:d + 1] * bim + cim[d:d + 1] * bre for d in range(N_DIR)])
    return lbr, lbi, bbr, bbi


def s5_prep(lre, lim, lstep, bre, bim):
    ns = lre.shape[1]

    def body(lre_r, lim_r, ls_r, bre_r, bim_r, lbr_r, lbi_r, bbr_r, bbi_r):
        lbr, lbi, bbr, bbi = _s5_prep_fn(lre_r[...], lim_r[...], ls_r[...], bre_r[...], bim_r[...])
        lbr_r[...] = lbr
        lbi_r[...] = lbi
        bbr_r[...] = bbr
        bbi_r[...] = bbi

    return pl.pallas_call(
        body, name="s5_prep",
        out_shape=[jax.ShapeDtypeStruct((N_DIR, ns), F32)] * 2 + [jax.ShapeDtypeStruct((N_DIR, S5_GROUP, ns), F32)] * 2,
        compiler_params=_cparams(),
    )(lre, lim, lstep, bre, bim)


def s5_prep_bwd(lre, lim, lstep, bre, bim, dlbr, dlbi, dbbr, dbbi):
    ns = lre.shape[1]

    def body(lre_r, lim_r, ls_r, bre_r, bim_r, d1, d2, d3, d4, o1, o2, o3, o4, o5):
        _, vjp = jax.vjp(_s5_prep_fn, lre_r[...], lim_r[...], ls_r[...], bre_r[...], bim_r[...])
        g = vjp((d1[...], d2[...], d3[...], d4[...]))
        for o, v in zip((o1, o2, o3, o4, o5), g):
            o[...] = v

    return pl.pallas_call(
        body, name="s5_prep_bwd",
        out_shape=[jax.ShapeDtypeStruct((N_DIR, ns), F32)] * 3 + [jax.ShapeDtypeStruct((S5_GROUP, ns), F32)] * 2,
        compiler_params=_cparams(),
    )(lre, lim, lstep, bre, bim, dlbr, dlbi, dbbr, dbbi)


def s5_scan(bre, bim, lre, lim, *, reverse, name):
    T, NS = bre.shape
    tt = _tile(T, S5_SCAN_ROWS, SUBLANE)
    wl = _tile(NS, 512)
    nT = T // tt
    ngrp = tt // SUBLANE

    def tmap(j, i):
        return ((nT - 1 - i) if reverse else i, j)

    def body(bre_r, bim_r, lre_r, lim_r, sre_r, sim_r, cre, cim):
        @pl.when(pl.program_id(1) == 0)
        def _():
            cre[...] = jnp.zeros_like(cre)
            cim[...] = jnp.zeros_like(cim)

        lr = jnp.broadcast_to(lre_r[...], (SUBLANE, wl))
        li = jnp.broadcast_to(lim_r[...], (SUBLANE, wl))
        row = lax.broadcasted_iota(jnp.int32, (SUBLANE, wl), 0)
        pows = [(lr, li)]
        for _ in range(3):
            pr, pi = pows[-1]
            pows.append((pr * pr - pi * pi, 2.0 * pr * pi))
        e = (SUBLANE - row) if reverse else (row + 1)
        Pr = jnp.ones((SUBLANE, wl), F32)
        Pi = jnp.zeros((SUBLANE, wl), F32)
        for bit, (qr, qi) in enumerate(pows):
            on = ((e >> bit) & 1) == 1
            nr, ni = Pr * qr - Pi * qi, Pr * qi + Pi * qr
            Pr, Pi = jnp.where(on, nr, Pr), jnp.where(on, ni, Pi)

        def group(g, carry):
            gg = (ngrp - 1 - g) if reverse else g
            rows = pl.ds(pl.multiple_of(gg * SUBLANE, SUBLANE), SUBLANE)
            sr, si = bre_r[rows, :], bim_r[rows, :]
            for lvl, k in enumerate((1, 2, 4)):
                qr, qi = pows[lvl]
                if reverse:
                    shr = pltpu.roll(sr, SUBLANE - k, 0)
                    shi = pltpu.roll(si, SUBLANE - k, 0)
                    keep = row < SUBLANE - k
                else:
                    shr = pltpu.roll(sr, k, 0)
                    shi = pltpu.roll(si, k, 0)
                    keep = row >= k
                shr = jnp.where(keep, shr, 0.0)
                shi = jnp.where(keep, shi, 0.0)
                sr, si = sr + qr * shr - qi * shi, si + qr * shi + qi * shr
            cr, ci = cre[...], cim[...]
            sr, si = sr + Pr * cr - Pi * ci, si + Pr * ci + Pi * cr
            sre_r[rows, :] = sr
            sim_r[rows, :] = si
            last = 0 if reverse else SUBLANE - 1
            cre[...] = jnp.broadcast_to(sr[last:last + 1, :], (SUBLANE, wl))
            cim[...] = jnp.broadcast_to(si[last:last + 1, :], (SUBLANE, wl))
            return carry

        lax.fori_loop(0, ngrp, group, 0)

    blk = pl.BlockSpec((tt, wl), tmap)
    row_spec = pl.BlockSpec((1, wl), lambda j, i: (0, j))
    return pl.pallas_call(
        body, name=name, grid=(NS // wl, nT),
        in_specs=[blk, blk, row_spec, row_spec], out_specs=[blk, blk],
        out_shape=[jax.ShapeDtypeStruct((T, NS), F32)] * 2,
        scratch_shapes=[pltpu.VMEM((SUBLANE, wl), F32)] * 2,
        compiler_params=_cparams(("parallel", "arbitrary")),
    )(bre, bim, lre, lim)


def s5_dlam(sre, sim, gre, gim, *, reverse, name):
    T, NS = sre.shape
    wl = _tile(NS, 256)

    def body(sr_r, si_r, gr_r, gi_r, dr_r, di_r):
        row = lax.broadcasted_iota(jnp.int32, (T, wl), 0)
        if reverse:
            keep = row < T - 1
            pr = jnp.where(keep, pltpu.roll(sr_r[...], T - 1, 0), 0.0)
            pi = jnp.where(keep, pltpu.roll(si_r[...], T - 1, 0), 0.0)
        else:
            keep = row >= 1
            pr = jnp.where(keep, pltpu.roll(sr_r[...], 1, 0), 0.0)
            pi = jnp.where(keep, pltpu.roll(si_r[...], 1, 0), 0.0)
        gr, gi = gr_r[...], gi_r[...]
        dr_r[...] = jnp.sum(pr * gr + pi * gi, axis=0, keepdims=True)
        di_r[...] = jnp.sum(pr * gi - pi * gr, axis=0, keepdims=True)

    blk = pl.BlockSpec((T, wl), lambda j: (0, j))
    o = pl.BlockSpec((1, wl), lambda j: (0, j))
    return pl.pallas_call(
        body, name=name, grid=(NS // wl,), in_specs=[blk] * 4, out_specs=[o, o],
        out_shape=[jax.ShapeDtypeStruct((1, NS), F32)] * 2,
        compiler_params=_cparams(("parallel",)),
    )(sre, sim, gre, gim)


def _s5_bu_fn(u, wblk):
    nch = wblk.shape[0]
    cw = S5_CHUNK_GROUPS * S5_GROUP
    sw = S5_CHUNK_GROUPS * S5_STATE
    parts = [[] for _ in range(4)]
    for ch in range(nch):
        res = _mm(u[:, ch * cw:(ch + 1) * cw], wblk[ch])
        for q in range(4):
            parts[q].append(res[:, q * sw:(q + 1) * sw])
    return tuple(jnp.concatenate(p, axis=1) if nch > 1 else p[0] for p in parts)


def _s5_out_fn(x0r, x0i, x1r, x1i, u, cre, cim, dsk, wglu, bglu):
    xr, xi = x0r + x1r, x0i + x1i
    nch = cre.shape[0]
    sw = S5_CHUNK_GROUPS * S5_STATE
    ys = [_mm(xr[:, ch * sw:(ch + 1) * sw], cre[ch]) - _mm(xi[:, ch * sw:(ch + 1) * sw], cim[ch]) for ch in range(nch)]
    y = jnp.concatenate(ys, axis=1) if nch > 1 else ys[0]
    z = _gelu_tanh(y + dsk * u)
    gate = _sigmoid(_mm(z, wglu) + bglu)
    return z * gate


def _rk_dims():
    C = D_MODEL // 2
    LW = N_DIR * DECAY_LORA
    GP = _round_up(GATE_LORA, LANE)
    return C, LW, GP


def _rk_pre_fn(ps, w0, wup0, wup1, a0, aup0, aup1, gup, k_k, k_a):
    C, LW, GP = _rk_dims()
    r, k, v = ps[:, 0:C], ps[:, C:2 * C], ps[:, 2 * C:3 * C]
    wdn = ps[:, 3 * C:3 * C + LW]
    adn = ps[:, 3 * C + LW:3 * C + 2 * LW]
    gdn = ps[:, 3 * C + 2 * LW:3 * C + 2 * LW + GP]
    s, st = _seg_mats(C, RWKV_HEAD)
    kk = k * k_k
    n2 = _segsum(kk * kk, s)
    n2 = jnp.where(n2 > 0.0, n2, 1.0)
    inv = 1.0 / jnp.maximum(jnp.sqrt(n2), L2_EPS)
    kkn = kk * _segsum(inv, st)
    tw = jnp.tanh(wdn)
    wup, aup = (wup0, wup1), (aup0, aup1)
    ws, ks, bs = [], [], []
    for d in range(N_DIR):
        wraw = w0[d:d + 1] + _mm(tw, wup[d])
        w = -_softplus(-wraw) - 0.5
        ws.append(jnp.exp(-jnp.exp(w)))
        a = _sigmoid(a0[d:d + 1] + _mm(adn, aup[d]))
        ks.append(k * (1.0 + (a - 1.0) * k_a))
        bs.append(kkn * a)
    g = _mm(_sigmoid(gdn), gup)
    return r, v, kkn, ws[0], ws[1], ks[0], ks[1], bs[0], bs[1], g


def _rk_post_fn(y0, y1, r, v, k0, k1, g, r_k, lng, lnb):
    C = r.shape[1]
    s, st = _seg_mats(C, RWKV_HEAD)
    y = y0 + y1
    mu = _segsum(_segsum(y, s) * (1.0 / RWKV_HEAD), st)
    yc = y - mu
    var = _segsum(_segsum(yc * yc, s) * (1.0 / RWKV_HEAD), st)
    yn = yc * lax.rsqrt(var + GN_EPS) * lng + lnb
    bonus = _segsum(_segsum(r * (k0 + k1) * r_k, s), st)
    return (yn + bonus * v) * g


def rk_shift(proj, mp, mn, col0):
    T = proj.shape[0]
    W = mp.shape[1]
    wl = _tile(math.gcd(W, col0), 256)
    cb0 = col0 // wl

    def body(p_r, mp_r, mn_r, o_r):
        p = p_r[...]
        row = lax.broadcasted_iota(jnp.int32, (T, wl), 0)
        prev = jnp.where(row >= 1, pltpu.roll(p, 1, 0), 0.0)
        nxt = jnp.where(row < T - 1, pltpu.roll(p, T - 1, 0), 0.0)
        o_r[...] = p + mp_r[...] * (prev - p) + mn_r[...] * (nxt - p)

    rs = pl.BlockSpec((1, wl), lambda j: (0, j))
    return pl.pallas_call(
        body, name="rk_shift", grid=(W // wl,),
        in_specs=[pl.BlockSpec((T, wl), lambda j: (0, cb0 + j)), rs, rs],
        out_specs=pl.BlockSpec((T, wl), lambda j: (0, j)),
        out_shape=jax.ShapeDtypeStruct((T, W), F32),
        compiler_params=_cparams(("parallel",)),
    )(proj, mp, mn)


def rk_shift_bwd(dps, proj, mp, mn, col0):
    T, W = dps.shape
    wl = _tile(math.gcd(W, col0), 256)
    cb0 = col0 // wl

    def body(d_r, p_r, mp_r, mn_r, dp_r, dmp_r, dmn_r):
        d, p = d_r[...], p_r[...]
        mpv, mnv = mp_r[...], mn_r[...]
        row = lax.broadcasted_iota(jnp.int32, (T, wl), 0)
        first, last = row >= 1, row < T - 1
        prev = jnp.where(first, pltpu.roll(p, 1, 0), 0.0)
        nxt = jnp.where(last, pltpu.roll(p, T - 1, 0), 0.0)
        dmp_r[...] = jnp.sum(d * (prev - p), axis=0, keepdims=True)
        dmn_r[...] = jnp.sum(d * (nxt - p), axis=0, keepdims=True)
        dp_r[...] = (d * (1.0 - mpv - mnv) + jnp.where(last, pltpu.roll(d * mpv, T - 1, 0), 0.0)
                     + jnp.where(first, pltpu.roll(d * mnv, 1, 0), 0.0))

    rs = pl.BlockSpec((1, wl), lambda j: (0, j))
    blk = pl.BlockSpec((T, wl), lambda j: (0, j))
    return pl.pallas_call(
        body, name="rk_shift_bwd", grid=(W // wl,),
        in_specs=[blk, pl.BlockSpec((T, wl), lambda j: (0, cb0 + j)), rs, rs],
        out_specs=[blk, rs, rs],
        out_shape=[jax.ShapeDtypeStruct((T, W), F32), jax.ShapeDtypeStruct((1, W), F32), jax.ShapeDtypeStruct((1, W), F32)],
        compiler_params=_cparams(("parallel",)),
    )(dps, proj, mp, mn)


RK_HEADS_PER_STEP = 4
RK_TIME_BLOCK = 128


def rk_scan(r, kk, w, k, b, vT, *, reverse, name):
    H, T, N = r.shape
    hh = min(RK_HEADS_PER_STEP, H)
    tb = min(RK_TIME_BLOCK, T)
    nb = T // tb

    def tix(i):
        return (nb - 1 - i) if reverse else i

    def body(r_r, kk_r, w_r, k_r, b_r, vT_r, yT_r, hist_r, S):
        @pl.when(pl.program_id(1) == 0)
        def _():
            S[...] = jnp.zeros_like(S)

        yT_r[...] = jnp.zeros_like(yT_r)
        lane = lax.broadcasted_iota(jnp.int32, (N, tb), 1)

        def step(s, carry):
            t = (tb - 1 - s) if reverse else s
            row = pl.ds(t, 1)
            hit = lane == t
            for h in range(hh):
                sp = S[h]
                hist_r[t, h] = sp
                sa = -jnp.sum(sp * kk_r[h, row, :], axis=1, keepdims=True)
                vcol = jnp.sum(jnp.where(hit, vT_r[h], 0.0), axis=1, keepdims=True)
                sn = sp * w_r[h, row, :] + sa * b_r[h, row, :] + vcol * k_r[h, row, :]
                S[h] = sn
                ycol = jnp.sum(sn * r_r[h, row, :], axis=1, keepdims=True)
                yT_r[h] = jnp.where(hit, ycol, yT_r[h])
            return carry

        lax.fori_loop(0, tb, step, 0)

    rows = pl.BlockSpec((hh, tb, N), lambda g, i: (g, tix(i), 0))
    cols = pl.BlockSpec((hh, N, tb), lambda g, i: (g, 0, tix(i)))
    return pl.pallas_call(
        body, name=name, grid=(H // hh, nb),
        in_specs=[rows] * 5 + [cols],
        out_specs=[cols, pl.BlockSpec((tb, hh, N, N), lambda g, i: (tix(i), g, 0, 0))],
        out_shape=[jax.ShapeDtypeStruct((H, N, T), F32), jax.ShapeDtypeStruct((T, H, N, N), F32)],
        scratch_shapes=[pltpu.VMEM((hh, N, N), F32)],
        compiler_params=_cparams(("parallel", "arbitrary")),
    )(r, kk, w, k, b, vT)


def rk_scan_bwd(r, kk, w, k, b, vT, dyT, hist, *, reverse, name):
    H, T, N = r.shape
    hh = min(RK_HEADS_PER_STEP, H)
    tb = min(RK_TIME_BLOCK, T)
    nb = T // tb

    def tix(i):
        return i if reverse else (nb - 1 - i)

    def body(r_r, kk_r, w_r, k_r, b_r, vT_r, dyT_r, hist_r, dr_r, dkk_r, dw_r, dk_r, db_r, dvT_r, G):
        @pl.when(pl.program_id(1) == 0)
        def _():
            G[...] = jnp.zeros_like(G)

        dvT_r[...] = jnp.zeros_like(dvT_r)
        lane = lax.broadcasted_iota(jnp.int32, (N, tb), 1)

        def step(s, carry):
            t = s if reverse else (tb - 1 - s)
            row = pl.ds(t, 1)
            hit = lane == t
            for h in range(hh):
                sp = hist_r[t, h]
                kkv, wv, kv, bv, rv = kk_r[h, row, :], w_r[h, row, :], k_r[h, row, :], b_r[h, row, :], r_r[h, row, :]
                vcol = jnp.sum(jnp.where(hit, vT_r[h], 0.0), axis=1, keepdims=True)
                dycol = jnp.sum(jnp.where(hit, dyT_r[h], 0.0), axis=1, keepdims=True)
                sa = -jnp.sum(sp * kkv, axis=1, keepdims=True)
                sn = sp * wv + sa * bv + vcol * kv
                g = G[h] + dycol * rv
                dr_r[h, row, :] = jnp.sum(sn * dycol, axis=0, keepdims=True)
                dw_r[h, row, :] = jnp.sum(g * sp, axis=0, keepdims=True)
                db_r[h, row, :] = jnp.sum(g * sa, axis=0, keepdims=True)
                dk_r[h, row, :] = jnp.sum(g * vcol, axis=0, keepdims=True)
                dsa = jnp.sum(g * bv, axis=1, keepdims=True)
                dvcol = jnp.sum(g * kv, axis=1, keepdims=True)
                dkk_r[h, row, :] = -jnp.sum(sp * dsa, axis=0, keepdims=True)
                G[h] = g * wv - dsa * kkv
                dvT_r[h] = jnp.where(hit, dvcol, dvT_r[h])
            return carry

        lax.fori_loop(0, tb, step, 0)

    rows = pl.BlockSpec((hh, tb, N), lambda g, i: (g, tix(i), 0))
    cols = pl.BlockSpec((hh, N, tb), lambda g, i: (g, 0, tix(i)))
    hist_spec = pl.BlockSpec((tb, hh, N, N), lambda g, i: (tix(i), g, 0, 0))
    return pl.pallas_call(
        body, name=name, grid=(H // hh, nb),
        in_specs=[rows] * 5 + [cols, cols, hist_spec],
        out_specs=[rows] * 5 + [cols],
        out_shape=[jax.ShapeDtypeStruct((H, T, N), F32)] * 5 + [jax.ShapeDtypeStruct((H, N, T), F32)],
        scratch_shapes=[pltpu.VMEM((hh, N, N), F32)],
        compiler_params=_cparams(("parallel", "arbitrary")),
    )(r, kk, w, k, b, vT, dyT, hist)


def adam(parts, w, m, v, name):
    P, R, C = parts.shape
    tr = _tile(R, max(SUBLANE, (1 << 19) // max(C, 1) // SUBLANE * SUBLANE), SUBLANE)
    c1 = 1.0 / (1.0 - ADAM_B1 ** ADAM_STEP)
    c2 = 1.0 / (1.0 - ADAM_B2 ** ADAM_STEP)

    def body(p_r, w_r, m_r, v_r, g_o, d_o, m_o, v_o):
        g = p_r[0].astype(F32)
        for q in range(1, P):
            g = g + p_r[q].astype(F32)
        m2 = ADAM_B1 * m_r[...] + (1.0 - ADAM_B1) * g
        v2 = ADAM_B2 * v_r[...] + (1.0 - ADAM_B2) * (g * g)
        g_o[...] = g
        m_o[...] = m2
        v_o[...] = v2
        d_o[...] = -ADAM_LR * ((m2 * c1) / (jnp.sqrt(v2 * c2) + ADAM_EPS) + ADAM_WD * w_r[...])

    blk = pl.BlockSpec((tr, C), lambda i: (i, 0))
    return pl.pallas_call(
        body, name=name, grid=(R // tr,),
        in_specs=[pl.BlockSpec((P, tr, C), lambda i: (0, i, 0)), blk, blk, blk], out_specs=[blk] * 4,
        out_shape=[jax.ShapeDtypeStruct((R, C), F32)] * 4,
        compiler_params=_cparams(("parallel",)),
    )(parts, w, m, v)


def _pack_flat(arrs):
    rows = []
    for a in arrs:
        f = a.reshape(-1).astype(F32)
        n = _round_up(f.shape[0], SUBLANE * LANE)
        rows.append(jnp.pad(f, (0, n - f.shape[0])).reshape(-1, LANE))
    return jnp.concatenate(rows, axis=0)


def _unpack_flat(packed, shapes):
    out, r0 = [], 0
    for s in shapes:
        n = math.prod(s)
        nr = _round_up(n, SUBLANE * LANE) // LANE
        out.append(packed[r0:r0 + nr].reshape(-1)[:n].reshape(s))
        r0 += nr
    return out


def _pack_rows(arrs):
    rows = []
    for a in arrs:
        f = a.reshape(-1, a.shape[-1]).astype(F32)
        n = _round_up(f.shape[0], SUBLANE)
        rows.append(jnp.pad(f, ((0, n - f.shape[0]), (0, 0))))
    return jnp.concatenate(rows, axis=0)


def _unpack_rows(packed, shapes):
    out, r0 = [], 0
    for s in shapes:
        nr = math.prod(s[:-1])
        out.append(packed[r0:r0 + nr].reshape(s))
        r0 += _round_up(nr, SUBLANE)
    return out


def _cols_from_slots(g):
    return jnp.moveaxis(g, 0, -2).reshape(g.shape[1:-1] + (N_DEV * g.shape[-1],))


def _cols_to_slots(a):
    cs = a.shape[-1] // N_DEV
    return jnp.moveaxis(a.reshape(a.shape[:-1] + (N_DEV, cs)), -2, 0)


def _step(P, M, V):
    D, T = D_MODEL, SEQ
    S5W = D // 2
    C, LW, GP = _rk_dims()
    H = C // RWKV_HEAD
    G = S5W // S5_GROUP
    NS = G * S5_STATE
    NCH = G // S5_CHUNK_GROUPS
    SW = S5_CHUNK_GROUPS * S5_STATE
    RIN = 3 * C + 2 * LW + GATE_LORA
    RINP = 3 * C + 2 * LW + GP
    PROJ = S5W + RIN
    PROJP = S5W + RINP
    FF = 4 * D
    me = 4 * lax.axis_index("x") + 2 * lax.axis_index("y") + lax.axis_index("c")
    cs_mod = N_MOD * D // N_DEV
    eye = jnp.eye(S5_CHUNK_GROUPS, dtype=F32)

    x = P['x'][0]
    target = P['loss_target'][0]

    (c_all,) = exchange([P['c']], ['gather'], "comm_gather_c")
    c_all = c_all.reshape(N_DEV, D)
    (c_act,), _ = rowcall(lambda cv: ((cv * _sigmoid(cv),), ()), "silu_c", [c_all], [], [(D, F32)], [], N_DEV)
    ada_b_loc = lax.dynamic_slice(P['ada_b'], (0, me * cs_mod), (1, cs_mod))
    mod_loc = matmul(c_act, P['ada_w'][0], name="mod_mm", precise=True, extras=[(ada_b_loc, 'n')],
                     epi=lambda acc, bias: (acc + bias,))

    rk_shapes = [P[n][0].shape for n in RKPACK]
    rk_pack = _pack_rows([P[n][0] for n in RKPACK])
    gathered = exchange(
        [mod_loc, P['w_in'][0].astype(BF16), P['w_out'][0].astype(BF16), P['ffn_w1'][0].astype(BF16),
         P['ffn_w2'][0].astype(BF16), P['s5_w_glu'][0].astype(BF16), rk_pack],
        ['gather'] * 7, "comm_gather_weights")
    mod_all, w_in_g, w_out_g, w1_g, w2_g, wglu_g, rk_g = gathered
    mod_me = lax.dynamic_index_in_dim(mod_all, me, axis=1, keepdims=False).reshape(N_MOD, 1, D)
    shift1, scale1, gate1, shift2, scale2, gate2 = (mod_me[i] for i in range(N_MOD))
    w_in = jnp.pad(_cols_from_slots(w_in_g), ((0, 0), (0, PROJP - PROJ)))
    w_out = w_out_g.reshape(D, D)
    w2 = w2_g.reshape(FF, D)
    wglu = wglu_g.reshape(S5W, S5W)
    rk_full = _unpack_rows(_cols_from_slots(rk_g), [s[:-1] + (C,) for s in rk_shapes])
    rk_w0, rk_a0, rk_wup, rk_aup, rk_gup = rk_full

    def lora_pad(up):
        z = jnp.zeros((N_DIR, LW, C), F32)
        for d in range(N_DIR):
            z = z.at[d, d * DECAY_LORA:(d + 1) * DECAY_LORA].set(up[d])
        return z

    wup_p, aup_p = lora_pad(rk_wup), lora_pad(rk_aup)
    gup_p = jnp.pad(rk_gup, ((0, GP - GATE_LORA), (0, 0)))
    mu_prev = jnp.pad(P['rk_shift_prev'], ((0, 0), (0, RINP - RIN)))
    mu_next = jnp.pad(P['rk_shift_next'], ((0, 0), (0, RINP - RIN)))
    r_k = P['rk_r_k'].reshape(1, C)
    fgain = P['final_gain'].reshape(1, D)

    TT = 256
    (h1,), _ = rowcall(lambda xv, g, sc, sh: ((_normmod(xv, g, sc, sh),), ()), "norm1",
                       [x], [P['norm1_gain'], scale1, shift1], [(D, BF16)], [], TT)
    proj = matmul(h1, w_in, name="proj_mm")

    lre = P['s5_lambda_re'][0].reshape(N_DIR, NS)
    lim = P['s5_lambda_im'][0].reshape(N_DIR, NS)
    lstep = jnp.broadcast_to(P['s5_log_step'][0][:, :, None], (N_DIR, G, S5_STATE)).reshape(N_DIR, NS)
    bre = P['s5_b_re'][0].reshape(NS, S5_GROUP).T
    bim = P['s5_b_im'][0].reshape(NS, S5_GROUP).T
    lbr, lbi, bbr, bbi = s5_prep(lre, lim, lstep, bre, bim)
    bbar = jnp.stack([bbr, bbi], axis=1).reshape(N_DIR, 2, S5_GROUP, NCH, S5_CHUNK_GROUPS, S5_STATE)
    wblk = jnp.einsum('drhcgp,gk->cghdrkp', bbar, eye).reshape(NCH, S5_CHUNK_GROUPS * S5_GROUP, 4 * SW)
    wblk = wblk.astype(MXU_DTYPE)
    u_view = (proj, S5W, 0)
    bus, _ = rowcall(lambda uv, wb: (_s5_bu_fn(uv, wb), ()), "s5_bu", [u_view], [wblk], [(NS, F32)] * 4, [], TT)
    s0r, s0i = s5_scan(bus[0], bus[1], lbr[0:1], lbi[0:1], reverse=False, name="s5_scan_f0")
    s1r, s1i = s5_scan(bus[2], bus[3], lbr[1:2], lbi[1:2], reverse=True, name="s5_scan_f1")

    def cblk(cm):
        c4 = cm.reshape(NCH, S5_CHUNK_GROUPS, S5_GROUP, S5_STATE)
        return jnp.einsum('cghp,gk->cgpkh', c4, eye).reshape(NCH, SW, S5_CHUNK_GROUPS * S5_GROUP)

    cre_b = cblk(P['s5_c_re'][0]).astype(MXU_DTYPE)
    cim_b = cblk(P['s5_c_im'][0]).astype(MXU_DTYPE)
    s5_full = [cre_b, cim_b, P['s5_d'], wglu, P['s5_b_glu']]
    TS = 128
    (y_s5,), _ = rowcall(lambda *a: ((_s5_out_fn(*a),), ()), "s5_out", [s0r, s0i, s1r, s1i, u_view], s5_full,
                         [(S5W, BF16)], [], TS)

    ps = rk_shift(proj, mu_prev, mu_next, S5W)
    pre_full = [rk_w0, wup_p[0], wup_p[1], rk_a0, aup_p[0], aup_p[1], gup_p, P['rk_k_k'], P['rk_k_a']]
    pre_out, _ = rowcall(lambda *a: (_rk_pre_fn(*a)[2:], ()), "rk_pre", [ps], pre_full, [(C, F32)] * 8, [], TS)
    kkn, w_0, w_1, k_0, k_1, b_0, b_1, g_gate = pre_out
    r_t, v_t = ps[:, 0:C], ps[:, 2 * C:3 * C]

    def hm(a):
        return a.reshape(T, H, RWKV_HEAD).transpose(1, 0, 2)

    def hmT(a):
        return a.reshape(T, H, RWKV_HEAD).transpose(1, 2, 0)

    def unT(a):
        return a.transpose(2, 0, 1).reshape(T, C)

    def unhm(a):
        return a.transpose(1, 0, 2).reshape(T, C)

    r_h, kk_h, vT_h = hm(r_t), hm(kkn), hmT(v_t)
    dir_rows = [(hm(w_0), hm(k_0), hm(b_0)), (hm(w_1), hm(k_1), hm(b_1))]
    yT, hist = [], []
    for d in range(N_DIR):
        wd, kd, bd = dir_rows[d]
        yd, hd = rk_scan(r_h, kk_h, wd, kd, bd, vT_h, reverse=(d == 1), name=f"rk_scan_f{d}")
        yT.append(yd)
        hist.append(hd)
    y_0, y_1 = unT(yT[0]), unT(yT[1])
    post_full = [r_k, P['rk_ln_gain'], P['rk_ln_bias']]
    post_tiled = [y_0, y_1, (ps, C, 0), (ps, C, 2), k_0, k_1, g_gate]
    (y_rk,), _ = rowcall(lambda *a: ((_rk_post_fn(*a),), ()), "rk_post", post_tiled, post_full, [(C, BF16)], [], TS)

    ycat = jnp.concatenate([y_s5, y_rk], axis=1)
    mixed = matmul(ycat, w_out, name="out_mm")

    def res_norm(xv, mv, gate, g, sc, sh):
        x1v = xv + gate * mv
        return x1v, _normmod(x1v, g, sc, sh)

    (x1, h2), _ = rowcall(lambda *a: (res_norm(*a), ()), "norm2", [x, mixed], [gate1, P['norm2_gain'], scale2, shift2],
                          [(D, F32), (D, BF16)], [], TT)
    a_ff, hh_ff = matmul(h2, w1_g, name="ffn1_mm", b_slots=True, out_dtypes=(F32, BF16),
                         epi=lambda acc: (acc, jnp.square(jnp.maximum(acc, 0.0))))
    ffn = matmul(hh_ff, w2, name="ffn2_mm")

    def loss_fn(x1v, fv, tg, gate, fg):
        def f(x1_, f_, gate_, fg_):
            out = _rms(x1_ + gate_ * f_, fg_)
            err = out - tg
            return 0.5 * jnp.sum(jnp.sum(err * err, axis=1, keepdims=True), axis=0, keepdims=True) * (1.0 / D)
        lv, vjp = jax.vjp(f, x1v, fv, gate, fg)
        dx1, dff, dgate, dfg = vjp(jnp.ones((1, 1), F32))
        return (dx1, dff), (jnp.broadcast_to(lv, (SUBLANE, LANE)), dgate, dfg)

    (dx2, dffn), (loss_t, dgate2, dfgain) = rowcall(
        loss_fn, "loss", [x1, ffn, target], [gate2, fgain], [(D, F32), (D, BF16)], [(SUBLANE, LANE), (1, D), (1, D)], TT)
    loss = lax.psum(loss_t[0, 0], MESH_AXES)

    da = matmul(dffn, w2, name="dffn2_mm", tb=True, out_dtypes=(BF16,), extras=[(a_ff, 'mn')],
                epi=lambda acc, av: (acc * (2.0 * jnp.maximum(av, 0.0)),))
    g_w2 = matmul(hh_ff, dffn, name="gw2_mm", ta=True, out_dtypes=(BF16,))
    dh2 = matmul(da, w1_g, name="dh2_mm", tb=True, b_slots=True)
    g_w1 = matmul(h2, da, name="gw1_mm", ta=True, out_slots=N_DEV, out_dtypes=(BF16,))

    def res_norm_bwd(dx2v, dh2v, xv, mv, gate, g, sc, sh):
        _, vjp = jax.vjp(res_norm, xv, mv, gate, g, sc, sh)
        dx, dm, dgate, dg, dsc, dsh = vjp((dx2v, dh2v))
        return (dx, dm), (dgate, dg, dsc, dsh)

    (dx1, dmixed), (dgate1, dgain2, dscale2, dshift2) = rowcall(
        res_norm_bwd, "norm2_bwd", [dx2, dh2, x, mixed], [gate1, P['norm2_gain'], scale2, shift2],
        [(D, F32), (D, BF16)], [(1, D)] * 4, TT)

    dycat = matmul(dmixed, w_out, name="dycat_mm", tb=True)
    g_wout = matmul(ycat, dmixed, name="gwout_mm", ta=True, out_dtypes=(BF16,))

    def post_bwd(dy, *a):
        _, vjp = jax.vjp(_rk_post_fn, *a)
        gy0, gy1, gr, gv, gk0, gk1, gg, grk, glg, glb = vjp(dy)
        return (gy0, gr, gv, gk0, gk1, gg), (grk, glg, glb)

    cb_rk = S5W // C if C else 0
    (dy_rk, dr_p, dv_p, dk0_p, dk1_p, dg_p), (g_rk_rk, g_lng, g_lnb) = rowcall(
        post_bwd, "rk_post_bwd", [(dycat, C, cb_rk)] + post_tiled, post_full, [(C, F32)] * 6, [(1, C)] * 3, TS)
    dyT_h = hmT(dy_rk)
    scan_g = []
    for d in range(N_DIR):
        wd, kd, bd = dir_rows[d]
        scan_g.append(rk_scan_bwd(r_h, kk_h, wd, kd, bd, vT_h, dyT_h, hist[d], reverse=(d == 1), name=f"rk_scan_b{d}"))
    cot = [dr_p, unhm(scan_g[0][0]), unhm(scan_g[1][0]),
           dv_p, unT(scan_g[0][5]), unT(scan_g[1][5]),
           unhm(scan_g[0][1]), unhm(scan_g[1][1]),
           unhm(scan_g[0][2]), unhm(scan_g[1][2]),
           dk0_p, unhm(scan_g[0][3]), dk1_p, unhm(scan_g[1][3]),
           unhm(scan_g[0][4]), unhm(scan_g[1][4]),
           dg_p]

    def pre_bwd(psv, r0, r1, r2, v0, v1, v2, q0, q1, dw0, dw1, k0a, k0b, k1a, k1b, db0, db1, dgv, *params):
        _, vjp = jax.vjp(_rk_pre_fn, psv, *params)
        grads = vjp((r0 + r1 + r2, v0 + v1 + v2, q0 + q1, dw0, dw1, k0a + k0b, k1a + k1b, db0, db1, dgv))
        return (grads[0],), tuple(grads[1:])

    (dps,), pre_g = rowcall(pre_bwd, "rk_pre_bwd", [ps] + cot, pre_full, [(RINP, F32)],
                            [f.shape for f in pre_full], TS)
    g_w0, g_wup0, g_wup1, g_a0, g_aup0, g_aup1, g_gup_p, g_kk, g_ka = pre_g
    g_wup_p, g_aup_p = jnp.stack([g_wup0, g_wup1]), jnp.stack([g_aup0, g_aup1])
    dp_rk, g_mup, g_mun = rk_shift_bwd(dps, proj, mu_prev, mu_next, S5W)

    def s5_out_bwd(dy, *a):
        a = [t.astype(F32) for t in a]
        _, vjp = jax.vjp(_s5_out_fn, *a)
        g = vjp(dy)
        return (g[0], g[1], g[4]), tuple(g[5:])

    (dxr, dxi, du_a), s5_pg = rowcall(
        s5_out_bwd, "s5_out_bwd", [(dycat, S5W, 0), s0r, s0i, s1r, s1i, u_view], s5_full,
        [(NS, F32), (NS, F32), (S5W, F32)], [f.shape for f in s5_full], TS)
    g_creb, g_cimb, g_s5d, g_wglu, g_bglu = s5_pg
    l0r, l0i = s5_scan(dxr, dxi, lbr[0:1], -lbi[0:1], reverse=True, name="s5_scan_b0")
    l1r, l1i = s5_scan(dxr, dxi, lbr[1:2], -lbi[1:2], reverse=False, name="s5_scan_b1")
    dl0r, dl0i = s5_dlam(s0r, s0i, l0r, l0i, reverse=False, name="s5_dlam0")
    dl1r, dl1i = s5_dlam(s1r, s1i, l1r, l1i, reverse=True, name="s5_dlam1")

    def bu_bwd(uv, g0, g1, g2, g3, wb):
        _, vjp = jax.vjp(_s5_bu_fn, uv, wb.astype(F32))
        du, dwb = vjp((g0, g1, g2, g3))
        return (du,), (dwb,)

    (du_b,), (g_wblk,) = rowcall(bu_bwd, "s5_bu_bwd", [u_view, l0r, l0i, l1r, l1i], [wblk], [(S5W, F32)],
                                 [wblk.shape], TS)
    g_bbar = jnp.einsum('cghdrkp,gk->drhcgp',
                        g_wblk.reshape(NCH, S5_CHUNK_GROUPS, S5_GROUP, N_DIR, 2, S5_CHUNK_GROUPS, S5_STATE), eye)
    g_bbar = g_bbar.reshape(N_DIR, 2, S5_GROUP, NS)
    g_lre, g_lim, g_lstep, g_bre, g_bim = s5_prep_bwd(
        lre, lim, lstep, bre, bim, jnp.concatenate([dl0r, dl1r], 0), jnp.concatenate([dl0i, dl1i], 0),
        g_bbar[:, 0], g_bbar[:, 1])

    def uncblk(gb):
        g5 = gb.reshape(NCH, S5_CHUNK_GROUPS, S5_STATE, S5_CHUNK_GROUPS, S5_GROUP)
        return jnp.einsum('cgpkh,gk->cghp', g5, eye).reshape(G, S5_GROUP, S5_STATE)

    (du_tot,), _ = rowcall(lambda a, b_: ((a + b_,), ()), "s5_du_sum", [du_a, du_b], [], [(S5W, BF16)], [], TT)
    dproj = jnp.concatenate([du_tot, dp_rk.astype(BF16)], axis=1)
    dh1 = matmul(dproj, w_in, name="dh1_mm", tb=True)
    g_win = matmul(h1, dproj, name="gwin_mm", ta=True, out_dtypes=(BF16,))

    def norm1_bwd(dx1v, dh1v, xv, g, sc, sh):
        _, vjp = jax.vjp(_normmod, xv, g, sc, sh)
        dx, dg, dsc, dsh = vjp(dh1v)
        return (dx1v + dx,), (dg, dsc, dsh)

    (grad_x,), (dgain1, dscale1, dshift1) = rowcall(
        norm1_bwd, "norm1_bwd", [dx1, dh1, x], [P['norm1_gain'], scale1, shift1], [(D, F32)], [(1, D)] * 3, TT)

    dmod = jnp.concatenate([dshift1, dscale1, dgate1, dshift2, dscale2, dgate2], axis=1)
    lstep_g = g_lstep.reshape(N_DIR, G, S5_STATE)
    small_g = {
        'ada_b': dmod, 'norm1_gain': dgain1, 'norm2_gain': dgain2, 'final_gain': dfgain.reshape(D),
        's5_lambda_re': g_lre.reshape(1, N_DIR, G, S5_STATE), 's5_lambda_im': g_lim.reshape(1, N_DIR, G, S5_STATE),
        's5_log_step': lstep_g,
        's5_b_re': g_bre.T.reshape(1, G, S5_STATE, S5_GROUP), 's5_b_im': g_bim.T.reshape(1, G, S5_STATE, S5_GROUP),
        's5_c_re': uncblk(g_creb)[None], 's5_c_im': uncblk(g_cimb)[None],
        's5_d': g_s5d, 's5_b_glu': g_bglu,
        'rk_shift_prev': g_mup[:, :RIN], 'rk_shift_next': g_mun[:, :RIN],
        'rk_k_k': g_kk, 'rk_k_a': g_ka, 'rk_r_k': g_rk_rk.reshape(1, H, RWKV_HEAD),
        'rk_ln_gain': g_lng, 'rk_ln_bias': g_lnb,
    }
    small_shapes = {n: P[n].shape for n in SMALL}
    small_shapes['s5_log_step'] = (N_DIR, G, S5_STATE)
    small_pack = _pack_flat([small_g[n] for n in SMALL])

    def lora_unpad(gp):
        return jnp.stack([gp[d, d * DECAY_LORA:(d + 1) * DECAY_LORA] for d in range(N_DIR)])

    rk_grads = {'rk_w0': g_w0, 'rk_a0': g_a0, 'rk_w_up': lora_unpad(g_wup_p), 'rk_a_up': lora_unpad(g_aup_p),
                'rk_g_up': g_gup_p[:GATE_LORA]}
    rk_gpack = jnp.stack([_pack_rows([_cols_to_slots(rk_grads[n])[j] for n in RKPACK]) for j in range(N_DEV)])
    g_win_s = _cols_to_slots(g_win[:, :PROJ])
    ex = exchange(
        [small_pack, g_win_s, g_wout.reshape(N_DEV, D // N_DEV, D), g_w1, g_w2.reshape(N_DEV, FF // N_DEV, D),
         g_wglu.reshape(N_DEV, S5W // N_DEV, S5W), rk_gpack],
        ['gather'] + ['scatter'] * 6, "comm_grads")
    small_all, win_parts, wout_parts, w1_parts, w2_parts, wglu_parts, rk_parts = ex

    res = {}

    def put(name, g, dl, m2, v2):
        shp = P[name].shape
        res[name] = tuple(t.reshape(shp) for t in (g, dl, m2, v2))

    def adam2d(name, parts):
        shp = P[name].shape
        r2 = (math.prod(shp[:-1]), shp[-1])
        put(name, *adam(parts.reshape((parts.shape[0],) + r2), P[name].reshape(r2), M[name].reshape(r2),
                        V[name].reshape(r2), "adam_" + name))

    adam2d('w_in', win_parts)
    adam2d('w_out', wout_parts)
    adam2d('ffn_w1', w1_parts)
    adam2d('ffn_w2', w2_parts)
    adam2d('s5_w_glu', wglu_parts)
    off = 0
    for n in SMALL:
        if n == 'ada_b':
            break
        off += _round_up(math.prod(small_shapes[n]), SUBLANE * LANE) // LANE
    nrow_b = N_MOD * D // LANE
    dmod_all = small_all[:, off:off + nrow_b].reshape(N_DEV, N_MOD * D)
    dmod_cols = lax.dynamic_slice(dmod_all, (0, me * cs_mod), (N_DEV, cs_mod))
    g_adaw = matmul(c_act, dmod_cols, name="gadaw_mm", ta=True, precise=True)
    adam2d('ada_w', g_adaw[None])
    small_w = dict(P)
    small_m, small_v = dict(M), dict(V)
    rk_res = adam(rk_parts, rk_pack, _pack_rows([M[n][0] for n in RKPACK]), _pack_rows([V[n][0] for n in RKPACK]),
                  "adam_rkpack")
    for name, parts4 in zip(RKPACK, zip(*[_unpack_rows(t, rk_shapes) for t in rk_res])):
        put(name, *parts4)
    return loss, grad_x, res, (small_all, small_shapes, small_w, small_m, small_v)


def _small_update(small_all, small_shapes, P, M, V, res):
    G = (D_MODEL // 2) // S5_GROUP
    names = [n for n in SMALL if n != 's5_log_step']
    shapes = [small_shapes[n] for n in SMALL]
    parts = _unpack_flat_batched(small_all, shapes)
    by = dict(zip(SMALL, parts))
    ls = by['s5_log_step']
    ls = ls.transpose(0, 3, 1, 2).reshape(N_DEV * S5_STATE, N_DIR * G)
    pk = lambda d: _pack_flat([d[n] for n in names])
    packs = jnp.stack([_pack_flat([by[n][j] for n in names]) for j in range(N_DEV)])
    out = adam(packs, pk(P), pk(M), pk(V), "adam_small")
    shp = [P[n].shape for n in names]
    for name, parts4 in zip(names, zip(*[_unpack_flat(t, shp) for t in out])):
        res[name] = parts4
    lsw = lambda d: jnp.pad(d['s5_log_step'].reshape(1, N_DIR * G), ((0, SUBLANE - 1), (0, 0)))
    ls_parts = jnp.pad(ls[:, None, :], ((0, 0), (0, SUBLANE - 1), (0, 0)))
    o = adam(ls_parts, lsw(P), lsw(M), lsw(V), "adam_log_step")
    res['s5_log_step'] = tuple(t[0:1].reshape(P['s5_log_step'].shape) for t in o)


def _unpack_flat_batched(packed, shapes):
    out, r0 = [], 0
    B = packed.shape[0]
    for s in shapes:
        n = math.prod(s)
        nr = _round_up(n, SUBLANE * LANE) // LANE
        out.append(packed[:, r0:r0 + nr].reshape(B, -1)[:, :n].reshape((B,) + tuple(s)))
        r0 += nr
    return out


def kernel(x, c, ada_w, ada_b, norm1_gain, norm2_gain, final_gain, w_in, w_out, s5_lambda_re, s5_lambda_im, s5_log_step, s5_b_re, s5_b_im, s5_c_re, s5_c_im, s5_d, s5_w_glu, s5_b_glu, rk_shift_prev, rk_shift_next, rk_w0, rk_w_up, rk_a0, rk_a_up, rk_g_up, rk_k_k, rk_k_a, rk_r_k, rk_ln_gain, rk_ln_bias, ffn_w1, ffn_w2, loss_target, m_ada_w, m_ada_b, m_norm1_gain, m_norm2_gain, m_final_gain, m_w_in, m_w_out, m_s5_lambda_re, m_s5_lambda_im, m_s5_log_step, m_s5_b_re, m_s5_b_im, m_s5_c_re, m_s5_c_im, m_s5_d, m_s5_w_glu, m_s5_b_glu, m_rk_shift_prev, m_rk_shift_next, m_rk_w0, m_rk_w_up, m_rk_a0, m_rk_a_up, m_rk_g_up, m_rk_k_k, m_rk_k_a, m_rk_r_k, m_rk_ln_gain, m_rk_ln_bias, m_ffn_w1, m_ffn_w2, v_ada_w, v_ada_b, v_norm1_gain, v_norm2_gain, v_final_gain, v_w_in, v_w_out, v_s5_lambda_re, v_s5_lambda_im, v_s5_log_step, v_s5_b_re, v_s5_b_im, v_s5_c_re, v_s5_c_im, v_s5_d, v_s5_w_glu, v_s5_b_glu, v_rk_shift_prev, v_rk_shift_next, v_rk_w0, v_rk_w_up, v_rk_a0, v_rk_a_up, v_rk_g_up, v_rk_k_k, v_rk_k_a, v_rk_r_k, v_rk_ln_gain, v_rk_ln_bias, v_ffn_w1, v_ffn_w2):
    given = dict(locals())
    P = {n: given[n] for n in ['x', 'c', 'loss_target'] + WEIGHTS}
    M = {n: given['m_' + n] for n in WEIGHTS}
    V = {n: given['v_' + n] for n in WEIGHTS}
    loss, grad_x, res, small = _step(P, M, V)
    small_all, small_shapes, _, _, _ = small
    _small_update(small_all, small_shapes, P, M, V, res)
    outs = [loss, grad_x[None]]
    for q in range(4):
        outs += [res[n][q] for n in WEIGHTS]
    return tuple(outs)
```

```python
import functools
import math

import jax
import jax.numpy as jnp
from jax import lax
from jax.experimental import pallas as pl
from jax.experimental.pallas import tpu as pltpu

F32 = jnp.float32
BF16 = jnp.bfloat16
HI = lax.Precision.HIGHEST
MXU_DTYPE = jnp.bfloat16

N_DEV = 8
MESH_AXES = ("x", "y", "c")
D_MODEL = 2048
SEQ = 2048
S5_GROUP = 16
S5_STATE = 64
RWKV_HEAD = 64
DECAY_LORA = 64
GATE_LORA = 160
N_DIR = 2
N_MOD = 6
NORM_EPS = 1e-6
GN_EPS = 64e-5
L2_EPS = 1e-12
ADAM_LR = 0.001
ADAM_B1 = 0.9
ADAM_B2 = 0.999
ADAM_EPS = 1e-08
ADAM_WD = 0.01
ADAM_STEP = 10
LANE = 128
SUBLANE = 8
S5_CHUNK_GROUPS = 8
S5_SCAN_ROWS = 256
VMEM_LIMIT = 56 * 1024 * 1024

WEIGHTS = ['ada_w', 'ada_b', 'norm1_gain', 'norm2_gain', 'final_gain', 'w_in', 'w_out', 's5_lambda_re',
           's5_lambda_im', 's5_log_step', 's5_b_re', 's5_b_im', 's5_c_re', 's5_c_im', 's5_d', 's5_w_glu',
           's5_b_glu', 'rk_shift_prev', 'rk_shift_next', 'rk_w0', 'rk_w_up', 'rk_a0', 'rk_a_up', 'rk_g_up',
           'rk_k_k', 'rk_k_a', 'rk_r_k', 'rk_ln_gain', 'rk_ln_bias', 'ffn_w1', 'ffn_w2']
SMALL = ['ada_b', 'norm1_gain', 'norm2_gain', 'final_gain', 's5_lambda_re', 's5_lambda_im', 's5_log_step',
         's5_b_re', 's5_b_im', 's5_c_re', 's5_c_im', 's5_d', 's5_b_glu', 'rk_shift_prev', 'rk_shift_next',
         'rk_k_k', 'rk_k_a', 'rk_r_k', 'rk_ln_gain', 'rk_ln_bias']
RKPACK = ['rk_w0', 'rk_a0', 'rk_w_up', 'rk_a_up', 'rk_g_up']


def _round_up(n, m):
    return (n + m - 1) // m * m


def _tile(dim, pref, unit=LANE):
    t = min(pref, dim) // unit * unit
    while t >= unit:
        if dim % t == 0:
            return t
        t -= unit
    return dim


def _cparams(sem=None):
    return pltpu.CompilerParams(dimension_semantics=sem, vmem_limit_bytes=VMEM_LIMIT)


def _full_spec(a):
    nd = a.ndim
    return pl.BlockSpec(a.shape, lambda *_: (0,) * nd)


def exchange(arrs, modes, name):
    n = len(arrs)
    out_shape = [jax.ShapeDtypeStruct((N_DEV,) + a.shape if m == 'gather' else a.shape, a.dtype)
                 for a, m in zip(arrs, modes)]

    def body(*refs):
        ins, outs = refs[:n], refs[n:2 * n]
        send_sems, recv_sems, local_sems = refs[2 * n:]
        x, y, c = (lax.axis_index(a) for a in MESH_AXES)
        me = 4 * x + 2 * y + c
        copies = []
        for i in range(n):
            gather = modes[i] == 'gather'
            mine = pltpu.make_async_copy(ins[i] if gather else ins[i].at[me], outs[i].at[me], local_sems.at[i])
            mine.start()
            copies.append(mine)
        remote = []
        for k in range(1, N_DEV):
            px = 1 - x if (k >> 2) & 1 else x
            py = 1 - y if (k >> 1) & 1 else y
            pc = 1 - c if k & 1 else c
            peer = 4 * px + 2 * py + pc
            for i in range(n):
                src = ins[i] if modes[i] == 'gather' else ins[i].at[peer]
                cp = pltpu.make_async_remote_copy(
                    src_ref=src, dst_ref=outs[i].at[me], send_sem=send_sems.at[i, k - 1],
                    recv_sem=recv_sems.at[i, k - 1], device_id=(px, py, pc), device_id_type=pl.DeviceIdType.MESH)
                cp.start()
                remote.append(cp)
        for cp in remote:
            cp.wait_recv()
        for cp in remote:
            cp.wait_send()
        for cp in copies:
            cp.wait()

    any_spec = pl.BlockSpec(memory_space=pl.ANY)
    return pl.pallas_call(
        body, name=name, out_shape=out_shape,
        in_specs=[any_spec] * n, out_specs=[any_spec] * n,
        scratch_shapes=[pltpu.SemaphoreType.DMA((n, N_DEV - 1)), pltpu.SemaphoreType.DMA((n, N_DEV - 1)),
                        pltpu.SemaphoreType.DMA((n,))],
        compiler_params=pltpu.CompilerParams(has_side_effects=True),
    )(*arrs)


def matmul(a, b, *, name, ta=False, tb=False, b_slots=False, out_slots=0, out_dtypes=(F32,), epi=None,
           extras=(), precise=False, tm=512, tn=512, tk=512):
    if ta:
        K, M = a.shape
    else:
        M, K = a.shape
    if b_slots:
        ns, br, bc = b.shape
        bshape = (br, ns * bc)
    else:
        bshape = b.shape
    N = bshape[0] if tb else bshape[1]
    assert (bshape[1] if tb else bshape[0]) == K, (a.shape, b.shape, ta, tb)
    tm, tn, tk = _tile(M, tm, SUBLANE), _tile(N, tn), _tile(K, tk, SUBLANE if K < LANE else LANE)
    if b_slots and tb:
        tk = _tile(b.shape[2], tk)
    elif b_slots:
        tn = _tile(b.shape[2], tn)
    if out_slots:
        tn = _tile(N // out_slots, tn)
    if b_slots:
        cs = b.shape[2]
        tcol = tk if tb else tn
        assert cs % tcol == 0
        per = cs // tcol
    if out_slots:
        ncs = N // out_slots
        assert ncs % tn == 0
        operc = ncs // tn
    nk = K // tk
    a_spec = pl.BlockSpec((tk, tm), lambda i, j, k: (k, i)) if ta else pl.BlockSpec((tm, tk), lambda i, j, k: (i, k))
    if b_slots:
        if tb:
            b_spec = pl.BlockSpec((None, tn, tk), lambda i, j, k: (k // per, j, k % per))
        else:
            b_spec = pl.BlockSpec((None, tk, tn), lambda i, j, k: (j // per, k, j % per))
    else:
        b_spec = pl.BlockSpec((tn, tk), lambda i, j, k: (j, k)) if tb else pl.BlockSpec((tk, tn), lambda i, j, k: (k, j))
    ex_specs = []
    for arr, kind in extras:
        if kind == 'mn':
            ex_specs.append(pl.BlockSpec((tm, tn), lambda i, j, k: (i, j)))
        else:
            ex_specs.append(pl.BlockSpec((1, tn), lambda i, j, k: (0, j)))
    if out_slots:
        o_spec = pl.BlockSpec((None, tm, tn), lambda i, j, k: (j // operc, i, j % operc))
        o_shape = (out_slots, M, ncs)
    else:
        o_spec = pl.BlockSpec((tm, tn), lambda i, j, k: (i, j))
        o_shape = (M, N)
    ne, no = len(extras), len(out_dtypes)
    dims = (((0 if ta else 1,), (1 if tb else 0,)), ((), ()))
    op_dtype = F32 if precise else MXU_DTYPE

    def body(a_ref, b_ref, *rest):
        ex_refs, out_refs, acc = rest[:ne], rest[ne:ne + no], rest[-1]
        k = pl.program_id(2)

        @pl.when(k == 0)
        def _():
            acc[...] = jnp.zeros_like(acc)

        acc[...] += lax.dot_general(a_ref[...].astype(op_dtype), b_ref[...].astype(op_dtype), dims,
                                    precision=HI if precise else None, preferred_element_type=F32)

        @pl.when(k == nk - 1)
        def _():
            res = epi(acc[...], *[e[...] for e in ex_refs]) if epi is not None else (acc[...],)
            for o, r in zip(out_refs, res):
                o[...] = r.astype(o.dtype)

    outs = pl.pallas_call(
        body, name=name, grid=(M // tm, N // tn, nk),
        in_specs=[a_spec, b_spec] + ex_specs, out_specs=[o_spec] * no,
        out_shape=[jax.ShapeDtypeStruct(o_shape, dt) for dt in out_dtypes],
        scratch_shapes=[pltpu.VMEM((tm, tn), F32)],
        compiler_params=_cparams(("parallel", "parallel", "arbitrary")),
    )(a, b, *[e[0] for e in extras])
    return outs[0] if no == 1 else outs


def rowcall(fn, name, tiled, full, tiled_out, acc_out, tt):
    views = [(t, t.shape[1], 0) if not isinstance(t, tuple) else t for t in tiled]
    T = views[0][0].shape[0]
    tt = _tile(T, tt, SUBLANE)
    nt, nf, nto, nao = len(views), len(full), len(tiled_out), len(acc_out)

    def view_spec(w, cb):
        return pl.BlockSpec((tt, w), lambda i: (i, cb))

    in_specs = [view_spec(w, cb) for _, w, cb in views] + [_full_spec(f) for f in full]
    out_specs = [pl.BlockSpec((tt, w), lambda i: (i, 0)) for w, _ in tiled_out]
    out_specs += [pl.BlockSpec(s, lambda i, nd=len(s): (0,) * nd) for s in acc_out]
    out_shape = [jax.ShapeDtypeStruct((T, w), dt) for w, dt in tiled_out]
    out_shape += [jax.ShapeDtypeStruct(s, F32) for s in acc_out]

    def body(*refs):
        tin, fin = refs[:nt], refs[nt:nt + nf]
        tout, aout = refs[nt + nf:nt + nf + nto], refs[nt + nf + nto:]
        touts, aouts = fn(*[r[...] for r in tin], *[r[...] for r in fin])
        for r, v in zip(tout, touts):
            r[...] = v.astype(r.dtype)
        if nao:
            @pl.when(pl.program_id(0) == 0)
            def _():
                for r in aout:
                    r[...] = jnp.zeros_like(r)

            for r, v in zip(aout, aouts):
                r[...] += v.astype(F32)

    outs = pl.pallas_call(
        body, name=name, grid=(T // tt,), in_specs=in_specs, out_specs=out_specs, out_shape=out_shape,
        compiler_params=_cparams(("arbitrary",) if nao else ("parallel",)),
    )(*[v[0] for v in views], *full)
    return outs[:nto], outs[nto:]


def _mm(a, b):
    return jnp.dot(a.astype(MXU_DTYPE), b.astype(MXU_DTYPE), preferred_element_type=F32)


def _rms(x, gain):
    ms = jnp.mean(x * x, axis=-1, keepdims=True)
    return x * lax.rsqrt(ms + NORM_EPS) * gain


def _normmod(x, gain, scale, shift):
    return _rms(x, gain) * (1.0 + scale) + shift


def _gelu_tanh(y):
    return 0.5 * y * (1.0 + jnp.tanh(math.sqrt(2.0 / math.pi) * (y + 0.044715 * (y * y * y))))


def _sigmoid(x):
    return 1.0 / (1.0 + jnp.exp(-x))


def _softplus(x):
    return jnp.maximum(x, 0.0) + jnp.log(1.0 + jnp.exp(-jnp.abs(x)))


def _seg_mats(width, seg):
    r = lax.broadcasted_iota(jnp.int32, (width, LANE), 0) // seg
    c = lax.broadcasted_iota(jnp.int32, (width, LANE), 1)
    s = (r == c).astype(F32)
    rt = lax.broadcasted_iota(jnp.int32, (LANE, width), 0)
    ct = lax.broadcasted_iota(jnp.int32, (LANE, width), 1) // seg
    st = (rt == ct).astype(F32)
    return s, st


def _segsum(x, s):
    return jnp.dot(x, s, precision=HI, preferred_element_type=F32)


def _s5_prep_fn(lre, lim, lstep, bre, bim):
    step = jnp.exp(lstep)
    mag = jnp.exp(lre * step)
    lbr = mag * jnp.cos(lim * step)
    lbi = mag * jnp.sin(lim * step)
    den = lre * lre + lim * lim
    nr = lbr - 1.0
    ni = lbi
    cre = (nr * lre + ni * lim) / den
    cim = (ni * lre - nr * lim) / den
    bbr = jnp.stack([cre[d:d + 1] * bre - cim[d:d + 1] * bim for d in range(N_DIR)])
    bbi = jnp.stack([cre[d:d + 1] * bim + cim[d:d + 1] * bre for d in range(N_DIR)])
    return lbr, lbi, bbr, bbi


def s5_prep(lre, lim, lstep, bre, bim):
    ns = lre.shape[1]

    def body(lre_r, lim_r, ls_r, bre_r, bim_r, lbr_r, lbi_r, bbr_r, bbi_r):
        lbr, lbi, bbr, bbi = _s5_prep_fn(lre_r[...], lim_r[...], ls_r[...], bre_r[...], bim_r[...])
        lbr_r[...] = lbr
        lbi_r[...] = lbi
        bbr_r[...] = bbr
        bbi_r[...] = bbi

    return pl.pallas_call(
        body, name="s5_prep",
        out_shape=[jax.ShapeDtypeStruct((N_DIR, ns), F32)] * 2 + [jax.ShapeDtypeStruct((N_DIR, S5_GROUP, ns), F32)] * 2,
        compiler_params=_cparams(),
    )(lre, lim, lstep, bre, bim)


def s5_prep_bwd(lre, lim, lstep, bre, bim, dlbr, dlbi, dbbr, dbbi):
    ns = lre.shape[1]

    def body(lre_r, lim_r, ls_r, bre_r, bim_r, d1, d2, d3, d4, o1, o2, o3, o4, o5):
        _, vjp = jax.vjp(_s5_prep_fn, lre_r[...], lim_r[...], ls_r[...], bre_r[...], bim_r[...])
        g = vjp((d1[...], d2[...], d3[...], d4[...]))
        for o, v in zip((o1, o2, o3, o4, o5), g):
            o[...] = v

    return pl.pallas_call(
        body, name="s5_prep_bwd",
        out_shape=[jax.ShapeDtypeStruct((N_DIR, ns), F32)] * 3 + [jax.ShapeDtypeStruct((S5_GROUP, ns), F32)] * 2,
        compiler_params=_cparams(),
    )(lre, lim, lstep, bre, bim, dlbr, dlbi, dbbr, dbbi)


def s5_scan(bre, bim, lre, lim, *, reverse, name):
    T, NS = bre.shape
    tt = _tile(T, S5_SCAN_ROWS, SUBLANE)
    wl = _tile(NS, 512)
    nT = T // tt
    ngrp = tt // SUBLANE

    def tmap(j, i):
        return ((nT - 1 - i) if reverse else i, j)

    def body(bre_r, bim_r, lre_r, lim_r, sre_r, sim_r, cre, cim):
        @pl.when(pl.program_id(1) == 0)
        def _():
            cre[...] = jnp.zeros_like(cre)
            cim[...] = jnp.zeros_like(cim)

        lr = jnp.broadcast_to(lre_r[...], (SUBLANE, wl))
        li = jnp.broadcast_to(lim_r[...], (SUBLANE, wl))
        row = lax.broadcasted_iota(jnp.int32, (SUBLANE, wl), 0)
        pows = [(lr, li)]
        for _ in range(3):
            pr, pi = pows[-1]
            pows.append((pr * pr - pi * pi, 2.0 * pr * pi))
        e = (SUBLANE - row) if reverse else (row + 1)
        Pr = jnp.ones((SUBLANE, wl), F32)
        Pi = jnp.zeros((SUBLANE, wl), F32)
        for bit, (qr, qi) in enumerate(pows):
            on = ((e >> bit) & 1) == 1
            nr, ni = Pr * qr - Pi * qi, Pr * qi + Pi * qr
            Pr, Pi = jnp.where(on, nr, Pr), jnp.where(on, ni, Pi)

        def group(g, carry):
            gg = (ngrp - 1 - g) if reverse else g
            rows = pl.ds(pl.multiple_of(gg * SUBLANE, SUBLANE), SUBLANE)
            sr, si = bre_r[rows, :], bim_r[rows, :]
            for lvl, k in enumerate((1, 2, 4)):
                qr, qi = pows[lvl]
                if reverse:
                    shr = pltpu.roll(sr, SUBLANE - k, 0)
                    shi = pltpu.roll(si, SUBLANE - k, 0)
                    keep = row < SUBLANE - k
                else:
                    shr = pltpu.roll(sr, k, 0)
                    shi = pltpu.roll(si, k, 0)
                    keep = row >= k
                shr = jnp.where(keep, shr, 0.0)
                shi = jnp.where(keep, shi, 0.0)
                sr, si = sr + qr * shr - qi * shi, si + qr * shi + qi * shr
            cr, ci = cre[...], cim[...]
            sr, si = sr + Pr * cr - Pi * ci, si + Pr * ci + Pi * cr
            sre_r[rows, :] = sr
            sim_r[rows, :] = si
            last = 0 if reverse else SUBLANE - 1
            cre[...] = jnp.broadcast_to(sr[last:last + 1, :], (SUBLANE, wl))
            cim[...] = jnp.broadcast_to(si[last:last + 1, :], (SUBLANE, wl))
            return carry

        lax.fori_loop(0, ngrp, group, 0)

    blk = pl.BlockSpec((tt, wl), tmap)
    row_spec = pl.BlockSpec((1, wl), lambda j, i: (0, j))
    return pl.pallas_call(
        body, name=name, grid=(NS // wl, nT),
        in_specs=[blk, blk, row_spec, row_spec], out_specs=[blk, blk],
        out_shape=[jax.ShapeDtypeStruct((T, NS), F32)] * 2,
        scratch_shapes=[pltpu.VMEM((SUBLANE, wl), F32)] * 2,
        compiler_params=_cparams(("parallel", "arbitrary")),
    )(bre, bim, lre, lim)


def s5_dlam(sre, sim, gre, gim, *, reverse, name):
    T, NS = sre.shape
    wl = _tile(NS, 256)

    def body(sr_r, si_r, gr_r, gi_r, dr_r, di_r):
        row = lax.broadcasted_iota(jnp.int32, (T, wl), 0)
        if reverse:
            keep = row < T - 1
            pr = jnp.where(keep, pltpu.roll(sr_r[...], T - 1, 0), 0.0)
            pi = jnp.where(keep, pltpu.roll(si_r[...], T - 1, 0), 0.0)
        else:
            keep = row >= 1
            pr = jnp.where(keep, pltpu.roll(sr_r[...], 1, 0), 0.0)
            pi = jnp.where(keep, pltpu.roll(si_r[...], 1, 0), 0.0)
        gr, gi = gr_r[...], gi_r[...]
        dr_r[...] = jnp.sum(pr * gr + pi * gi, axis=0, keepdims=True)
        di_r[...] = jnp.sum(pr * gi - pi * gr, axis=0, keepdims=True)

    blk = pl.BlockSpec((T, wl), lambda j: (0, j))
    o = pl.BlockSpec((1, wl), lambda j: (0, j))
    return pl.pallas_call(
        body, name=name, grid=(NS // wl,), in_specs=[blk] * 4, out_specs=[o, o],
        out_shape=[jax.ShapeDtypeStruct((1, NS), F32)] * 2,
        compiler_params=_cparams(("parallel",)),
    )(sre, sim, gre, gim)


def _s5_bu_fn(u, wblk):
    nch = wblk.shape[0]
    cw = S5_CHUNK_GROUPS * S5_GROUP
    sw = S5_CHUNK_GROUPS * S5_STATE
    parts = [[] for _ in range(4)]
    for ch in range(nch):
        res = _mm(u[:, ch * cw:(ch + 1) * cw], wblk[ch])
        for q in range(4):
            parts[q].append(res[:, q * sw:(q + 1) * sw])
    return tuple(jnp.concatenate(p, axis=1) if nch > 1 else p[0] for p in parts)


def _s5_out_fn(x0r, x0i, x1r, x1i, u, cre, cim, dsk, wglu, bglu):
    xr, xi = x0r + x1r, x0i + x1i
    nch = cre.shape[0]
    sw = S5_CHUNK_GROUPS * S5_STATE
    ys = [_mm(xr[:, ch * sw:(ch + 1) * sw], cre[ch]) - _mm(xi[:, ch * sw:(ch + 1) * sw], cim[ch]) for ch in range(nch)]
    y = jnp.concatenate(ys, axis=1) if nch > 1 else ys[0]
    z = _gelu_tanh(y + dsk * u)
    gate = _sigmoid(_mm(z, wglu) + bglu)
    return z * gate


def _rk_dims():
    C = D_MODEL // 2
    LW = N_DIR * DECAY_LORA
    GP = _round_up(GATE_LORA, LANE)
    return C, LW, GP


def _rk_pre_fn(ps, w0, wup0, wup1, a0, aup0, aup1, gup, k_k, k_a):
    C, LW, GP = _rk_dims()
    r, k, v = ps[:, 0:C], ps[:, C:2 * C], ps[:, 2 * C:3 * C]
    wdn = ps[:, 3 * C:3 * C + LW]
    adn = ps[:, 3 * C + LW:3 * C + 2 * LW]
    gdn = ps[:, 3 * C + 2 * LW:3 * C + 2 * LW + GP]
    s, st = _seg_mats(C, RWKV_HEAD)
    kk = k * k_k
    n2 = _segsum(kk * kk, s)
    n2 = jnp.where(n2 > 0.0, n2, 1.0)
    inv = 1.0 / jnp.maximum(jnp.sqrt(n2), L2_EPS)
    kkn = kk * _segsum(inv, st)
    tw = jnp.tanh(wdn)
    wup, aup = (wup0, wup1), (aup0, aup1)
    ws, ks, bs = [], [], []
    for d in range(N_DIR):
        wraw = w0[d:d + 1] + _mm(tw, wup[d])
        w = -_softplus(-wraw) - 0.5
        ws.append(jnp.exp(-jnp.exp(w)))
        a = _sigmoid(a0[d:d + 1] + _mm(adn, aup[d]))
        ks.append(k * (1.0 + (a - 1.0) * k_a))
        bs.append(kkn * a)
    g = _mm(_sigmoid(gdn), gup)
    return r, v, kkn, ws[0], ws[1], ks[0], ks[1], bs[0], bs[1], g


def _rk_post_fn(y0, y1, r, v, k0, k1, g, r_k, lng, lnb):
    C = r.shape[1]
    s, st = _seg_mats(C, RWKV_HEAD)
    y = y0 + y1
    mu = _segsum(_segsum(y, s) * (1.0 / RWKV_HEAD), st)
    yc = y - mu
    var = _segsum(_segsum(yc * yc, s) * (1.0 / RWKV_HEAD), st)
    yn = yc * lax.rsqrt(var + GN_EPS) * lng + lnb
    bonus = _segsum(_segsum(r * (k0 + k1) * r_k, s), st)
    return (yn + bonus * v) * g


def rk_shift(proj, mp, mn, col0):
    T = proj.shape[0]
    W = mp.shape[1]
    wl = _tile(math.gcd(W, col0), 256)
    cb0 = col0 // wl

    def body(p_r, mp_r, mn_r, o_r):
        p = p_r[...]
        row = lax.broadcasted_iota(jnp.int32, (T, wl), 0)
        prev = jnp.where(row >= 1, pltpu.roll(p, 1, 0), 0.0)
        nxt = jnp.where(row < T - 1, pltpu.roll(p, T - 1, 0), 0.0)
        o_r[...] = p + mp_r[...] * (prev - p) + mn_r[...] * (nxt - p)

    rs = pl.BlockSpec((1, wl), lambda j: (0, j))
    return pl.pallas_call(
        body, name="rk_shift", grid=(W // wl,),
        in_specs=[pl.BlockSpec((T, wl), lambda j: (0, cb0 + j)), rs, rs],
        out_specs=pl.BlockSpec((T, wl), lambda j: (0, j)),
        out_shape=jax.ShapeDtypeStruct((T, W), F32),
        compiler_params=_cparams(("parallel",)),
    )(proj, mp, mn)


def rk_shift_bwd(dps, proj, mp, mn, col0):
    T, W = dps.shape
    wl = _tile(math.gcd(W, col0), 256)
    cb0 = col0 // wl

    def body(d_r, p_r, mp_r, mn_r, dp_r, dmp_r, dmn_r):
        d, p = d_r[...], p_r[...]
        mpv, mnv = mp_r[...], mn_r[...]
        row = lax.broadcasted_iota(jnp.int32, (T, wl), 0)
        first, last = row >= 1, row < T - 1
        prev = jnp.where(first, pltpu.roll(p, 1, 0), 0.0)
        nxt = jnp.where(last, pltpu.roll(p, T - 1, 0), 0.0)
        dmp_r[...] = jnp.sum(d * (prev - p), axis=0, keepdims=True)
        dmn_r[...] = jnp.sum(d * (nxt - p), axis=0, keepdims=True)
        dp_r[...] = (d * (1.0 - mpv - mnv) + jnp.where(last, pltpu.roll(d * mpv, T - 1, 0), 0.0)
                     + jnp.where(first, pltpu.roll(d * mnv, 1, 0), 0.0))

    rs = pl.BlockSpec((1, wl), lambda j: (0, j))
    blk = pl.BlockSpec((T, wl), lambda j: (0, j))
    return pl.pallas_call(
        body, name="rk_shift_bwd", grid=(W // wl,),
        in_specs=[blk, pl.BlockSpec((T, wl), lambda j: (0, cb0 + j)), rs, rs],
        out_specs=[blk, rs, rs],
        out_shape=[jax.ShapeDtypeStruct((T, W), F32), jax.ShapeDtypeStruct((1, W), F32), jax.ShapeDtypeStruct((1, W), F32)],
        compiler_params=_cparams(("parallel",)),
    )(dps, proj, mp, mn)


RK_HEADS_PER_STEP = 8
RK_TIME_BLOCK = 32
RK_LANE_BLOCK = 128


def _rk_blocks(H, T, N, order_reversed):
    hh = min(RK_HEADS_PER_STEP, H)
    tb = min(RK_TIME_BLOCK, T)
    lb = min(RK_LANE_BLOCK, T)
    nb, per = T // tb, lb // tb

    def tix(i):
        return (nb - 1 - i) if order_reversed else i

    rows = pl.BlockSpec((hh, tb, N), lambda g, i: (g, tix(i), 0))
    cols = pl.BlockSpec((hh, N, lb), lambda g, i: (g, 0, tix(i) // per))
    hist = pl.BlockSpec((tb, hh, N, N), lambda g, i: (tix(i), g, 0, 0))
    return hh, tb, lb, nb, per, tix, rows, cols, hist


def rk_scan(r, kk, w, k, b, vT, *, reverse, name):
    H, T, N = r.shape
    hh, tb, lb, nb, per, tix, rows, cols, hist_spec = _rk_blocks(H, T, N, reverse)

    def body(r_r, kk_r, w_r, k_r, b_r, vT_r, yT_r, hist_r, S):
        i = pl.program_id(1)

        @pl.when(i == 0)
        def _():
            S[...] = jnp.zeros_like(S)

        @pl.when(i % per == 0)
        def _():
            yT_r[...] = jnp.zeros_like(yT_r)

        off = (tix(i) % per) * tb
        lane = lax.broadcasted_iota(jnp.int32, (N, lb), 1)

        def step(s, carry):
            t = (tb - 1 - s) if reverse else s
            row = pl.ds(t, 1)
            hit = lane == off + t
            for h in range(hh):
                sp = S[h]
                hist_r[t, h] = sp
                sa = -jnp.sum(sp * kk_r[h, row, :], axis=1, keepdims=True)
                vcol = jnp.sum(jnp.where(hit, vT_r[h], 0.0), axis=1, keepdims=True)
                sn = sp * w_r[h, row, :] + sa * b_r[h, row, :] + vcol * k_r[h, row, :]
                S[h] = sn
                ycol = jnp.sum(sn * r_r[h, row, :], axis=1, keepdims=True)
                yT_r[h] = jnp.where(hit, ycol, yT_r[h])
            return carry

        lax.fori_loop(0, tb, step, 0)

    return pl.pallas_call(
        body, name=name, grid=(H // hh, nb),
        in_specs=[rows] * 5 + [cols], out_specs=[cols, hist_spec],
        out_shape=[jax.ShapeDtypeStruct((H, N, T), F32), jax.ShapeDtypeStruct((T, H, N, N), F32)],
        scratch_shapes=[pltpu.VMEM((hh, N, N), F32)],
        compiler_params=_cparams(("parallel", "arbitrary")),
    )(r, kk, w, k, b, vT)


def rk_scan_bwd(r, kk, w, k, b, vT, dyT, hist, *, reverse, name):
    H, T, N = r.shape
    hh, tb, lb, nb, per, tix, rows, cols, hist_spec = _rk_blocks(H, T, N, not reverse)

    def body(r_r, kk_r, w_r, k_r, b_r, vT_r, dyT_r, hist_r, dr_r, dkk_r, dw_r, dk_r, db_r, dvT_r, G):
        i = pl.program_id(1)

        @pl.when(i == 0)
        def _():
            G[...] = jnp.zeros_like(G)

        @pl.when(i % per == 0)
        def _():
            dvT_r[...] = jnp.zeros_like(dvT_r)

        off = (tix(i) % per) * tb
        lane = lax.broadcasted_iota(jnp.int32, (N, lb), 1)

        def step(s, carry):
            t = s if reverse else (tb - 1 - s)
            row = pl.ds(t, 1)
            hit = lane == off + t
            for h in range(hh):
                sp = hist_r[t, h]
                kkv, wv, kv, bv, rv = kk_r[h, row, :], w_r[h, row, :], k_r[h, row, :], b_r[h, row, :], r_r[h, row, :]
                vcol = jnp.sum(jnp.where(hit, vT_r[h], 0.0), axis=1, keepdims=True)
                dycol = jnp.sum(jnp.where(hit, dyT_r[h], 0.0), axis=1, keepdims=True)
                sa = -jnp.sum(sp * kkv, axis=1, keepdims=True)
                sn = sp * wv + sa * bv + vcol * kv
                g = G[h] + dycol * rv
                dr_r[h, row, :] = jnp.sum(sn * dycol, axis=0, keepdims=True)
                dw_r[h, row, :] = jnp.sum(g * sp, axis=0, keepdims=True)
                db_r[h, row, :] = jnp.sum(g * sa, axis=0, keepdims=True)
                dk_r[h, row, :] = jnp.sum(g * vcol, axis=0, keepdims=True)
                dsa = jnp.sum(g * bv, axis=1, keepdims=True)
                dvcol = jnp.sum(g * kv, axis=1, keepdims=True)
                dkk_r[h, row, :] = -jnp.sum(sp * dsa, axis=0, keepdims=True)
                G[h] = g * wv - dsa * kkv
                dvT_r[h] = jnp.where(hit, dvcol, dvT_r[h])
            return carry

        lax.fori_loop(0, tb, step, 0)

    return pl.pallas_call(
        body, name=name, grid=(H // hh, nb),
        in_specs=[rows] * 5 + [cols, cols, hist_spec], out_specs=[rows] * 5 + [cols],
        out_shape=[jax.ShapeDtypeStruct((H, T, N), F32)] * 5 + [jax.ShapeDtypeStruct((H, N, T), F32)],
        scratch_shapes=[pltpu.VMEM((hh, N, N), F32)],
        compiler_params=_cparams(("parallel", "arbitrary")),
    )(r, kk, w, k, b, vT, dyT, hist)


def adam(parts, w, m, v, name):
    P, R, C = parts.shape
    tr = _tile(R, max(SUBLANE, (1 << 19) // max(C, 1) // SUBLANE * SUBLANE), SUBLANE)
    c1 = 1.0 / (1.0 - ADAM_B1 ** ADAM_STEP)
    c2 = 1.0 / (1.0 - ADAM_B2 ** ADAM_STEP)

    def body(p_r, w_r, m_r, v_r, g_o, d_o, m_o, v_o):
        g = p_r[0].astype(F32)
        for q in range(1, P):
            g = g + p_r[q].astype(F32)
        m2 = ADAM_B1 * m_r[...] + (1.0 - ADAM_B1) * g
        v2 = ADAM_B2 * v_r[...] + (1.0 - ADAM_B2) * (g * g)
        g_o[...] = g
        m_o[...] = m2
        v_o[...] = v2
        d_o[...] = -ADAM_LR * ((m2 * c1) / (jnp.sqrt(v2 * c2) + ADAM_EPS) + ADAM_WD * w_r[...])

    blk = pl.BlockSpec((tr, C), lambda i: (i, 0))
    return pl.pallas_call(
        body, name=name, grid=(R // tr,),
        in_specs=[pl.BlockSpec((P, tr, C), lambda i: (0, i, 0)), blk, blk, blk], out_specs=[blk] * 4,
        out_shape=[jax.ShapeDtypeStruct((R, C), F32)] * 4,
        compiler_params=_cparams(("parallel",)),
    )(parts, w, m, v)


def _pack_flat(arrs):
    rows = []
    for a in arrs:
        f = a.reshape(-1).astype(F32)
        n = _round_up(f.shape[0], SUBLANE * LANE)
        rows.append(jnp.pad(f, (0, n - f.shape[0])).reshape(-1, LANE))
    return jnp.concatenate(rows, axis=0)


def _unpack_flat(packed, shapes):
    out, r0 = [], 0
    for s in shapes:
        n = math.prod(s)
        nr = _round_up(n, SUBLANE * LANE) // LANE
        out.append(packed[r0:r0 + nr].reshape(-1)[:n].reshape(s))
        r0 += nr
    return out


def _pack_rows(arrs):
    rows = []
    for a in arrs:
        f = a.reshape(-1, a.shape[-1]).astype(F32)
        n = _round_up(f.shape[0], SUBLANE)
        rows.append(jnp.pad(f, ((0, n - f.shape[0]), (0, 0))))
    return jnp.concatenate(rows, axis=0)


def _unpack_rows(packed, shapes):
    out, r0 = [], 0
    for s in shapes:
        nr = math.prod(s[:-1])
        out.append(packed[r0:r0 + nr].reshape(s))
        r0 += _round_up(nr, SUBLANE)
    return out


def _cols_from_slots(g):
    return jnp.moveaxis(g, 0, -2).reshape(g.shape[1:-1] + (N_DEV * g.shape[-1],))


def _cols_to_slots(a):
    cs = a.shape[-1] // N_DEV
    return jnp.moveaxis(a.reshape(a.shape[:-1] + (N_DEV, cs)), -2, 0)


def _step(P, M, V):
    D, T = D_MODEL, SEQ
    S5W = D // 2
    C, LW, GP = _rk_dims()
    H = C // RWKV_HEAD
    G = S5W // S5_GROUP
    NS = G * S5_STATE
    NCH = G // S5_CHUNK_GROUPS
    SW = S5_CHUNK_GROUPS * S5_STATE
    RIN = 3 * C + 2 * LW + GATE_LORA
    RINP = 3 * C + 2 * LW + GP
    PROJ = S5W + RIN
    PROJP = S5W + RINP
    FF = 4 * D
    me = 4 * lax.axis_index("x") + 2 * lax.axis_index("y") + lax.axis_index("c")
    cs_mod = N_MOD * D // N_DEV
    eye = jnp.eye(S5_CHUNK_GROUPS, dtype=F32)

    x = P['x'][0]
    target = P['loss_target'][0]

    (c_all,) = exchange([P['c']], ['gather'], "comm_gather_c")
    c_all = c_all.reshape(N_DEV, D)
    (c_act,), _ = rowcall(lambda cv: ((cv * _sigmoid(cv),), ()), "silu_c", [c_all], [], [(D, F32)], [], N_DEV)
    ada_b_loc = lax.dynamic_slice(P['ada_b'], (0, me * cs_mod), (1, cs_mod))
    mod_loc = matmul(c_act, P['ada_w'][0], name="mod_mm", precise=True, extras=[(ada_b_loc, 'n')],
                     epi=lambda acc, bias: (acc + bias,))

    rk_shapes = [P[n][0].shape for n in RKPACK]
    rk_pack = _pack_rows([P[n][0] for n in RKPACK])
    gathered = exchange(
        [mod_loc, P['w_in'][0].astype(BF16), P['w_out'][0].astype(BF16), P['ffn_w1'][0].astype(BF16),
         P['ffn_w2'][0].astype(BF16), P['s5_w_glu'][0].astype(BF16), rk_pack],
        ['gather'] * 7, "comm_gather_weights")
    mod_all, w_in_g, w_out_g, w1_g, w2_g, wglu_g, rk_g = gathered
    mod_me = lax.dynamic_index_in_dim(mod_all, me, axis=1, keepdims=False).reshape(N_MOD, 1, D)
    shift1, scale1, gate1, shift2, scale2, gate2 = (mod_me[i] for i in range(N_MOD))
    w_in = jnp.pad(_cols_from_slots(w_in_g), ((0, 0), (0, PROJP - PROJ)))
    w_out = w_out_g.reshape(D, D)
    w2 = w2_g.reshape(FF, D)
    wglu = wglu_g.reshape(S5W, S5W)
    rk_full = _unpack_rows(_cols_from_slots(rk_g), [s[:-1] + (C,) for s in rk_shapes])
    rk_w0, rk_a0, rk_wup, rk_aup, rk_gup = rk_full

    def lora_pad(up):
        z = jnp.zeros((N_DIR, LW, C), F32)
        for d in range(N_DIR):
            z = z.at[d, d * DECAY_LORA:(d + 1) * DECAY_LORA].set(up[d])
        return z

    wup_p, aup_p = lora_pad(rk_wup), lora_pad(rk_aup)
    gup_p = jnp.pad(rk_gup, ((0, GP - GATE_LORA), (0, 0)))
    mu_prev = jnp.pad(P['rk_shift_prev'], ((0, 0), (0, RINP - RIN)))
    mu_next = jnp.pad(P['rk_shift_next'], ((0, 0), (0, RINP - RIN)))
    r_k = P['rk_r_k'].reshape(1, C)
    fgain = P['final_gain'].reshape(1, D)

    TT = 256
    (h1,), _ = rowcall(lambda xv, g, sc, sh: ((_normmod(xv, g, sc, sh),), ()), "norm1",
                       [x], [P['norm1_gain'], scale1, shift1], [(D, BF16)], [], TT)
    proj = matmul(h1, w_in, name="proj_mm")

    lre = P['s5_lambda_re'][0].reshape(N_DIR, NS)
    lim = P['s5_lambda_im'][0].reshape(N_DIR, NS)
    lstep = jnp.broadcast_to(P['s5_log_step'][0][:, :, None], (N_DIR, G, S5_STATE)).reshape(N_DIR, NS)
    bre = P['s5_b_re'][0].reshape(NS, S5_GROUP).T
    bim = P['s5_b_im'][0].reshape(NS, S5_GROUP).T
    lbr, lbi, bbr, bbi = s5_prep(lre, lim, lstep, bre, bim)
    bbar = jnp.stack([bbr, bbi], axis=1).reshape(N_DIR, 2, S5_GROUP, NCH, S5_CHUNK_GROUPS, S5_STATE)
    wblk = jnp.einsum('drhcgp,gk->cghdrkp', bbar, eye).reshape(NCH, S5_CHUNK_GROUPS * S5_GROUP, 4 * SW)
    wblk = wblk.astype(MXU_DTYPE)
    u_view = (proj, S5W, 0)
    bus, _ = rowcall(lambda uv, wb: (_s5_bu_fn(uv, wb), ()), "s5_bu", [u_view], [wblk], [(NS, F32)] * 4, [], TT)
    s0r, s0i = s5_scan(bus[0], bus[1], lbr[0:1], lbi[0:1], reverse=False, name="s5_scan_f0")
    s1r, s1i = s5_scan(bus[2], bus[3], lbr[1:2], lbi[1:2], reverse=True, name="s5_scan_f1")

    def cblk(cm):
        c4 = cm.reshape(NCH, S5_CHUNK_GROUPS, S5_GROUP, S5_STATE)
        return jnp.einsum('cghp,gk->cgpkh', c4, eye).reshape(NCH, SW, S5_CHUNK_GROUPS * S5_GROUP)

    cre_b = cblk(P['s5_c_re'][0]).astype(MXU_DTYPE)
    cim_b = cblk(P['s5_c_im'][0]).astype(MXU_DTYPE)
    s5_full = [cre_b, cim_b, P['s5_d'], wglu, P['s5_b_glu']]
    TS = 128
    (y_s5,), _ = rowcall(lambda *a: ((_s5_out_fn(*a),), ()), "s5_out", [s0r, s0i, s1r, s1i, u_view], s5_full,
                         [(S5W, BF16)], [], TS)

    ps = rk_shift(proj, mu_prev, mu_next, S5W)
    pre_full = [rk_w0, wup_p[0], wup_p[1], rk_a0, aup_p[0], aup_p[1], gup_p, P['rk_k_k'], P['rk_k_a']]
    pre_out, _ = rowcall(lambda *a: (_rk_pre_fn(*a)[2:], ()), "rk_pre", [ps], pre_full, [(C, F32)] * 8, [], TS)
    kkn, w_0, w_1, k_0, k_1, b_0, b_1, g_gate = pre_out
    r_t, v_t = ps[:, 0:C], ps[:, 2 * C:3 * C]

    def hm(a):
        return a.reshape(T, H, RWKV_HEAD).transpose(1, 0, 2)

    def hmT(a):
        return a.reshape(T, H, RWKV_HEAD).transpose(1, 2, 0)

    def unT(a):
        return a.transpose(2, 0, 1).reshape(T, C)

    def unhm(a):
        return a.transpose(1, 0, 2).reshape(T, C)

    r_h, kk_h, vT_h = hm(r_t), hm(kkn), hmT(v_t)
    dir_rows = [(hm(w_0), hm(k_0), hm(b_0)), (hm(w_1), hm(k_1), hm(b_1))]
    yT, hist = [], []
    for d in range(N_DIR):
        wd, kd, bd = dir_rows[d]
        yd, hd = rk_scan(r_h, kk_h, wd, kd, bd, vT_h, reverse=(d == 1), name=f"rk_scan_f{d}")
        yT.append(yd)
        hist.append(hd)
    y_0, y_1 = unT(yT[0]), unT(yT[1])
    post_full = [r_k, P['rk_ln_gain'], P['rk_ln_bias']]
    post_tiled = [y_0, y_1, (ps, C, 0), (ps, C, 2), k_0, k_1, g_gate]
    (y_rk,), _ = rowcall(lambda *a: ((_rk_post_fn(*a),), ()), "rk_post", post_tiled, post_full, [(C, BF16)], [], TS)

    ycat = jnp.concatenate([y_s5, y_rk], axis=1)
    mixed = matmul(ycat, w_out, name="out_mm")

    def res_norm(xv, mv, gate, g, sc, sh):
        x1v = xv + gate * mv
        return x1v, _normmod(x1v, g, sc, sh)

    (x1, h2), _ = rowcall(lambda *a: (res_norm(*a), ()), "norm2", [x, mixed], [gate1, P['norm2_gain'], scale2, shift2],
                          [(D, F32), (D, BF16)], [], TT)
    a_ff, hh_ff = matmul(h2, w1_g, name="ffn1_mm", b_slots=True, out_dtypes=(F32, BF16),
                         epi=lambda acc: (acc, jnp.square(jnp.maximum(acc, 0.0))))
    ffn = matmul(hh_ff, w2, name="ffn2_mm")

    def loss_fn(x1v, fv, tg, gate, fg):
        def f(x1_, f_, gate_, fg_):
            out = _rms(x1_ + gate_ * f_, fg_)
            err = out - tg
            return 0.5 * jnp.sum(jnp.sum(err * err, axis=1, keepdims=True), axis=0, keepdims=True) * (1.0 / D)
        lv, vjp = jax.vjp(f, x1v, fv, gate, fg)
        dx1, dff, dgate, dfg = vjp(jnp.ones((1, 1), F32))
        return (dx1, dff), (jnp.broadcast_to(lv, (SUBLANE, LANE)), dgate, dfg)

    (dx2, dffn), (loss_t, dgate2, dfgain) = rowcall(
        loss_fn, "loss", [x1, ffn, target], [gate2, fgain], [(D, F32), (D, BF16)], [(SUBLANE, LANE), (1, D), (1, D)], TT)
    loss = lax.psum(loss_t[0, 0], MESH_AXES)

    da = matmul(dffn, w2, name="dffn2_mm", tb=True, out_dtypes=(BF16,), extras=[(a_ff, 'mn')],
                epi=lambda acc, av: (acc * (2.0 * jnp.maximum(av, 0.0)),))
    g_w2 = matmul(hh_ff, dffn, name="gw2_mm", ta=True, out_dtypes=(BF16,))
    dh2 = matmul(da, w1_g, name="dh2_mm", tb=True, b_slots=True)
    g_w1 = matmul(h2, da, name="gw1_mm", ta=True, out_slots=N_DEV, out_dtypes=(BF16,))

    def res_norm_bwd(dx2v, dh2v, xv, mv, gate, g, sc, sh):
        _, vjp = jax.vjp(res_norm, xv, mv, gate, g, sc, sh)
        dx, dm, dgate, dg, dsc, dsh = vjp((dx2v, dh2v))
        return (dx, dm), (dgate, dg, dsc, dsh)

    (dx1, dmixed), (dgate1, dgain2, dscale2, dshift2) = rowcall(
        res_norm_bwd, "norm2_bwd", [dx2, dh2, x, mixed], [gate1, P['norm2_gain'], scale2, shift2],
        [(D, F32), (D, BF16)], [(1, D)] * 4, TT)

    dycat = matmul(dmixed, w_out, name="dycat_mm", tb=True)
    g_wout = matmul(ycat, dmixed, name="gwout_mm", ta=True, out_dtypes=(BF16,))

    def post_bwd(dy, *a):
        _, vjp = jax.vjp(_rk_post_fn, *a)
        gy0, gy1, gr, gv, gk0, gk1, gg, grk, glg, glb = vjp(dy)
        return (gy0, gr, gv, gk0, gk1, gg), (grk, glg, glb)

    cb_rk = S5W // C if C else 0
    (dy_rk, dr_p, dv_p, dk0_p, dk1_p, dg_p), (g_rk_rk, g_lng, g_lnb) = rowcall(
        post_bwd, "rk_post_bwd", [(dycat, C, cb_rk)] + post_tiled, post_full, [(C, F32)] * 6, [(1, C)] * 3, TS)
    dyT_h = hmT(dy_rk)
    scan_g = []
    for d in range(N_DIR):
        wd, kd, bd = dir_rows[d]
        scan_g.append(rk_scan_bwd(r_h, kk_h, wd, kd, bd, vT_h, dyT_h, hist[d], reverse=(d == 1), name=f"rk_scan_b{d}"))
    cot = [dr_p, unhm(scan_g[0][0]), unhm(scan_g[1][0]),
           dv_p, unT(scan_g[0][5]), unT(scan_g[1][5]),
           unhm(scan_g[0][1]), unhm(scan_g[1][1]),
           unhm(scan_g[0][2]), unhm(scan_g[1][2]),
           dk0_p, unhm(scan_g[0][3]), dk1_p, unhm(scan_g[1][3]),
           unhm(scan_g[0][4]), unhm(scan_g[1][4]),
           dg_p]

    def pre_bwd(psv, r0, r1, r2, v0, v1, v2, q0, q1, dw0, dw1, k0a, k0b, k1a, k1b, db0, db1, dgv, *params):
        _, vjp = jax.vjp(_rk_pre_fn, psv, *params)
        grads = vjp((r0 + r1 + r2, v0 + v1 + v2, q0 + q1, dw0, dw1, k0a + k0b, k1a + k1b, db0, db1, dgv))
        return (grads[0],), tuple(grads[1:])

    (dps,), pre_g = rowcall(pre_bwd, "rk_pre_bwd", [ps] + cot, pre_full, [(RINP, F32)],
                            [f.shape for f in pre_full], TS)
    g_w0, g_wup0, g_wup1, g_a0, g_aup0, g_aup1, g_gup_p, g_kk, g_ka = pre_g
    g_wup_p, g_aup_p = jnp.stack([g_wup0, g_wup1]), jnp.stack([g_aup0, g_aup1])
    dp_rk, g_mup, g_mun = rk_shift_bwd(dps, proj, mu_prev, mu_next, S5W)

    def s5_out_bwd(dy, *a):
        a = [t.astype(F32) for t in a]
        _, vjp = jax.vjp(_s5_out_fn, *a)
        g = vjp(dy)
        return (g[0], g[1], g[4]), tuple(g[5:])

    (dxr, dxi, du_a), s5_pg = rowcall(
        s5_out_bwd, "s5_out_bwd", [(dycat, S5W, 0), s0r, s0i, s1r, s1i, u_view], s5_full,
        [(NS, F32), (NS, F32), (S5W, F32)], [f.shape for f in s5_full], TS)
    g_creb, g_cimb, g_s5d, g_wglu, g_bglu = s5_pg
    l0r, l0i = s5_scan(dxr, dxi, lbr[0:1], -lbi[0:1], reverse=True, name="s5_scan_b0")
    l1r, l1i = s5_scan(dxr, dxi, lbr[1:2], -lbi[1:2], reverse=False, name="s5_scan_b1")
    dl0r, dl0i = s5_dlam(s0r, s0i, l0r, l0i, reverse=False, name="s5_dlam0")
    dl1r, dl1i = s5_dlam(s1r, s1i, l1r, l1i, reverse=True, name="s5_dlam1")

    def bu_bwd(uv, g0, g1, g2, g3, wb):
        _, vjp = jax.vjp(_s5_bu_fn, uv, wb.astype(F32))
        du, dwb = vjp((g0, g1, g2, g3))
        return (du,), (dwb,)

    (du_b,), (g_wblk,) = rowcall(bu_bwd, "s5_bu_bwd", [u_view, l0r, l0i, l1r, l1i], [wblk], [(S5W, F32)],
                                 [wblk.shape], TS)
    g_bbar = jnp.einsum('cghdrkp,gk->drhcgp',
                        g_wblk.reshape(NCH, S5_CHUNK_GROUPS, S5_GROUP, N_DIR, 2, S5_CHUNK_GROUPS, S5_STATE), eye)
    g_bbar = g_bbar.reshape(N_DIR, 2, S5_GROUP, NS)
    g_lre, g_lim, g_lstep, g_bre, g_bim = s5_prep_bwd(
        lre, lim, lstep, bre, bim, jnp.concatenate([dl0r, dl1r], 0), jnp.concatenate([dl0i, dl1i], 0),
        g_bbar[:, 0], g_bbar[:, 1])

    def uncblk(gb):
        g5 = gb.reshape(NCH, S5_CHUNK_GROUPS, S5_STATE, S5_CHUNK_GROUPS, S5_GROUP)
        return jnp.einsum('cgpkh,gk->cghp', g5, eye).reshape(G, S5_GROUP, S5_STATE)

    (du_tot,), _ = rowcall(lambda a, b_: ((a + b_,), ()), "s5_du_sum", [du_a, du_b], [], [(S5W, BF16)], [], TT)
    dproj = jnp.concatenate([du_tot, dp_rk.astype(BF16)], axis=1)
    dh1 = matmul(dproj, w_in, name="dh1_mm", tb=True)
    g_win = matmul(h1, dproj, name="gwin_mm", ta=True, out_dtypes=(BF16,))

    def norm1_bwd(dx1v, dh1v, xv, g, sc, sh):
        _, vjp = jax.vjp(_normmod, xv, g, sc, sh)
        dx, dg, dsc, dsh = vjp(dh1v)
        return (dx1v + dx,), (dg, dsc, dsh)

    (grad_x,), (dgain1, dscale1, dshift1) = rowcall(
        norm1_bwd, "norm1_bwd", [dx1, dh1, x], [P['norm1_gain'], scale1, shift1], [(D, F32)], [(1, D)] * 3, TT)

    dmod = jnp.concatenate([dshift1, dscale1, dgate1, dshift2, dscale2, dgate2], axis=1)
    lstep_g = g_lstep.reshape(N_DIR, G, S5_STATE)
    small_g = {
        'ada_b': dmod, 'norm1_gain': dgain1, 'norm2_gain': dgain2, 'final_gain': dfgain.reshape(D),
        's5_lambda_re': g_lre.reshape(1, N_DIR, G, S5_STATE), 's5_lambda_im': g_lim.reshape(1, N_DIR, G, S5_STATE),
        's5_log_step': lstep_g,
        's5_b_re': g_bre.T.reshape(1, G, S5_STATE, S5_GROUP), 's5_b_im': g_bim.T.reshape(1, G, S5_STATE, S5_GROUP),
        's5_c_re': uncblk(g_creb)[None], 's5_c_im': uncblk(g_cimb)[None],
        's5_d': g_s5d, 's5_b_glu': g_bglu,
        'rk_shift_prev': g_mup[:, :RIN], 'rk_shift_next': g_mun[:, :RIN],
        'rk_k_k': g_kk, 'rk_k_a': g_ka, 'rk_r_k': g_rk_rk.reshape(1, H, RWKV_HEAD),
        'rk_ln_gain': g_lng, 'rk_ln_bias': g_lnb,
    }
    small_shapes = {n: P[n].shape for n in SMALL}
    small_shapes['s5_log_step'] = (N_DIR, G, S5_STATE)
    small_pack = _pack_flat([small_g[n] for n in SMALL])

    def lora_unpad(gp):
        return jnp.stack([gp[d, d * DECAY_LORA:(d + 1) * DECAY_LORA] for d in range(N_DIR)])

    rk_grads = {'rk_w0': g_w0, 'rk_a0': g_a0, 'rk_w_up': lora_unpad(g_wup_p), 'rk_a_up': lora_unpad(g_aup_p),
                'rk_g_up': g_gup_p[:GATE_LORA]}
    rk_gpack = jnp.stack([_pack_rows([_cols_to_slots(rk_grads[n])[j] for n in RKPACK]) for j in range(N_DEV)])
    g_win_s = _cols_to_slots(g_win[:, :PROJ])
    ex = exchange(
        [small_pack, g_win_s, g_wout.reshape(N_DEV, D // N_DEV, D), g_w1, g_w2.reshape(N_DEV, FF // N_DEV, D),
         g_wglu.reshape(N_DEV, S5W // N_DEV, S5W), rk_gpack],
        ['gather'] + ['scatter'] * 6, "comm_grads")
    small_all, win_parts, wout_parts, w1_parts, w2_parts, wglu_parts, rk_parts = ex

    res = {}

    def put(name, g, dl, m2, v2):
        shp = P[name].shape
        res[name] = tuple(t.reshape(shp) for t in (g, dl, m2, v2))

    def adam2d(name, parts):
        shp = P[name].shape
        r2 = (math.prod(shp[:-1]), shp[-1])
        put(name, *adam(parts.reshape((parts.shape[0],) + r2), P[name].reshape(r2), M[name].reshape(r2),
                        V[name].reshape(r2), "adam_" + name))

    adam2d('w_in', win_parts)
    adam2d('w_out', wout_parts)
    adam2d('ffn_w1', w1_parts)
    adam2d('ffn_w2', w2_parts)
    adam2d('s5_w_glu', wglu_parts)
    off = 0
    for n in SMALL:
        if n == 'ada_b':
            break
        off += _round_up(math.prod(small_shapes[n]), SUBLANE * LANE) // LANE
    nrow_b = N_MOD * D // LANE
    dmod_all = small_all[:, off:off + nrow_b].reshape(N_DEV, N_MOD * D)
    dmod_cols = lax.dynamic_slice(dmod_all, (0, me * cs_mod), (N_DEV, cs_mod))
    g_adaw = matmul(c_act, dmod_cols, name="gadaw_mm", ta=True, precise=True)
    adam2d('ada_w', g_adaw[None])
    small_w = dict(P)
    small_m, small_v = dict(M), dict(V)
    rk_res = adam(rk_parts, rk_pack, _pack_rows([M[n][0] for n in RKPACK]), _pack_rows([V[n][0] for n in RKPACK]),
                  "adam_rkpack")
    for name, parts4 in zip(RKPACK, zip(*[_unpack_rows(t, rk_shapes) for t in rk_res])):
        put(name, *parts4)
    return loss, grad_x, res, (small_all, small_shapes, small_w, small_m, small_v)


def _small_update(small_all, small_shapes, P, M, V, res):
    G = (D_MODEL // 2) // S5_GROUP
    names = [n for n in SMALL if n != 's5_log_step']
    shapes = [small_shapes[n] for n in SMALL]
    parts = _unpack_flat_batched(small_all, shapes)
    by = dict(zip(SMALL, parts))
    ls = by['s5_log_step']
    ls = ls.transpose(0, 3, 1, 2).reshape(N_DEV * S5_STATE, N_DIR * G)
    pk = lambda d: _pack_flat([d[n] for n in names])
    packs = jnp.stack([_pack_flat([by[n][j] for n in names]) for j in range(N_DEV)])
    out = adam(packs, pk(P), pk(M), pk(V), "adam_small")
    shp = [P[n].shape for n in names]
    for name, parts4 in zip(names, zip(*[_unpack_flat(t, shp) for t in out])):
        res[name] = parts4
    lsw = lambda d: jnp.pad(d['s5_log_step'].reshape(1, N_DIR * G), ((0, SUBLANE - 1), (0, 0)))
    ls_parts = jnp.pad(ls[:, None, :], ((0, 0), (0, SUBLANE - 1), (0, 0)))
    o = adam(ls_parts, lsw(P), lsw(M), lsw(V), "adam_log_step")
    res['s5_log_step'] = tuple(t[0:1].reshape(P['s5_log_step'].shape) for t in o)


def _unpack_flat_batched(packed, shapes):
    out, r0 = [], 0
    B = packed.shape[0]
    for s in shapes:
        n = math.prod(s)
        nr = _round_up(n, SUBLANE * LANE) // LANE
        out.append(packed[:, r0:r0 + nr].reshape(B, -1)[:, :n].reshape((B,) + tuple(s)))
        r0 += nr
    return out


def kernel(x, c, ada_w, ada_b, norm1_gain, norm2_gain, final_gain, w_in, w_out, s5_lambda_re, s5_lambda_im, s5_log_step, s5_b_re, s5_b_im, s5_c_re, s5_c_im, s5_d, s5_w_glu, s5_b_glu, rk_shift_prev, rk_shift_next, rk_w0, rk_w_up, rk_a0, rk_a_up, rk_g_up, rk_k_k, rk_k_a, rk_r_k, rk_ln_gain, rk_ln_bias, ffn_w1, ffn_w2, loss_target, m_ada_w, m_ada_b, m_norm1_gain, m_norm2_gain, m_final_gain, m_w_in, m_w_out, m_s5_lambda_re, m_s5_lambda_im, m_s5_log_step, m_s5_b_re, m_s5_b_im, m_s5_c_re, m_s5_c_im, m_s5_d, m_s5_w_glu, m_s5_b_glu, m_rk_shift_prev, m_rk_shift_next, m_rk_w0, m_rk_w_up, m_rk_a0, m_rk_a_up, m_rk_g_up, m_rk_k_k, m_rk_k_a, m_rk_r_k, m_rk_ln_gain, m_rk_ln_bias, m_ffn_w1, m_ffn_w2, v_ada_w, v_ada_b, v_norm1_gain, v_norm2_gain, v_final_gain, v_w_in, v_w_out, v_s5_lambda_re, v_s5_lambda_im, v_s5_log_step, v_s5_b_re, v_s5_b_im, v_s5_c_re, v_s5_c_im, v_s5_d, v_s5_w_glu, v_s5_b_glu, v_rk_shift_prev, v_rk_shift_next, v_rk_w0, v_rk_w_up, v_rk_a0, v_rk_a_up, v_rk_g_up, v_rk_k_k, v_rk_k_a, v_rk_r_k, v_rk_ln_gain, v_rk_ln_bias, v_ffn_w1, v_ffn_w2):
    given = dict(locals())
    P = {n: given[n] for n in ['x', 'c', 'loss_target'] + WEIGHTS}
    M = {n: given['m_' + n] for n in WEIGHTS}
    V = {n: given['v_' + n] for n in WEIGHTS}
    loss, grad_x, res, small = _step(P, M, V)
    small_all, small_shapes, _, _, _ = small
    _small_update(small_all, small_shapes, P, M, V, res)
    outs = [loss, grad_x[None]]
    for q in range(4):
        outs += [res[n][q] for n in WEIGHTS]
    return tuple(outs)
```

```python
import functools
import math

import jax
import jax.numpy as jnp
from jax import lax
from jax.experimental import pallas as pl
from jax.experimental.pallas import tpu as pltpu

F32 = jnp.float32
BF16 = jnp.bfloat16
HI = lax.Precision.HIGHEST
MXU_DTYPE = jnp.bfloat16

N_DEV = 8
MESH_AXES = ("x", "y", "c")
D_MODEL = 2048
SEQ = 2048
S5_GROUP = 16
S5_STATE = 64
RWKV_HEAD = 64
DECAY_LORA = 64
GATE_LORA = 160
N_DIR = 2
N_MOD = 6
NORM_EPS = 1e-6
GN_EPS = 64e-5
L2_EPS = 1e-12
ADAM_LR = 0.001
ADAM_B1 = 0.9
ADAM_B2 = 0.999
ADAM_EPS = 1e-08
ADAM_WD = 0.01
ADAM_STEP = 10
LANE = 128
SUBLANE = 8
S5_CHUNK_GROUPS = 8
S5_SCAN_ROWS = 256
VMEM_LIMIT = 56 * 1024 * 1024

WEIGHTS = ['ada_w', 'ada_b', 'norm1_gain', 'norm2_gain', 'final_gain', 'w_in', 'w_out', 's5_lambda_re',
           's5_lambda_im', 's5_log_step', 's5_b_re', 's5_b_im', 's5_c_re', 's5_c_im', 's5_d', 's5_w_glu',
           's5_b_glu', 'rk_shift_prev', 'rk_shift_next', 'rk_w0', 'rk_w_up', 'rk_a0', 'rk_a_up', 'rk_g_up',
           'rk_k_k', 'rk_k_a', 'rk_r_k', 'rk_ln_gain', 'rk_ln_bias', 'ffn_w1', 'ffn_w2']
SMALL = ['ada_b', 'norm1_gain', 'norm2_gain', 'final_gain', 's5_lambda_re', 's5_lambda_im', 's5_log_step',
         's5_b_re', 's5_b_im', 's5_c_re', 's5_c_im', 's5_d', 's5_b_glu', 'rk_shift_prev', 'rk_shift_next',
         'rk_k_k', 'rk_k_a', 'rk_r_k', 'rk_ln_gain', 'rk_ln_bias']
RKPACK = ['rk_w0', 'rk_a0', 'rk_w_up', 'rk_a_up', 'rk_g_up']


def _round_up(n, m):
    return (n + m - 1) // m * m


def _tile(dim, pref, unit=LANE):
    t = min(pref, dim) // unit * unit
    while t >= unit:
        if dim % t == 0:
            return t
        t -= unit
    return dim


def _cparams(sem=None):
    return pltpu.CompilerParams(dimension_semantics=sem, vmem_limit_bytes=VMEM_LIMIT)


def _full_spec(a):
    nd = a.ndim
    return pl.BlockSpec(a.shape, lambda *_: (0,) * nd)


def exchange(arrs, modes, name):
    n = len(arrs)
    out_shape = [jax.ShapeDtypeStruct((N_DEV,) + a.shape if m == 'gather' else a.shape, a.dtype)
                 for a, m in zip(arrs, modes)]

    def body(*refs):
        ins, outs = refs[:n], refs[n:2 * n]
        send_sems, recv_sems, local_sems = refs[2 * n:]
        x, y, c = (lax.axis_index(a) for a in MESH_AXES)
        me = 4 * x + 2 * y + c
        copies = []
        for i in range(n):
            gather = modes[i] == 'gather'
            mine = pltpu.make_async_copy(ins[i] if gather else ins[i].at[me], outs[i].at[me], local_sems.at[i])
            mine.start()
            copies.append(mine)
        remote = []
        for k in range(1, N_DEV):
            px = 1 - x if (k >> 2) & 1 else x
            py = 1 - y if (k >> 1) & 1 else y
            pc = 1 - c if k & 1 else c
            peer = 4 * px + 2 * py + pc
            for i in range(n):
                src = ins[i] if modes[i] == 'gather' else ins[i].at[peer]
                cp = pltpu.make_async_remote_copy(
                    src_ref=src, dst_ref=outs[i].at[me], send_sem=send_sems.at[i, k - 1],
                    recv_sem=recv_sems.at[i, k - 1], device_id=(px, py, pc), device_id_type=pl.DeviceIdType.MESH)
                cp.start()
                remote.append(cp)
        for cp in remote:
            cp.wait_recv()
        for cp in remote:
            cp.wait_send()
        for cp in copies:
            cp.wait()

    any_spec = pl.BlockSpec(memory_space=pl.ANY)
    return pl.pallas_call(
        body, name=name, out_shape=out_shape,
        in_specs=[any_spec] * n, out_specs=[any_spec] * n,
        scratch_shapes=[pltpu.SemaphoreType.DMA((n, N_DEV - 1)), pltpu.SemaphoreType.DMA((n, N_DEV - 1)),
                        pltpu.SemaphoreType.DMA((n,))],
        compiler_params=pltpu.CompilerParams(has_side_effects=True),
    )(*arrs)


def matmul(a, b, *, name, ta=False, tb=False, b_slots=False, out_slots=0, out_dtypes=(F32,), epi=None,
           extras=(), precise=False, tm=512, tn=512, tk=512):
    if ta:
        K, M = a.shape
    else:
        M, K = a.shape
    if b_slots:
        ns, br, bc = b.shape
        bshape = (br, ns * bc)
    else:
        bshape = b.shape
    N = bshape[0] if tb else bshape[1]
    assert (bshape[1] if tb else bshape[0]) == K, (a.shape, b.shape, ta, tb)
    tm, tn, tk = _tile(M, tm, SUBLANE), _tile(N, tn), _tile(K, tk, SUBLANE if K < LANE else LANE)
    if b_slots and tb:
        tk = _tile(b.shape[2], tk)
    elif b_slots:
        tn = _tile(b.shape[2], tn)
    if out_slots:
        tn = _tile(N // out_slots, tn)
    if b_slots:
        cs = b.shape[2]
        tcol = tk if tb else tn
        assert cs % tcol == 0
        per = cs // tcol
    if out_slots:
        ncs = N // out_slots
        assert ncs % tn == 0
        operc = ncs // tn
    nk = K // tk
    a_spec = pl.BlockSpec((tk, tm), lambda i, j, k: (k, i)) if ta else pl.BlockSpec((tm, tk), lambda i, j, k: (i, k))
    if b_slots:
        if tb:
            b_spec = pl.BlockSpec((None, tn, tk), lambda i, j, k: (k // per, j, k % per))
        else:
            b_spec = pl.BlockSpec((None, tk, tn), lambda i, j, k: (j // per, k, j % per))
    else:
        b_spec = pl.BlockSpec((tn, tk), lambda i, j, k: (j, k)) if tb else pl.BlockSpec((tk, tn), lambda i, j, k: (k, j))
    ex_specs = []
    for arr, kind in extras:
        if kind == 'mn':
            ex_specs.append(pl.BlockSpec((tm, tn), lambda i, j, k: (i, j)))
        else:
            ex_specs.append(pl.BlockSpec((1, tn), lambda i, j, k: (0, j)))
    if out_slots:
        o_spec = pl.BlockSpec((None, tm, tn), lambda i, j, k: (j // operc, i, j % operc))
        o_shape = (out_slots, M, ncs)
    else:
        o_spec = pl.BlockSpec((tm, tn), lambda i, j, k: (i, j))
        o_shape = (M, N)
    ne, no = len(extras), len(out_dtypes)
    dims = (((0 if ta else 1,), (1 if tb else 0,)), ((), ()))
    op_dtype = F32 if precise else MXU_DTYPE

    def body(a_ref, b_ref, *rest):
        ex_refs, out_refs, acc = rest[:ne], rest[ne:ne + no], rest[-1]
        k = pl.program_id(2)

        @pl.when(k == 0)
        def _():
            acc[...] = jnp.zeros_like(acc)

        acc[...] += lax.dot_general(a_ref[...].astype(op_dtype), b_ref[...].astype(op_dtype), dims,
                                    precision=HI if precise else None, preferred_element_type=F32)

        @pl.when(k == nk - 1)
        def _():
            res = epi(acc[...], *[e[...] for e in ex_refs]) if epi is not None else (acc[...],)
            for o, r in zip(out_refs, res):
                o[...] = r.astype(o.dtype)

    outs = pl.pallas_call(
        body, name=name, grid=(M // tm, N // tn, nk),
        in_specs=[a_spec, b_spec] + ex_specs, out_specs=[o_spec] * no,
        out_shape=[jax.ShapeDtypeStruct(o_shape, dt) for dt in out_dtypes],
        scratch_shapes=[pltpu.VMEM((tm, tn), F32)],
        compiler_params=_cparams(("parallel", "parallel", "arbitrary")),
    )(a, b, *[e[0] for e in extras])
    return outs[0] if no == 1 else outs


def rowcall(fn, name, tiled, full, tiled_out, acc_out, tt):
    views = [(t, t.shape[1], 0) if not isinstance(t, tuple) else t for t in tiled]
    T = views[0][0].shape[0]
    tt = _tile(T, tt, SUBLANE)
    nt, nf, nto, nao = len(views), len(full), len(tiled_out), len(acc_out)

    def view_spec(w, cb):
        return pl.BlockSpec((tt, w), lambda i: (i, cb))

    in_specs = [view_spec(w, cb) for _, w, cb in views] + [_full_spec(f) for f in full]
    out_specs = [pl.BlockSpec((tt, w), lambda i: (i, 0)) for w, _ in tiled_out]
    out_specs += [pl.BlockSpec(s, lambda i, nd=len(s): (0,) * nd) for s in acc_out]
    out_shape = [jax.ShapeDtypeStruct((T, w), dt) for w, dt in tiled_out]
    out_shape += [jax.ShapeDtypeStruct(s, F32) for s in acc_out]

    def body(*refs):
        tin, fin = refs[:nt], refs[nt:nt + nf]
        tout, aout = refs[nt + nf:nt + nf + nto], refs[nt + nf + nto:]
        touts, aouts = fn(*[r[...] for r in tin], *[r[...] for r in fin])
        for r, v in zip(tout, touts):
            r[...] = v.astype(r.dtype)
        if nao:
            @pl.when(pl.program_id(0) == 0)
            def _():
                for r in aout:
                    r[...] = jnp.zeros_like(r)

            for r, v in zip(aout, aouts):
                r[...] += v.astype(F32)

    outs = pl.pallas_call(
        body, name=name, grid=(T // tt,), in_specs=in_specs, out_specs=out_specs, out_shape=out_shape,
        compiler_params=_cparams(("arbitrary",) if nao else ("parallel",)),
    )(*[v[0] for v in views], *full)
    return outs[:nto], outs[nto:]


def _mm(a, b):
    return jnp.dot(a.astype(MXU_DTYPE), b.astype(MXU_DTYPE), preferred_element_type=F32)


def _rms(x, gain):
    ms = jnp.mean(x * x, axis=-1, keepdims=True)
    return x * lax.rsqrt(ms + NORM_EPS) * gain


def _normmod(x, gain, scale, shift):
    return _rms(x, gain) * (1.0 + scale) + shift


def _gelu_tanh(y):
    return 0.5 * y * (1.0 + jnp.tanh(math.sqrt(2.0 / math.pi) * (y + 0.044715 * (y * y * y))))


def _sigmoid(x):
    return 1.0 / (1.0 + jnp.exp(-x))


def _softplus(x):
    return jnp.maximum(x, 0.0) + jnp.log(1.0 + jnp.exp(-jnp.abs(x)))


def _seg_mats(width, seg):
    r = lax.broadcasted_iota(jnp.int32, (width, LANE), 0) // seg
    c = lax.broadcasted_iota(jnp.int32, (width, LANE), 1)
    s = (r == c).astype(F32)
    rt = lax.broadcasted_iota(jnp.int32, (LANE, width), 0)
    ct = lax.broadcasted_iota(jnp.int32, (LANE, width), 1) // seg
    st = (rt == ct).astype(F32)
    return s, st


def _segsum(x, s):
    return jnp.dot(x, s, precision=HI, preferred_element_type=F32)


def _s5_prep_fn(lre, lim, lstep, bre, bim):
    step = jnp.exp(lstep)
    mag = jnp.exp(lre * step)
    lbr = mag * jnp.cos(lim * step)
    lbi = mag * jnp.sin(lim * step)
    den = lre * lre + lim * lim
    nr = lbr - 1.0
    ni = lbi
    cre = (nr * lre + ni * lim) / den
    cim = (ni * lre - nr * lim) / den
    bbr = jnp.stack([cre[d:d + 1] * bre - cim[d:d + 1] * bim for d in range(N_DIR)])
    bbi = jnp.stack([cre[d:d + 1] * bim + cim[d:d + 1] * bre for d in range(N_DIR)])
    return lbr, lbi, bbr, bbi


def s5_prep(lre, lim, lstep, bre, bim):
    ns = lre.shape[1]

    def body(lre_r, lim_r, ls_r, bre_r, bim_r, lbr_r, lbi_r, bbr_r, bbi_r):
        lbr, lbi, bbr, bbi = _s5_prep_fn(lre_r[...], lim_r[...], ls_r[...], bre_r[...], bim_r[...])
        lbr_r[...] = lbr
        lbi_r[...] = lbi
        bbr_r[...] = bbr
        bbi_r[...] = bbi

    return pl.pallas_call(
        body, name="s5_prep",
        out_shape=[jax.ShapeDtypeStruct((N_DIR, ns), F32)] * 2 + [jax.ShapeDtypeStruct((N_DIR, S5_GROUP, ns), F32)] * 2,
        compiler_params=_cparams(),
    )(lre, lim, lstep, bre, bim)


def s5_prep_bwd(lre, lim, lstep, bre, bim, dlbr, dlbi, dbbr, dbbi):
    ns = lre.shape[1]

    def body(lre_r, lim_r, ls_r, bre_r, bim_r, d1, d2, d3, d4, o1, o2, o3, o4, o5):
        _, vjp = jax.vjp(_s5_prep_fn, lre_r[...], lim_r[...], ls_r[...], bre_r[...], bim_r[...])
        g = vjp((d1[...], d2[...], d3[...], d4[...]))
        for o, v in zip((o1, o2, o3, o4, o5), g):
            o[...] = v

    return pl.pallas_call(
        body, name="s5_prep_bwd",
        out_shape=[jax.ShapeDtypeStruct((N_DIR, ns), F32)] * 3 + [jax.ShapeDtypeStruct((S5_GROUP, ns), F32)] * 2,
        compiler_params=_cparams(),
    )(lre, lim, lstep, bre, bim, dlbr, dlbi, dbbr, dbbi)


def s5_scan(bre, bim, lre, lim, *, reverse, name):
    T, NS = bre.shape
    tt = _tile(T, S5_SCAN_ROWS, SUBLANE)
    wl = _tile(NS, 512)
    nT = T // tt
    ngrp = tt // SUBLANE

    def tmap(j, i):
        return ((nT - 1 - i) if reverse else i, j)

    def body(bre_r, bim_r, lre_r, lim_r, sre_r, sim_r, cre, cim):
        @pl.when(pl.program_id(1) == 0)
        def _():
            cre[...] = jnp.zeros_like(cre)
            cim[...] = jnp.zeros_like(cim)

        lr = jnp.broadcast_to(lre_r[...], (SUBLANE, wl))
        li = jnp.broadcast_to(lim_r[...], (SUBLANE, wl))
        row = lax.broadcasted_iota(jnp.int32, (SUBLANE, wl), 0)
        pows = [(lr, li)]
        for _ in range(3):
            pr, pi = pows[-1]
            pows.append((pr * pr - pi * pi, 2.0 * pr * pi))
        e = (SUBLANE - row) if reverse else (row + 1)
        Pr = jnp.ones((SUBLANE, wl), F32)
        Pi = jnp.zeros((SUBLANE, wl), F32)
        for bit, (qr, qi) in enumerate(pows):
            on = ((e >> bit) & 1) == 1
            nr, ni = Pr * qr - Pi * qi, Pr * qi + Pi * qr
            Pr, Pi = jnp.where(on, nr, Pr), jnp.where(on, ni, Pi)

        def group(g, carry):
            gg = (ngrp - 1 - g) if reverse else g
            rows = pl.ds(pl.multiple_of(gg * SUBLANE, SUBLANE), SUBLANE)
            sr, si = bre_r[rows, :], bim_r[rows, :]
            for lvl, k in enumerate((1, 2, 4)):
                qr, qi = pows[lvl]
                if reverse:
                    shr = pltpu.roll(sr, SUBLANE - k, 0)
                    shi = pltpu.roll(si, SUBLANE - k, 0)
                    keep = row < SUBLANE - k
                else:
                    shr = pltpu.roll(sr, k, 0)
                    shi = pltpu.roll(si, k, 0)
                    keep = row >= k
                shr = jnp.where(keep, shr, 0.0)
                shi = jnp.where(keep, shi, 0.0)
                sr, si = sr + qr * shr - qi * shi, si + qr * shi + qi * shr
            cr, ci = cre[...], cim[...]
            sr, si = sr + Pr * cr - Pi * ci, si + Pr * ci + Pi * cr
            sre_r[rows, :] = sr
            sim_r[rows, :] = si
            last = 0 if reverse else SUBLANE - 1
            cre[...] = jnp.broadcast_to(sr[last:last + 1, :], (SUBLANE, wl))
            cim[...] = jnp.broadcast_to(si[last:last + 1, :], (SUBLANE, wl))
            return carry

        lax.fori_loop(0, ngrp, group, 0)

    blk = pl.BlockSpec((tt, wl), tmap)
    row_spec = pl.BlockSpec((1, wl), lambda j, i: (0, j))
    return pl.pallas_call(
        body, name=name, grid=(NS // wl, nT),
        in_specs=[blk, blk, row_spec, row_spec], out_specs=[blk, blk],
        out_shape=[jax.ShapeDtypeStruct((T, NS), F32)] * 2,
        scratch_shapes=[pltpu.VMEM((SUBLANE, wl), F32)] * 2,
        compiler_params=_cparams(("parallel", "arbitrary")),
    )(bre, bim, lre, lim)


def s5_dlam(sre, sim, gre, gim, *, reverse, name):
    T, NS = sre.shape
    wl = _tile(NS, 256)

    def body(sr_r, si_r, gr_r, gi_r, dr_r, di_r):
        row = lax.broadcasted_iota(jnp.int32, (T, wl), 0)
        if reverse:
            keep = row < T - 1
            pr = jnp.where(keep, pltpu.roll(sr_r[...], T - 1, 0), 0.0)
            pi = jnp.where(keep, pltpu.roll(si_r[...], T - 1, 0), 0.0)
        else:
            keep = row >= 1
            pr = jnp.where(keep, pltpu.roll(sr_r[...], 1, 0), 0.0)
            pi = jnp.where(keep, pltpu.roll(si_r[...], 1, 0), 0.0)
        gr, gi = gr_r[...], gi_r[...]
        dr_r[...] = jnp.sum(pr * gr + pi * gi, axis=0, keepdims=True)
        di_r[...] = jnp.sum(pr * gi - pi * gr, axis=0, keepdims=True)

    blk = pl.BlockSpec((T, wl), lambda j: (0, j))
    o = pl.BlockSpec((1, wl), lambda j: (0, j))
    return pl.pallas_call(
        body, name=name, grid=(NS // wl,), in_specs=[blk] * 4, out_specs=[o, o],
        out_shape=[jax.ShapeDtypeStruct((1, NS), F32)] * 2,
        compiler_params=_cparams(("parallel",)),
    )(sre, sim, gre, gim)


def _s5_bu_fn(u, wblk):
    nch = wblk.shape[0]
    cw = S5_CHUNK_GROUPS * S5_GROUP
    sw = S5_CHUNK_GROUPS * S5_STATE
    parts = [[] for _ in range(4)]
    for ch in range(nch):
        res = _mm(u[:, ch * cw:(ch + 1) * cw], wblk[ch])
        for q in range(4):
            parts[q].append(res[:, q * sw:(q + 1) * sw])
    return tuple(jnp.concatenate(p, axis=1) if nch > 1 else p[0] for p in parts)


def _s5_out_fn(x0r, x0i, x1r, x1i, u, cre, cim, dsk, wglu, bglu):
    xr, xi = x0r + x1r, x0i + x1i
    nch = cre.shape[0]
    sw = S5_CHUNK_GROUPS * S5_STATE
    ys = [_mm(xr[:, ch * sw:(ch + 1) * sw], cre[ch]) - _mm(xi[:, ch * sw:(ch + 1) * sw], cim[ch]) for ch in range(nch)]
    y = jnp.concatenate(ys, axis=1) if nch > 1 else ys[0]
    z = _gelu_tanh(y + dsk * u)
    gate = _sigmoid(_mm(z, wglu) + bglu)
    return z * gate


def _rk_dims():
    C = D_MODEL // 2
    LW = N_DIR * DECAY_LORA
    GP = _round_up(GATE_LORA, LANE)
    return C, LW, GP


def _rk_pre_fn(ps, w0, wup0, wup1, a0, aup0, aup1, gup, k_k, k_a):
    C, LW, GP = _rk_dims()
    r, k, v = ps[:, 0:C], ps[:, C:2 * C], ps[:, 2 * C:3 * C]
    wdn = ps[:, 3 * C:3 * C + LW]
    adn = ps[:, 3 * C + LW:3 * C + 2 * LW]
    gdn = ps[:, 3 * C + 2 * LW:3 * C + 2 * LW + GP]
    s, st = _seg_mats(C, RWKV_HEAD)
    kk = k * k_k
    n2 = _segsum(kk * kk, s)
    n2 = jnp.where(n2 > 0.0, n2, 1.0)
    inv = 1.0 / jnp.maximum(jnp.sqrt(n2), L2_EPS)
    kkn = kk * _segsum(inv, st)
    tw = jnp.tanh(wdn)
    wup, aup = (wup0, wup1), (aup0, aup1)
    ws, ks, bs = [], [], []
    for d in range(N_DIR):
        wraw = w0[d:d + 1] + _mm(tw, wup[d])
        w = -_softplus(-wraw) - 0.5
        ws.append(jnp.exp(-jnp.exp(w)))
        a = _sigmoid(a0[d:d + 1] + _mm(adn, aup[d]))
        ks.append(k * (1.0 + (a - 1.0) * k_a))
        bs.append(kkn * a)
    g = _mm(_sigmoid(gdn), gup)
    return r, v, kkn, ws[0], ws[1], ks[0], ks[1], bs[0], bs[1], g


def _rk_post_fn(y0, y1, r, v, k0, k1, g, r_k, lng, lnb):
    C = r.shape[1]
    s, st = _seg_mats(C, RWKV_HEAD)
    y = y0 + y1
    mu = _segsum(_segsum(y, s) * (1.0 / RWKV_HEAD), st)
    yc = y - mu
    var = _segsum(_segsum(yc * yc, s) * (1.0 / RWKV_HEAD), st)
    yn = yc * lax.rsqrt(var + GN_EPS) * lng + lnb
    bonus = _segsum(_segsum(r * (k0 + k1) * r_k, s), st)
    return (yn + bonus * v) * g


def rk_shift(proj, mp, mn, col0):
    T = proj.shape[0]
    W = mp.shape[1]
    wl = _tile(math.gcd(W, col0), 256)
    cb0 = col0 // wl

    def body(p_r, mp_r, mn_r, o_r):
        p = p_r[...]
        row = lax.broadcasted_iota(jnp.int32, (T, wl), 0)
        prev = jnp.where(row >= 1, pltpu.roll(p, 1, 0), 0.0)
        nxt = jnp.where(row < T - 1, pltpu.roll(p, T - 1, 0), 0.0)
        o_r[...] = p + mp_r[...] * (prev - p) + mn_r[...] * (nxt - p)

    rs = pl.BlockSpec((1, wl), lambda j: (0, j))
    return pl.pallas_call(
        body, name="rk_shift", grid=(W // wl,),
        in_specs=[pl.BlockSpec((T, wl), lambda j: (0, cb0 + j)), rs, rs],
        out_specs=pl.BlockSpec((T, wl), lambda j: (0, j)),
        out_shape=jax.ShapeDtypeStruct((T, W), F32),
        compiler_params=_cparams(("parallel",)),
    )(proj, mp, mn)


def rk_shift_bwd(dps, proj, mp, mn, col0):
    T, W = dps.shape
    wl = _tile(math.gcd(W, col0), 256)
    cb0 = col0 // wl

    def body(d_r, p_r, mp_r, mn_r, dp_r, dmp_r, dmn_r):
        d, p = d_r[...], p_r[...]
        mpv, mnv = mp_r[...], mn_r[...]
        row = lax.broadcasted_iota(jnp.int32, (T, wl), 0)
        first, last = row >= 1, row < T - 1
        prev = jnp.where(first, pltpu.roll(p, 1, 0), 0.0)
        nxt = jnp.where(last, pltpu.roll(p, T - 1, 0), 0.0)
        dmp_r[...] = jnp.sum(d * (prev - p), axis=0, keepdims=True)
        dmn_r[...] = jnp.sum(d * (nxt - p), axis=0, keepdims=True)
        dp_r[...] = (d * (1.0 - mpv - mnv) + jnp.where(last, pltpu.roll(d * mpv, T - 1, 0), 0.0)
                     + jnp.where(first, pltpu.roll(d * mnv, 1, 0), 0.0))

    rs = pl.BlockSpec((1, wl), lambda j: (0, j))
    blk = pl.BlockSpec((T, wl), lambda j: (0, j))
    return pl.pallas_call(
        body, name="rk_shift_bwd", grid=(W // wl,),
        in_specs=[blk, pl.BlockSpec((T, wl), lambda j: (0, cb0 + j)), rs, rs],
        out_specs=[blk, rs, rs],
        out_shape=[jax.ShapeDtypeStruct((T, W), F32), jax.ShapeDtypeStruct((1, W), F32), jax.ShapeDtypeStruct((1, W), F32)],
        compiler_params=_cparams(("parallel",)),
    )(dps, proj, mp, mn)


RK_FWD_HEADS = 8
RK_BWD_HEADS = 4
RK_TIME_BLOCK = 32
RK_LANE_BLOCK = 128


def _rk_blocks(H, T, N, order_reversed, heads):
    hh = min(heads, H)
    tb = min(RK_TIME_BLOCK, T)
    lb = min(RK_LANE_BLOCK, T)
    nb, per = T // tb, lb // tb

    def tix(i):
        return (nb - 1 - i) if order_reversed else i

    rows = pl.BlockSpec((hh, tb, N), lambda g, i: (g, tix(i), 0))
    cols = pl.BlockSpec((hh, N, lb), lambda g, i: (g, 0, tix(i) // per))
    hist = pl.BlockSpec((tb, hh, N, N), lambda g, i: (tix(i), g, 0, 0))
    return hh, tb, lb, nb, per, tix, rows, cols, hist


def _aligned(tile_ref, h, off, lb, per):
    return pltpu.roll(tile_ref[h], (lb - off) % lb, 1) if per > 1 else tile_ref[h]


def _spread_columns(tile_ref, out_ref, hh, tb, off, lb, per):
    N = out_ref.shape[-1]
    lane = lax.broadcasted_iota(jnp.int32, (N, lb), 1)
    for h in range(hh):
        tile = _aligned(tile_ref, h, off, lb, per)
        for t in range(tb):
            col = jnp.sum(jnp.where(lane == t, tile, 0.0), axis=1, keepdims=True)
            out_ref[t, h] = jnp.broadcast_to(col, (N, N))


def rk_scan(r, kk, w, k, b, vT, *, reverse, name):
    H, T, N = r.shape
    hh, tb, lb, nb, per, tix, rows, cols, hist_spec = _rk_blocks(H, T, N, reverse, RK_FWD_HEADS)

    def body(r_r, kk_r, w_r, k_r, b_r, vT_r, yT_r, hist_r, S, VC, YA):
        i = pl.program_id(1)

        @pl.when(i == 0)
        def _():
            S[...] = jnp.zeros_like(S)

        @pl.when(i % per == 0)
        def _():
            yT_r[...] = jnp.zeros_like(yT_r)

        off = (tix(i) % per) * tb
        YA[...] = jnp.zeros_like(YA)
        _spread_columns(vT_r, VC, hh, tb, off, lb, per)
        st = [S[h] for h in range(hh)]
        for s in range(tb):
            t = (tb - 1 - s) if reverse else s
            row = slice(t, t + 1)
            sas = [-jnp.sum(st[h] * kk_r[h, row, :], axis=1, keepdims=True) for h in range(hh)]
            for h in range(hh):
                hist_r[t, h] = st[h]
                st[h] = st[h] * w_r[h, row, :] + sas[h] * b_r[h, row, :] + VC[t, h] * k_r[h, row, :]
            ys = [jnp.sum(st[h] * r_r[h, row, :], axis=1, keepdims=True) for h in range(hh)]
            for h in range(hh):
                YA[h, :, row] = ys[h]
        for h in range(hh):
            S[h] = st[h]
            yT_r[h] = yT_r[h] + (pltpu.roll(YA[h], off, 1) if per > 1 else YA[h])

    return pl.pallas_call(
        body, name=name, grid=(H // hh, nb),
        in_specs=[rows] * 5 + [cols], out_specs=[cols, hist_spec],
        out_shape=[jax.ShapeDtypeStruct((H, N, T), F32), jax.ShapeDtypeStruct((T, H, N, N), F32)],
        scratch_shapes=[pltpu.VMEM((hh, N, N), F32), pltpu.VMEM((tb, hh, N, N), F32), pltpu.VMEM((hh, N, lb), F32)],
        compiler_params=_cparams(("parallel", "arbitrary")),
    )(r, kk, w, k, b, vT)


def rk_scan_bwd(r, kk, w, k, b, vT, dyT, hist, *, reverse, name):
    H, T, N = r.shape
    hh, tb, lb, nb, per, tix, rows, cols, hist_spec = _rk_blocks(H, T, N, not reverse, RK_BWD_HEADS)

    def body(r_r, kk_r, w_r, k_r, b_r, vT_r, dyT_r, hist_r, dr_r, dkk_r, dw_r, dk_r, db_r, dvT_r, G, VC, DC, YA):
        i = pl.program_id(1)

        @pl.when(i == 0)
        def _():
            G[...] = jnp.zeros_like(G)

        @pl.when(i % per == 0)
        def _():
            dvT_r[...] = jnp.zeros_like(dvT_r)

        off = (tix(i) % per) * tb
        YA[...] = jnp.zeros_like(YA)
        _spread_columns(vT_r, VC, hh, tb, off, lb, per)
        _spread_columns(dyT_r, DC, hh, tb, off, lb, per)
        gs = [G[h] for h in range(hh)]
        for s in range(tb):
            t = s if reverse else (tb - 1 - s)
            row = slice(t, t + 1)
            g = [gs[h] + DC[t, h] * r_r[h, row, :] for h in range(hh)]
            sa = [-jnp.sum(hist_r[t, h] * kk_r[h, row, :], axis=1, keepdims=True) for h in range(hh)]
            dsa = [jnp.sum(g[h] * b_r[h, row, :], axis=1, keepdims=True) for h in range(hh)]
            dvcol = [jnp.sum(g[h] * k_r[h, row, :], axis=1, keepdims=True) for h in range(hh)]
            for h in range(hh):
                sp = hist_r[t, h]
                kkv, wv, kv, bv = kk_r[h, row, :], w_r[h, row, :], k_r[h, row, :], b_r[h, row, :]
                vcol, dycol = VC[t, h], DC[t, h]
                sn = sp * wv + sa[h] * bv + vcol * kv
                dr_r[h, row, :] = jnp.sum(sn * dycol, axis=0, keepdims=True)
                dw_r[h, row, :] = jnp.sum(g[h] * sp, axis=0, keepdims=True)
                db_r[h, row, :] = jnp.sum(g[h] * sa[h], axis=0, keepdims=True)
                dk_r[h, row, :] = jnp.sum(g[h] * vcol, axis=0, keepdims=True)
                dkk_r[h, row, :] = -jnp.sum(sp * dsa[h], axis=0, keepdims=True)
                gs[h] = g[h] * wv - dsa[h] * kkv
                YA[h, :, row] = dvcol[h]
        for h in range(hh):
            G[h] = gs[h]
            dvT_r[h] = dvT_r[h] + (pltpu.roll(YA[h], off, 1) if per > 1 else YA[h])

    return pl.pallas_call(
        body, name=name, grid=(H // hh, nb),
        in_specs=[rows] * 5 + [cols, cols, hist_spec], out_specs=[rows] * 5 + [cols],
        out_shape=[jax.ShapeDtypeStruct((H, T, N), F32)] * 5 + [jax.ShapeDtypeStruct((H, N, T), F32)],
        scratch_shapes=[pltpu.VMEM((hh, N, N), F32), pltpu.VMEM((tb, hh, N, N), F32), pltpu.VMEM((tb, hh, N, N), F32),
                        pltpu.VMEM((hh, N, lb), F32)],
        compiler_params=_cparams(("parallel", "arbitrary")),
    )(r, kk, w, k, b, vT, dyT, hist)


def adam(parts, w, m, v, name):
    P, R, C = parts.shape
    tr = _tile(R, max(SUBLANE, (1 << 19) // max(C, 1) // SUBLANE * SUBLANE), SUBLANE)
    c1 = 1.0 / (1.0 - ADAM_B1 ** ADAM_STEP)
    c2 = 1.0 / (1.0 - ADAM_B2 ** ADAM_STEP)

    def body(p_r, w_r, m_r, v_r, g_o, d_o, m_o, v_o):
        g = p_r[0].astype(F32)
        for q in range(1, P):
            g = g + p_r[q].astype(F32)
        m2 = ADAM_B1 * m_r[...] + (1.0 - ADAM_B1) * g
        v2 = ADAM_B2 * v_r[...] + (1.0 - ADAM_B2) * (g * g)
        g_o[...] = g
        m_o[...] = m2
        v_o[...] = v2
        d_o[...] = -ADAM_LR * ((m2 * c1) / (jnp.sqrt(v2 * c2) + ADAM_EPS) + ADAM_WD * w_r[...])

    blk = pl.BlockSpec((tr, C), lambda i: (i, 0))
    return pl.pallas_call(
        body, name=name, grid=(R // tr,),
        in_specs=[pl.BlockSpec((P, tr, C), lambda i: (0, i, 0)), blk, blk, blk], out_specs=[blk] * 4,
        out_shape=[jax.ShapeDtypeStruct((R, C), F32)] * 4,
        compiler_params=_cparams(("parallel",)),
    )(parts, w, m, v)


def _pack_flat(arrs):
    rows = []
    for a in arrs:
        f = a.reshape(-1).astype(F32)
        n = _round_up(f.shape[0], SUBLANE * LANE)
        rows.append(jnp.pad(f, (0, n - f.shape[0])).reshape(-1, LANE))
    return jnp.concatenate(rows, axis=0)


def _unpack_flat(packed, shapes):
    out, r0 = [], 0
    for s in shapes:
        n = math.prod(s)
        nr = _round_up(n, SUBLANE * LANE) // LANE
        out.append(packed[r0:r0 + nr].reshape(-1)[:n].reshape(s))
        r0 += nr
    return out


def _pack_rows(arrs):
    rows = []
    for a in arrs:
        f = a.reshape(-1, a.shape[-1]).astype(F32)
        n = _round_up(f.shape[0], SUBLANE)
        rows.append(jnp.pad(f, ((0, n - f.shape[0]), (0, 0))))
    return jnp.concatenate(rows, axis=0)


def _unpack_rows(packed, shapes):
    out, r0 = [], 0
    for s in shapes:
        nr = math.prod(s[:-1])
        out.append(packed[r0:r0 + nr].reshape(s))
        r0 += _round_up(nr, SUBLANE)
    return out


def _cols_from_slots(g):
    return jnp.moveaxis(g, 0, -2).reshape(g.shape[1:-1] + (N_DEV * g.shape[-1],))


def _cols_to_slots(a):
    cs = a.shape[-1] // N_DEV
    return jnp.moveaxis(a.reshape(a.shape[:-1] + (N_DEV, cs)), -2, 0)


def _step(P, M, V):
    D, T = D_MODEL, SEQ
    S5W = D // 2
    C, LW, GP = _rk_dims()
    H = C // RWKV_HEAD
    G = S5W // S5_GROUP
    NS = G * S5_STATE
    NCH = G // S5_CHUNK_GROUPS
    SW = S5_CHUNK_GROUPS * S5_STATE
    RIN = 3 * C + 2 * LW + GATE_LORA
    RINP = 3 * C + 2 * LW + GP
    PROJ = S5W + RIN
    PROJP = S5W + RINP
    FF = 4 * D
    me = 4 * lax.axis_index("x") + 2 * lax.axis_index("y") + lax.axis_index("c")
    cs_mod = N_MOD * D // N_DEV
    eye = jnp.eye(S5_CHUNK_GROUPS, dtype=F32)

    x = P['x'][0]
    target = P['loss_target'][0]

    (c_all,) = exchange([P['c']], ['gather'], "comm_gather_c")
    c_all = c_all.reshape(N_DEV, D)
    (c_act,), _ = rowcall(lambda cv: ((cv * _sigmoid(cv),), ()), "silu_c", [c_all], [], [(D, F32)], [], N_DEV)
    ada_b_loc = lax.dynamic_slice(P['ada_b'], (0, me * cs_mod), (1, cs_mod))
    mod_loc = matmul(c_act, P['ada_w'][0], name="mod_mm", precise=True, extras=[(ada_b_loc, 'n')],
                     epi=lambda acc, bias: (acc + bias,))

    rk_shapes = [P[n][0].shape for n in RKPACK]
    rk_pack = _pack_rows([P[n][0] for n in RKPACK])
    gathered = exchange(
        [mod_loc, P['w_in'][0].astype(BF16), P['w_out'][0].astype(BF16), P['ffn_w1'][0].astype(BF16),
         P['ffn_w2'][0].astype(BF16), P['s5_w_glu'][0].astype(BF16), rk_pack],
        ['gather'] * 7, "comm_gather_weights")
    mod_all, w_in_g, w_out_g, w1_g, w2_g, wglu_g, rk_g = gathered
    mod_me = lax.dynamic_index_in_dim(mod_all, me, axis=1, keepdims=False).reshape(N_MOD, 1, D)
    shift1, scale1, gate1, shift2, scale2, gate2 = (mod_me[i] for i in range(N_MOD))
    w_in = jnp.pad(_cols_from_slots(w_in_g), ((0, 0), (0, PROJP - PROJ)))
    w_out = w_out_g.reshape(D, D)
    w2 = w2_g.reshape(FF, D)
    wglu = wglu_g.reshape(S5W, S5W)
    rk_full = _unpack_rows(_cols_from_slots(rk_g), [s[:-1] + (C,) for s in rk_shapes])
    rk_w0, rk_a0, rk_wup, rk_aup, rk_gup = rk_full

    def lora_pad(up):
        z = jnp.zeros((N_DIR, LW, C), F32)
        for d in range(N_DIR):
            z = z.at[d, d * DECAY_LORA:(d + 1) * DECAY_LORA].set(up[d])
        return z

    wup_p, aup_p = lora_pad(rk_wup), lora_pad(rk_aup)
    gup_p = jnp.pad(rk_gup, ((0, GP - GATE_LORA), (0, 0)))
    mu_prev = jnp.pad(P['rk_shift_prev'], ((0, 0), (0, RINP - RIN)))
    mu_next = jnp.pad(P['rk_shift_next'], ((0, 0), (0, RINP - RIN)))
    r_k = P['rk_r_k'].reshape(1, C)
    fgain = P['final_gain'].reshape(1, D)

    TT = 256
    (h1,), _ = rowcall(lambda xv, g, sc, sh: ((_normmod(xv, g, sc, sh),), ()), "norm1",
                       [x], [P['norm1_gain'], scale1, shift1], [(D, BF16)], [], TT)
    proj = matmul(h1, w_in, name="proj_mm")

    lre = P['s5_lambda_re'][0].reshape(N_DIR, NS)
    lim = P['s5_lambda_im'][0].reshape(N_DIR, NS)
    lstep = jnp.broadcast_to(P['s5_log_step'][0][:, :, None], (N_DIR, G, S5_STATE)).reshape(N_DIR, NS)
    bre = P['s5_b_re'][0].reshape(NS, S5_GROUP).T
    bim = P['s5_b_im'][0].reshape(NS, S5_GROUP).T
    lbr, lbi, bbr, bbi = s5_prep(lre, lim, lstep, bre, bim)
    bbar = jnp.stack([bbr, bbi], axis=1).reshape(N_DIR, 2, S5_GROUP, NCH, S5_CHUNK_GROUPS, S5_STATE)
    wblk = jnp.einsum('drhcgp,gk->cghdrkp', bbar, eye).reshape(NCH, S5_CHUNK_GROUPS * S5_GROUP, 4 * SW)
    wblk = wblk.astype(MXU_DTYPE)
    u_view = (proj, S5W, 0)
    bus, _ = rowcall(lambda uv, wb: (_s5_bu_fn(uv, wb), ()), "s5_bu", [u_view], [wblk], [(NS, F32)] * 4, [], TT)
    s0r, s0i = s5_scan(bus[0], bus[1], lbr[0:1], lbi[0:1], reverse=False, name="s5_scan_f0")
    s1r, s1i = s5_scan(bus[2], bus[3], lbr[1:2], lbi[1:2], reverse=True, name="s5_scan_f1")

    def cblk(cm):
        c4 = cm.reshape(NCH, S5_CHUNK_GROUPS, S5_GROUP, S5_STATE)
        return jnp.einsum('cghp,gk->cgpkh', c4, eye).reshape(NCH, SW, S5_CHUNK_GROUPS * S5_GROUP)

    cre_b = cblk(P['s5_c_re'][0]).astype(MXU_DTYPE)
    cim_b = cblk(P['s5_c_im'][0]).astype(MXU_DTYPE)
    s5_full = [cre_b, cim_b, P['s5_d'], wglu, P['s5_b_glu']]
    TS = 128
    (y_s5,), _ = rowcall(lambda *a: ((_s5_out_fn(*a),), ()), "s5_out", [s0r, s0i, s1r, s1i, u_view], s5_full,
                         [(S5W, BF16)], [], TS)

    ps = rk_shift(proj, mu_prev, mu_next, S5W)
    pre_full = [rk_w0, wup_p[0], wup_p[1], rk_a0, aup_p[0], aup_p[1], gup_p, P['rk_k_k'], P['rk_k_a']]
    pre_out, _ = rowcall(lambda *a: (_rk_pre_fn(*a)[2:], ()), "rk_pre", [ps], pre_full, [(C, F32)] * 8, [], TS)
    kkn, w_0, w_1, k_0, k_1, b_0, b_1, g_gate = pre_out
    r_t, v_t = ps[:, 0:C], ps[:, 2 * C:3 * C]

    def hm(a):
        return a.reshape(T, H, RWKV_HEAD).transpose(1, 0, 2)

    def hmT(a):
        return a.reshape(T, H, RWKV_HEAD).transpose(1, 2, 0)

    def unT(a):
        return a.transpose(2, 0, 1).reshape(T, C)

    def unhm(a):
        return a.transpose(1, 0, 2).reshape(T, C)

    r_h, kk_h, vT_h = hm(r_t), hm(kkn), hmT(v_t)
    dir_rows = [(hm(w_0), hm(k_0), hm(b_0)), (hm(w_1), hm(k_1), hm(b_1))]
    yT, hist = [], []
    for d in range(N_DIR):
        wd, kd, bd = dir_rows[d]
        yd, hd = rk_scan(r_h, kk_h, wd, kd, bd, vT_h, reverse=(d == 1), name=f"rk_scan_f{d}")
        yT.append(yd)
        hist.append(hd)
    y_0, y_1 = unT(yT[0]), unT(yT[1])
    post_full = [r_k, P['rk_ln_gain'], P['rk_ln_bias']]
    post_tiled = [y_0, y_1, (ps, C, 0), (ps, C, 2), k_0, k_1, g_gate]
    (y_rk,), _ = rowcall(lambda *a: ((_rk_post_fn(*a),), ()), "rk_post", post_tiled, post_full, [(C, BF16)], [], TS)

    ycat = jnp.concatenate([y_s5, y_rk], axis=1)
    mixed = matmul(ycat, w_out, name="out_mm")

    def res_norm(xv, mv, gate, g, sc, sh):
        x1v = xv + gate * mv
        return x1v, _normmod(x1v, g, sc, sh)

    (x1, h2), _ = rowcall(lambda *a: (res_norm(*a), ()), "norm2", [x, mixed], [gate1, P['norm2_gain'], scale2, shift2],
                          [(D, F32), (D, BF16)], [], TT)
    a_ff, hh_ff = matmul(h2, w1_g, name="ffn1_mm", b_slots=True, out_dtypes=(F32, BF16),
                         epi=lambda acc: (acc, jnp.square(jnp.maximum(acc, 0.0))))
    ffn = matmul(hh_ff, w2, name="ffn2_mm")

    def loss_fn(x1v, fv, tg, gate, fg):
        def f(x1_, f_, gate_, fg_):
            out = _rms(x1_ + gate_ * f_, fg_)
            err = out - tg
            return 0.5 * jnp.sum(jnp.sum(err * err, axis=1, keepdims=True), axis=0, keepdims=True) * (1.0 / D)
        lv, vjp = jax.vjp(f, x1v, fv, gate, fg)
        dx1, dff, dgate, dfg = vjp(jnp.ones((1, 1), F32))
        return (dx1, dff), (jnp.broadcast_to(lv, (SUBLANE, LANE)), dgate, dfg)

    (dx2, dffn), (loss_t, dgate2, dfgain) = rowcall(
        loss_fn, "loss", [x1, ffn, target], [gate2, fgain], [(D, F32), (D, BF16)], [(SUBLANE, LANE), (1, D), (1, D)], TT)
    loss = lax.psum(loss_t[0, 0], MESH_AXES)

    da = matmul(dffn, w2, name="dffn2_mm", tb=True, out_dtypes=(BF16,), extras=[(a_ff, 'mn')],
                epi=lambda acc, av: (acc * (2.0 * jnp.maximum(av, 0.0)),))
    g_w2 = matmul(hh_ff, dffn, name="gw2_mm", ta=True, out_dtypes=(BF16,))
    dh2 = matmul(da, w1_g, name="dh2_mm", tb=True, b_slots=True)
    g_w1 = matmul(h2, da, name="gw1_mm", ta=True, out_slots=N_DEV, out_dtypes=(BF16,))

    def res_norm_bwd(dx2v, dh2v, xv, mv, gate, g, sc, sh):
        _, vjp = jax.vjp(res_norm, xv, mv, gate, g, sc, sh)
        dx, dm, dgate, dg, dsc, dsh = vjp((dx2v, dh2v))
        return (dx, dm), (dgate, dg, dsc, dsh)

    (dx1, dmixed), (dgate1, dgain2, dscale2, dshift2) = rowcall(
        res_norm_bwd, "norm2_bwd", [dx2, dh2, x, mixed], [gate1, P['norm2_gain'], scale2, shift2],
        [(D, F32), (D, BF16)], [(1, D)] * 4, TT)

    dycat = matmul(dmixed, w_out, name="dycat_mm", tb=True)
    g_wout = matmul(ycat, dmixed, name="gwout_mm", ta=True, out_dtypes=(BF16,))

    def post_bwd(dy, *a):
        _, vjp = jax.vjp(_rk_post_fn, *a)
        gy0, gy1, gr, gv, gk0, gk1, gg, grk, glg, glb = vjp(dy)
        return (gy0, gr, gv, gk0, gk1, gg), (grk, glg, glb)

    cb_rk = S5W // C if C else 0
    (dy_rk, dr_p, dv_p, dk0_p, dk1_p, dg_p), (g_rk_rk, g_lng, g_lnb) = rowcall(
        post_bwd, "rk_post_bwd", [(dycat, C, cb_rk)] + post_tiled, post_full, [(C, F32)] * 6, [(1, C)] * 3, TS)
    dyT_h = hmT(dy_rk)
    scan_g = []
    for d in range(N_DIR):
        wd, kd, bd = dir_rows[d]
        scan_g.append(rk_scan_bwd(r_h, kk_h, wd, kd, bd, vT_h, dyT_h, hist[d], reverse=(d == 1), name=f"rk_scan_b{d}"))
    cot = [dr_p, unhm(scan_g[0][0]), unhm(scan_g[1][0]),
           dv_p, unT(scan_g[0][5]), unT(scan_g[1][5]),
           unhm(scan_g[0][1]), unhm(scan_g[1][1]),
           unhm(scan_g[0][2]), unhm(scan_g[1][2]),
           dk0_p, unhm(scan_g[0][3]), dk1_p, unhm(scan_g[1][3]),
           unhm(scan_g[0][4]), unhm(scan_g[1][4]),
           dg_p]

    def pre_bwd(psv, r0, r1, r2, v0, v1, v2, q0, q1, dw0, dw1, k0a, k0b, k1a, k1b, db0, db1, dgv, *params):
        _, vjp = jax.vjp(_rk_pre_fn, psv, *params)
        grads = vjp((r0 + r1 + r2, v0 + v1 + v2, q0 + q1, dw0, dw1, k0a + k0b, k1a + k1b, db0, db1, dgv))
        return (grads[0],), tuple(grads[1:])

    (dps,), pre_g = rowcall(pre_bwd, "rk_pre_bwd", [ps] + cot, pre_full, [(RINP, F32)],
                            [f.shape for f in pre_full], TS)
    g_w0, g_wup0, g_wup1, g_a0, g_aup0, g_aup1, g_gup_p, g_kk, g_ka = pre_g
    g_wup_p, g_aup_p = jnp.stack([g_wup0, g_wup1]), jnp.stack([g_aup0, g_aup1])
    dp_rk, g_mup, g_mun = rk_shift_bwd(dps, proj, mu_prev, mu_next, S5W)

    def s5_out_bwd(dy, *a):
        a = [t.astype(F32) for t in a]
        _, vjp = jax.vjp(_s5_out_fn, *a)
        g = vjp(dy)
        return (g[0], g[1], g[4]), tuple(g[5:])

    (dxr, dxi, du_a), s5_pg = rowcall(
        s5_out_bwd, "s5_out_bwd", [(dycat, S5W, 0), s0r, s0i, s1r, s1i, u_view], s5_full,
        [(NS, F32), (NS, F32), (S5W, F32)], [f.shape for f in s5_full], TS)
    g_creb, g_cimb, g_s5d, g_wglu, g_bglu = s5_pg
    l0r, l0i = s5_scan(dxr, dxi, lbr[0:1], -lbi[0:1], reverse=True, name="s5_scan_b0")
    l1r, l1i = s5_scan(dxr, dxi, lbr[1:2], -lbi[1:2], reverse=False, name="s5_scan_b1")
    dl0r, dl0i = s5_dlam(s0r, s0i, l0r, l0i, reverse=False, name="s5_dlam0")
    dl1r, dl1i = s5_dlam(s1r, s1i, l1r, l1i, reverse=True, name="s5_dlam1")

    def bu_bwd(uv, g0, g1, g2, g3, wb):
        _, vjp = jax.vjp(_s5_bu_fn, uv, wb.astype(F32))
        du, dwb = vjp((g0, g1, g2, g3))
        return (du,), (dwb,)

    (du_b,), (g_wblk,) = rowcall(bu_bwd, "s5_bu_bwd", [u_view, l0r, l0i, l1r, l1i], [wblk], [(S5W, F32)],
                                 [wblk.shape], TS)
    g_bbar = jnp.einsum('cghdrkp,gk->drhcgp',
                        g_wblk.reshape(NCH, S5_CHUNK_GROUPS, S5_GROUP, N_DIR, 2, S5_CHUNK_GROUPS, S5_STATE), eye)
    g_bbar = g_bbar.reshape(N_DIR, 2, S5_GROUP, NS)
    g_lre, g_lim, g_lstep, g_bre, g_bim = s5_prep_bwd(
        lre, lim, lstep, bre, bim, jnp.concatenate([dl0r, dl1r], 0), jnp.concatenate([dl0i, dl1i], 0),
        g_bbar[:, 0], g_bbar[:, 1])

    def uncblk(gb):
        g5 = gb.reshape(NCH, S5_CHUNK_GROUPS, S5_STATE, S5_CHUNK_GROUPS, S5_GROUP)
        return jnp.einsum('cgpkh,gk->cghp', g5, eye).reshape(G, S5_GROUP, S5_STATE)

    (du_tot,), _ = rowcall(lambda a, b_: ((a + b_,), ()), "s5_du_sum", [du_a, du_b], [], [(S5W, BF16)], [], TT)
    dproj = jnp.concatenate([du_tot, dp_rk.astype(BF16)], axis=1)
    dh1 = matmul(dproj, w_in, name="dh1_mm", tb=True)
    g_win = matmul(h1, dproj, name="gwin_mm", ta=True, out_dtypes=(BF16,))

    def norm1_bwd(dx1v, dh1v, xv, g, sc, sh):
        _, vjp = jax.vjp(_normmod, xv, g, sc, sh)
        dx, dg, dsc, dsh = vjp(dh1v)
        return (dx1v + dx,), (dg, dsc, dsh)

    (grad_x,), (dgain1, dscale1, dshift1) = rowcall(
        norm1_bwd, "norm1_bwd", [dx1, dh1, x], [P['norm1_gain'], scale1, shift1], [(D, F32)], [(1, D)] * 3, TT)

    dmod = jnp.concatenate([dshift1, dscale1, dgate1, dshift2, dscale2, dgate2], axis=1)
    lstep_g = g_lstep.reshape(N_DIR, G, S5_STATE)
    small_g = {
        'ada_b': dmod, 'norm1_gain': dgain1, 'norm2_gain': dgain2, 'final_gain': dfgain.reshape(D),
        's5_lambda_re': g_lre.reshape(1, N_DIR, G, S5_STATE), 's5_lambda_im': g_lim.reshape(1, N_DIR, G, S5_STATE),
        's5_log_step': lstep_g,
        's5_b_re': g_bre.T.reshape(1, G, S5_STATE, S5_GROUP), 's5_b_im': g_bim.T.reshape(1, G, S5_STATE, S5_GROUP),
        's5_c_re': uncblk(g_creb)[None], 's5_c_im': uncblk(g_cimb)[None],
        's5_d': g_s5d, 's5_b_glu': g_bglu,
        'rk_shift_prev': g_mup[:, :RIN], 'rk_shift_next': g_mun[:, :RIN],
        'rk_k_k': g_kk, 'rk_k_a': g_ka, 'rk_r_k': g_rk_rk.reshape(1, H, RWKV_HEAD),
        'rk_ln_gain': g_lng, 'rk_ln_bias': g_lnb,
    }
    small_shapes = {n: P[n].shape for n in SMALL}
    small_shapes['s5_log_step'] = (N_DIR, G, S5_STATE)
    small_pack = _pack_flat([small_g[n] for n in SMALL])

    def lora_unpad(gp):
        return jnp.stack([gp[d, d * DECAY_LORA:(d + 1) * DECAY_LORA] for d in range(N_DIR)])

    rk_grads = {'rk_w0': g_w0, 'rk_a0': g_a0, 'rk_w_up': lora_unpad(g_wup_p), 'rk_a_up': lora_unpad(g_aup_p),
                'rk_g_up': g_gup_p[:GATE_LORA]}
    rk_gpack = jnp.stack([_pack_rows([_cols_to_slots(rk_grads[n])[j] for n in RKPACK]) for j in range(N_DEV)])
    g_win_s = _cols_to_slots(g_win[:, :PROJ])
    ex = exchange(
        [small_pack, g_win_s, g_wout.reshape(N_DEV, D // N_DEV, D), g_w1, g_w2.reshape(N_DEV, FF // N_DEV, D),
         g_wglu.reshape(N_DEV, S5W // N_DEV, S5W), rk_gpack],
        ['gather'] + ['scatter'] * 6, "comm_grads")
    small_all, win_parts, wout_parts, w1_parts, w2_parts, wglu_parts, rk_parts = ex

    res = {}

    def put(name, g, dl, m2, v2):
        shp = P[name].shape
        res[name] = tuple(t.reshape(shp) for t in (g, dl, m2, v2))

    def adam2d(name, parts):
        shp = P[name].shape
        r2 = (math.prod(shp[:-1]), shp[-1])
        put(name, *adam(parts.reshape((parts.shape[0],) + r2), P[name].reshape(r2), M[name].reshape(r2),
                        V[name].reshape(r2), "adam_" + name))

    adam2d('w_in', win_parts)
    adam2d('w_out', wout_parts)
    adam2d('ffn_w1', w1_parts)
    adam2d('ffn_w2', w2_parts)
    adam2d('s5_w_glu', wglu_parts)
    off = 0
    for n in SMALL:
        if n == 'ada_b':
            break
        off += _round_up(math.prod(small_shapes[n]), SUBLANE * LANE) // LANE
    nrow_b = N_MOD * D // LANE
    dmod_all = small_all[:, off:off + nrow_b].reshape(N_DEV, N_MOD * D)
    dmod_cols = lax.dynamic_slice(dmod_all, (0, me * cs_mod), (N_DEV, cs_mod))
    g_adaw = matmul(c_act, dmod_cols, name="gadaw_mm", ta=True, precise=True)
    adam2d('ada_w', g_adaw[None])
    small_w = dict(P)
    small_m, small_v = dict(M), dict(V)
    rk_res = adam(rk_parts, rk_pack, _pack_rows([M[n][0] for n in RKPACK]), _pack_rows([V[n][0] for n in RKPACK]),
                  "adam_rkpack")
    for name, parts4 in zip(RKPACK, zip(*[_unpack_rows(t, rk_shapes) for t in rk_res])):
        put(name, *parts4)
    return loss, grad_x, res, (small_all, small_shapes, small_w, small_m, small_v)


def _small_update(small_all, small_shapes, P, M, V, res):
    G = (D_MODEL // 2) // S5_GROUP
    names = [n for n in SMALL if n != 's5_log_step']
    shapes = [small_shapes[n] for n in SMALL]
    parts = _unpack_flat_batched(small_all, shapes)
    by = dict(zip(SMALL, parts))
    ls = by['s5_log_step']
    ls = ls.transpose(0, 3, 1, 2).reshape(N_DEV * S5_STATE, N_DIR * G)
    pk = lambda d: _pack_flat([d[n] for n in names])
    packs = jnp.stack([_pack_flat([by[n][j] for n in names]) for j in range(N_DEV)])
    out = adam(packs, pk(P), pk(M), pk(V), "adam_small")
    shp = [P[n].shape for n in names]
    for name, parts4 in zip(names, zip(*[_unpack_flat(t, shp) for t in out])):
        res[name] = parts4
    lsw = lambda d: jnp.pad(d['s5_log_step'].reshape(1, N_DIR * G), ((0, SUBLANE - 1), (0, 0)))
    ls_parts = jnp.pad(ls[:, None, :], ((0, 0), (0, SUBLANE - 1), (0, 0)))
    o = adam(ls_parts, lsw(P), lsw(M), lsw(V), "adam_log_step")
    res['s5_log_step'] = tuple(t[0:1].reshape(P['s5_log_step'].shape) for t in o)


def _unpack_flat_batched(packed, shapes):
    out, r0 = [], 0
    B = packed.shape[0]
    for s in shapes:
        n = math.prod(s)
        nr = _round_up(n, SUBLANE * LANE) // LANE
        out.append(packed[:, r0:r0 + nr].reshape(B, -1)[:, :n].reshape((B,) + tuple(s)))
        r0 += nr
    return out


def kernel(x, c, ada_w, ada_b, norm1_gain, norm2_gain, final_gain, w_in, w_out, s5_lambda_re, s5_lambda_im, s5_log_step, s5_b_re, s5_b_im, s5_c_re, s5_c_im, s5_d, s5_w_glu, s5_b_glu, rk_shift_prev, rk_shift_next, rk_w0, rk_w_up, rk_a0, rk_a_up, rk_g_up, rk_k_k, rk_k_a, rk_r_k, rk_ln_gain, rk_ln_bias, ffn_w1, ffn_w2, loss_target, m_ada_w, m_ada_b, m_norm1_gain, m_norm2_gain, m_final_gain, m_w_in, m_w_out, m_s5_lambda_re, m_s5_lambda_im, m_s5_log_step, m_s5_b_re, m_s5_b_im, m_s5_c_re, m_s5_c_im, m_s5_d, m_s5_w_glu, m_s5_b_glu, m_rk_shift_prev, m_rk_shift_next, m_rk_w0, m_rk_w_up, m_rk_a0, m_rk_a_up, m_rk_g_up, m_rk_k_k, m_rk_k_a, m_rk_r_k, m_rk_ln_gain, m_rk_ln_bias, m_ffn_w1, m_ffn_w2, v_ada_w, v_ada_b, v_norm1_gain, v_norm2_gain, v_final_gain, v_w_in, v_w_out, v_s5_lambda_re, v_s5_lambda_im, v_s5_log_step, v_s5_b_re, v_s5_b_im, v_s5_c_re, v_s5_c_im, v_s5_d, v_s5_w_glu, v_s5_b_glu, v_rk_shift_prev, v_rk_shift_next, v_rk_w0, v_rk_w_up, v_rk_a0, v_rk_a_up, v_rk_g_up, v_rk_k_k, v_rk_k_a, v_rk_r_k, v_rk_ln_gain, v_rk_ln_bias, v_ffn_w1, v_ffn_w2):
    given = dict(locals())
    P = {n: given[n] for n in ['x', 'c', 'loss_target'] + WEIGHTS}
    M = {n: given['m_' + n] for n in WEIGHTS}
    V = {n: given['v_' + n] for n in WEIGHTS}
    loss, grad_x, res, small = _step(P, M, V)
    small_all, small_shapes, _, _, _ = small
    _small_update(small_all, small_shapes, P, M, V, res)
    outs = [loss, grad_x[None]]
    for q in range(4):
        outs += [res[n][q] for n in WEIGHTS]
    return tuple(outs)
```

```python
import functools
import math

import jax
import jax.numpy as jnp
from jax import lax
from jax.experimental import pallas as pl
from jax.experimental.pallas import tpu as pltpu

F32 = jnp.float32
BF16 = jnp.bfloat16
HI = lax.Precision.HIGHEST
MXU_DTYPE = jnp.bfloat16

N_DEV = 8
MESH_AXES = ("x", "y", "c")
D_MODEL = 2048
SEQ = 2048
S5_GROUP = 16
S5_STATE = 64
RWKV_HEAD = 64
DECAY_LORA = 64
GATE_LORA = 160
N_DIR = 2
N_MOD = 6
NORM_EPS = 1e-6
GN_EPS = 64e-5
L2_EPS = 1e-12
ADAM_LR = 0.001
ADAM_B1 = 0.9
ADAM_B2 = 0.999
ADAM_EPS = 1e-08
ADAM_WD = 0.01
ADAM_STEP = 10
LANE = 128
SUBLANE = 8
S5_CHUNK_GROUPS = 8
S5_SCAN_ROWS = 256
VMEM_LIMIT = 56 * 1024 * 1024

WEIGHTS = ['ada_w', 'ada_b', 'norm1_gain', 'norm2_gain', 'final_gain', 'w_in', 'w_out', 's5_lambda_re',
           's5_lambda_im', 's5_log_step', 's5_b_re', 's5_b_im', 's5_c_re', 's5_c_im', 's5_d', 's5_w_glu',
           's5_b_glu', 'rk_shift_prev', 'rk_shift_next', 'rk_w0', 'rk_w_up', 'rk_a0', 'rk_a_up', 'rk_g_up',
           'rk_k_k', 'rk_k_a', 'rk_r_k', 'rk_ln_gain', 'rk_ln_bias', 'ffn_w1', 'ffn_w2']
SMALL = ['ada_b', 'norm1_gain', 'norm2_gain', 'final_gain', 's5_lambda_re', 's5_lambda_im', 's5_log_step',
         's5_b_re', 's5_b_im', 's5_c_re', 's5_c_im', 's5_d', 's5_b_glu', 'rk_shift_prev', 'rk_shift_next',
         'rk_k_k', 'rk_k_a', 'rk_r_k', 'rk_ln_gain', 'rk_ln_bias']
RKPACK = ['rk_w0', 'rk_a0', 'rk_w_up', 'rk_a_up', 'rk_g_up']


def _round_up(n, m):
    return (n + m - 1) // m * m


def _tile(dim, pref, unit=LANE):
    t = min(pref, dim) // unit * unit
    while t >= unit:
        if dim % t == 0:
            return t
        t -= unit
    return dim


def _cparams(sem=None):
    return pltpu.CompilerParams(dimension_semantics=sem, vmem_limit_bytes=VMEM_LIMIT)


def _full_spec(a):
    nd = a.ndim
    return pl.BlockSpec(a.shape, lambda *_: (0,) * nd)


def exchange(arrs, modes, name):
    n = len(arrs)
    out_shape = [jax.ShapeDtypeStruct((N_DEV,) + a.shape if m == 'gather' else a.shape, a.dtype)
                 for a, m in zip(arrs, modes)]

    def body(*refs):
        ins, outs = refs[:n], refs[n:2 * n]
        send_sems, recv_sems, local_sems = refs[2 * n:]
        x, y, c = (lax.axis_index(a) for a in MESH_AXES)
        me = 4 * x + 2 * y + c
        copies = []
        for i in range(n):
            gather = modes[i] == 'gather'
            mine = pltpu.make_async_copy(ins[i] if gather else ins[i].at[me], outs[i].at[me], local_sems.at[i])
            mine.start()
            copies.append(mine)
        remote = []
        for k in range(1, N_DEV):
            px = 1 - x if (k >> 2) & 1 else x
            py = 1 - y if (k >> 1) & 1 else y
            pc = 1 - c if k & 1 else c
            peer = 4 * px + 2 * py + pc
            for i in range(n):
                src = ins[i] if modes[i] == 'gather' else ins[i].at[peer]
                cp = pltpu.make_async_remote_copy(
                    src_ref=src, dst_ref=outs[i].at[me], send_sem=send_sems.at[i, k - 1],
                    recv_sem=recv_sems.at[i, k - 1], device_id=(px, py, pc), device_id_type=pl.DeviceIdType.MESH)
                cp.start()
                remote.append(cp)
        for cp in remote:
            cp.wait_recv()
        for cp in remote:
            cp.wait_send()
        for cp in copies:
            cp.wait()

    any_spec = pl.BlockSpec(memory_space=pl.ANY)
    return pl.pallas_call(
        body, name=name, out_shape=out_shape,
        in_specs=[any_spec] * n, out_specs=[any_spec] * n,
        scratch_shapes=[pltpu.SemaphoreType.DMA((n, N_DEV - 1)), pltpu.SemaphoreType.DMA((n, N_DEV - 1)),
                        pltpu.SemaphoreType.DMA((n,))],
        compiler_params=pltpu.CompilerParams(has_side_effects=True),
    )(*arrs)


def matmul(a, b, *, name, ta=False, tb=False, b_slots=False, out_slots=0, out_dtypes=(F32,), epi=None,
           extras=(), precise=False, tm=512, tn=512, tk=2048):
    if ta:
        K, M = a.shape
    else:
        M, K = a.shape
    if b_slots:
        ns, br, bc = b.shape
        bshape = (br, ns * bc)
    else:
        bshape = b.shape
    N = bshape[0] if tb else bshape[1]
    assert (bshape[1] if tb else bshape[0]) == K, (a.shape, b.shape, ta, tb)
    tm, tn, tk = _tile(M, tm, SUBLANE), _tile(N, tn), _tile(K, tk, SUBLANE if K < LANE else LANE)
    if b_slots and tb:
        tk = _tile(b.shape[2], tk)
    elif b_slots:
        tn = _tile(b.shape[2], tn)
    if out_slots:
        tn = _tile(N // out_slots, tn)
    if b_slots:
        cs = b.shape[2]
        tcol = tk if tb else tn
        assert cs % tcol == 0
        per = cs // tcol
    if out_slots:
        ncs = N // out_slots
        assert ncs % tn == 0
        operc = ncs // tn
    nk = K // tk
    a_spec = pl.BlockSpec((tk, tm), lambda i, j, k: (k, i)) if ta else pl.BlockSpec((tm, tk), lambda i, j, k: (i, k))
    if b_slots:
        if tb:
            b_spec = pl.BlockSpec((None, tn, tk), lambda i, j, k: (k // per, j, k % per))
        else:
            b_spec = pl.BlockSpec((None, tk, tn), lambda i, j, k: (j // per, k, j % per))
    else:
        b_spec = pl.BlockSpec((tn, tk), lambda i, j, k: (j, k)) if tb else pl.BlockSpec((tk, tn), lambda i, j, k: (k, j))
    ex_specs = []
    for arr, kind in extras:
        if kind == 'mn':
            ex_specs.append(pl.BlockSpec((tm, tn), lambda i, j, k: (i, j)))
        else:
            ex_specs.append(pl.BlockSpec((1, tn), lambda i, j, k: (0, j)))
    if out_slots:
        o_spec = pl.BlockSpec((None, tm, tn), lambda i, j, k: (j // operc, i, j % operc))
        o_shape = (out_slots, M, ncs)
    else:
        o_spec = pl.BlockSpec((tm, tn), lambda i, j, k: (i, j))
        o_shape = (M, N)
    ne, no = len(extras), len(out_dtypes)
    dims = (((0 if ta else 1,), (1 if tb else 0,)), ((), ()))
    op_dtype = F32 if precise else MXU_DTYPE

    def body(a_ref, b_ref, *rest):
        ex_refs, out_refs, acc = rest[:ne], rest[ne:ne + no], rest[-1]
        k = pl.program_id(2)
        part = lax.dot_general(a_ref[...].astype(op_dtype), b_ref[...].astype(op_dtype), dims,
                               precision=HI if precise else None, preferred_element_type=F32)

        def finish(total):
            res = epi(total, *[e[...] for e in ex_refs]) if epi is not None else (total,)
            for o, r in zip(out_refs, res):
                o[...] = r.astype(o.dtype)

        if nk == 1:
            finish(part)
        else:
            @pl.when(k == 0)
            def _():
                acc[...] = part

            @pl.when(jnp.logical_and(k > 0, k < nk - 1))
            def _():
                acc[...] += part

            @pl.when(k == nk - 1)
            def _():
                finish(acc[...] + part)

    outs = pl.pallas_call(
        body, name=name, grid=(M // tm, N // tn, nk),
        in_specs=[a_spec, b_spec] + ex_specs, out_specs=[o_spec] * no,
        out_shape=[jax.ShapeDtypeStruct(o_shape, dt) for dt in out_dtypes],
        scratch_shapes=[pltpu.VMEM((tm, tn), F32)],
        compiler_params=_cparams(("parallel", "parallel", "arbitrary")),
    )(a, b, *[e[0] for e in extras])
    return outs[0] if no == 1 else outs


def rowcall(fn, name, tiled, full, tiled_out, acc_out, tt):
    views = [(t, t.shape[1], 0) if not isinstance(t, tuple) else t for t in tiled]
    T = views[0][0].shape[0]
    tt = _tile(T, tt, SUBLANE)
    nt, nf, nto, nao = len(views), len(full), len(tiled_out), len(acc_out)

    def view_spec(w, cb):
        return pl.BlockSpec((tt, w), lambda i: (i, cb))

    in_specs = [view_spec(w, cb) for _, w, cb in views] + [_full_spec(f) for f in full]
    out_specs = [pl.BlockSpec((tt, w), lambda i: (i, 0)) for w, _ in tiled_out]
    out_specs += [pl.BlockSpec(s, lambda i, nd=len(s): (0,) * nd) for s in acc_out]
    out_shape = [jax.ShapeDtypeStruct((T, w), dt) for w, dt in tiled_out]
    out_shape += [jax.ShapeDtypeStruct(s, F32) for s in acc_out]

    def body(*refs):
        tin, fin = refs[:nt], refs[nt:nt + nf]
        tout, aout = refs[nt + nf:nt + nf + nto], refs[nt + nf + nto:]
        touts, aouts = fn(*[r[...] for r in tin], *[r[...] for r in fin])
        for r, v in zip(tout, touts):
            r[...] = v.astype(r.dtype)
        if nao:
            @pl.when(pl.program_id(0) == 0)
            def _():
                for r in aout:
                    r[...] = jnp.zeros_like(r)

            for r, v in zip(aout, aouts):
                r[...] += v.astype(F32)

    outs = pl.pallas_call(
        body, name=name, grid=(T // tt,), in_specs=in_specs, out_specs=out_specs, out_shape=out_shape,
        compiler_params=_cparams(("arbitrary",) if nao else ("parallel",)),
    )(*[v[0] for v in views], *full)
    return outs[:nto], outs[nto:]


def _mm(a, b):
    return jnp.dot(a.astype(MXU_DTYPE), b.astype(MXU_DTYPE), preferred_element_type=F32)


def _rms(x, gain):
    ms = jnp.mean(x * x, axis=-1, keepdims=True)
    return x * lax.rsqrt(ms + NORM_EPS) * gain


def _normmod(x, gain, scale, shift):
    return _rms(x, gain) * (1.0 + scale) + shift


def _gelu_tanh(y):
    return 0.5 * y * (1.0 + jnp.tanh(math.sqrt(2.0 / math.pi) * (y + 0.044715 * (y * y * y))))


def _sigmoid(x):
    return 1.0 / (1.0 + jnp.exp(-x))


def _softplus(x):
    return jnp.maximum(x, 0.0) + jnp.log(1.0 + jnp.exp(-jnp.abs(x)))


def _seg_mats(width, seg):
    r = lax.broadcasted_iota(jnp.int32, (width, LANE), 0) // seg
    c = lax.broadcasted_iota(jnp.int32, (width, LANE), 1)
    s = (r == c).astype(F32)
    rt = lax.broadcasted_iota(jnp.int32, (LANE, width), 0)
    ct = lax.broadcasted_iota(jnp.int32, (LANE, width), 1) // seg
    st = (rt == ct).astype(F32)
    return s, st


def _segsum(x, s):
    return jnp.dot(x, s, precision=HI, preferred_element_type=F32)


def _s5_prep_fn(lre, lim, lstep, bre, bim):
    step = jnp.exp(lstep)
    mag = jnp.exp(lre * step)
    lbr = mag * jnp.cos(lim * step)
    lbi = mag * jnp.sin(lim * step)
    den = lre * lre + lim * lim
    nr = lbr - 1.0
    ni = lbi
    cre = (nr * lre + ni * lim) / den
    cim = (ni * lre - nr * lim) / den
    bbr = jnp.stack([cre[d:d + 1] * bre - cim[d:d + 1] * bim for d in range(N_DIR)])
    bbi = jnp.stack([cre[d:d + 1] * bim + cim[d:d + 1] * bre for d in range(N_DIR)])
    return lbr, lbi, bbr, bbi


def s5_prep(lre, lim, lstep, bre, bim):
    ns = lre.shape[1]

    def body(lre_r, lim_r, ls_r, bre_r, bim_r, lbr_r, lbi_r, bbr_r, bbi_r):
        lbr, lbi, bbr, bbi = _s5_prep_fn(lre_r[...], lim_r[...], ls_r[...], bre_r[...], bim_r[...])
        lbr_r[...] = lbr
        lbi_r[...] = lbi
        bbr_r[...] = bbr
        bbi_r[...] = bbi

    return pl.pallas_call(
        body, name="s5_prep",
        out_shape=[jax.ShapeDtypeStruct((N_DIR, ns), F32)] * 2 + [jax.ShapeDtypeStruct((N_DIR, S5_GROUP, ns), F32)] * 2,
        compiler_params=_cparams(),
    )(lre, lim, lstep, bre, bim)


def s5_prep_bwd(lre, lim, lstep, bre, bim, dlbr, dlbi, dbbr, dbbi):
    ns = lre.shape[1]

    def body(lre_r, lim_r, ls_r, bre_r, bim_r, d1, d2, d3, d4, o1, o2, o3, o4, o5):
        _, vjp = jax.vjp(_s5_prep_fn, lre_r[...], lim_r[...], ls_r[...], bre_r[...], bim_r[...])
        g = vjp((d1[...], d2[...], d3[...], d4[...]))
        for o, v in zip((o1, o2, o3, o4, o5), g):
            o[...] = v

    return pl.pallas_call(
        body, name="s5_prep_bwd",
        out_shape=[jax.ShapeDtypeStruct((N_DIR, ns), F32)] * 3 + [jax.ShapeDtypeStruct((S5_GROUP, ns), F32)] * 2,
        compiler_params=_cparams(),
    )(lre, lim, lstep, bre, bim, dlbr, dlbi, dbbr, dbbi)


def s5_scan(bre, bim, lre, lim, *, reverse, name):
    T, NS = bre.shape
    tt = _tile(T, S5_SCAN_ROWS, SUBLANE)
    wl = _tile(NS, 512)
    nT = T // tt
    ngrp = tt // SUBLANE

    def tmap(j, i):
        return ((nT - 1 - i) if reverse else i, j)

    def body(bre_r, bim_r, lre_r, lim_r, sre_r, sim_r, cre, cim):
        @pl.when(pl.program_id(1) == 0)
        def _():
            cre[...] = jnp.zeros_like(cre)
            cim[...] = jnp.zeros_like(cim)

        lr = jnp.broadcast_to(lre_r[...], (SUBLANE, wl))
        li = jnp.broadcast_to(lim_r[...], (SUBLANE, wl))
        row = lax.broadcasted_iota(jnp.int32, (SUBLANE, wl), 0)
        pows = [(lr, li)]
        for _ in range(3):
            pr, pi = pows[-1]
            pows.append((pr * pr - pi * pi, 2.0 * pr * pi))
        e = (SUBLANE - row) if reverse else (row + 1)
        Pr = jnp.ones((SUBLANE, wl), F32)
        Pi = jnp.zeros((SUBLANE, wl), F32)
        for bit, (qr, qi) in enumerate(pows):
            on = ((e >> bit) & 1) == 1
            nr, ni = Pr * qr - Pi * qi, Pr * qi + Pi * qr
            Pr, Pi = jnp.where(on, nr, Pr), jnp.where(on, ni, Pi)

        def group(g, carry):
            gg = (ngrp - 1 - g) if reverse else g
            rows = pl.ds(pl.multiple_of(gg * SUBLANE, SUBLANE), SUBLANE)
            sr, si = bre_r[rows, :], bim_r[rows, :]
            for lvl, k in enumerate((1, 2, 4)):
                qr, qi = pows[lvl]
                if reverse:
                    shr = pltpu.roll(sr, SUBLANE - k, 0)
                    shi = pltpu.roll(si, SUBLANE - k, 0)
                    keep = row < SUBLANE - k
                else:
                    shr = pltpu.roll(sr, k, 0)
                    shi = pltpu.roll(si, k, 0)
                    keep = row >= k
                shr = jnp.where(keep, shr, 0.0)
                shi = jnp.where(keep, shi, 0.0)
                sr, si = sr + qr * shr - qi * shi, si + qr * shi + qi * shr
            cr, ci = cre[...], cim[...]
            sr, si = sr + Pr * cr - Pi * ci, si + Pr * ci + Pi * cr
            sre_r[rows, :] = sr
            sim_r[rows, :] = si
            last = 0 if reverse else SUBLANE - 1
            cre[...] = jnp.broadcast_to(sr[last:last + 1, :], (SUBLANE, wl))
            cim[...] = jnp.broadcast_to(si[last:last + 1, :], (SUBLANE, wl))
            return carry

        lax.fori_loop(0, ngrp, group, 0)

    blk = pl.BlockSpec((tt, wl), tmap)
    row_spec = pl.BlockSpec((1, wl), lambda j, i: (0, j))
    return pl.pallas_call(
        body, name=name, grid=(NS // wl, nT),
        in_specs=[blk, blk, row_spec, row_spec], out_specs=[blk, blk],
        out_shape=[jax.ShapeDtypeStruct((T, NS), F32)] * 2,
        scratch_shapes=[pltpu.VMEM((SUBLANE, wl), F32)] * 2,
        compiler_params=_cparams(("parallel", "arbitrary")),
    )(bre, bim, lre, lim)


def s5_dlam(sre, sim, gre, gim, *, reverse, name):
    T, NS = sre.shape
    wl = _tile(NS, 256)

    def body(sr_r, si_r, gr_r, gi_r, dr_r, di_r):
        row = lax.broadcasted_iota(jnp.int32, (T, wl), 0)
        if reverse:
            keep = row < T - 1
            pr = jnp.where(keep, pltpu.roll(sr_r[...], T - 1, 0), 0.0)
            pi = jnp.where(keep, pltpu.roll(si_r[...], T - 1, 0), 0.0)
        else:
            keep = row >= 1
            pr = jnp.where(keep, pltpu.roll(sr_r[...], 1, 0), 0.0)
            pi = jnp.where(keep, pltpu.roll(si_r[...], 1, 0), 0.0)
        gr, gi = gr_r[...], gi_r[...]
        dr_r[...] = jnp.sum(pr * gr + pi * gi, axis=0, keepdims=True)
        di_r[...] = jnp.sum(pr * gi - pi * gr, axis=0, keepdims=True)

    blk = pl.BlockSpec((T, wl), lambda j: (0, j))
    o = pl.BlockSpec((1, wl), lambda j: (0, j))
    return pl.pallas_call(
        body, name=name, grid=(NS // wl,), in_specs=[blk] * 4, out_specs=[o, o],
        out_shape=[jax.ShapeDtypeStruct((1, NS), F32)] * 2,
        compiler_params=_cparams(("parallel",)),
    )(sre, sim, gre, gim)


def _s5_bu_fn(u, wblk):
    nch = wblk.shape[0]
    cw = S5_CHUNK_GROUPS * S5_GROUP
    sw = S5_CHUNK_GROUPS * S5_STATE
    parts = [[] for _ in range(4)]
    for ch in range(nch):
        res = _mm(u[:, ch * cw:(ch + 1) * cw], wblk[ch])
        for q in range(4):
            parts[q].append(res[:, q * sw:(q + 1) * sw])
    return tuple(jnp.concatenate(p, axis=1) if nch > 1 else p[0] for p in parts)


def _s5_out_fn(x0r, x0i, x1r, x1i, u, cre, cim, dsk, wglu, bglu):
    xr, xi = x0r + x1r, x0i + x1i
    nch = cre.shape[0]
    sw = S5_CHUNK_GROUPS * S5_STATE
    ys = [_mm(xr[:, ch * sw:(ch + 1) * sw], cre[ch]) - _mm(xi[:, ch * sw:(ch + 1) * sw], cim[ch]) for ch in range(nch)]
    y = jnp.concatenate(ys, axis=1) if nch > 1 else ys[0]
    z = _gelu_tanh(y + dsk * u)
    gate = _sigmoid(_mm(z, wglu) + bglu)
    return z * gate


def _rk_dims():
    C = D_MODEL // 2
    LW = N_DIR * DECAY_LORA
    GP = _round_up(GATE_LORA, LANE)
    return C, LW, GP


def _rk_pre_fn(ps, w0, wup0, wup1, a0, aup0, aup1, gup, k_k, k_a):
    C, LW, GP = _rk_dims()
    r, k, v = ps[:, 0:C], ps[:, C:2 * C], ps[:, 2 * C:3 * C]
    wdn = ps[:, 3 * C:3 * C + LW]
    adn = ps[:, 3 * C + LW:3 * C + 2 * LW]
    gdn = ps[:, 3 * C + 2 * LW:3 * C + 2 * LW + GP]
    s, st = _seg_mats(C, RWKV_HEAD)
    kk = k * k_k
    n2 = _segsum(kk * kk, s)
    n2 = jnp.where(n2 > 0.0, n2, 1.0)
    inv = 1.0 / jnp.maximum(jnp.sqrt(n2), L2_EPS)
    kkn = kk * _segsum(inv, st)
    tw = jnp.tanh(wdn)
    wup, aup = (wup0, wup1), (aup0, aup1)
    ws, ks, bs = [], [], []
    for d in range(N_DIR):
        wraw = w0[d:d + 1] + _mm(tw, wup[d])
        w = -_softplus(-wraw) - 0.5
        ws.append(jnp.exp(-jnp.exp(w)))
        a = _sigmoid(a0[d:d + 1] + _mm(adn, aup[d]))
        ks.append(k * (1.0 + (a - 1.0) * k_a))
        bs.append(kkn * a)
    g = _mm(_sigmoid(gdn), gup)
    return r, v, kkn, ws[0], ws[1], ks[0], ks[1], bs[0], bs[1], g


def _rk_post_fn(y0, y1, r, v, k0, k1, g, r_k, lng, lnb):
    C = r.shape[1]
    s, st = _seg_mats(C, RWKV_HEAD)
    y = y0 + y1
    mu = _segsum(_segsum(y, s) * (1.0 / RWKV_HEAD), st)
    yc = y - mu
    var = _segsum(_segsum(yc * yc, s) * (1.0 / RWKV_HEAD), st)
    yn = yc * lax.rsqrt(var + GN_EPS) * lng + lnb
    bonus = _segsum(_segsum(r * (k0 + k1) * r_k, s), st)
    return (yn + bonus * v) * g


def rk_shift(proj, mp, mn, col0):
    T = proj.shape[0]
    W = mp.shape[1]
    wl = _tile(math.gcd(W, col0), 256)
    cb0 = col0 // wl

    def body(p_r, mp_r, mn_r, o_r):
        p = p_r[...]
        row = lax.broadcasted_iota(jnp.int32, (T, wl), 0)
        prev = jnp.where(row >= 1, pltpu.roll(p, 1, 0), 0.0)
        nxt = jnp.where(row < T - 1, pltpu.roll(p, T - 1, 0), 0.0)
        o_r[...] = p + mp_r[...] * (prev - p) + mn_r[...] * (nxt - p)

    rs = pl.BlockSpec((1, wl), lambda j: (0, j))
    return pl.pallas_call(
        body, name="rk_shift", grid=(W // wl,),
        in_specs=[pl.BlockSpec((T, wl), lambda j: (0, cb0 + j)), rs, rs],
        out_specs=pl.BlockSpec((T, wl), lambda j: (0, j)),
        out_shape=jax.ShapeDtypeStruct((T, W), F32),
        compiler_params=_cparams(("parallel",)),
    )(proj, mp, mn)


def rk_shift_bwd(dps, proj, mp, mn, col0):
    T, W = dps.shape
    wl = _tile(math.gcd(W, col0), 256)
    cb0 = col0 // wl

    def body(d_r, p_r, mp_r, mn_r, dp_r, dmp_r, dmn_r):
        d, p = d_r[...], p_r[...]
        mpv, mnv = mp_r[...], mn_r[...]
        row = lax.broadcasted_iota(jnp.int32, (T, wl), 0)
        first, last = row >= 1, row < T - 1
        prev = jnp.where(first, pltpu.roll(p, 1, 0), 0.0)
        nxt = jnp.where(last, pltpu.roll(p, T - 1, 0), 0.0)
        dmp_r[...] = jnp.sum(d * (prev - p), axis=0, keepdims=True)
        dmn_r[...] = jnp.sum(d * (nxt - p), axis=0, keepdims=True)
        dp_r[...] = (d * (1.0 - mpv - mnv) + jnp.where(last, pltpu.roll(d * mpv, T - 1, 0), 0.0)
                     + jnp.where(first, pltpu.roll(d * mnv, 1, 0), 0.0))

    rs = pl.BlockSpec((1, wl), lambda j: (0, j))
    blk = pl.BlockSpec((T, wl), lambda j: (0, j))
    return pl.pallas_call(
        body, name="rk_shift_bwd", grid=(W // wl,),
        in_specs=[blk, pl.BlockSpec((T, wl), lambda j: (0, cb0 + j)), rs, rs],
        out_specs=[blk, rs, rs],
        out_shape=[jax.ShapeDtypeStruct((T, W), F32), jax.ShapeDtypeStruct((1, W), F32), jax.ShapeDtypeStruct((1, W), F32)],
        compiler_params=_cparams(("parallel",)),
    )(dps, proj, mp, mn)


RK_FWD_HEADS = 8
RK_BWD_HEADS = 4
RK_TIME_BLOCK = 32
RK_LANE_BLOCK = 128


def _rk_blocks(H, T, N, order_reversed, heads):
    hh = min(heads, H)
    tb = min(RK_TIME_BLOCK, T)
    lb = min(RK_LANE_BLOCK, T)
    nb, per = T // tb, lb // tb

    def tix(i):
        return (nb - 1 - i) if order_reversed else i

    rows = pl.BlockSpec((hh, tb, N), lambda g, i: (g, tix(i), 0))
    cols = pl.BlockSpec((hh, N, lb), lambda g, i: (g, 0, tix(i) // per))
    hist = pl.BlockSpec((tb, hh, N, N), lambda g, i: (tix(i), g, 0, 0))
    return hh, tb, lb, nb, per, tix, rows, cols, hist


def _aligned(tile_ref, h, off, lb, per):
    return pltpu.roll(tile_ref[h], (lb - off) % lb, 1) if per > 1 else tile_ref[h]


def _spread_columns(tile_ref, out_ref, hh, tb, off, lb, per):
    N = out_ref.shape[-1]
    lane = lax.broadcasted_iota(jnp.int32, (N, lb), 1)
    for h in range(hh):
        tile = _aligned(tile_ref, h, off, lb, per)
        for t in range(tb):
            col = jnp.sum(jnp.where(lane == t, tile, 0.0), axis=1, keepdims=True)
            out_ref[t, h] = jnp.broadcast_to(col, (N, N))


def rk_scan(r, kk, w, k, b, vT, *, reverse, name):
    H, T, N = r.shape
    hh, tb, lb, nb, per, tix, rows, cols, hist_spec = _rk_blocks(H, T, N, reverse, RK_FWD_HEADS)

    def body(r_r, kk_r, w_r, k_r, b_r, vT_r, yT_r, hist_r, S, VC, YA):
        i = pl.program_id(1)

        @pl.when(i == 0)
        def _():
            S[...] = jnp.zeros_like(S)

        @pl.when(i % per == 0)
        def _():
            yT_r[...] = jnp.zeros_like(yT_r)

        off = (tix(i) % per) * tb
        YA[...] = jnp.zeros_like(YA)
        _spread_columns(vT_r, VC, hh, tb, off, lb, per)
        st = [S[h] for h in range(hh)]
        for s in range(tb):
            t = (tb - 1 - s) if reverse else s
            row = slice(t, t + 1)
            sas = [-jnp.sum(st[h] * kk_r[h, row, :], axis=1, keepdims=True) for h in range(hh)]
            for h in range(hh):
                hist_r[t, h] = st[h]
                st[h] = st[h] * w_r[h, row, :] + sas[h] * b_r[h, row, :] + VC[t, h] * k_r[h, row, :]
            ys = [jnp.sum(st[h] * r_r[h, row, :], axis=1, keepdims=True) for h in range(hh)]
            for h in range(hh):
                YA[h, :, row] = ys[h]
        for h in range(hh):
            S[h] = st[h]
            yT_r[h] = yT_r[h] + (pltpu.roll(YA[h], off, 1) if per > 1 else YA[h])

    return pl.pallas_call(
        body, name=name, grid=(H // hh, nb),
        in_specs=[rows] * 5 + [cols], out_specs=[cols, hist_spec],
        out_shape=[jax.ShapeDtypeStruct((H, N, T), F32), jax.ShapeDtypeStruct((T, H, N, N), F32)],
        scratch_shapes=[pltpu.VMEM((hh, N, N), F32), pltpu.VMEM((tb, hh, N, N), F32), pltpu.VMEM((hh, N, lb), F32)],
        compiler_params=_cparams(("parallel", "arbitrary")),
    )(r, kk, w, k, b, vT)


def rk_scan_bwd(r, kk, w, k, b, vT, dyT, hist, *, reverse, name):
    H, T, N = r.shape
    hh, tb, lb, nb, per, tix, rows, cols, hist_spec = _rk_blocks(H, T, N, not reverse, RK_BWD_HEADS)

    def body(r_r, kk_r, w_r, k_r, b_r, vT_r, dyT_r, hist_r, dr_r, dkk_r, dw_r, dk_r, db_r, dvT_r, G, VC, DC, YA):
        i = pl.program_id(1)

        @pl.when(i == 0)
        def _():
            G[...] = jnp.zeros_like(G)

        @pl.when(i % per == 0)
        def _():
            dvT_r[...] = jnp.zeros_like(dvT_r)

        off = (tix(i) % per) * tb
        YA[...] = jnp.zeros_like(YA)
        _spread_columns(vT_r, VC, hh, tb, off, lb, per)
        _spread_columns(dyT_r, DC, hh, tb, off, lb, per)
        gs = [G[h] for h in range(hh)]
        for s in range(tb):
            t = s if reverse else (tb - 1 - s)
            row = slice(t, t + 1)
            g = [gs[h] + DC[t, h] * r_r[h, row, :] for h in range(hh)]
            sa = [-jnp.sum(hist_r[t, h] * kk_r[h, row, :], axis=1, keepdims=True) for h in range(hh)]
            dsa = [jnp.sum(g[h] * b_r[h, row, :], axis=1, keepdims=True) for h in range(hh)]
            dvcol = [jnp.sum(g[h] * k_r[h, row, :], axis=1, keepdims=True) for h in range(hh)]
            for h in range(hh):
                sp = hist_r[t, h]
                kkv, wv, kv, bv = kk_r[h, row, :], w_r[h, row, :], k_r[h, row, :], b_r[h, row, :]
                vcol, dycol = VC[t, h], DC[t, h]
                sn = sp * wv + sa[h] * bv + vcol * kv
                dr_r[h, row, :] = jnp.sum(sn * dycol, axis=0, keepdims=True)
                dw_r[h, row, :] = jnp.sum(g[h] * sp, axis=0, keepdims=True)
                db_r[h, row, :] = jnp.sum(g[h] * sa[h], axis=0, keepdims=True)
                dk_r[h, row, :] = jnp.sum(g[h] * vcol, axis=0, keepdims=True)
                dkk_r[h, row, :] = -jnp.sum(sp * dsa[h], axis=0, keepdims=True)
                gs[h] = g[h] * wv - dsa[h] * kkv
                YA[h, :, row] = dvcol[h]
        for h in range(hh):
            G[h] = gs[h]
            dvT_r[h] = dvT_r[h] + (pltpu.roll(YA[h], off, 1) if per > 1 else YA[h])

    return pl.pallas_call(
        body, name=name, grid=(H // hh, nb),
        in_specs=[rows] * 5 + [cols, cols, hist_spec], out_specs=[rows] * 5 + [cols],
        out_shape=[jax.ShapeDtypeStruct((H, T, N), F32)] * 5 + [jax.ShapeDtypeStruct((H, N, T), F32)],
        scratch_shapes=[pltpu.VMEM((hh, N, N), F32), pltpu.VMEM((tb, hh, N, N), F32), pltpu.VMEM((tb, hh, N, N), F32),
                        pltpu.VMEM((hh, N, lb), F32)],
        compiler_params=_cparams(("parallel", "arbitrary")),
    )(r, kk, w, k, b, vT, dyT, hist)


def adam(parts, w, m, v, name):
    P, R, C = parts.shape
    tr = _tile(R, max(SUBLANE, (1 << 19) // max(C, 1) // SUBLANE * SUBLANE), SUBLANE)
    c1 = 1.0 / (1.0 - ADAM_B1 ** ADAM_STEP)
    c2 = 1.0 / (1.0 - ADAM_B2 ** ADAM_STEP)

    def body(p_r, w_r, m_r, v_r, g_o, d_o, m_o, v_o):
        g = p_r[0].astype(F32)
        for q in range(1, P):
            g = g + p_r[q].astype(F32)
        m2 = ADAM_B1 * m_r[...] + (1.0 - ADAM_B1) * g
        v2 = ADAM_B2 * v_r[...] + (1.0 - ADAM_B2) * (g * g)
        g_o[...] = g
        m_o[...] = m2
        v_o[...] = v2
        d_o[...] = -ADAM_LR * ((m2 * c1) / (jnp.sqrt(v2 * c2) + ADAM_EPS) + ADAM_WD * w_r[...])

    blk = pl.BlockSpec((tr, C), lambda i: (i, 0))
    return pl.pallas_call(
        body, name=name, grid=(R // tr,),
        in_specs=[pl.BlockSpec((P, tr, C), lambda i: (0, i, 0)), blk, blk, blk], out_specs=[blk] * 4,
        out_shape=[jax.ShapeDtypeStruct((R, C), F32)] * 4,
        compiler_params=_cparams(("parallel",)),
    )(parts, w, m, v)


def _pack_flat(arrs):
    rows = []
    for a in arrs:
        f = a.reshape(-1).astype(F32)
        n = _round_up(f.shape[0], SUBLANE * LANE)
        rows.append(jnp.pad(f, (0, n - f.shape[0])).reshape(-1, LANE))
    return jnp.concatenate(rows, axis=0)


def _unpack_flat(packed, shapes):
    out, r0 = [], 0
    for s in shapes:
        n = math.prod(s)
        nr = _round_up(n, SUBLANE * LANE) // LANE
        out.append(packed[r0:r0 + nr].reshape(-1)[:n].reshape(s))
        r0 += nr
    return out


def _pack_rows(arrs):
    rows = []
    for a in arrs:
        f = a.reshape(-1, a.shape[-1]).astype(F32)
        n = _round_up(f.shape[0], SUBLANE)
        rows.append(jnp.pad(f, ((0, n - f.shape[0]), (0, 0))))
    return jnp.concatenate(rows, axis=0)


def _unpack_rows(packed, shapes):
    out, r0 = [], 0
    for s in shapes:
        nr = math.prod(s[:-1])
        out.append(packed[r0:r0 + nr].reshape(s))
        r0 += _round_up(nr, SUBLANE)
    return out


def _cols_from_slots(g):
    return jnp.moveaxis(g, 0, -2).reshape(g.shape[1:-1] + (N_DEV * g.shape[-1],))


def _cols_to_slots(a):
    cs = a.shape[-1] // N_DEV
    return jnp.moveaxis(a.reshape(a.shape[:-1] + (N_DEV, cs)), -2, 0)


def _step(P, M, V):
    D, T = D_MODEL, SEQ
    S5W = D // 2
    C, LW, GP = _rk_dims()
    H = C // RWKV_HEAD
    G = S5W // S5_GROUP
    NS = G * S5_STATE
    NCH = G // S5_CHUNK_GROUPS
    SW = S5_CHUNK_GROUPS * S5_STATE
    RIN = 3 * C + 2 * LW + GATE_LORA
    RINP = 3 * C + 2 * LW + GP
    PROJ = S5W + RIN
    PROJP = S5W + RINP
    FF = 4 * D
    me = 4 * lax.axis_index("x") + 2 * lax.axis_index("y") + lax.axis_index("c")
    cs_mod = N_MOD * D // N_DEV
    eye = jnp.eye(S5_CHUNK_GROUPS, dtype=F32)

    x = P['x'][0]
    target = P['loss_target'][0]

    (c_all,) = exchange([P['c']], ['gather'], "comm_gather_c")
    c_all = c_all.reshape(N_DEV, D)
    (c_act,), _ = rowcall(lambda cv: ((cv * _sigmoid(cv),), ()), "silu_c", [c_all], [], [(D, F32)], [], N_DEV)
    ada_b_loc = lax.dynamic_slice(P['ada_b'], (0, me * cs_mod), (1, cs_mod))
    mod_loc = matmul(c_act, P['ada_w'][0], name="mod_mm", precise=True, extras=[(ada_b_loc, 'n')],
                     epi=lambda acc, bias: (acc + bias,))

    rk_shapes = [P[n][0].shape for n in RKPACK]
    rk_pack = _pack_rows([P[n][0] for n in RKPACK])
    gathered = exchange(
        [mod_loc, P['w_in'][0].astype(BF16), P['w_out'][0].astype(BF16), P['ffn_w1'][0].astype(BF16),
         P['ffn_w2'][0].astype(BF16), P['s5_w_glu'][0].astype(BF16), rk_pack],
        ['gather'] * 7, "comm_gather_weights")
    mod_all, w_in_g, w_out_g, w1_g, w2_g, wglu_g, rk_g = gathered
    mod_me = lax.dynamic_index_in_dim(mod_all, me, axis=1, keepdims=False).reshape(N_MOD, 1, D)
    shift1, scale1, gate1, shift2, scale2, gate2 = (mod_me[i] for i in range(N_MOD))
    w_in = jnp.pad(_cols_from_slots(w_in_g), ((0, 0), (0, PROJP - PROJ)))
    w_out = w_out_g.reshape(D, D)
    w2 = w2_g.reshape(FF, D)
    wglu = wglu_g.reshape(S5W, S5W)
    rk_full = _unpack_rows(_cols_from_slots(rk_g), [s[:-1] + (C,) for s in rk_shapes])
    rk_w0, rk_a0, rk_wup, rk_aup, rk_gup = rk_full

    def lora_pad(up):
        z = jnp.zeros((N_DIR, LW, C), F32)
        for d in range(N_DIR):
            z = z.at[d, d * DECAY_LORA:(d + 1) * DECAY_LORA].set(up[d])
        return z

    wup_p, aup_p = lora_pad(rk_wup), lora_pad(rk_aup)
    gup_p = jnp.pad(rk_gup, ((0, GP - GATE_LORA), (0, 0)))
    mu_prev = jnp.pad(P['rk_shift_prev'], ((0, 0), (0, RINP - RIN)))
    mu_next = jnp.pad(P['rk_shift_next'], ((0, 0), (0, RINP - RIN)))
    r_k = P['rk_r_k'].reshape(1, C)
    fgain = P['final_gain'].reshape(1, D)

    TT = 256
    (h1,), _ = rowcall(lambda xv, g, sc, sh: ((_normmod(xv, g, sc, sh),), ()), "norm1",
                       [x], [P['norm1_gain'], scale1, shift1], [(D, BF16)], [], TT)
    proj = matmul(h1, w_in, name="proj_mm")

    lre = P['s5_lambda_re'][0].reshape(N_DIR, NS)
    lim = P['s5_lambda_im'][0].reshape(N_DIR, NS)
    lstep = jnp.broadcast_to(P['s5_log_step'][0][:, :, None], (N_DIR, G, S5_STATE)).reshape(N_DIR, NS)
    bre = P['s5_b_re'][0].reshape(NS, S5_GROUP).T
    bim = P['s5_b_im'][0].reshape(NS, S5_GROUP).T
    lbr, lbi, bbr, bbi = s5_prep(lre, lim, lstep, bre, bim)
    bbar = jnp.stack([bbr, bbi], axis=1).reshape(N_DIR, 2, S5_GROUP, NCH, S5_CHUNK_GROUPS, S5_STATE)
    wblk = jnp.einsum('drhcgp,gk->cghdrkp', bbar, eye).reshape(NCH, S5_CHUNK_GROUPS * S5_GROUP, 4 * SW)
    wblk = wblk.astype(MXU_DTYPE)
    u_view = (proj, S5W, 0)
    bus, _ = rowcall(lambda uv, wb: (_s5_bu_fn(uv, wb), ()), "s5_bu", [u_view], [wblk], [(NS, F32)] * 4, [], TT)
    s0r, s0i = s5_scan(bus[0], bus[1], lbr[0:1], lbi[0:1], reverse=False, name="s5_scan_f0")
    s1r, s1i = s5_scan(bus[2], bus[3], lbr[1:2], lbi[1:2], reverse=True, name="s5_scan_f1")

    def cblk(cm):
        c4 = cm.reshape(NCH, S5_CHUNK_GROUPS, S5_GROUP, S5_STATE)
        return jnp.einsum('cghp,gk->cgpkh', c4, eye).reshape(NCH, SW, S5_CHUNK_GROUPS * S5_GROUP)

    cre_b = cblk(P['s5_c_re'][0]).astype(MXU_DTYPE)
    cim_b = cblk(P['s5_c_im'][0]).astype(MXU_DTYPE)
    s5_full = [cre_b, cim_b, P['s5_d'], wglu, P['s5_b_glu']]
    TS = 128
    (y_s5,), _ = rowcall(lambda *a: ((_s5_out_fn(*a),), ()), "s5_out", [s0r, s0i, s1r, s1i, u_view], s5_full,
                         [(S5W, BF16)], [], TS)

    ps = rk_shift(proj, mu_prev, mu_next, S5W)
    pre_full = [rk_w0, wup_p[0], wup_p[1], rk_a0, aup_p[0], aup_p[1], gup_p, P['rk_k_k'], P['rk_k_a']]
    pre_out, _ = rowcall(lambda *a: (_rk_pre_fn(*a)[2:], ()), "rk_pre", [ps], pre_full, [(C, F32)] * 8, [], TS)
    kkn, w_0, w_1, k_0, k_1, b_0, b_1, g_gate = pre_out
    r_t, v_t = ps[:, 0:C], ps[:, 2 * C:3 * C]

    def hm(a):
        return a.reshape(T, H, RWKV_HEAD).transpose(1, 0, 2)

    def hmT(a):
        return a.reshape(T, H, RWKV_HEAD).transpose(1, 2, 0)

    def unT(a):
        return a.transpose(2, 0, 1).reshape(T, C)

    def unhm(a):
        return a.transpose(1, 0, 2).reshape(T, C)

    r_h, kk_h, vT_h = hm(r_t), hm(kkn), hmT(v_t)
    dir_rows = [(hm(w_0), hm(k_0), hm(b_0)), (hm(w_1), hm(k_1), hm(b_1))]
    yT, hist = [], []
    for d in range(N_DIR):
        wd, kd, bd = dir_rows[d]
        yd, hd = rk_scan(r_h, kk_h, wd, kd, bd, vT_h, reverse=(d == 1), name=f"rk_scan_f{d}")
        yT.append(yd)
        hist.append(hd)
    y_0, y_1 = unT(yT[0]), unT(yT[1])
    post_full = [r_k, P['rk_ln_gain'], P['rk_ln_bias']]
    post_tiled = [y_0, y_1, (ps, C, 0), (ps, C, 2), k_0, k_1, g_gate]
    (y_rk,), _ = rowcall(lambda *a: ((_rk_post_fn(*a),), ()), "rk_post", post_tiled, post_full, [(C, BF16)], [], TS)

    ycat = jnp.concatenate([y_s5, y_rk], axis=1)
    mixed = matmul(ycat, w_out, name="out_mm")

    def res_norm(xv, mv, gate, g, sc, sh):
        x1v = xv + gate * mv
        return x1v, _normmod(x1v, g, sc, sh)

    (x1, h2), _ = rowcall(lambda *a: (res_norm(*a), ()), "norm2", [x, mixed], [gate1, P['norm2_gain'], scale2, shift2],
                          [(D, F32), (D, BF16)], [], TT)
    a_ff, hh_ff = matmul(h2, w1_g, name="ffn1_mm", b_slots=True, out_dtypes=(F32, BF16),
                         epi=lambda acc: (acc, jnp.square(jnp.maximum(acc, 0.0))))
    ffn = matmul(hh_ff, w2, name="ffn2_mm")

    def loss_fn(x1v, fv, tg, gate, fg):
        def f(x1_, f_, gate_, fg_):
            out = _rms(x1_ + gate_ * f_, fg_)
            err = out - tg
            return 0.5 * jnp.sum(jnp.sum(err * err, axis=1, keepdims=True), axis=0, keepdims=True) * (1.0 / D)
        lv, vjp = jax.vjp(f, x1v, fv, gate, fg)
        dx1, dff, dgate, dfg = vjp(jnp.ones((1, 1), F32))
        return (dx1, dff), (jnp.broadcast_to(lv, (SUBLANE, LANE)), dgate, dfg)

    (dx2, dffn), (loss_t, dgate2, dfgain) = rowcall(
        loss_fn, "loss", [x1, ffn, target], [gate2, fgain], [(D, F32), (D, BF16)], [(SUBLANE, LANE), (1, D), (1, D)], TT)
    loss = lax.psum(loss_t[0, 0], MESH_AXES)

    da = matmul(dffn, w2, name="dffn2_mm", tb=True, out_dtypes=(BF16,), extras=[(a_ff, 'mn')],
                epi=lambda acc, av: (acc * (2.0 * jnp.maximum(av, 0.0)),))
    g_w2 = matmul(hh_ff, dffn, name="gw2_mm", ta=True, out_dtypes=(BF16,))
    dh2 = matmul(da, w1_g, name="dh2_mm", tb=True, b_slots=True)
    g_w1 = matmul(h2, da, name="gw1_mm", ta=True, out_slots=N_DEV, out_dtypes=(BF16,))

    def res_norm_bwd(dx2v, dh2v, xv, mv, gate, g, sc, sh):
        _, vjp = jax.vjp(res_norm, xv, mv, gate, g, sc, sh)
        dx, dm, dgate, dg, dsc, dsh = vjp((dx2v, dh2v))
        return (dx, dm), (dgate, dg, dsc, dsh)

    (dx1, dmixed), (dgate1, dgain2, dscale2, dshift2) = rowcall(
        res_norm_bwd, "norm2_bwd", [dx2, dh2, x, mixed], [gate1, P['norm2_gain'], scale2, shift2],
        [(D, F32), (D, BF16)], [(1, D)] * 4, TT)

    dycat = matmul(dmixed, w_out, name="dycat_mm", tb=True)
    g_wout = matmul(ycat, dmixed, name="gwout_mm", ta=True, out_dtypes=(BF16,))

    def post_bwd(dy, *a):
        _, vjp = jax.vjp(_rk_post_fn, *a)
        gy0, gy1, gr, gv, gk0, gk1, gg, grk, glg, glb = vjp(dy)
        return (gy0, gr, gv, gk0, gk1, gg), (grk, glg, glb)

    cb_rk = S5W // C if C else 0
    (dy_rk, dr_p, dv_p, dk0_p, dk1_p, dg_p), (g_rk_rk, g_lng, g_lnb) = rowcall(
        post_bwd, "rk_post_bwd", [(dycat, C, cb_rk)] + post_tiled, post_full, [(C, F32)] * 6, [(1, C)] * 3, TS)
    dyT_h = hmT(dy_rk)
    scan_g = []
    for d in range(N_DIR):
        wd, kd, bd = dir_rows[d]
        scan_g.append(rk_scan_bwd(r_h, kk_h, wd, kd, bd, vT_h, dyT_h, hist[d], reverse=(d == 1), name=f"rk_scan_b{d}"))
    cot = [dr_p, unhm(scan_g[0][0]), unhm(scan_g[1][0]),
           dv_p, unT(scan_g[0][5]), unT(scan_g[1][5]),
           unhm(scan_g[0][1]), unhm(scan_g[1][1]),
           unhm(scan_g[0][2]), unhm(scan_g[1][2]),
           dk0_p, unhm(scan_g[0][3]), dk1_p, unhm(scan_g[1][3]),
           unhm(scan_g[0][4]), unhm(scan_g[1][4]),
           dg_p]

    def pre_bwd(psv, r0, r1, r2, v0, v1, v2, q0, q1, dw0, dw1, k0a, k0b, k1a, k1b, db0, db1, dgv, *params):
        _, vjp = jax.vjp(_rk_pre_fn, psv, *params)
        grads = vjp((r0 + r1 + r2, v0 + v1 + v2, q0 + q1, dw0, dw1, k0a + k0b, k1a + k1b, db0, db1, dgv))
        return (grads[0],), tuple(grads[1:])

    (dps,), pre_g = rowcall(pre_bwd, "rk_pre_bwd", [ps] + cot, pre_full, [(RINP, F32)],
                            [f.shape for f in pre_full], TS)
    g_w0, g_wup0, g_wup1, g_a0, g_aup0, g_aup1, g_gup_p, g_kk, g_ka = pre_g
    g_wup_p, g_aup_p = jnp.stack([g_wup0, g_wup1]), jnp.stack([g_aup0, g_aup1])
    dp_rk, g_mup, g_mun = rk_shift_bwd(dps, proj, mu_prev, mu_next, S5W)

    def s5_out_bwd(dy, *a):
        a = [t.astype(F32) for t in a]
        _, vjp = jax.vjp(_s5_out_fn, *a)
        g = vjp(dy)
        return (g[0], g[1], g[4]), tuple(g[5:])

    (dxr, dxi, du_a), s5_pg = rowcall(
        s5_out_bwd, "s5_out_bwd", [(dycat, S5W, 0), s0r, s0i, s1r, s1i, u_view], s5_full,
        [(NS, F32), (NS, F32), (S5W, F32)], [f.shape for f in s5_full], TS)
    g_creb, g_cimb, g_s5d, g_wglu, g_bglu = s5_pg
    l0r, l0i = s5_scan(dxr, dxi, lbr[0:1], -lbi[0:1], reverse=True, name="s5_scan_b0")
    l1r, l1i = s5_scan(dxr, dxi, lbr[1:2], -lbi[1:2], reverse=False, name="s5_scan_b1")
    dl0r, dl0i = s5_dlam(s0r, s0i, l0r, l0i, reverse=False, name="s5_dlam0")
    dl1r, dl1i = s5_dlam(s1r, s1i, l1r, l1i, reverse=True, name="s5_dlam1")

    def bu_bwd(uv, g0, g1, g2, g3, wb):
        _, vjp = jax.vjp(_s5_bu_fn, uv, wb.astype(F32))
        du, dwb = vjp((g0, g1, g2, g3))
        return (du,), (dwb,)

    (du_b,), (g_wblk,) = rowcall(bu_bwd, "s5_bu_bwd", [u_view, l0r, l0i, l1r, l1i], [wblk], [(S5W, F32)],
                                 [wblk.shape], TS)
    g_bbar = jnp.einsum('cghdrkp,gk->drhcgp',
                        g_wblk.reshape(NCH, S5_CHUNK_GROUPS, S5_GROUP, N_DIR, 2, S5_CHUNK_GROUPS, S5_STATE), eye)
    g_bbar = g_bbar.reshape(N_DIR, 2, S5_GROUP, NS)
    g_lre, g_lim, g_lstep, g_bre, g_bim = s5_prep_bwd(
        lre, lim, lstep, bre, bim, jnp.concatenate([dl0r, dl1r], 0), jnp.concatenate([dl0i, dl1i], 0),
        g_bbar[:, 0], g_bbar[:, 1])

    def uncblk(gb):
        g5 = gb.reshape(NCH, S5_CHUNK_GROUPS, S5_STATE, S5_CHUNK_GROUPS, S5_GROUP)
        return jnp.einsum('cgpkh,gk->cghp', g5, eye).reshape(G, S5_GROUP, S5_STATE)

    (du_tot,), _ = rowcall(lambda a, b_: ((a + b_,), ()), "s5_du_sum", [du_a, du_b], [], [(S5W, BF16)], [], TT)
    dproj = jnp.concatenate([du_tot, dp_rk.astype(BF16)], axis=1)
    dh1 = matmul(dproj, w_in, name="dh1_mm", tb=True)
    g_win = matmul(h1, dproj, name="gwin_mm", ta=True, out_dtypes=(BF16,))

    def norm1_bwd(dx1v, dh1v, xv, g, sc, sh):
        _, vjp = jax.vjp(_normmod, xv, g, sc, sh)
        dx, dg, dsc, dsh = vjp(dh1v)
        return (dx1v + dx,), (dg, dsc, dsh)

    (grad_x,), (dgain1, dscale1, dshift1) = rowcall(
        norm1_bwd, "norm1_bwd", [dx1, dh1, x], [P['norm1_gain'], scale1, shift1], [(D, F32)], [(1, D)] * 3, TT)

    dmod = jnp.concatenate([dshift1, dscale1, dgate1, dshift2, dscale2, dgate2], axis=1)
    lstep_g = g_lstep.reshape(N_DIR, G, S5_STATE)
    small_g = {
        'ada_b': dmod, 'norm1_gain': dgain1, 'norm2_gain': dgain2, 'final_gain': dfgain.reshape(D),
        's5_lambda_re': g_lre.reshape(1, N_DIR, G, S5_STATE), 's5_lambda_im': g_lim.reshape(1, N_DIR, G, S5_STATE),
        's5_log_step': lstep_g,
        's5_b_re': g_bre.T.reshape(1, G, S5_STATE, S5_GROUP), 's5_b_im': g_bim.T.reshape(1, G, S5_STATE, S5_GROUP),
        's5_c_re': uncblk(g_creb)[None], 's5_c_im': uncblk(g_cimb)[None],
        's5_d': g_s5d, 's5_b_glu': g_bglu,
        'rk_shift_prev': g_mup[:, :RIN], 'rk_shift_next': g_mun[:, :RIN],
        'rk_k_k': g_kk, 'rk_k_a': g_ka, 'rk_r_k': g_rk_rk.reshape(1, H, RWKV_HEAD),
        'rk_ln_gain': g_lng, 'rk_ln_bias': g_lnb,
    }
    small_shapes = {n: P[n].shape for n in SMALL}
    small_shapes['s5_log_step'] = (N_DIR, G, S5_STATE)
    small_pack = _pack_flat([small_g[n] for n in SMALL])

    def lora_unpad(gp):
        return jnp.stack([gp[d, d * DECAY_LORA:(d + 1) * DECAY_LORA] for d in range(N_DIR)])

    rk_grads = {'rk_w0': g_w0, 'rk_a0': g_a0, 'rk_w_up': lora_unpad(g_wup_p), 'rk_a_up': lora_unpad(g_aup_p),
                'rk_g_up': g_gup_p[:GATE_LORA]}
    rk_gpack = jnp.stack([_pack_rows([_cols_to_slots(rk_grads[n])[j] for n in RKPACK]) for j in range(N_DEV)])
    g_win_s = _cols_to_slots(g_win[:, :PROJ])
    ex = exchange(
        [small_pack, g_win_s, g_wout.reshape(N_DEV, D // N_DEV, D), g_w1, g_w2.reshape(N_DEV, FF // N_DEV, D),
         g_wglu.reshape(N_DEV, S5W // N_DEV, S5W), rk_gpack],
        ['gather'] + ['scatter'] * 6, "comm_grads")
    small_all, win_parts, wout_parts, w1_parts, w2_parts, wglu_parts, rk_parts = ex

    res = {}

    def put(name, g, dl, m2, v2):
        shp = P[name].shape
        res[name] = tuple(t.reshape(shp) for t in (g, dl, m2, v2))

    def adam2d(name, parts):
        shp = P[name].shape
        r2 = (math.prod(shp[:-1]), shp[-1])
        put(name, *adam(parts.reshape((parts.shape[0],) + r2), P[name].reshape(r2), M[name].reshape(r2),
                        V[name].reshape(r2), "adam_" + name))

    adam2d('w_in', win_parts)
    adam2d('w_out', wout_parts)
    adam2d('ffn_w1', w1_parts)
    adam2d('ffn_w2', w2_parts)
    adam2d('s5_w_glu', wglu_parts)
    off = 0
    for n in SMALL:
        if n == 'ada_b':
            break
        off += _round_up(math.prod(small_shapes[n]), SUBLANE * LANE) // LANE
    nrow_b = N_MOD * D // LANE
    dmod_all = small_all[:, off:off + nrow_b].reshape(N_DEV, N_MOD * D)
    dmod_cols = lax.dynamic_slice(dmod_all, (0, me * cs_mod), (N_DEV, cs_mod))
    g_adaw = matmul(c_act, dmod_cols, name="gadaw_mm", ta=True, precise=True)
    adam2d('ada_w', g_adaw[None])
    small_w = dict(P)
    small_m, small_v = dict(M), dict(V)
    rk_res = adam(rk_parts, rk_pack, _pack_rows([M[n][0] for n in RKPACK]), _pack_rows([V[n][0] for n in RKPACK]),
                  "adam_rkpack")
    for name, parts4 in zip(RKPACK, zip(*[_unpack_rows(t, rk_shapes) for t in rk_res])):
        put(name, *parts4)
    return loss, grad_x, res, (small_all, small_shapes, small_w, small_m, small_v)


def _small_update(small_all, small_shapes, P, M, V, res):
    G = (D_MODEL // 2) // S5_GROUP
    names = [n for n in SMALL if n != 's5_log_step']
    shapes = [small_shapes[n] for n in SMALL]
    parts = _unpack_flat_batched(small_all, shapes)
    by = dict(zip(SMALL, parts))
    ls = by['s5_log_step']
    ls = ls.transpose(0, 3, 1, 2).reshape(N_DEV * S5_STATE, N_DIR * G)
    pk = lambda d: _pack_flat([d[n] for n in names])
    packs = jnp.stack([_pack_flat([by[n][j] for n in names]) for j in range(N_DEV)])
    out = adam(packs, pk(P), pk(M), pk(V), "adam_small")
    shp = [P[n].shape for n in names]
    for name, parts4 in zip(names, zip(*[_unpack_flat(t, shp) for t in out])):
        res[name] = parts4
    lsw = lambda d: jnp.pad(d['s5_log_step'].reshape(1, N_DIR * G), ((0, SUBLANE - 1), (0, 0)))
    ls_parts = jnp.pad(ls[:, None, :], ((0, 0), (0, SUBLANE - 1), (0, 0)))
    o = adam(ls_parts, lsw(P), lsw(M), lsw(V), "adam_log_step")
    res['s5_log_step'] = tuple(t[0:1].reshape(P['s5_log_step'].shape) for t in o)


def _unpack_flat_batched(packed, shapes):
    out, r0 = [], 0
    B = packed.shape[0]
    for s in shapes:
        n = math.prod(s)
        nr = _round_up(n, SUBLANE * LANE) // LANE
        out.append(packed[:, r0:r0 + nr].reshape(B, -1)[:, :n].reshape((B,) + tuple(s)))
        r0 += nr
    return out


def kernel(x, c, ada_w, ada_b, norm1_gain, norm2_gain, final_gain, w_in, w_out, s5_lambda_re, s5_lambda_im, s5_log_step, s5_b_re, s5_b_im, s5_c_re, s5_c_im, s5_d, s5_w_glu, s5_b_glu, rk_shift_prev, rk_shift_next, rk_w0, rk_w_up, rk_a0, rk_a_up, rk_g_up, rk_k_k, rk_k_a, rk_r_k, rk_ln_gain, rk_ln_bias, ffn_w1, ffn_w2, loss_target, m_ada_w, m_ada_b, m_norm1_gain, m_norm2_gain, m_final_gain, m_w_in, m_w_out, m_s5_lambda_re, m_s5_lambda_im, m_s5_log_step, m_s5_b_re, m_s5_b_im, m_s5_c_re, m_s5_c_im, m_s5_d, m_s5_w_glu, m_s5_b_glu, m_rk_shift_prev, m_rk_shift_next, m_rk_w0, m_rk_w_up, m_rk_a0, m_rk_a_up, m_rk_g_up, m_rk_k_k, m_rk_k_a, m_rk_r_k, m_rk_ln_gain, m_rk_ln_bias, m_ffn_w1, m_ffn_w2, v_ada_w, v_ada_b, v_norm1_gain, v_norm2_gain, v_final_gain, v_w_in, v_w_out, v_s5_lambda_re, v_s5_lambda_im, v_s5_log_step, v_s5_b_re, v_s5_b_im, v_s5_c_re, v_s5_c_im, v_s5_d, v_s5_w_glu, v_s5_b_glu, v_rk_shift_prev, v_rk_shift_next, v_rk_w0, v_rk_w_up, v_rk_a0, v_rk_a_up, v_rk_g_up, v_rk_k_k, v_rk_k_a, v_rk_r_k, v_rk_ln_gain, v_rk_ln_bias, v_ffn_w1, v_ffn_w2):
    given = dict(locals())
    P = {n: given[n] for n in ['x', 'c', 'loss_target'] + WEIGHTS}
    M = {n: given['m_' + n] for n in WEIGHTS}
    V = {n: given['v_' + n] for n in WEIGHTS}
    loss, grad_x, res, small = _step(P, M, V)
    small_all, small_shapes, _, _, _ = small
    _small_update(small_all, small_shapes, P, M, V, res)
    outs = [loss, grad_x[None]]
    for q in range(4):
        outs += [res[n][q] for n in WEIGHTS]
    return tuple(outs)
```

```python
import functools
import math

import jax
import jax.numpy as jnp
from jax import lax
from jax.experimental import pallas as pl
from jax.experimental.pallas import tpu as pltpu

F32 = jnp.float32
BF16 = jnp.bfloat16
HI = lax.Precision.HIGHEST
MXU_DTYPE = jnp.bfloat16

N_DEV = 8
MESH_AXES = ("x", "y", "c")
D_MODEL = 2048
SEQ = 2048
S5_GROUP = 16
S5_STATE = 64
RWKV_HEAD = 64
DECAY_LORA = 64
GATE_LORA = 160
N_DIR = 2
N_MOD = 6
NORM_EPS = 1e-6
GN_EPS = 64e-5
L2_EPS = 1e-12
ADAM_LR = 0.001
ADAM_B1 = 0.9
ADAM_B2 = 0.999
ADAM_EPS = 1e-08
ADAM_WD = 0.01
ADAM_STEP = 10
LANE = 128
SUBLANE = 8
S5_CHUNK_GROUPS = 8
S5_SCAN_ROWS = 256
VMEM_LIMIT = 56 * 1024 * 1024

WEIGHTS = ['ada_w', 'ada_b', 'norm1_gain', 'norm2_gain', 'final_gain', 'w_in', 'w_out', 's5_lambda_re',
           's5_lambda_im', 's5_log_step', 's5_b_re', 's5_b_im', 's5_c_re', 's5_c_im', 's5_d', 's5_w_glu',
           's5_b_glu', 'rk_shift_prev', 'rk_shift_next', 'rk_w0', 'rk_w_up', 'rk_a0', 'rk_a_up', 'rk_g_up',
           'rk_k_k', 'rk_k_a', 'rk_r_k', 'rk_ln_gain', 'rk_ln_bias', 'ffn_w1', 'ffn_w2']
SMALL = ['ada_b', 'norm1_gain', 'norm2_gain', 'final_gain', 's5_lambda_re', 's5_lambda_im', 's5_log_step',
         's5_b_re', 's5_b_im', 's5_c_re', 's5_c_im', 's5_d', 's5_b_glu', 'rk_shift_prev', 'rk_shift_next',
         'rk_k_k', 'rk_k_a', 'rk_r_k', 'rk_ln_gain', 'rk_ln_bias']
RKPACK = ['rk_w0', 'rk_a0', 'rk_w_up', 'rk_a_up', 'rk_g_up']


def _round_up(n, m):
    return (n + m - 1) // m * m


def _tile(dim, pref, unit=LANE):
    t = min(pref, dim) // unit * unit
    while t >= unit:
        if dim % t == 0:
            return t
        t -= unit
    return dim


def _cparams(sem=None):
    return pltpu.CompilerParams(dimension_semantics=sem, vmem_limit_bytes=VMEM_LIMIT)


def _full_spec(a):
    nd = a.ndim
    return pl.BlockSpec(a.shape, lambda *_: (0,) * nd)


def exchange(arrs, modes, name):
    n = len(arrs)
    out_shape = [jax.ShapeDtypeStruct((N_DEV,) + a.shape if m == 'gather' else a.shape, a.dtype)
                 for a, m in zip(arrs, modes)]

    def body(*refs):
        ins, outs = refs[:n], refs[n:2 * n]
        send_sems, recv_sems, local_sems = refs[2 * n:]
        x, y, c = (lax.axis_index(a) for a in MESH_AXES)
        me = 4 * x + 2 * y + c
        copies = []
        for i in range(n):
            gather = modes[i] == 'gather'
            mine = pltpu.make_async_copy(ins[i] if gather else ins[i].at[me], outs[i].at[me], local_sems.at[i])
            mine.start()
            copies.append(mine)
        remote = []
        for k in range(1, N_DEV):
            px = 1 - x if (k >> 2) & 1 else x
            py = 1 - y if (k >> 1) & 1 else y
            pc = 1 - c if k & 1 else c
            peer = 4 * px + 2 * py + pc
            for i in range(n):
                src = ins[i] if modes[i] == 'gather' else ins[i].at[peer]
                cp = pltpu.make_async_remote_copy(
                    src_ref=src, dst_ref=outs[i].at[me], send_sem=send_sems.at[i, k - 1],
                    recv_sem=recv_sems.at[i, k - 1], device_id=(px, py, pc), device_id_type=pl.DeviceIdType.MESH)
                cp.start()
                remote.append(cp)
        for cp in remote:
            cp.wait_recv()
        for cp in remote:
            cp.wait_send()
        for cp in copies:
            cp.wait()

    any_spec = pl.BlockSpec(memory_space=pl.ANY)
    return pl.pallas_call(
        body, name=name, out_shape=out_shape,
        in_specs=[any_spec] * n, out_specs=[any_spec] * n,
        scratch_shapes=[pltpu.SemaphoreType.DMA((n, N_DEV - 1)), pltpu.SemaphoreType.DMA((n, N_DEV - 1)),
                        pltpu.SemaphoreType.DMA((n,))],
        compiler_params=pltpu.CompilerParams(has_side_effects=True),
    )(*arrs)


def matmul(a, b, *, name, ta=False, tb=False, b_slots=False, out_slots=0, out_dtypes=(F32,), epi=None,
           extras=(), precise=False, tm=512, tn=512, tk=2048):
    if ta:
        K, M = a.shape
    else:
        M, K = a.shape
    if b_slots:
        ns, br, bc = b.shape
        bshape = (br, ns * bc)
    else:
        bshape = b.shape
    N = bshape[0] if tb else bshape[1]
    assert (bshape[1] if tb else bshape[0]) == K, (a.shape, b.shape, ta, tb)
    tm, tn, tk = _tile(M, tm, SUBLANE), _tile(N, tn), _tile(K, tk, SUBLANE if K < LANE else LANE)
    if b_slots and tb:
        tk = _tile(b.shape[2], tk)
    elif b_slots:
        tn = _tile(b.shape[2], tn)
    if out_slots:
        tn = _tile(N // out_slots, tn)
    if b_slots:
        cs = b.shape[2]
        tcol = tk if tb else tn
        assert cs % tcol == 0
        per = cs // tcol
    if out_slots:
        ncs = N // out_slots
        assert ncs % tn == 0
        operc = ncs // tn
    nk = K // tk
    a_spec = pl.BlockSpec((tk, tm), lambda i, j, k: (k, i)) if ta else pl.BlockSpec((tm, tk), lambda i, j, k: (i, k))
    if b_slots:
        if tb:
            b_spec = pl.BlockSpec((None, tn, tk), lambda i, j, k: (k // per, j, k % per))
        else:
            b_spec = pl.BlockSpec((None, tk, tn), lambda i, j, k: (j // per, k, j % per))
    else:
        b_spec = pl.BlockSpec((tn, tk), lambda i, j, k: (j, k)) if tb else pl.BlockSpec((tk, tn), lambda i, j, k: (k, j))
    ex_specs = []
    for arr, kind in extras:
        if kind == 'mn':
            ex_specs.append(pl.BlockSpec((tm, tn), lambda i, j, k: (i, j)))
        else:
            ex_specs.append(pl.BlockSpec((1, tn), lambda i, j, k: (0, j)))
    if out_slots:
        o_spec = pl.BlockSpec((None, tm, tn), lambda i, j, k: (j // operc, i, j % operc))
        o_shape = (out_slots, M, ncs)
    else:
        o_spec = pl.BlockSpec((tm, tn), lambda i, j, k: (i, j))
        o_shape = (M, N)
    ne, no = len(extras), len(out_dtypes)
    dims = (((0 if ta else 1,), (1 if tb else 0,)), ((), ()))
    op_dtype = F32 if precise else MXU_DTYPE

    def body(a_ref, b_ref, *rest):
        ex_refs, out_refs, acc = rest[:ne], rest[ne:ne + no], rest[-1]
        k = pl.program_id(2)
        part = lax.dot_general(a_ref[...].astype(op_dtype), b_ref[...].astype(op_dtype), dims,
                               precision=HI if precise else None, preferred_element_type=F32)

        def finish(total):
            res = epi(total, *[e[...] for e in ex_refs]) if epi is not None else (total,)
            for o, r in zip(out_refs, res):
                o[...] = r.astype(o.dtype)

        if nk == 1:
            finish(part)
        else:
            @pl.when(k == 0)
            def _():
                acc[...] = part

            @pl.when(jnp.logical_and(k > 0, k < nk - 1))
            def _():
                acc[...] += part

            @pl.when(k == nk - 1)
            def _():
                finish(acc[...] + part)

    outs = pl.pallas_call(
        body, name=name, grid=(M // tm, N // tn, nk),
        in_specs=[a_spec, b_spec] + ex_specs, out_specs=[o_spec] * no,
        out_shape=[jax.ShapeDtypeStruct(o_shape, dt) for dt in out_dtypes],
        scratch_shapes=[pltpu.VMEM((tm, tn), F32)],
        compiler_params=_cparams(("parallel", "parallel", "arbitrary")),
    )(a, b, *[e[0] for e in extras])
    return outs[0] if no == 1 else outs


def rowcall(fn, name, tiled, full, tiled_out, acc_out, tt):
    views = [(t, t.shape[1], 0) if not isinstance(t, tuple) else t for t in tiled]
    T = views[0][0].shape[0]
    tt = _tile(T, tt, SUBLANE)
    nt, nf, nto, nao = len(views), len(full), len(tiled_out), len(acc_out)

    def view_spec(w, cb):
        return pl.BlockSpec((tt, w), lambda i: (i, cb))

    in_specs = [view_spec(w, cb) for _, w, cb in views] + [_full_spec(f) for f in full]
    out_specs = [pl.BlockSpec((tt, w), lambda i: (i, 0)) for w, _ in tiled_out]
    out_specs += [pl.BlockSpec(s, lambda i, nd=len(s): (0,) * nd) for s in acc_out]
    out_shape = [jax.ShapeDtypeStruct((T, w), dt) for w, dt in tiled_out]
    out_shape += [jax.ShapeDtypeStruct(s, F32) for s in acc_out]

    def body(*refs):
        tin, fin = refs[:nt], refs[nt:nt + nf]
        tout, aout = refs[nt + nf:nt + nf + nto], refs[nt + nf + nto:]
        touts, aouts = fn(*[r[...] for r in tin], *[r[...] for r in fin])
        for r, v in zip(tout, touts):
            r[...] = v.astype(r.dtype)
        if nao:
            @pl.when(pl.program_id(0) == 0)
            def _():
                for r in aout:
                    r[...] = jnp.zeros_like(r)

            for r, v in zip(aout, aouts):
                r[...] += v.astype(F32)

    outs = pl.pallas_call(
        body, name=name, grid=(T // tt,), in_specs=in_specs, out_specs=out_specs, out_shape=out_shape,
        compiler_params=_cparams(("arbitrary",) if nao else ("parallel",)),
    )(*[v[0] for v in views], *full)
    return outs[:nto], outs[nto:]


def _mm(a, b):
    return jnp.dot(a.astype(MXU_DTYPE), b.astype(MXU_DTYPE), preferred_element_type=F32)


def _rms(x, gain):
    ms = jnp.mean(x * x, axis=-1, keepdims=True)
    return x * lax.rsqrt(ms + NORM_EPS) * gain


def _normmod(x, gain, scale, shift):
    return _rms(x, gain) * (1.0 + scale) + shift


def _gelu_tanh(y):
    return 0.5 * y * (1.0 + jnp.tanh(math.sqrt(2.0 / math.pi) * (y + 0.044715 * (y * y * y))))


def _sigmoid(x):
    return 1.0 / (1.0 + jnp.exp(-x))


def _softplus(x):
    return jnp.maximum(x, 0.0) + jnp.log(1.0 + jnp.exp(-jnp.abs(x)))


def _seg_mats(width, seg):
    r = lax.broadcasted_iota(jnp.int32, (width, LANE), 0) // seg
    c = lax.broadcasted_iota(jnp.int32, (width, LANE), 1)
    s = (r == c).astype(F32)
    rt = lax.broadcasted_iota(jnp.int32, (LANE, width), 0)
    ct = lax.broadcasted_iota(jnp.int32, (LANE, width), 1) // seg
    st = (rt == ct).astype(F32)
    return s, st


def _segsum(x, s):
    return jnp.dot(x, s, precision=HI, preferred_element_type=F32)


def _s5_prep_fn(lre, lim, lstep, bre, bim):
    step = jnp.exp(lstep)
    mag = jnp.exp(lre * step)
    lbr = mag * jnp.cos(lim * step)
    lbi = mag * jnp.sin(lim * step)
    den = lre * lre + lim * lim
    nr = lbr - 1.0
    ni = lbi
    cre = (nr * lre + ni * lim) / den
    cim = (ni * lre - nr * lim) / den
    bbr = jnp.stack([cre[d:d + 1] * bre - cim[d:d + 1] * bim for d in range(N_DIR)])
    bbi = jnp.stack([cre[d:d + 1] * bim + cim[d:d + 1] * bre for d in range(N_DIR)])
    return lbr, lbi, bbr, bbi


def s5_prep(lre, lim, lstep, bre, bim):
    ns = lre.shape[1]

    def body(lre_r, lim_r, ls_r, bre_r, bim_r, lbr_r, lbi_r, bbr_r, bbi_r):
        lbr, lbi, bbr, bbi = _s5_prep_fn(lre_r[...], lim_r[...], ls_r[...], bre_r[...], bim_r[...])
        lbr_r[...] = lbr
        lbi_r[...] = lbi
        bbr_r[...] = bbr
        bbi_r[...] = bbi

    return pl.pallas_call(
        body, name="s5_prep",
        out_shape=[jax.ShapeDtypeStruct((N_DIR, ns), F32)] * 2 + [jax.ShapeDtypeStruct((N_DIR, S5_GROUP, ns), F32)] * 2,
        compiler_params=_cparams(),
    )(lre, lim, lstep, bre, bim)


def s5_prep_bwd(lre, lim, lstep, bre, bim, dlbr, dlbi, dbbr, dbbi):
    ns = lre.shape[1]

    def body(lre_r, lim_r, ls_r, bre_r, bim_r, d1, d2, d3, d4, o1, o2, o3, o4, o5):
        _, vjp = jax.vjp(_s5_prep_fn, lre_r[...], lim_r[...], ls_r[...], bre_r[...], bim_r[...])
        g = vjp((d1[...], d2[...], d3[...], d4[...]))
        for o, v in zip((o1, o2, o3, o4, o5), g):
            o[...] = v

    return pl.pallas_call(
        body, name="s5_prep_bwd",
        out_shape=[jax.ShapeDtypeStruct((N_DIR, ns), F32)] * 3 + [jax.ShapeDtypeStruct((S5_GROUP, ns), F32)] * 2,
        compiler_params=_cparams(),
    )(lre, lim, lstep, bre, bim, dlbr, dlbi, dbbr, dbbi)


def s5_scan(bre, bim, lre, lim, *, reverse, name):
    T, NS = bre.shape
    tt = _tile(T, S5_SCAN_ROWS, SUBLANE)
    wl = _tile(NS, 512)
    nT = T // tt
    ngrp = tt // SUBLANE

    def tmap(j, i):
        return ((nT - 1 - i) if reverse else i, j)

    def body(bre_r, bim_r, lre_r, lim_r, sre_r, sim_r, cre, cim):
        @pl.when(pl.program_id(1) == 0)
        def _():
            cre[...] = jnp.zeros_like(cre)
            cim[...] = jnp.zeros_like(cim)

        lr = jnp.broadcast_to(lre_r[...], (SUBLANE, wl))
        li = jnp.broadcast_to(lim_r[...], (SUBLANE, wl))
        row = lax.broadcasted_iota(jnp.int32, (SUBLANE, wl), 0)
        pows = [(lr, li)]
        for _ in range(3):
            pr, pi = pows[-1]
            pows.append((pr * pr - pi * pi, 2.0 * pr * pi))
        e = (SUBLANE - row) if reverse else (row + 1)
        Pr = jnp.ones((SUBLANE, wl), F32)
        Pi = jnp.zeros((SUBLANE, wl), F32)
        for bit, (qr, qi) in enumerate(pows):
            on = ((e >> bit) & 1) == 1
            nr, ni = Pr * qr - Pi * qi, Pr * qi + Pi * qr
            Pr, Pi = jnp.where(on, nr, Pr), jnp.where(on, ni, Pi)

        def group(g, carry):
            gg = (ngrp - 1 - g) if reverse else g
            rows = pl.ds(pl.multiple_of(gg * SUBLANE, SUBLANE), SUBLANE)
            sr, si = bre_r[rows, :], bim_r[rows, :]
            for lvl, k in enumerate((1, 2, 4)):
                qr, qi = pows[lvl]
                if reverse:
                    shr = pltpu.roll(sr, SUBLANE - k, 0)
                    shi = pltpu.roll(si, SUBLANE - k, 0)
                    keep = row < SUBLANE - k
                else:
                    shr = pltpu.roll(sr, k, 0)
                    shi = pltpu.roll(si, k, 0)
                    keep = row >= k
                shr = jnp.where(keep, shr, 0.0)
                shi = jnp.where(keep, shi, 0.0)
                sr, si = sr + qr * shr - qi * shi, si + qr * shi + qi * shr
            cr, ci = cre[...], cim[...]
            sr, si = sr + Pr * cr - Pi * ci, si + Pr * ci + Pi * cr
            sre_r[rows, :] = sr
            sim_r[rows, :] = si
            last = 0 if reverse else SUBLANE - 1
            cre[...] = jnp.broadcast_to(sr[last:last + 1, :], (SUBLANE, wl))
            cim[...] = jnp.broadcast_to(si[last:last + 1, :], (SUBLANE, wl))
            return carry

        lax.fori_loop(0, ngrp, group, 0)

    blk = pl.BlockSpec((tt, wl), tmap)
    row_spec = pl.BlockSpec((1, wl), lambda j, i: (0, j))
    return pl.pallas_call(
        body, name=name, grid=(NS // wl, nT),
        in_specs=[blk, blk, row_spec, row_spec], out_specs=[blk, blk],
        out_shape=[jax.ShapeDtypeStruct((T, NS), F32)] * 2,
        scratch_shapes=[pltpu.VMEM((SUBLANE, wl), F32)] * 2,
        compiler_params=_cparams(("parallel", "arbitrary")),
    )(bre, bim, lre, lim)


def s5_dlam(sre, sim, gre, gim, *, reverse, name):
    T, NS = sre.shape
    wl = _tile(NS, 256)

    def body(sr_r, si_r, gr_r, gi_r, dr_r, di_r):
        row = lax.broadcasted_iota(jnp.int32, (T, wl), 0)
        if reverse:
            keep = row < T - 1
            pr = jnp.where(keep, pltpu.roll(sr_r[...], T - 1, 0), 0.0)
            pi = jnp.where(keep, pltpu.roll(si_r[...], T - 1, 0), 0.0)
        else:
            keep = row >= 1
            pr = jnp.where(keep, pltpu.roll(sr_r[...], 1, 0), 0.0)
            pi = jnp.where(keep, pltpu.roll(si_r[...], 1, 0), 0.0)
        gr, gi = gr_r[...], gi_r[...]
        dr_r[...] = jnp.sum(pr * gr + pi * gi, axis=0, keepdims=True)
        di_r[...] = jnp.sum(pr * gi - pi * gr, axis=0, keepdims=True)

    blk = pl.BlockSpec((T, wl), lambda j: (0, j))
    o = pl.BlockSpec((1, wl), lambda j: (0, j))
    return pl.pallas_call(
        body, name=name, grid=(NS // wl,), in_specs=[blk] * 4, out_specs=[o, o],
        out_shape=[jax.ShapeDtypeStruct((1, NS), F32)] * 2,
        compiler_params=_cparams(("parallel",)),
    )(sre, sim, gre, gim)


def _s5_bu_fn(u, wblk):
    nch = wblk.shape[0]
    cw = S5_CHUNK_GROUPS * S5_GROUP
    sw = S5_CHUNK_GROUPS * S5_STATE
    parts = [[] for _ in range(4)]
    for ch in range(nch):
        res = _mm(u[:, ch * cw:(ch + 1) * cw], wblk[ch])
        for q in range(4):
            parts[q].append(res[:, q * sw:(q + 1) * sw])
    return tuple(jnp.concatenate(p, axis=1) if nch > 1 else p[0] for p in parts)


def _s5_out_fn(x0r, x0i, x1r, x1i, u, cre, cim, dsk, wglu, bglu):
    xr, xi = x0r + x1r, x0i + x1i
    nch = cre.shape[0]
    sw = S5_CHUNK_GROUPS * S5_STATE
    ys = [_mm(xr[:, ch * sw:(ch + 1) * sw], cre[ch]) - _mm(xi[:, ch * sw:(ch + 1) * sw], cim[ch]) for ch in range(nch)]
    y = jnp.concatenate(ys, axis=1) if nch > 1 else ys[0]
    z = _gelu_tanh(y + dsk * u)
    gate = _sigmoid(_mm(z, wglu) + bglu)
    return z * gate


def _rk_dims():
    C = D_MODEL // 2
    LW = N_DIR * DECAY_LORA
    GP = _round_up(GATE_LORA, LANE)
    return C, LW, GP


def _rk_pre_fn(ps, w0, wup0, wup1, a0, aup0, aup1, gup, k_k, k_a):
    C, LW, GP = _rk_dims()
    r, k, v = ps[:, 0:C], ps[:, C:2 * C], ps[:, 2 * C:3 * C]
    wdn = ps[:, 3 * C:3 * C + LW]
    adn = ps[:, 3 * C + LW:3 * C + 2 * LW]
    gdn = ps[:, 3 * C + 2 * LW:3 * C + 2 * LW + GP]
    s, st = _seg_mats(C, RWKV_HEAD)
    kk = k * k_k
    n2 = _segsum(kk * kk, s)
    n2 = jnp.where(n2 > 0.0, n2, 1.0)
    inv = 1.0 / jnp.maximum(jnp.sqrt(n2), L2_EPS)
    kkn = kk * _segsum(inv, st)
    tw = jnp.tanh(wdn)
    wup, aup = (wup0, wup1), (aup0, aup1)
    ws, ks, bs = [], [], []
    for d in range(N_DIR):
        wraw = w0[d:d + 1] + _mm(tw, wup[d])
        w = -_softplus(-wraw) - 0.5
        ws.append(jnp.exp(-jnp.exp(w)))
        a = _sigmoid(a0[d:d + 1] + _mm(adn, aup[d]))
        ks.append(k * (1.0 + (a - 1.0) * k_a))
        bs.append(kkn * a)
    g = _mm(_sigmoid(gdn), gup)
    return r, v, kkn, ws[0], ws[1], ks[0], ks[1], bs[0], bs[1], g


def _rk_post_fn(y0, y1, r, v, k0, k1, g, r_k, lng, lnb):
    C = r.shape[1]
    s, st = _seg_mats(C, RWKV_HEAD)
    y = y0 + y1
    mu = _segsum(_segsum(y, s) * (1.0 / RWKV_HEAD), st)
    yc = y - mu
    var = _segsum(_segsum(yc * yc, s) * (1.0 / RWKV_HEAD), st)
    yn = yc * lax.rsqrt(var + GN_EPS) * lng + lnb
    bonus = _segsum(_segsum(r * (k0 + k1) * r_k, s), st)
    return (yn + bonus * v) * g


def rk_shift(proj, mp, mn, col0):
    T = proj.shape[0]
    W = mp.shape[1]
    wl = _tile(math.gcd(W, col0), 256)
    cb0 = col0 // wl

    def body(p_r, mp_r, mn_r, o_r):
        p = p_r[...]
        row = lax.broadcasted_iota(jnp.int32, (T, wl), 0)
        prev = jnp.where(row >= 1, pltpu.roll(p, 1, 0), 0.0)
        nxt = jnp.where(row < T - 1, pltpu.roll(p, T - 1, 0), 0.0)
        o_r[...] = p + mp_r[...] * (prev - p) + mn_r[...] * (nxt - p)

    rs = pl.BlockSpec((1, wl), lambda j: (0, j))
    return pl.pallas_call(
        body, name="rk_shift", grid=(W // wl,),
        in_specs=[pl.BlockSpec((T, wl), lambda j: (0, cb0 + j)), rs, rs],
        out_specs=pl.BlockSpec((T, wl), lambda j: (0, j)),
        out_shape=jax.ShapeDtypeStruct((T, W), F32),
        compiler_params=_cparams(("parallel",)),
    )(proj, mp, mn)


def rk_shift_bwd(dps, proj, mp, mn, col0):
    T, W = dps.shape
    wl = _tile(math.gcd(W, col0), 256)
    cb0 = col0 // wl

    def body(d_r, p_r, mp_r, mn_r, dp_r, dmp_r, dmn_r):
        d, p = d_r[...], p_r[...]
        mpv, mnv = mp_r[...], mn_r[...]
        row = lax.broadcasted_iota(jnp.int32, (T, wl), 0)
        first, last = row >= 1, row < T - 1
        prev = jnp.where(first, pltpu.roll(p, 1, 0), 0.0)
        nxt = jnp.where(last, pltpu.roll(p, T - 1, 0), 0.0)
        dmp_r[...] = jnp.sum(d * (prev - p), axis=0, keepdims=True)
        dmn_r[...] = jnp.sum(d * (nxt - p), axis=0, keepdims=True)
        dp_r[...] = (d * (1.0 - mpv - mnv) + jnp.where(last, pltpu.roll(d * mpv, T - 1, 0), 0.0)
                     + jnp.where(first, pltpu.roll(d * mnv, 1, 0), 0.0))

    rs = pl.BlockSpec((1, wl), lambda j: (0, j))
    blk = pl.BlockSpec((T, wl), lambda j: (0, j))
    return pl.pallas_call(
        body, name="rk_shift_bwd", grid=(W // wl,),
        in_specs=[blk, pl.BlockSpec((T, wl), lambda j: (0, cb0 + j)), rs, rs],
        out_specs=[blk, rs, rs],
        out_shape=[jax.ShapeDtypeStruct((T, W), F32), jax.ShapeDtypeStruct((1, W), F32), jax.ShapeDtypeStruct((1, W), F32)],
        compiler_params=_cparams(("parallel",)),
    )(dps, proj, mp, mn)


RK_FWD_PAIRS = 4
RK_BWD_PAIRS = 2
RK_TIME_BLOCK = 32
RK_LANE_BLOCK = 128


def _rk_blocks(T, C, N, order_reversed, pairs):
    pw = 2 * N
    pp = min(pairs, C // pw)
    tb = min(RK_TIME_BLOCK, T)
    lb = min(RK_LANE_BLOCK, T)
    nb, per = T // tb, lb // tb

    def tix(i):
        return (nb - 1 - i) if order_reversed else i

    rows = pl.BlockSpec((tb, pp * pw), lambda g, i: (tix(i), g))
    cols = pl.BlockSpec((2 * pp, N, lb), lambda g, i: (g, 0, tix(i) // per))
    hist = pl.BlockSpec((tb, pp, N, pw), lambda g, i: (tix(i), g, 0, 0))
    return pp, pw, tb, lb, nb, per, tix, rows, cols, hist


def _aligned(tile_ref, h, off, lb, per):
    return pltpu.roll(tile_ref[h], (lb - off) % lb, 1) if per > 1 else tile_ref[h]


def _spread_columns(tile_ref, out_ref, pp, tb, off, lb, per):
    N, pw = out_ref.shape[-2], out_ref.shape[-1]
    lane = lax.broadcasted_iota(jnp.int32, (N, lb), 1)
    first = lax.broadcasted_iota(jnp.int32, (N, pw), 1) < N
    for p in range(pp):
        tiles = [_aligned(tile_ref, 2 * p + q, off, lb, per) for q in range(2)]
        for t in range(tb):
            c = [jnp.sum(jnp.where(lane == t, tl, 0.0), axis=1, keepdims=True) for tl in tiles]
            out_ref[t, p] = jnp.where(first, c[0], c[1])


def _half_sums(x, first):
    return (jnp.sum(jnp.where(first, x, 0.0), axis=1, keepdims=True),
            jnp.sum(jnp.where(first, 0.0, x), axis=1, keepdims=True))


def rk_scan(r, kk, w, k, b, vT, *, reverse, name):
    T, C = r.shape
    H, N, _ = vT.shape
    pp, pw, tb, lb, nb, per, tix, rows, cols, hist_spec = _rk_blocks(T, C, N, reverse, RK_FWD_PAIRS)

    def body(r_r, kk_r, w_r, k_r, b_r, vT_r, yT_r, hist_r, S, VC, YA):
        i = pl.program_id(1)

        @pl.when(i == 0)
        def _():
            S[...] = jnp.zeros_like(S)

        @pl.when(i % per == 0)
        def _():
            yT_r[...] = jnp.zeros_like(yT_r)

        off = (tix(i) % per) * tb
        first = lax.broadcasted_iota(jnp.int32, (N, pw), 1) < N
        YA[...] = jnp.zeros_like(YA)
        _spread_columns(vT_r, VC, pp, tb, off, lb, per)
        st = [S[p] for p in range(pp)]
        seg = [slice(p * pw, (p + 1) * pw) for p in range(pp)]
        for s in range(tb):
            t = (tb - 1 - s) if reverse else s
            row = slice(t, t + 1)
            sk = [_half_sums(st[p] * kk_r[row, seg[p]], first) for p in range(pp)]
            for p in range(pp):
                hist_r[t, p] = st[p]
                skp = jnp.where(first, sk[p][0], sk[p][1])
                st[p] = st[p] * w_r[row, seg[p]] - skp * b_r[row, seg[p]] + VC[t, p] * k_r[row, seg[p]]
            ys = [_half_sums(st[p] * r_r[row, seg[p]], first) for p in range(pp)]
            for p in range(pp):
                for q in range(2):
                    YA[2 * p + q, :, row] = ys[p][q]
        for p in range(pp):
            S[p] = st[p]
        for h in range(2 * pp):
            yT_r[h] = yT_r[h] + (pltpu.roll(YA[h], off, 1) if per > 1 else YA[h])

    return pl.pallas_call(
        body, name=name, grid=(C // (pp * pw), nb),
        in_specs=[rows] * 5 + [cols], out_specs=[cols, hist_spec],
        out_shape=[jax.ShapeDtypeStruct((H, N, T), F32), jax.ShapeDtypeStruct((T, H // 2, N, pw), F32)],
        scratch_shapes=[pltpu.VMEM((pp, N, pw), F32), pltpu.VMEM((tb, pp, N, pw), F32), pltpu.VMEM((2 * pp, N, lb), F32)],
        compiler_params=_cparams(("parallel", "arbitrary")),
    )(r, kk, w, k, b, vT)


def rk_scan_bwd(r, kk, w, k, b, vT, dyT, hist, *, reverse, name):
    T, C = r.shape
    H, N, _ = vT.shape
    pp, pw, tb, lb, nb, per, tix, rows, cols, hist_spec = _rk_blocks(T, C, N, not reverse, RK_BWD_PAIRS)

    def body(r_r, kk_r, w_r, k_r, b_r, vT_r, dyT_r, hist_r, dr_r, dkk_r, dw_r, dk_r, db_r, dvT_r, G, VC, DC, YA):
        i = pl.program_id(1)

        @pl.when(i == 0)
        def _():
            G[...] = jnp.zeros_like(G)

        @pl.when(i % per == 0)
        def _():
            dvT_r[...] = jnp.zeros_like(dvT_r)

        off = (tix(i) % per) * tb
        first = lax.broadcasted_iota(jnp.int32, (N, pw), 1) < N
        YA[...] = jnp.zeros_like(YA)
        _spread_columns(vT_r, VC, pp, tb, off, lb, per)
        _spread_columns(dyT_r, DC, pp, tb, off, lb, per)
        gs = [G[p] for p in range(pp)]
        seg = [slice(p * pw, (p + 1) * pw) for p in range(pp)]
        for s in range(tb):
            t = s if reverse else (tb - 1 - s)
            row = slice(t, t + 1)
            g = [gs[p] + DC[t, p] * r_r[row, seg[p]] for p in range(pp)]
            sk = [_half_sums(hist_r[t, p] * kk_r[row, seg[p]], first) for p in range(pp)]
            gb = [_half_sums(g[p] * b_r[row, seg[p]], first) for p in range(pp)]
            gk = [_half_sums(g[p] * k_r[row, seg[p]], first) for p in range(pp)]
            for p in range(pp):
                sp = hist_r[t, p]
                kkv, wv, kv, bv = kk_r[row, seg[p]], w_r[row, seg[p]], k_r[row, seg[p]], b_r[row, seg[p]]
                vcol, dycol = VC[t, p], DC[t, p]
                sa = -jnp.where(first, sk[p][0], sk[p][1])
                dsa = jnp.where(first, gb[p][0], gb[p][1])
                sn = sp * wv + sa * bv + vcol * kv
                dr_r[row, seg[p]] = jnp.sum(sn * dycol, axis=0, keepdims=True)
                dw_r[row, seg[p]] = jnp.sum(g[p] * sp, axis=0, keepdims=True)
                db_r[row, seg[p]] = jnp.sum(g[p] * sa, axis=0, keepdims=True)
                dk_r[row, seg[p]] = jnp.sum(g[p] * vcol, axis=0, keepdims=True)
                dkk_r[row, seg[p]] = -jnp.sum(sp * dsa, axis=0, keepdims=True)
                gs[p] = g[p] * wv - dsa * kkv
                for q in range(2):
                    YA[2 * p + q, :, row] = gk[p][q]
        for p in range(pp):
            G[p] = gs[p]
        for h in range(2 * pp):
            dvT_r[h] = dvT_r[h] + (pltpu.roll(YA[h], off, 1) if per > 1 else YA[h])

    return pl.pallas_call(
        body, name=name, grid=(C // (pp * pw), nb),
        in_specs=[rows] * 5 + [cols, cols, hist_spec], out_specs=[rows] * 5 + [cols],
        out_shape=[jax.ShapeDtypeStruct((T, C), F32)] * 5 + [jax.ShapeDtypeStruct((H, N, T), F32)],
        scratch_shapes=[pltpu.VMEM((pp, N, pw), F32), pltpu.VMEM((tb, pp, N, pw), F32), pltpu.VMEM((tb, pp, N, pw), F32),
                        pltpu.VMEM((2 * pp, N, lb), F32)],
        compiler_params=_cparams(("parallel", "arbitrary")),
    )(r, kk, w, k, b, vT, dyT, hist)


def adam(parts, w, m, v, name):
    P, R, C = parts.shape
    tr = _tile(R, max(SUBLANE, (1 << 19) // max(C, 1) // SUBLANE * SUBLANE), SUBLANE)
    c1 = 1.0 / (1.0 - ADAM_B1 ** ADAM_STEP)
    c2 = 1.0 / (1.0 - ADAM_B2 ** ADAM_STEP)

    def body(p_r, w_r, m_r, v_r, g_o, d_o, m_o, v_o):
        g = p_r[0].astype(F32)
        for q in range(1, P):
            g = g + p_r[q].astype(F32)
        m2 = ADAM_B1 * m_r[...] + (1.0 - ADAM_B1) * g
        v2 = ADAM_B2 * v_r[...] + (1.0 - ADAM_B2) * (g * g)
        g_o[...] = g
        m_o[...] = m2
        v_o[...] = v2
        d_o[...] = -ADAM_LR * ((m2 * c1) / (jnp.sqrt(v2 * c2) + ADAM_EPS) + ADAM_WD * w_r[...])

    blk = pl.BlockSpec((tr, C), lambda i: (i, 0))
    return pl.pallas_call(
        body, name=name, grid=(R // tr,),
        in_specs=[pl.BlockSpec((P, tr, C), lambda i: (0, i, 0)), blk, blk, blk], out_specs=[blk] * 4,
        out_shape=[jax.ShapeDtypeStruct((R, C), F32)] * 4,
        compiler_params=_cparams(("parallel",)),
    )(parts, w, m, v)


def _pack_flat(arrs):
    rows = []
    for a in arrs:
        f = a.reshape(-1).astype(F32)
        n = _round_up(f.shape[0], SUBLANE * LANE)
        rows.append(jnp.pad(f, (0, n - f.shape[0])).reshape(-1, LANE))
    return jnp.concatenate(rows, axis=0)


def _unpack_flat(packed, shapes):
    out, r0 = [], 0
    for s in shapes:
        n = math.prod(s)
        nr = _round_up(n, SUBLANE * LANE) // LANE
        out.append(packed[r0:r0 + nr].reshape(-1)[:n].reshape(s))
        r0 += nr
    return out


def _pack_rows(arrs):
    rows = []
    for a in arrs:
        f = a.reshape(-1, a.shape[-1]).astype(F32)
        n = _round_up(f.shape[0], SUBLANE)
        rows.append(jnp.pad(f, ((0, n - f.shape[0]), (0, 0))))
    return jnp.concatenate(rows, axis=0)


def _unpack_rows(packed, shapes):
    out, r0 = [], 0
    for s in shapes:
        nr = math.prod(s[:-1])
        out.append(packed[r0:r0 + nr].reshape(s))
        r0 += _round_up(nr, SUBLANE)
    return out


def _cols_from_slots(g):
    return jnp.moveaxis(g, 0, -2).reshape(g.shape[1:-1] + (N_DEV * g.shape[-1],))


def _cols_to_slots(a):
    cs = a.shape[-1] // N_DEV
    return jnp.moveaxis(a.reshape(a.shape[:-1] + (N_DEV, cs)), -2, 0)


def _step(P, M, V):
    D, T = D_MODEL, SEQ
    S5W = D // 2
    C, LW, GP = _rk_dims()
    H = C // RWKV_HEAD
    G = S5W // S5_GROUP
    NS = G * S5_STATE
    NCH = G // S5_CHUNK_GROUPS
    SW = S5_CHUNK_GROUPS * S5_STATE
    RIN = 3 * C + 2 * LW + GATE_LORA
    RINP = 3 * C + 2 * LW + GP
    PROJ = S5W + RIN
    PROJP = S5W + RINP
    FF = 4 * D
    me = 4 * lax.axis_index("x") + 2 * lax.axis_index("y") + lax.axis_index("c")
    cs_mod = N_MOD * D // N_DEV
    eye = jnp.eye(S5_CHUNK_GROUPS, dtype=F32)

    x = P['x'][0]
    target = P['loss_target'][0]

    (c_all,) = exchange([P['c']], ['gather'], "comm_gather_c")
    c_all = c_all.reshape(N_DEV, D)
    (c_act,), _ = rowcall(lambda cv: ((cv * _sigmoid(cv),), ()), "silu_c", [c_all], [], [(D, F32)], [], N_DEV)
    ada_b_loc = lax.dynamic_slice(P['ada_b'], (0, me * cs_mod), (1, cs_mod))
    mod_loc = matmul(c_act, P['ada_w'][0], name="mod_mm", precise=True, extras=[(ada_b_loc, 'n')],
                     epi=lambda acc, bias: (acc + bias,))

    rk_shapes = [P[n][0].shape for n in RKPACK]
    rk_pack = _pack_rows([P[n][0] for n in RKPACK])
    gathered = exchange(
        [mod_loc, P['w_in'][0].astype(BF16), P['w_out'][0].astype(BF16), P['ffn_w1'][0].astype(BF16),
         P['ffn_w2'][0].astype(BF16), P['s5_w_glu'][0].astype(BF16), rk_pack],
        ['gather'] * 7, "comm_gather_weights")
    mod_all, w_in_g, w_out_g, w1_g, w2_g, wglu_g, rk_g = gathered
    mod_me = lax.dynamic_index_in_dim(mod_all, me, axis=1, keepdims=False).reshape(N_MOD, 1, D)
    shift1, scale1, gate1, shift2, scale2, gate2 = (mod_me[i] for i in range(N_MOD))
    w_in = jnp.pad(_cols_from_slots(w_in_g), ((0, 0), (0, PROJP - PROJ)))
    w_out = w_out_g.reshape(D, D)
    w2 = w2_g.reshape(FF, D)
    wglu = wglu_g.reshape(S5W, S5W)
    rk_full = _unpack_rows(_cols_from_slots(rk_g), [s[:-1] + (C,) for s in rk_shapes])
    rk_w0, rk_a0, rk_wup, rk_aup, rk_gup = rk_full

    def lora_pad(up):
        z = jnp.zeros((N_DIR, LW, C), F32)
        for d in range(N_DIR):
            z = z.at[d, d * DECAY_LORA:(d + 1) * DECAY_LORA].set(up[d])
        return z

    wup_p, aup_p = lora_pad(rk_wup), lora_pad(rk_aup)
    gup_p = jnp.pad(rk_gup, ((0, GP - GATE_LORA), (0, 0)))
    mu_prev = jnp.pad(P['rk_shift_prev'], ((0, 0), (0, RINP - RIN)))
    mu_next = jnp.pad(P['rk_shift_next'], ((0, 0), (0, RINP - RIN)))
    r_k = P['rk_r_k'].reshape(1, C)
    fgain = P['final_gain'].reshape(1, D)

    TT = 256
    (h1,), _ = rowcall(lambda xv, g, sc, sh: ((_normmod(xv, g, sc, sh),), ()), "norm1",
                       [x], [P['norm1_gain'], scale1, shift1], [(D, BF16)], [], TT)
    proj = matmul(h1, w_in, name="proj_mm")

    lre = P['s5_lambda_re'][0].reshape(N_DIR, NS)
    lim = P['s5_lambda_im'][0].reshape(N_DIR, NS)
    lstep = jnp.broadcast_to(P['s5_log_step'][0][:, :, None], (N_DIR, G, S5_STATE)).reshape(N_DIR, NS)
    bre = P['s5_b_re'][0].reshape(NS, S5_GROUP).T
    bim = P['s5_b_im'][0].reshape(NS, S5_GROUP).T
    lbr, lbi, bbr, bbi = s5_prep(lre, lim, lstep, bre, bim)
    bbar = jnp.stack([bbr, bbi], axis=1).reshape(N_DIR, 2, S5_GROUP, NCH, S5_CHUNK_GROUPS, S5_STATE)
    wblk = jnp.einsum('drhcgp,gk->cghdrkp', bbar, eye).reshape(NCH, S5_CHUNK_GROUPS * S5_GROUP, 4 * SW)
    wblk = wblk.astype(MXU_DTYPE)
    u_view = (proj, S5W, 0)
    bus, _ = rowcall(lambda uv, wb: (_s5_bu_fn(uv, wb), ()), "s5_bu", [u_view], [wblk], [(NS, F32)] * 4, [], TT)
    s0r, s0i = s5_scan(bus[0], bus[1], lbr[0:1], lbi[0:1], reverse=False, name="s5_scan_f0")
    s1r, s1i = s5_scan(bus[2], bus[3], lbr[1:2], lbi[1:2], reverse=True, name="s5_scan_f1")

    def cblk(cm):
        c4 = cm.reshape(NCH, S5_CHUNK_GROUPS, S5_GROUP, S5_STATE)
        return jnp.einsum('cghp,gk->cgpkh', c4, eye).reshape(NCH, SW, S5_CHUNK_GROUPS * S5_GROUP)

    cre_b = cblk(P['s5_c_re'][0]).astype(MXU_DTYPE)
    cim_b = cblk(P['s5_c_im'][0]).astype(MXU_DTYPE)
    s5_full = [cre_b, cim_b, P['s5_d'], wglu, P['s5_b_glu']]
    TS = 128
    (y_s5,), _ = rowcall(lambda *a: ((_s5_out_fn(*a),), ()), "s5_out", [s0r, s0i, s1r, s1i, u_view], s5_full,
                         [(S5W, BF16)], [], TS)

    ps = rk_shift(proj, mu_prev, mu_next, S5W)
    pre_full = [rk_w0, wup_p[0], wup_p[1], rk_a0, aup_p[0], aup_p[1], gup_p, P['rk_k_k'], P['rk_k_a']]
    pre_out, _ = rowcall(lambda *a: (_rk_pre_fn(*a)[2:], ()), "rk_pre", [ps], pre_full, [(C, F32)] * 8, [], TS)
    kkn, w_0, w_1, k_0, k_1, b_0, b_1, g_gate = pre_out
    r_t, v_t = ps[:, 0:C], ps[:, 2 * C:3 * C]

    def hmT(a):
        return a.reshape(T, H, RWKV_HEAD).transpose(1, 2, 0)

    def unT(a):
        return a.transpose(2, 0, 1).reshape(T, C)

    vT_h = hmT(v_t)
    dir_rows = [(w_0, k_0, b_0), (w_1, k_1, b_1)]
    yT, hist = [], []
    for d in range(N_DIR):
        wd, kd, bd = dir_rows[d]
        yd, hd = rk_scan(r_t, kkn, wd, kd, bd, vT_h, reverse=(d == 1), name=f"rk_scan_f{d}")
        yT.append(yd)
        hist.append(hd)
    y_0, y_1 = unT(yT[0]), unT(yT[1])
    post_full = [r_k, P['rk_ln_gain'], P['rk_ln_bias']]
    post_tiled = [y_0, y_1, (ps, C, 0), (ps, C, 2), k_0, k_1, g_gate]
    (y_rk,), _ = rowcall(lambda *a: ((_rk_post_fn(*a),), ()), "rk_post", post_tiled, post_full, [(C, BF16)], [], TS)

    ycat = jnp.concatenate([y_s5, y_rk], axis=1)
    mixed = matmul(ycat, w_out, name="out_mm")

    def res_norm(xv, mv, gate, g, sc, sh):
        x1v = xv + gate * mv
        return x1v, _normmod(x1v, g, sc, sh)

    (x1, h2), _ = rowcall(lambda *a: (res_norm(*a), ()), "norm2", [x, mixed], [gate1, P['norm2_gain'], scale2, shift2],
                          [(D, F32), (D, BF16)], [], TT)
    a_ff, hh_ff = matmul(h2, w1_g, name="ffn1_mm", b_slots=True, out_dtypes=(F32, BF16),
                         epi=lambda acc: (acc, jnp.square(jnp.maximum(acc, 0.0))))
    ffn = matmul(hh_ff, w2, name="ffn2_mm")

    def loss_fn(x1v, fv, tg, gate, fg):
        def f(x1_, f_, gate_, fg_):
            out = _rms(x1_ + gate_ * f_, fg_)
            err = out - tg
            return 0.5 * jnp.sum(jnp.sum(err * err, axis=1, keepdims=True), axis=0, keepdims=True) * (1.0 / D)
        lv, vjp = jax.vjp(f, x1v, fv, gate, fg)
        dx1, dff, dgate, dfg = vjp(jnp.ones((1, 1), F32))
        return (dx1, dff), (jnp.broadcast_to(lv, (SUBLANE, LANE)), dgate, dfg)

    (dx2, dffn), (loss_t, dgate2, dfgain) = rowcall(
        loss_fn, "loss", [x1, ffn, target], [gate2, fgain], [(D, F32), (D, BF16)], [(SUBLANE, LANE), (1, D), (1, D)], TT)
    loss = lax.psum(loss_t[0, 0], MESH_AXES)

    da = matmul(dffn, w2, name="dffn2_mm", tb=True, out_dtypes=(BF16,), extras=[(a_ff, 'mn')],
                epi=lambda acc, av: (acc * (2.0 * jnp.maximum(av, 0.0)),))
    g_w2 = matmul(hh_ff, dffn, name="gw2_mm", ta=True, out_dtypes=(BF16,))
    dh2 = matmul(da, w1_g, name="dh2_mm", tb=True, b_slots=True)
    g_w1 = matmul(h2, da, name="gw1_mm", ta=True, out_slots=N_DEV, out_dtypes=(BF16,))

    def res_norm_bwd(dx2v, dh2v, xv, mv, gate, g, sc, sh):
        _, vjp = jax.vjp(res_norm, xv, mv, gate, g, sc, sh)
        dx, dm, dgate, dg, dsc, dsh = vjp((dx2v, dh2v))
        return (dx, dm), (dgate, dg, dsc, dsh)

    (dx1, dmixed), (dgate1, dgain2, dscale2, dshift2) = rowcall(
        res_norm_bwd, "norm2_bwd", [dx2, dh2, x, mixed], [gate1, P['norm2_gain'], scale2, shift2],
        [(D, F32), (D, BF16)], [(1, D)] * 4, TT)

    dycat = matmul(dmixed, w_out, name="dycat_mm", tb=True)
    g_wout = matmul(ycat, dmixed, name="gwout_mm", ta=True, out_dtypes=(BF16,))

    def post_bwd(dy, *a):
        _, vjp = jax.vjp(_rk_post_fn, *a)
        gy0, gy1, gr, gv, gk0, gk1, gg, grk, glg, glb = vjp(dy)
        return (gy0, gr, gv, gk0, gk1, gg), (grk, glg, glb)

    cb_rk = S5W // C if C else 0
    (dy_rk, dr_p, dv_p, dk0_p, dk1_p, dg_p), (g_rk_rk, g_lng, g_lnb) = rowcall(
        post_bwd, "rk_post_bwd", [(dycat, C, cb_rk)] + post_tiled, post_full, [(C, F32)] * 6, [(1, C)] * 3, TS)
    dyT_h = hmT(dy_rk)
    scan_g = []
    for d in range(N_DIR):
        wd, kd, bd = dir_rows[d]
        scan_g.append(rk_scan_bwd(r_t, kkn, wd, kd, bd, vT_h, dyT_h, hist[d], reverse=(d == 1), name=f"rk_scan_b{d}"))
    cot = [dr_p, scan_g[0][0], scan_g[1][0],
           dv_p, unT(scan_g[0][5]), unT(scan_g[1][5]),
           scan_g[0][1], scan_g[1][1],
           scan_g[0][2], scan_g[1][2],
           dk0_p, scan_g[0][3], dk1_p, scan_g[1][3],
           scan_g[0][4], scan_g[1][4],
           dg_p]

    def pre_bwd(psv, r0, r1, r2, v0, v1, v2, q0, q1, dw0, dw1, k0a, k0b, k1a, k1b, db0, db1, dgv, *params):
        _, vjp = jax.vjp(_rk_pre_fn, psv, *params)
        grads = vjp((r0 + r1 + r2, v0 + v1 + v2, q0 + q1, dw0, dw1, k0a + k0b, k1a + k1b, db0, db1, dgv))
        return (grads[0],), tuple(grads[1:])

    (dps,), pre_g = rowcall(pre_bwd, "rk_pre_bwd", [ps] + cot, pre_full, [(RINP, F32)],
                            [f.shape for f in pre_full], TS)
    g_w0, g_wup0, g_wup1, g_a0, g_aup0, g_aup1, g_gup_p, g_kk, g_ka = pre_g
    g_wup_p, g_aup_p = jnp.stack([g_wup0, g_wup1]), jnp.stack([g_aup0, g_aup1])
    dp_rk, g_mup, g_mun = rk_shift_bwd(dps, proj, mu_prev, mu_next, S5W)

    def s5_out_bwd(dy, *a):
        a = [t.astype(F32) for t in a]
        _, vjp = jax.vjp(_s5_out_fn, *a)
        g = vjp(dy)
        return (g[0], g[1], g[4]), tuple(g[5:])

    (dxr, dxi, du_a), s5_pg = rowcall(
        s5_out_bwd, "s5_out_bwd", [(dycat, S5W, 0), s0r, s0i, s1r, s1i, u_view], s5_full,
        [(NS, F32), (NS, F32), (S5W, F32)], [f.shape for f in s5_full], TS)
    g_creb, g_cimb, g_s5d, g_wglu, g_bglu = s5_pg
    l0r, l0i = s5_scan(dxr, dxi, lbr[0:1], -lbi[0:1], reverse=True, name="s5_scan_b0")
    l1r, l1i = s5_scan(dxr, dxi, lbr[1:2], -lbi[1:2], reverse=False, name="s5_scan_b1")
    dl0r, dl0i = s5_dlam(s0r, s0i, l0r, l0i, reverse=False, name="s5_dlam0")
    dl1r, dl1i = s5_dlam(s1r, s1i, l1r, l1i, reverse=True, name="s5_dlam1")

    def bu_bwd(uv, g0, g1, g2, g3, wb):
        _, vjp = jax.vjp(_s5_bu_fn, uv, wb.astype(F32))
        du, dwb = vjp((g0, g1, g2, g3))
        return (du,), (dwb,)

    (du_b,), (g_wblk,) = rowcall(bu_bwd, "s5_bu_bwd", [u_view, l0r, l0i, l1r, l1i], [wblk], [(S5W, F32)],
                                 [wblk.shape], TS)
    g_bbar = jnp.einsum('cghdrkp,gk->drhcgp',
                        g_wblk.reshape(NCH, S5_CHUNK_GROUPS, S5_GROUP, N_DIR, 2, S5_CHUNK_GROUPS, S5_STATE), eye)
    g_bbar = g_bbar.reshape(N_DIR, 2, S5_GROUP, NS)
    g_lre, g_lim, g_lstep, g_bre, g_bim = s5_prep_bwd(
        lre, lim, lstep, bre, bim, jnp.concatenate([dl0r, dl1r], 0), jnp.concatenate([dl0i, dl1i], 0),
        g_bbar[:, 0], g_bbar[:, 1])

    def uncblk(gb):
        g5 = gb.reshape(NCH, S5_CHUNK_GROUPS, S5_STATE, S5_CHUNK_GROUPS, S5_GROUP)
        return jnp.einsum('cgpkh,gk->cghp', g5, eye).reshape(G, S5_GROUP, S5_STATE)

    (du_tot,), _ = rowcall(lambda a, b_: ((a + b_,), ()), "s5_du_sum", [du_a, du_b], [], [(S5W, BF16)], [], TT)
    dproj = jnp.concatenate([du_tot, dp_rk.astype(BF16)], axis=1)
    dh1 = matmul(dproj, w_in, name="dh1_mm", tb=True)
    g_win = matmul(h1, dproj, name="gwin_mm", ta=True, out_dtypes=(BF16,))

    def norm1_bwd(dx1v, dh1v, xv, g, sc, sh):
        _, vjp = jax.vjp(_normmod, xv, g, sc, sh)
        dx, dg, dsc, dsh = vjp(dh1v)
        return (dx1v + dx,), (dg, dsc, dsh)

    (grad_x,), (dgain1, dscale1, dshift1) = rowcall(
        norm1_bwd, "norm1_bwd", [dx1, dh1, x], [P['norm1_gain'], scale1, shift1], [(D, F32)], [(1, D)] * 3, TT)

    dmod = jnp.concatenate([dshift1, dscale1, dgate1, dshift2, dscale2, dgate2], axis=1)
    lstep_g = g_lstep.reshape(N_DIR, G, S5_STATE)
    small_g = {
        'ada_b': dmod, 'norm1_gain': dgain1, 'norm2_gain': dgain2, 'final_gain': dfgain.reshape(D),
        's5_lambda_re': g_lre.reshape(1, N_DIR, G, S5_STATE), 's5_lambda_im': g_lim.reshape(1, N_DIR, G, S5_STATE),
        's5_log_step': lstep_g,
        's5_b_re': g_bre.T.reshape(1, G, S5_STATE, S5_GROUP), 's5_b_im': g_bim.T.reshape(1, G, S5_STATE, S5_GROUP),
        's5_c_re': uncblk(g_creb)[None], 's5_c_im': uncblk(g_cimb)[None],
        's5_d': g_s5d, 's5_b_glu': g_bglu,
        'rk_shift_prev': g_mup[:, :RIN], 'rk_shift_next': g_mun[:, :RIN],
        'rk_k_k': g_kk, 'rk_k_a': g_ka, 'rk_r_k': g_rk_rk.reshape(1, H, RWKV_HEAD),
        'rk_ln_gain': g_lng, 'rk_ln_bias': g_lnb,
    }
    small_shapes = {n: P[n].shape for n in SMALL}
    small_shapes['s5_log_step'] = (N_DIR, G, S5_STATE)
    small_pack = _pack_flat([small_g[n] for n in SMALL])

    def lora_unpad(gp):
        return jnp.stack([gp[d, d * DECAY_LORA:(d + 1) * DECAY_LORA] for d in range(N_DIR)])

    rk_grads = {'rk_w0': g_w0, 'rk_a0': g_a0, 'rk_w_up': lora_unpad(g_wup_p), 'rk_a_up': lora_unpad(g_aup_p),
                'rk_g_up': g_gup_p[:GATE_LORA]}
    rk_gpack = jnp.stack([_pack_rows([_cols_to_slots(rk_grads[n])[j] for n in RKPACK]) for j in range(N_DEV)])
    g_win_s = _cols_to_slots(g_win[:, :PROJ])
    ex = exchange(
        [small_pack, g_win_s, g_wout.reshape(N_DEV, D // N_DEV, D), g_w1, g_w2.reshape(N_DEV, FF // N_DEV, D),
         g_wglu.reshape(N_DEV, S5W // N_DEV, S5W), rk_gpack],
        ['gather'] + ['scatter'] * 6, "comm_grads")
    small_all, win_parts, wout_parts, w1_parts, w2_parts, wglu_parts, rk_parts = ex

    res = {}

    def put(name, g, dl, m2, v2):
        shp = P[name].shape
        res[name] = tuple(t.reshape(shp) for t in (g, dl, m2, v2))

    def adam2d(name, parts):
        shp = P[name].shape
        r2 = (math.prod(shp[:-1]), shp[-1])
        put(name, *adam(parts.reshape((parts.shape[0],) + r2), P[name].reshape(r2), M[name].reshape(r2),
                        V[name].reshape(r2), "adam_" + name))

    adam2d('w_in', win_parts)
    adam2d('w_out', wout_parts)
    adam2d('ffn_w1', w1_parts)
    adam2d('ffn_w2', w2_parts)
    adam2d('s5_w_glu', wglu_parts)
    off = 0
    for n in SMALL:
        if n == 'ada_b':
            break
        off += _round_up(math.prod(small_shapes[n]), SUBLANE * LANE) // LANE
    nrow_b = N_MOD * D // LANE
    dmod_all = small_all[:, off:off + nrow_b].reshape(N_DEV, N_MOD * D)
    dmod_cols = lax.dynamic_slice(dmod_all, (0, me * cs_mod), (N_DEV, cs_mod))
    g_adaw = matmul(c_act, dmod_cols, name="gadaw_mm", ta=True, precise=True)
    adam2d('ada_w', g_adaw[None])
    small_w = dict(P)
    small_m, small_v = dict(M), dict(V)
    rk_res = adam(rk_parts, rk_pack, _pack_rows([M[n][0] for n in RKPACK]), _pack_rows([V[n][0] for n in RKPACK]),
                  "adam_rkpack")
    for name, parts4 in zip(RKPACK, zip(*[_unpack_rows(t, rk_shapes) for t in rk_res])):
        put(name, *parts4)
    return loss, grad_x, res, (small_all, small_shapes, small_w, small_m, small_v)


def _small_update(small_all, small_shapes, P, M, V, res):
    G = (D_MODEL // 2) // S5_GROUP
    names = [n for n in SMALL if n != 's5_log_step']
    shapes = [small_shapes[n] for n in SMALL]
    parts = _unpack_flat_batched(small_all, shapes)
    by = dict(zip(SMALL, parts))
    ls = by['s5_log_step']
    ls = ls.transpose(0, 3, 1, 2).reshape(N_DEV * S5_STATE, N_DIR * G)
    pk = lambda d: _pack_flat([d[n] for n in names])
    packs = jnp.stack([_pack_flat([by[n][j] for n in names]) for j in range(N_DEV)])
    out = adam(packs, pk(P), pk(M), pk(V), "adam_small")
    shp = [P[n].shape for n in names]
    for name, parts4 in zip(names, zip(*[_unpack_flat(t, shp) for t in out])):
        res[name] = parts4
    lsw = lambda d: jnp.pad(d['s5_log_step'].reshape(1, N_DIR * G), ((0, SUBLANE - 1), (0, 0)))
    ls_parts = jnp.pad(ls[:, None, :], ((0, 0), (0, SUBLANE - 1), (0, 0)))
    o = adam(ls_parts, lsw(P), lsw(M), lsw(V), "adam_log_step")
    res['s5_log_step'] = tuple(t[0:1].reshape(P['s5_log_step'].shape) for t in o)


def _unpack_flat_batched(packed, shapes):
    out, r0 = [], 0
    B = packed.shape[0]
    for s in shapes:
        n = math.prod(s)
        nr = _round_up(n, SUBLANE * LANE) // LANE
        out.append(packed[:, r0:r0 + nr].reshape(B, -1)[:, :n].reshape((B,) + tuple(s)))
        r0 += nr
    return out


def kernel(x, c, ada_w, ada_b, norm1_gain, norm2_gain, final_gain, w_in, w_out, s5_lambda_re, s5_lambda_im, s5_log_step, s5_b_re, s5_b_im, s5_c_re, s5_c_im, s5_d, s5_w_glu, s5_b_glu, rk_shift_prev, rk_shift_next, rk_w0, rk_w_up, rk_a0, rk_a_up, rk_g_up, rk_k_k, rk_k_a, rk_r_k, rk_ln_gain, rk_ln_bias, ffn_w1, ffn_w2, loss_target, m_ada_w, m_ada_b, m_norm1_gain, m_norm2_gain, m_final_gain, m_w_in, m_w_out, m_s5_lambda_re, m_s5_lambda_im, m_s5_log_step, m_s5_b_re, m_s5_b_im, m_s5_c_re, m_s5_c_im, m_s5_d, m_s5_w_glu, m_s5_b_glu, m_rk_shift_prev, m_rk_shift_next, m_rk_w0, m_rk_w_up, m_rk_a0, m_rk_a_up, m_rk_g_up, m_rk_k_k, m_rk_k_a, m_rk_r_k, m_rk_ln_gain, m_rk_ln_bias, m_ffn_w1, m_ffn_w2, v_ada_w, v_ada_b, v_norm1_gain, v_norm2_gain, v_final_gain, v_w_in, v_w_out, v_s5_lambda_re, v_s5_lambda_im, v_s5_log_step, v_s5_b_re, v_s5_b_im, v_s5_c_re, v_s5_c_im, v_s5_d, v_s5_w_glu, v_s5_b_glu, v_rk_shift_prev, v_rk_shift_next, v_rk_w0, v_rk_w_up, v_rk_a0, v_rk_a_up, v_rk_g_up, v_rk_k_k, v_rk_k_a, v_rk_r_k, v_rk_ln_gain, v_rk_ln_bias, v_ffn_w1, v_ffn_w2):
    given = dict(locals())
    P = {n: given[n] for n in ['x', 'c', 'loss_target'] + WEIGHTS}
    M = {n: given['m_' + n] for n in WEIGHTS}
    V = {n: given['v_' + n] for n in WEIGHTS}
    loss, grad_x, res, small = _step(P, M, V)
    small_all, small_shapes, _, _, _ = small
    _small_update(small_all, small_shapes, P, M, V, res)
    outs = [loss, grad_x[None]]
    for q in range(4):
        outs += [res[n][q] for n in WEIGHTS]
    return tuple(outs)
```

```python
import functools
import math

import jax
import jax.numpy as jnp
from jax import lax
from jax.experimental import pallas as pl
from jax.experimental.pallas import tpu as pltpu

F32 = jnp.float32
BF16 = jnp.bfloat16
HI = lax.Precision.HIGHEST
MXU_DTYPE = jnp.bfloat16

N_DEV = 8
MESH_AXES = ("x", "y", "c")
D_MODEL = 2048
SEQ = 2048
S5_GROUP = 16
S5_STATE = 64
RWKV_HEAD = 64
DECAY_LORA = 64
GATE_LORA = 160
N_DIR = 2
N_MOD = 6
NORM_EPS = 1e-6
GN_EPS = 64e-5
L2_EPS = 1e-12
ADAM_LR = 0.001
ADAM_B1 = 0.9
ADAM_B2 = 0.999
ADAM_EPS = 1e-08
ADAM_WD = 0.01
ADAM_STEP = 10
LANE = 128
SUBLANE = 8
S5_CHUNK_GROUPS = 8
S5_SCAN_ROWS = 256
VMEM_LIMIT = 56 * 1024 * 1024

WEIGHTS = ['ada_w', 'ada_b', 'norm1_gain', 'norm2_gain', 'final_gain', 'w_in', 'w_out', 's5_lambda_re',
           's5_lambda_im', 's5_log_step', 's5_b_re', 's5_b_im', 's5_c_re', 's5_c_im', 's5_d', 's5_w_glu',
           's5_b_glu', 'rk_shift_prev', 'rk_shift_next', 'rk_w0', 'rk_w_up', 'rk_a0', 'rk_a_up', 'rk_g_up',
           'rk_k_k', 'rk_k_a', 'rk_r_k', 'rk_ln_gain', 'rk_ln_bias', 'ffn_w1', 'ffn_w2']
SMALL = ['ada_b', 'norm1_gain', 'norm2_gain', 'final_gain', 's5_lambda_re', 's5_lambda_im', 's5_log_step',
         's5_b_re', 's5_b_im', 's5_c_re', 's5_c_im', 's5_d', 's5_b_glu', 'rk_shift_prev', 'rk_shift_next',
         'rk_k_k', 'rk_k_a', 'rk_r_k', 'rk_ln_gain', 'rk_ln_bias']
RKPACK = ['rk_w0', 'rk_a0', 'rk_w_up', 'rk_a_up', 'rk_g_up']


def _round_up(n, m):
    return (n + m - 1) // m * m


def _tile(dim, pref, unit=LANE):
    t = min(pref, dim) // unit * unit
    while t >= unit:
        if dim % t == 0:
            return t
        t -= unit
    return dim


def _cparams(sem=None):
    return pltpu.CompilerParams(dimension_semantics=sem, vmem_limit_bytes=VMEM_LIMIT)


def _full_spec(a):
    nd = a.ndim
    return pl.BlockSpec(a.shape, lambda *_: (0,) * nd)


def exchange(arrs, modes, name):
    n = len(arrs)
    out_shape = [jax.ShapeDtypeStruct((N_DEV,) + a.shape if m == 'gather' else a.shape, a.dtype)
                 for a, m in zip(arrs, modes)]

    def body(*refs):
        ins, outs = refs[:n], refs[n:2 * n]
        send_sems, recv_sems, local_sems = refs[2 * n:]
        x, y, c = (lax.axis_index(a) for a in MESH_AXES)
        me = 4 * x + 2 * y + c
        copies = []
        for i in range(n):
            gather = modes[i] == 'gather'
            mine = pltpu.make_async_copy(ins[i] if gather else ins[i].at[me], outs[i].at[me], local_sems.at[i])
            mine.start()
            copies.append(mine)
        remote = []
        for k in range(1, N_DEV):
            px = 1 - x if (k >> 2) & 1 else x
            py = 1 - y if (k >> 1) & 1 else y
            pc = 1 - c if k & 1 else c
            peer = 4 * px + 2 * py + pc
            for i in range(n):
                src = ins[i] if modes[i] == 'gather' else ins[i].at[peer]
                cp = pltpu.make_async_remote_copy(
                    src_ref=src, dst_ref=outs[i].at[me], send_sem=send_sems.at[i, k - 1],
                    recv_sem=recv_sems.at[i, k - 1], device_id=(px, py, pc), device_id_type=pl.DeviceIdType.MESH)
                cp.start()
                remote.append(cp)
        for cp in remote:
            cp.wait_recv()
        for cp in remote:
            cp.wait_send()
        for cp in copies:
            cp.wait()

    any_spec = pl.BlockSpec(memory_space=pl.ANY)
    return pl.pallas_call(
        body, name=name, out_shape=out_shape,
        in_specs=[any_spec] * n, out_specs=[any_spec] * n,
        scratch_shapes=[pltpu.SemaphoreType.DMA((n, N_DEV - 1)), pltpu.SemaphoreType.DMA((n, N_DEV - 1)),
                        pltpu.SemaphoreType.DMA((n,))],
        compiler_params=pltpu.CompilerParams(has_side_effects=True),
    )(*arrs)


def matmul(a, b, *, name, ta=False, tb=False, b_slots=False, out_slots=0, out_dtypes=(F32,), epi=None,
           extras=(), precise=False, tm=512, tn=512, tk=2048):
    if ta:
        K, M = a.shape
    else:
        M, K = a.shape
    if b_slots:
        ns, br, bc = b.shape
        bshape = (br, ns * bc)
    else:
        bshape = b.shape
    N = bshape[0] if tb else bshape[1]
    assert (bshape[1] if tb else bshape[0]) == K, (a.shape, b.shape, ta, tb)
    tm, tn, tk = _tile(M, tm, SUBLANE), _tile(N, tn), _tile(K, tk, SUBLANE if K < LANE else LANE)
    if b_slots and tb:
        tk = _tile(b.shape[2], tk)
    elif b_slots:
        tn = _tile(b.shape[2], tn)
    if out_slots:
        tn = _tile(N // out_slots, tn)
    if b_slots:
        cs = b.shape[2]
        tcol = tk if tb else tn
        assert cs % tcol == 0
        per = cs // tcol
    if out_slots:
        ncs = N // out_slots
        assert ncs % tn == 0
        operc = ncs // tn
    nk = K // tk
    a_spec = pl.BlockSpec((tk, tm), lambda i, j, k: (k, i)) if ta else pl.BlockSpec((tm, tk), lambda i, j, k: (i, k))
    if b_slots:
        if tb:
            b_spec = pl.BlockSpec((None, tn, tk), lambda i, j, k: (k // per, j, k % per))
        else:
            b_spec = pl.BlockSpec((None, tk, tn), lambda i, j, k: (j // per, k, j % per))
    else:
        b_spec = pl.BlockSpec((tn, tk), lambda i, j, k: (j, k)) if tb else pl.BlockSpec((tk, tn), lambda i, j, k: (k, j))
    ex_specs = []
    for arr, kind in extras:
        if kind == 'mn':
            ex_specs.append(pl.BlockSpec((tm, tn), lambda i, j, k: (i, j)))
        else:
            ex_specs.append(pl.BlockSpec((1, tn), lambda i, j, k: (0, j)))
    if out_slots:
        o_spec = pl.BlockSpec((None, tm, tn), lambda i, j, k: (j // operc, i, j % operc))
        o_shape = (out_slots, M, ncs)
    else:
        o_spec = pl.BlockSpec((tm, tn), lambda i, j, k: (i, j))
        o_shape = (M, N)
    ne, no = len(extras), len(out_dtypes)
    dims = (((0 if ta else 1,), (1 if tb else 0,)), ((), ()))
    op_dtype = F32 if precise else MXU_DTYPE

    def body(a_ref, b_ref, *rest):
        ex_refs, out_refs, acc = rest[:ne], rest[ne:ne + no], rest[-1]
        k = pl.program_id(2)
        part = lax.dot_general(a_ref[...].astype(op_dtype), b_ref[...].astype(op_dtype), dims,
                               precision=HI if precise else None, preferred_element_type=F32)

        def finish(total):
            res = epi(total, *[e[...] for e in ex_refs]) if epi is not None else (total,)
            for o, r in zip(out_refs, res):
                o[...] = r.astype(o.dtype)

        if nk == 1:
            finish(part)
        else:
            @pl.when(k == 0)
            def _():
                acc[...] = part

            @pl.when(jnp.logical_and(k > 0, k < nk - 1))
            def _():
                acc[...] += part

            @pl.when(k == nk - 1)
            def _():
                finish(acc[...] + part)

    outs = pl.pallas_call(
        body, name=name, grid=(M // tm, N // tn, nk),
        in_specs=[a_spec, b_spec] + ex_specs, out_specs=[o_spec] * no,
        out_shape=[jax.ShapeDtypeStruct(o_shape, dt) for dt in out_dtypes],
        scratch_shapes=[pltpu.VMEM((tm, tn), F32)],
        compiler_params=_cparams(("parallel", "parallel", "arbitrary")),
    )(a, b, *[e[0] for e in extras])
    return outs[0] if no == 1 else outs


def rowcall(fn, name, tiled, full, tiled_out, acc_out, tt):
    views = [(t, t.shape[1], 0) if not isinstance(t, tuple) else t for t in tiled]
    T = views[0][0].shape[0]
    tt = _tile(T, tt, SUBLANE)
    nt, nf, nto, nao = len(views), len(full), len(tiled_out), len(acc_out)

    def view_spec(w, cb):
        return pl.BlockSpec((tt, w), lambda i: (i, cb))

    in_specs = [view_spec(w, cb) for _, w, cb in views] + [_full_spec(f) for f in full]
    out_specs = [pl.BlockSpec((tt, w), lambda i: (i, 0)) for w, _ in tiled_out]
    out_specs += [pl.BlockSpec(s, lambda i, nd=len(s): (0,) * nd) for s in acc_out]
    out_shape = [jax.ShapeDtypeStruct((T, w), dt) for w, dt in tiled_out]
    out_shape += [jax.ShapeDtypeStruct(s, F32) for s in acc_out]

    def body(*refs):
        tin, fin = refs[:nt], refs[nt:nt + nf]
        tout, aout = refs[nt + nf:nt + nf + nto], refs[nt + nf + nto:]
        touts, aouts = fn(*[r[...] for r in tin], *[r[...] for r in fin])
        for r, v in zip(tout, touts):
            r[...] = v.astype(r.dtype)
        if nao:
            @pl.when(pl.program_id(0) == 0)
            def _():
                for r in aout:
                    r[...] = jnp.zeros_like(r)

            for r, v in zip(aout, aouts):
                r[...] += v.astype(F32)

    outs = pl.pallas_call(
        body, name=name, grid=(T // tt,), in_specs=in_specs, out_specs=out_specs, out_shape=out_shape,
        compiler_params=_cparams(("arbitrary",) if nao else ("parallel",)),
    )(*[v[0] for v in views], *full)
    return outs[:nto], outs[nto:]


def _mm(a, b):
    return jnp.dot(a.astype(MXU_DTYPE), b.astype(MXU_DTYPE), preferred_element_type=F32)


def _rms(x, gain):
    ms = jnp.mean(x * x, axis=-1, keepdims=True)
    return x * lax.rsqrt(ms + NORM_EPS) * gain


def _normmod(x, gain, scale, shift):
    return _rms(x, gain) * (1.0 + scale) + shift


def _gelu_tanh(y):
    return 0.5 * y * (1.0 + jnp.tanh(math.sqrt(2.0 / math.pi) * (y + 0.044715 * (y * y * y))))


def _sigmoid(x):
    return 1.0 / (1.0 + jnp.exp(-x))


def _softplus(x):
    return jnp.maximum(x, 0.0) + jnp.log(1.0 + jnp.exp(-jnp.abs(x)))


def _seg_mats(width, seg):
    r = lax.broadcasted_iota(jnp.int32, (width, LANE), 0) // seg
    c = lax.broadcasted_iota(jnp.int32, (width, LANE), 1)
    s = (r == c).astype(F32)
    rt = lax.broadcasted_iota(jnp.int32, (LANE, width), 0)
    ct = lax.broadcasted_iota(jnp.int32, (LANE, width), 1) // seg
    st = (rt == ct).astype(F32)
    return s, st


def _segsum(x, s):
    return jnp.dot(x, s, precision=HI, preferred_element_type=F32)


def _s5_prep_fn(lre, lim, lstep, bre, bim):
    step = jnp.exp(lstep)
    mag = jnp.exp(lre * step)
    lbr = mag * jnp.cos(lim * step)
    lbi = mag * jnp.sin(lim * step)
    den = lre * lre + lim * lim
    nr = lbr - 1.0
    ni = lbi
    cre = (nr * lre + ni * lim) / den
    cim = (ni * lre - nr * lim) / den
    bbr = jnp.stack([cre[d:d + 1] * bre - cim[d:d + 1] * bim for d in range(N_DIR)])
    bbi = jnp.stack([cre[d:d + 1] * bim + cim[d:d + 1] * bre for d in range(N_DIR)])
    return lbr, lbi, bbr, bbi


def s5_prep(lre, lim, lstep, bre, bim):
    ns = lre.shape[1]

    def body(lre_r, lim_r, ls_r, bre_r, bim_r, lbr_r, lbi_r, bbr_r, bbi_r):
        lbr, lbi, bbr, bbi = _s5_prep_fn(lre_r[...], lim_r[...], ls_r[...], bre_r[...], bim_r[...])
        lbr_r[...] = lbr
        lbi_r[...] = lbi
        bbr_r[...] = bbr
        bbi_r[...] = bbi

    return pl.pallas_call(
        body, name="s5_prep",
        out_shape=[jax.ShapeDtypeStruct((N_DIR, ns), F32)] * 2 + [jax.ShapeDtypeStruct((N_DIR, S5_GROUP, ns), F32)] * 2,
        compiler_params=_cparams(),
    )(lre, lim, lstep, bre, bim)


def s5_prep_bwd(lre, lim, lstep, bre, bim, dlbr, dlbi, dbbr, dbbi):
    ns = lre.shape[1]

    def body(lre_r, lim_r, ls_r, bre_r, bim_r, d1, d2, d3, d4, o1, o2, o3, o4, o5):
        _, vjp = jax.vjp(_s5_prep_fn, lre_r[...], lim_r[...], ls_r[...], bre_r[...], bim_r[...])
        g = vjp((d1[...], d2[...], d3[...], d4[...]))
        for o, v in zip((o1, o2, o3, o4, o5), g):
            o[...] = v

    return pl.pallas_call(
        body, name="s5_prep_bwd",
        out_shape=[jax.ShapeDtypeStruct((N_DIR, ns), F32)] * 3 + [jax.ShapeDtypeStruct((S5_GROUP, ns), F32)] * 2,
        compiler_params=_cparams(),
    )(lre, lim, lstep, bre, bim, dlbr, dlbi, dbbr, dbbi)


def s5_scan(bre, bim, lre, lim, *, reverse, name):
    T, NS = bre.shape
    tt = _tile(T, S5_SCAN_ROWS, SUBLANE)
    wl = _tile(NS, 512)
    nT = T // tt
    ngrp = tt // SUBLANE

    def tmap(j, i):
        return ((nT - 1 - i) if reverse else i, j)

    def body(bre_r, bim_r, lre_r, lim_r, sre_r, sim_r, cre, cim):
        @pl.when(pl.program_id(1) == 0)
        def _():
            cre[...] = jnp.zeros_like(cre)
            cim[...] = jnp.zeros_like(cim)

        lr = jnp.broadcast_to(lre_r[...], (SUBLANE, wl))
        li = jnp.broadcast_to(lim_r[...], (SUBLANE, wl))
        row = lax.broadcasted_iota(jnp.int32, (SUBLANE, wl), 0)
        pows = [(lr, li)]
        for _ in range(3):
            pr, pi = pows[-1]
            pows.append((pr * pr - pi * pi, 2.0 * pr * pi))
        e = (SUBLANE - row) if reverse else (row + 1)
        Pr = jnp.ones((SUBLANE, wl), F32)
        Pi = jnp.zeros((SUBLANE, wl), F32)
        for bit, (qr, qi) in enumerate(pows):
            on = ((e >> bit) & 1) == 1
            nr, ni = Pr * qr - Pi * qi, Pr * qi + Pi * qr
            Pr, Pi = jnp.where(on, nr, Pr), jnp.where(on, ni, Pi)

        def group(g, carry):
            gg = (ngrp - 1 - g) if reverse else g
            rows = pl.ds(pl.multiple_of(gg * SUBLANE, SUBLANE), SUBLANE)
            sr, si = bre_r[rows, :], bim_r[rows, :]
            for lvl, k in enumerate((1, 2, 4)):
                qr, qi = pows[lvl]
                if reverse:
                    shr = pltpu.roll(sr, SUBLANE - k, 0)
                    shi = pltpu.roll(si, SUBLANE - k, 0)
                    keep = row < SUBLANE - k
                else:
                    shr = pltpu.roll(sr, k, 0)
                    shi = pltpu.roll(si, k, 0)
                    keep = row >= k
                shr = jnp.where(keep, shr, 0.0)
                shi = jnp.where(keep, shi, 0.0)
                sr, si = sr + qr * shr - qi * shi, si + qr * shi + qi * shr
            cr, ci = cre[...], cim[...]
            sr, si = sr + Pr * cr - Pi * ci, si + Pr * ci + Pi * cr
            sre_r[rows, :] = sr
            sim_r[rows, :] = si
            last = 0 if reverse else SUBLANE - 1
            cre[...] = jnp.broadcast_to(sr[last:last + 1, :], (SUBLANE, wl))
            cim[...] = jnp.broadcast_to(si[last:last + 1, :], (SUBLANE, wl))
            return carry

        lax.fori_loop(0, ngrp, group, 0)

    blk = pl.BlockSpec((tt, wl), tmap)
    row_spec = pl.BlockSpec((1, wl), lambda j, i: (0, j))
    return pl.pallas_call(
        body, name=name, grid=(NS // wl, nT),
        in_specs=[blk, blk, row_spec, row_spec], out_specs=[blk, blk],
        out_shape=[jax.ShapeDtypeStruct((T, NS), F32)] * 2,
        scratch_shapes=[pltpu.VMEM((SUBLANE, wl), F32)] * 2,
        compiler_params=_cparams(("parallel", "arbitrary")),
    )(bre, bim, lre, lim)


def s5_dlam(sre, sim, gre, gim, *, reverse, name):
    T, NS = sre.shape
    wl = _tile(NS, 256)

    def body(sr_r, si_r, gr_r, gi_r, dr_r, di_r):
        row = lax.broadcasted_iota(jnp.int32, (T, wl), 0)
        if reverse:
            keep = row < T - 1
            pr = jnp.where(keep, pltpu.roll(sr_r[...], T - 1, 0), 0.0)
            pi = jnp.where(keep, pltpu.roll(si_r[...], T - 1, 0), 0.0)
        else:
            keep = row >= 1
            pr = jnp.where(keep, pltpu.roll(sr_r[...], 1, 0), 0.0)
            pi = jnp.where(keep, pltpu.roll(si_r[...], 1, 0), 0.0)
        gr, gi = gr_r[...], gi_r[...]
        dr_r[...] = jnp.sum(pr * gr + pi * gi, axis=0, keepdims=True)
        di_r[...] = jnp.sum(pr * gi - pi * gr, axis=0, keepdims=True)

    blk = pl.BlockSpec((T, wl), lambda j: (0, j))
    o = pl.BlockSpec((1, wl), lambda j: (0, j))
    return pl.pallas_call(
        body, name=name, grid=(NS // wl,), in_specs=[blk] * 4, out_specs=[o, o],
        out_shape=[jax.ShapeDtypeStruct((1, NS), F32)] * 2,
        compiler_params=_cparams(("parallel",)),
    )(sre, sim, gre, gim)


def _s5_bu_fn(u, wblk):
    nch = wblk.shape[0]
    cw = S5_CHUNK_GROUPS * S5_GROUP
    sw = S5_CHUNK_GROUPS * S5_STATE
    parts = [[] for _ in range(4)]
    for ch in range(nch):
        res = _mm(u[:, ch * cw:(ch + 1) * cw], wblk[ch])
        for q in range(4):
            parts[q].append(res[:, q * sw:(q + 1) * sw])
    return tuple(jnp.concatenate(p, axis=1) if nch > 1 else p[0] for p in parts)


def _s5_out_fn(x0r, x0i, x1r, x1i, u, cre, cim, dsk, wglu, bglu):
    xr, xi = x0r + x1r, x0i + x1i
    nch = cre.shape[0]
    sw = S5_CHUNK_GROUPS * S5_STATE
    ys = [_mm(xr[:, ch * sw:(ch + 1) * sw], cre[ch]) - _mm(xi[:, ch * sw:(ch + 1) * sw], cim[ch]) for ch in range(nch)]
    y = jnp.concatenate(ys, axis=1) if nch > 1 else ys[0]
    z = _gelu_tanh(y + dsk * u)
    gate = _sigmoid(_mm(z, wglu) + bglu)
    return z * gate


def _rk_dims():
    C = D_MODEL // 2
    LW = N_DIR * DECAY_LORA
    GP = _round_up(GATE_LORA, LANE)
    return C, LW, GP


def _rk_pre_fn(ps, w0, wup0, wup1, a0, aup0, aup1, gup, k_k, k_a):
    C, LW, GP = _rk_dims()
    r, k, v = ps[:, 0:C], ps[:, C:2 * C], ps[:, 2 * C:3 * C]
    wdn = ps[:, 3 * C:3 * C + LW]
    adn = ps[:, 3 * C + LW:3 * C + 2 * LW]
    gdn = ps[:, 3 * C + 2 * LW:3 * C + 2 * LW + GP]
    s, st = _seg_mats(C, RWKV_HEAD)
    kk = k * k_k
    n2 = _segsum(kk * kk, s)
    n2 = jnp.where(n2 > 0.0, n2, 1.0)
    inv = 1.0 / jnp.maximum(jnp.sqrt(n2), L2_EPS)
    kkn = kk * _segsum(inv, st)
    tw = jnp.tanh(wdn)
    wup, aup = (wup0, wup1), (aup0, aup1)
    ws, ks, bs = [], [], []
    for d in range(N_DIR):
        wraw = w0[d:d + 1] + _mm(tw, wup[d])
        w = -_softplus(-wraw) - 0.5
        ws.append(jnp.exp(-jnp.exp(w)))
        a = _sigmoid(a0[d:d + 1] + _mm(adn, aup[d]))
        ks.append(k * (1.0 + (a - 1.0) * k_a))
        bs.append(kkn * a)
    g = _mm(_sigmoid(gdn), gup)
    return r, v, kkn, ws[0], ws[1], ks[0], ks[1], bs[0], bs[1], g


def _rk_post_fn(y0, y1, r, v, k0, k1, g, r_k, lng, lnb):
    C = r.shape[1]
    s, st = _seg_mats(C, RWKV_HEAD)
    y = y0 + y1
    mu = _segsum(_segsum(y, s) * (1.0 / RWKV_HEAD), st)
    yc = y - mu
    var = _segsum(_segsum(yc * yc, s) * (1.0 / RWKV_HEAD), st)
    yn = yc * lax.rsqrt(var + GN_EPS) * lng + lnb
    bonus = _segsum(_segsum(r * (k0 + k1) * r_k, s), st)
    return (yn + bonus * v) * g


def rk_shift(proj, mp, mn, col0):
    T = proj.shape[0]
    W = mp.shape[1]
    wl = _tile(math.gcd(W, col0), 256)
    cb0 = col0 // wl

    def body(p_r, mp_r, mn_r, o_r):
        p = p_r[...]
        row = lax.broadcasted_iota(jnp.int32, (T, wl), 0)
        prev = jnp.where(row >= 1, pltpu.roll(p, 1, 0), 0.0)
        nxt = jnp.where(row < T - 1, pltpu.roll(p, T - 1, 0), 0.0)
        o_r[...] = p + mp_r[...] * (prev - p) + mn_r[...] * (nxt - p)

    rs = pl.BlockSpec((1, wl), lambda j: (0, j))
    return pl.pallas_call(
        body, name="rk_shift", grid=(W // wl,),
        in_specs=[pl.BlockSpec((T, wl), lambda j: (0, cb0 + j)), rs, rs],
        out_specs=pl.BlockSpec((T, wl), lambda j: (0, j)),
        out_shape=jax.ShapeDtypeStruct((T, W), F32),
        compiler_params=_cparams(("parallel",)),
    )(proj, mp, mn)


def rk_shift_bwd(dps, proj, mp, mn, col0):
    T, W = dps.shape
    wl = _tile(math.gcd(W, col0), 256)
    cb0 = col0 // wl

    def body(d_r, p_r, mp_r, mn_r, dp_r, dmp_r, dmn_r):
        d, p = d_r[...], p_r[...]
        mpv, mnv = mp_r[...], mn_r[...]
        row = lax.broadcasted_iota(jnp.int32, (T, wl), 0)
        first, last = row >= 1, row < T - 1
        prev = jnp.where(first, pltpu.roll(p, 1, 0), 0.0)
        nxt = jnp.where(last, pltpu.roll(p, T - 1, 0), 0.0)
        dmp_r[...] = jnp.sum(d * (prev - p), axis=0, keepdims=True)
        dmn_r[...] = jnp.sum(d * (nxt - p), axis=0, keepdims=True)
        dp_r[...] = (d * (1.0 - mpv - mnv) + jnp.where(last, pltpu.roll(d * mpv, T - 1, 0), 0.0)
                     + jnp.where(first, pltpu.roll(d * mnv, 1, 0), 0.0))

    rs = pl.BlockSpec((1, wl), lambda j: (0, j))
    blk = pl.BlockSpec((T, wl), lambda j: (0, j))
    return pl.pallas_call(
        body, name="rk_shift_bwd", grid=(W // wl,),
        in_specs=[blk, pl.BlockSpec((T, wl), lambda j: (0, cb0 + j)), rs, rs],
        out_specs=[blk, rs, rs],
        out_shape=[jax.ShapeDtypeStruct((T, W), F32), jax.ShapeDtypeStruct((1, W), F32), jax.ShapeDtypeStruct((1, W), F32)],
        compiler_params=_cparams(("parallel",)),
    )(dps, proj, mp, mn)


RK_FWD_PAIRS = 4
RK_BWD_PAIRS = 2
RK_SPREAD_PAIRS = 2
RK_TIME_BLOCK = 32
RK_LANE_BLOCK = 128


def _rk_blocks(T, C, N, order_reversed, pairs):
    pw = 2 * N
    pp = min(pairs, C // pw)
    tb = min(RK_TIME_BLOCK, T)
    lb = min(RK_LANE_BLOCK, T)
    nb, per = T // tb, lb // tb

    def tix(i):
        return (nb - 1 - i) if order_reversed else i

    rows = pl.BlockSpec((tb, pp * pw), lambda g, i: (tix(i), g))
    cols = pl.BlockSpec((2 * pp, N, lb), lambda g, i: (g, 0, tix(i) // per))
    hist = pl.BlockSpec((tb, pp, N, pw), lambda g, i: (tix(i), g, 0, 0))
    return pp, pw, tb, lb, nb, per, tix, rows, cols, hist


def rk_spread(xT, name):
    H, N, T = xT.shape
    pw = 2 * N
    lb = min(RK_LANE_BLOCK, T)
    pp = min(RK_SPREAD_PAIRS, H // 2)

    def body(x_r, o_r):
        lane = lax.broadcasted_iota(jnp.int32, (N, lb), 1)
        first = lax.broadcasted_iota(jnp.int32, (N, pw), 1) < N
        tiles = [x_r[h] for h in range(2 * pp)]

        def step(t, carry):
            for p in range(pp):
                c = [jnp.sum(jnp.where(lane == t, tiles[2 * p + q], 0.0), axis=1, keepdims=True) for q in range(2)]
                o_r[t, p] = jnp.where(first, c[0], c[1])
            return carry

        lax.fori_loop(0, lb, step, 0, unroll=8)

    return pl.pallas_call(
        body, name=name, grid=(H // (2 * pp), T // lb),
        in_specs=[pl.BlockSpec((2 * pp, N, lb), lambda g, i: (g, 0, i))],
        out_specs=pl.BlockSpec((lb, pp, N, pw), lambda g, i: (i, g, 0, 0)),
        out_shape=jax.ShapeDtypeStruct((T, H // 2, N, pw), F32),
        compiler_params=_cparams(("parallel", "parallel")),
    )(xT)


def _half_sums(x, first):
    return (jnp.sum(jnp.where(first, x, 0.0), axis=1, keepdims=True),
            jnp.sum(jnp.where(first, 0.0, x), axis=1, keepdims=True))


def rk_scan(r, kk, w, k, b, vc, *, reverse, name):
    T, C = r.shape
    N = vc.shape[2]
    H = C // N
    pp, pw, tb, lb, nb, per, tix, rows, cols, hist_spec = _rk_blocks(T, C, N, reverse, RK_FWD_PAIRS)

    def body(r_r, kk_r, w_r, k_r, b_r, VC, yT_r, hist_r, S, YA):
        i = pl.program_id(1)

        @pl.when(i == 0)
        def _():
            S[...] = jnp.zeros_like(S)

        @pl.when(i % per == 0)
        def _():
            yT_r[...] = jnp.zeros_like(yT_r)

        off = (tix(i) % per) * tb
        first = lax.broadcasted_iota(jnp.int32, (N, pw), 1) < N
        YA[...] = jnp.zeros_like(YA)
        st = [S[p] for p in range(pp)]
        seg = [slice(p * pw, (p + 1) * pw) for p in range(pp)]
        for s in range(tb):
            t = (tb - 1 - s) if reverse else s
            row = slice(t, t + 1)
            sk = [_half_sums(st[p] * kk_r[row, seg[p]], first) for p in range(pp)]
            for p in range(pp):
                hist_r[t, p] = st[p]
                skp = jnp.where(first, sk[p][0], sk[p][1])
                st[p] = st[p] * w_r[row, seg[p]] - skp * b_r[row, seg[p]] + VC[t, p] * k_r[row, seg[p]]
            ys = [_half_sums(st[p] * r_r[row, seg[p]], first) for p in range(pp)]
            for p in range(pp):
                for q in range(2):
                    YA[2 * p + q, :, row] = ys[p][q]
        for p in range(pp):
            S[p] = st[p]
        for h in range(2 * pp):
            yT_r[h] = yT_r[h] + (pltpu.roll(YA[h], off, 1) if per > 1 else YA[h])

    return pl.pallas_call(
        body, name=name, grid=(C // (pp * pw), nb),
        in_specs=[rows] * 5 + [hist_spec], out_specs=[cols, hist_spec],
        out_shape=[jax.ShapeDtypeStruct((H, N, T), F32), jax.ShapeDtypeStruct((T, H // 2, N, pw), F32)],
        scratch_shapes=[pltpu.VMEM((pp, N, pw), F32), pltpu.VMEM((2 * pp, N, lb), F32)],
        compiler_params=_cparams(("parallel", "arbitrary")),
    )(r, kk, w, k, b, vc)


def rk_scan_bwd(r, kk, w, k, b, vc, dc, hist, *, reverse, name):
    T, C = r.shape
    N = vc.shape[2]
    H = C // N
    pp, pw, tb, lb, nb, per, tix, rows, cols, hist_spec = _rk_blocks(T, C, N, not reverse, RK_BWD_PAIRS)

    def body(r_r, kk_r, w_r, k_r, b_r, VC, DC, hist_r, dr_r, dkk_r, dw_r, dk_r, db_r, dvT_r, G, YA):
        i = pl.program_id(1)

        @pl.when(i == 0)
        def _():
            G[...] = jnp.zeros_like(G)

        @pl.when(i % per == 0)
        def _():
            dvT_r[...] = jnp.zeros_like(dvT_r)

        off = (tix(i) % per) * tb
        first = lax.broadcasted_iota(jnp.int32, (N, pw), 1) < N
        YA[...] = jnp.zeros_like(YA)
        gs = [G[p] for p in range(pp)]
        seg = [slice(p * pw, (p + 1) * pw) for p in range(pp)]
        for s in range(tb):
            t = s if reverse else (tb - 1 - s)
            row = slice(t, t + 1)
            g = [gs[p] + DC[t, p] * r_r[row, seg[p]] for p in range(pp)]
            sk = [_half_sums(hist_r[t, p] * kk_r[row, seg[p]], first) for p in range(pp)]
            gb = [_half_sums(g[p] * b_r[row, seg[p]], first) for p in range(pp)]
            gk = [_half_sums(g[p] * k_r[row, seg[p]], first) for p in range(pp)]
            for p in range(pp):
                sp = hist_r[t, p]
                kkv, wv, kv, bv = kk_r[row, seg[p]], w_r[row, seg[p]], k_r[row, seg[p]], b_r[row, seg[p]]
                vcol, dycol = VC[t, p], DC[t, p]
                sa = -jnp.where(first, sk[p][0], sk[p][1])
                dsa = jnp.where(first, gb[p][0], gb[p][1])
                sn = sp * wv + sa * bv + vcol * kv
                dr_r[row, seg[p]] = jnp.sum(sn * dycol, axis=0, keepdims=True)
                dw_r[row, seg[p]] = jnp.sum(g[p] * sp, axis=0, keepdims=True)
                db_r[row, seg[p]] = jnp.sum(g[p] * sa, axis=0, keepdims=True)
                dk_r[row, seg[p]] = jnp.sum(g[p] * vcol, axis=0, keepdims=True)
                dkk_r[row, seg[p]] = -jnp.sum(sp * dsa, axis=0, keepdims=True)
                gs[p] = g[p] * wv - dsa * kkv
                for q in range(2):
                    YA[2 * p + q, :, row] = gk[p][q]
        for p in range(pp):
            G[p] = gs[p]
        for h in range(2 * pp):
            dvT_r[h] = dvT_r[h] + (pltpu.roll(YA[h], off, 1) if per > 1 else YA[h])

    return pl.pallas_call(
        body, name=name, grid=(C // (pp * pw), nb),
        in_specs=[rows] * 5 + [hist_spec] * 3, out_specs=[rows] * 5 + [cols],
        out_shape=[jax.ShapeDtypeStruct((T, C), F32)] * 5 + [jax.ShapeDtypeStruct((H, N, T), F32)],
        scratch_shapes=[pltpu.VMEM((pp, N, pw), F32), pltpu.VMEM((2 * pp, N, lb), F32)],
        compiler_params=_cparams(("parallel", "arbitrary")),
    )(r, kk, w, k, b, vc, dc, hist)


def adam(parts, w, m, v, name):
    P, R, C = parts.shape
    tr = _tile(R, max(SUBLANE, (1 << 19) // max(C, 1) // SUBLANE * SUBLANE), SUBLANE)
    c1 = 1.0 / (1.0 - ADAM_B1 ** ADAM_STEP)
    c2 = 1.0 / (1.0 - ADAM_B2 ** ADAM_STEP)

    def body(p_r, w_r, m_r, v_r, g_o, d_o, m_o, v_o):
        g = p_r[0].astype(F32)
        for q in range(1, P):
            g = g + p_r[q].astype(F32)
        m2 = ADAM_B1 * m_r[...] + (1.0 - ADAM_B1) * g
        v2 = ADAM_B2 * v_r[...] + (1.0 - ADAM_B2) * (g * g)
        g_o[...] = g
        m_o[...] = m2
        v_o[...] = v2
        d_o[...] = -ADAM_LR * ((m2 * c1) / (jnp.sqrt(v2 * c2) + ADAM_EPS) + ADAM_WD * w_r[...])

    blk = pl.BlockSpec((tr, C), lambda i: (i, 0))
    return pl.pallas_call(
        body, name=name, grid=(R // tr,),
        in_specs=[pl.BlockSpec((P, tr, C), lambda i: (0, i, 0)), blk, blk, blk], out_specs=[blk] * 4,
        out_shape=[jax.ShapeDtypeStruct((R, C), F32)] * 4,
        compiler_params=_cparams(("parallel",)),
    )(parts, w, m, v)


def _pack_flat(arrs):
    rows = []
    for a in arrs:
        f = a.reshape(-1).astype(F32)
        n = _round_up(f.shape[0], SUBLANE * LANE)
        rows.append(jnp.pad(f, (0, n - f.shape[0])).reshape(-1, LANE))
    return jnp.concatenate(rows, axis=0)


def _unpack_flat(packed, shapes):
    out, r0 = [], 0
    for s in shapes:
        n = math.prod(s)
        nr = _round_up(n, SUBLANE * LANE) // LANE
        out.append(packed[r0:r0 + nr].reshape(-1)[:n].reshape(s))
        r0 += nr
    return out


def _pack_rows(arrs):
    rows = []
    for a in arrs:
        f = a.reshape(-1, a.shape[-1]).astype(F32)
        n = _round_up(f.shape[0], SUBLANE)
        rows.append(jnp.pad(f, ((0, n - f.shape[0]), (0, 0))))
    return jnp.concatenate(rows, axis=0)


def _unpack_rows(packed, shapes):
    out, r0 = [], 0
    for s in shapes:
        nr = math.prod(s[:-1])
        out.append(packed[r0:r0 + nr].reshape(s))
        r0 += _round_up(nr, SUBLANE)
    return out


def _cols_from_slots(g):
    return jnp.moveaxis(g, 0, -2).reshape(g.shape[1:-1] + (N_DEV * g.shape[-1],))


def _cols_to_slots(a):
    cs = a.shape[-1] // N_DEV
    return jnp.moveaxis(a.reshape(a.shape[:-1] + (N_DEV, cs)), -2, 0)


def _step(P, M, V):
    D, T = D_MODEL, SEQ
    S5W = D // 2
    C, LW, GP = _rk_dims()
    H = C // RWKV_HEAD
    G = S5W // S5_GROUP
    NS = G * S5_STATE
    NCH = G // S5_CHUNK_GROUPS
    SW = S5_CHUNK_GROUPS * S5_STATE
    RIN = 3 * C + 2 * LW + GATE_LORA
    RINP = 3 * C + 2 * LW + GP
    PROJ = S5W + RIN
    PROJP = S5W + RINP
    FF = 4 * D
    me = 4 * lax.axis_index("x") + 2 * lax.axis_index("y") + lax.axis_index("c")
    cs_mod = N_MOD * D // N_DEV
    eye = jnp.eye(S5_CHUNK_GROUPS, dtype=F32)

    x = P['x'][0]
    target = P['loss_target'][0]

    (c_all,) = exchange([P['c']], ['gather'], "comm_gather_c")
    c_all = c_all.reshape(N_DEV, D)
    (c_act,), _ = rowcall(lambda cv: ((cv * _sigmoid(cv),), ()), "silu_c", [c_all], [], [(D, F32)], [], N_DEV)
    ada_b_loc = lax.dynamic_slice(P['ada_b'], (0, me * cs_mod), (1, cs_mod))
    mod_loc = matmul(c_act, P['ada_w'][0], name="mod_mm", precise=True, extras=[(ada_b_loc, 'n')],
                     epi=lambda acc, bias: (acc + bias,))

    rk_shapes = [P[n][0].shape for n in RKPACK]
    rk_pack = _pack_rows([P[n][0] for n in RKPACK])
    gathered = exchange(
        [mod_loc, P['w_in'][0].astype(BF16), P['w_out'][0].astype(BF16), P['ffn_w1'][0].astype(BF16),
         P['ffn_w2'][0].astype(BF16), P['s5_w_glu'][0].astype(BF16), rk_pack],
        ['gather'] * 7, "comm_gather_weights")
    mod_all, w_in_g, w_out_g, w1_g, w2_g, wglu_g, rk_g = gathered
    mod_me = lax.dynamic_index_in_dim(mod_all, me, axis=1, keepdims=False).reshape(N_MOD, 1, D)
    shift1, scale1, gate1, shift2, scale2, gate2 = (mod_me[i] for i in range(N_MOD))
    w_in = jnp.pad(_cols_from_slots(w_in_g), ((0, 0), (0, PROJP - PROJ)))
    w_out = w_out_g.reshape(D, D)
    w2 = w2_g.reshape(FF, D)
    wglu = wglu_g.reshape(S5W, S5W)
    rk_full = _unpack_rows(_cols_from_slots(rk_g), [s[:-1] + (C,) for s in rk_shapes])
    rk_w0, rk_a0, rk_wup, rk_aup, rk_gup = rk_full

    def lora_pad(up):
        z = jnp.zeros((N_DIR, LW, C), F32)
        for d in range(N_DIR):
            z = z.at[d, d * DECAY_LORA:(d + 1) * DECAY_LORA].set(up[d])
        return z

    wup_p, aup_p = lora_pad(rk_wup), lora_pad(rk_aup)
    gup_p = jnp.pad(rk_gup, ((0, GP - GATE_LORA), (0, 0)))
    mu_prev = jnp.pad(P['rk_shift_prev'], ((0, 0), (0, RINP - RIN)))
    mu_next = jnp.pad(P['rk_shift_next'], ((0, 0), (0, RINP - RIN)))
    r_k = P['rk_r_k'].reshape(1, C)
    fgain = P['final_gain'].reshape(1, D)

    TT = 256
    (h1,), _ = rowcall(lambda xv, g, sc, sh: ((_normmod(xv, g, sc, sh),), ()), "norm1",
                       [x], [P['norm1_gain'], scale1, shift1], [(D, BF16)], [], TT)
    proj = matmul(h1, w_in, name="proj_mm")

    lre = P['s5_lambda_re'][0].reshape(N_DIR, NS)
    lim = P['s5_lambda_im'][0].reshape(N_DIR, NS)
    lstep = jnp.broadcast_to(P['s5_log_step'][0][:, :, None], (N_DIR, G, S5_STATE)).reshape(N_DIR, NS)
    bre = P['s5_b_re'][0].reshape(NS, S5_GROUP).T
    bim = P['s5_b_im'][0].reshape(NS, S5_GROUP).T
    lbr, lbi, bbr, bbi = s5_prep(lre, lim, lstep, bre, bim)
    bbar = jnp.stack([bbr, bbi], axis=1).reshape(N_DIR, 2, S5_GROUP, NCH, S5_CHUNK_GROUPS, S5_STATE)
    wblk = jnp.einsum('drhcgp,gk->cghdrkp', bbar, eye).reshape(NCH, S5_CHUNK_GROUPS * S5_GROUP, 4 * SW)
    wblk = wblk.astype(MXU_DTYPE)
    u_view = (proj, S5W, 0)
    bus, _ = rowcall(lambda uv, wb: (_s5_bu_fn(uv, wb), ()), "s5_bu", [u_view], [wblk], [(NS, F32)] * 4, [], TT)
    s0r, s0i = s5_scan(bus[0], bus[1], lbr[0:1], lbi[0:1], reverse=False, name="s5_scan_f0")
    s1r, s1i = s5_scan(bus[2], bus[3], lbr[1:2], lbi[1:2], reverse=True, name="s5_scan_f1")

    def cblk(cm):
        c4 = cm.reshape(NCH, S5_CHUNK_GROUPS, S5_GROUP, S5_STATE)
        return jnp.einsum('cghp,gk->cgpkh', c4, eye).reshape(NCH, SW, S5_CHUNK_GROUPS * S5_GROUP)

    cre_b = cblk(P['s5_c_re'][0]).astype(MXU_DTYPE)
    cim_b = cblk(P['s5_c_im'][0]).astype(MXU_DTYPE)
    s5_full = [cre_b, cim_b, P['s5_d'], wglu, P['s5_b_glu']]
    TS = 128
    (y_s5,), _ = rowcall(lambda *a: ((_s5_out_fn(*a),), ()), "s5_out", [s0r, s0i, s1r, s1i, u_view], s5_full,
                         [(S5W, BF16)], [], TS)

    ps = rk_shift(proj, mu_prev, mu_next, S5W)
    pre_full = [rk_w0, wup_p[0], wup_p[1], rk_a0, aup_p[0], aup_p[1], gup_p, P['rk_k_k'], P['rk_k_a']]
    pre_out, _ = rowcall(lambda *a: (_rk_pre_fn(*a)[2:], ()), "rk_pre", [ps], pre_full, [(C, F32)] * 8, [], TS)
    kkn, w_0, w_1, k_0, k_1, b_0, b_1, g_gate = pre_out
    r_t, v_t = ps[:, 0:C], ps[:, 2 * C:3 * C]

    def hmT(a):
        return a.reshape(T, H, RWKV_HEAD).transpose(1, 2, 0)

    def unT(a):
        return a.transpose(2, 0, 1).reshape(T, C)

    v_cols = rk_spread(hmT(v_t), "rk_spread_v")
    dir_rows = [(w_0, k_0, b_0), (w_1, k_1, b_1)]
    yT, hist = [], []
    for d in range(N_DIR):
        wd, kd, bd = dir_rows[d]
        yd, hd = rk_scan(r_t, kkn, wd, kd, bd, v_cols, reverse=(d == 1), name=f"rk_scan_f{d}")
        yT.append(yd)
        hist.append(hd)
    y_0, y_1 = unT(yT[0]), unT(yT[1])
    post_full = [r_k, P['rk_ln_gain'], P['rk_ln_bias']]
    post_tiled = [y_0, y_1, (ps, C, 0), (ps, C, 2), k_0, k_1, g_gate]
    (y_rk,), _ = rowcall(lambda *a: ((_rk_post_fn(*a),), ()), "rk_post", post_tiled, post_full, [(C, BF16)], [], TS)

    ycat = jnp.concatenate([y_s5, y_rk], axis=1)
    mixed = matmul(ycat, w_out, name="out_mm")

    def res_norm(xv, mv, gate, g, sc, sh):
        x1v = xv + gate * mv
        return x1v, _normmod(x1v, g, sc, sh)

    (x1, h2), _ = rowcall(lambda *a: (res_norm(*a), ()), "norm2", [x, mixed], [gate1, P['norm2_gain'], scale2, shift2],
                          [(D, F32), (D, BF16)], [], TT)
    a_ff, hh_ff = matmul(h2, w1_g, name="ffn1_mm", b_slots=True, out_dtypes=(F32, BF16),
                         epi=lambda acc: (acc, jnp.square(jnp.maximum(acc, 0.0))))
    ffn = matmul(hh_ff, w2, name="ffn2_mm")

    def loss_fn(x1v, fv, tg, gate, fg):
        def f(x1_, f_, gate_, fg_):
            out = _rms(x1_ + gate_ * f_, fg_)
            err = out - tg
            return 0.5 * jnp.sum(jnp.sum(err * err, axis=1, keepdims=True), axis=0, keepdims=True) * (1.0 / D)
        lv, vjp = jax.vjp(f, x1v, fv, gate, fg)
        dx1, dff, dgate, dfg = vjp(jnp.ones((1, 1), F32))
        return (dx1, dff), (jnp.broadcast_to(lv, (SUBLANE, LANE)), dgate, dfg)

    (dx2, dffn), (loss_t, dgate2, dfgain) = rowcall(
        loss_fn, "loss", [x1, ffn, target], [gate2, fgain], [(D, F32), (D, BF16)], [(SUBLANE, LANE), (1, D), (1, D)], TT)
    loss = lax.psum(loss_t[0, 0], MESH_AXES)

    da = matmul(dffn, w2, name="dffn2_mm", tb=True, out_dtypes=(BF16,), extras=[(a_ff, 'mn')],
                epi=lambda acc, av: (acc * (2.0 * jnp.maximum(av, 0.0)),))
    g_w2 = matmul(hh_ff, dffn, name="gw2_mm", ta=True, out_dtypes=(BF16,))
    dh2 = matmul(da, w1_g, name="dh2_mm", tb=True, b_slots=True)
    g_w1 = matmul(h2, da, name="gw1_mm", ta=True, out_slots=N_DEV, out_dtypes=(BF16,))

    def res_norm_bwd(dx2v, dh2v, xv, mv, gate, g, sc, sh):
        _, vjp = jax.vjp(res_norm, xv, mv, gate, g, sc, sh)
        dx, dm, dgate, dg, dsc, dsh = vjp((dx2v, dh2v))
        return (dx, dm), (dgate, dg, dsc, dsh)

    (dx1, dmixed), (dgate1, dgain2, dscale2, dshift2) = rowcall(
        res_norm_bwd, "norm2_bwd", [dx2, dh2, x, mixed], [gate1, P['norm2_gain'], scale2, shift2],
        [(D, F32), (D, BF16)], [(1, D)] * 4, TT)

    dycat = matmul(dmixed, w_out, name="dycat_mm", tb=True)
    g_wout = matmul(ycat, dmixed, name="gwout_mm", ta=True, out_dtypes=(BF16,))

    def post_bwd(dy, *a):
        _, vjp = jax.vjp(_rk_post_fn, *a)
        gy0, gy1, gr, gv, gk0, gk1, gg, grk, glg, glb = vjp(dy)
        return (gy0, gr, gv, gk0, gk1, gg), (grk, glg, glb)

    cb_rk = S5W // C if C else 0
    (dy_rk, dr_p, dv_p, dk0_p, dk1_p, dg_p), (g_rk_rk, g_lng, g_lnb) = rowcall(
        post_bwd, "rk_post_bwd", [(dycat, C, cb_rk)] + post_tiled, post_full, [(C, F32)] * 6, [(1, C)] * 3, TS)
    dy_cols = rk_spread(hmT(dy_rk), "rk_spread_dy")
    scan_g = []
    for d in range(N_DIR):
        wd, kd, bd = dir_rows[d]
        scan_g.append(rk_scan_bwd(r_t, kkn, wd, kd, bd, v_cols, dy_cols, hist[d], reverse=(d == 1), name=f"rk_scan_b{d}"))
    cot = [dr_p, scan_g[0][0], scan_g[1][0],
           dv_p, unT(scan_g[0][5]), unT(scan_g[1][5]),
           scan_g[0][1], scan_g[1][1],
           scan_g[0][2], scan_g[1][2],
           dk0_p, scan_g[0][3], dk1_p, scan_g[1][3],
           scan_g[0][4], scan_g[1][4],
           dg_p]

    def pre_bwd(psv, r0, r1, r2, v0, v1, v2, q0, q1, dw0, dw1, k0a, k0b, k1a, k1b, db0, db1, dgv, *params):
        _, vjp = jax.vjp(_rk_pre_fn, psv, *params)
        grads = vjp((r0 + r1 + r2, v0 + v1 + v2, q0 + q1, dw0, dw1, k0a + k0b, k1a + k1b, db0, db1, dgv))
        return (grads[0],), tuple(grads[1:])

    (dps,), pre_g = rowcall(pre_bwd, "rk_pre_bwd", [ps] + cot, pre_full, [(RINP, F32)],
                            [f.shape for f in pre_full], TS)
    g_w0, g_wup0, g_wup1, g_a0, g_aup0, g_aup1, g_gup_p, g_kk, g_ka = pre_g
    g_wup_p, g_aup_p = jnp.stack([g_wup0, g_wup1]), jnp.stack([g_aup0, g_aup1])
    dp_rk, g_mup, g_mun = rk_shift_bwd(dps, proj, mu_prev, mu_next, S5W)

    def s5_out_bwd(dy, *a):
        a = [t.astype(F32) for t in a]
        _, vjp = jax.vjp(_s5_out_fn, *a)
        g = vjp(dy)
        return (g[0], g[1], g[4]), tuple(g[5:])

    (dxr, dxi, du_a), s5_pg = rowcall(
        s5_out_bwd, "s5_out_bwd", [(dycat, S5W, 0), s0r, s0i, s1r, s1i, u_view], s5_full,
        [(NS, F32), (NS, F32), (S5W, F32)], [f.shape for f in s5_full], TS)
    g_creb, g_cimb, g_s5d, g_wglu, g_bglu = s5_pg
    l0r, l0i = s5_scan(dxr, dxi, lbr[0:1], -lbi[0:1], reverse=True, name="s5_scan_b0")
    l1r, l1i = s5_scan(dxr, dxi, lbr[1:2], -lbi[1:2], reverse=False, name="s5_scan_b1")
    dl0r, dl0i = s5_dlam(s0r, s0i, l0r, l0i, reverse=False, name="s5_dlam0")
    dl1r, dl1i = s5_dlam(s1r, s1i, l1r, l1i, reverse=True, name="s5_dlam1")

    def bu_bwd(uv, g0, g1, g2, g3, wb):
        _, vjp = jax.vjp(_s5_bu_fn, uv, wb.astype(F32))
        du, dwb = vjp((g0, g1, g2, g3))
        return (du,), (dwb,)

    (du_b,), (g_wblk,) = rowcall(bu_bwd, "s5_bu_bwd", [u_view, l0r, l0i, l1r, l1i], [wblk], [(S5W, F32)],
                                 [wblk.shape], TS)
    g_bbar = jnp.einsum('cghdrkp,gk->drhcgp',
                        g_wblk.reshape(NCH, S5_CHUNK_GROUPS, S5_GROUP, N_DIR, 2, S5_CHUNK_GROUPS, S5_STATE), eye)
    g_bbar = g_bbar.reshape(N_DIR, 2, S5_GROUP, NS)
    g_lre, g_lim, g_lstep, g_bre, g_bim = s5_prep_bwd(
        lre, lim, lstep, bre, bim, jnp.concatenate([dl0r, dl1r], 0), jnp.concatenate([dl0i, dl1i], 0),
        g_bbar[:, 0], g_bbar[:, 1])

    def uncblk(gb):
        g5 = gb.reshape(NCH, S5_CHUNK_GROUPS, S5_STATE, S5_CHUNK_GROUPS, S5_GROUP)
        return jnp.einsum('cgpkh,gk->cghp', g5, eye).reshape(G, S5_GROUP, S5_STATE)

    (du_tot,), _ = rowcall(lambda a, b_: ((a + b_,), ()), "s5_du_sum", [du_a, du_b], [], [(S5W, BF16)], [], TT)
    dproj = jnp.concatenate([du_tot, dp_rk.astype(BF16)], axis=1)
    dh1 = matmul(dproj, w_in, name="dh1_mm", tb=True)
    g_win = matmul(h1, dproj, name="gwin_mm", ta=True, out_dtypes=(BF16,))

    def norm1_bwd(dx1v, dh1v, xv, g, sc, sh):
        _, vjp = jax.vjp(_normmod, xv, g, sc, sh)
        dx, dg, dsc, dsh = vjp(dh1v)
        return (dx1v + dx,), (dg, dsc, dsh)

    (grad_x,), (dgain1, dscale1, dshift1) = rowcall(
        norm1_bwd, "norm1_bwd", [dx1, dh1, x], [P['norm1_gain'], scale1, shift1], [(D, F32)], [(1, D)] * 3, TT)

    dmod = jnp.concatenate([dshift1, dscale1, dgate1, dshift2, dscale2, dgate2], axis=1)
    lstep_g = g_lstep.reshape(N_DIR, G, S5_STATE)
    small_g = {
        'ada_b': dmod, 'norm1_gain': dgain1, 'norm2_gain': dgain2, 'final_gain': dfgain.reshape(D),
        's5_lambda_re': g_lre.reshape(1, N_DIR, G, S5_STATE), 's5_lambda_im': g_lim.reshape(1, N_DIR, G, S5_STATE),
        's5_log_step': lstep_g,
        's5_b_re': g_bre.T.reshape(1, G, S5_STATE, S5_GROUP), 's5_b_im': g_bim.T.reshape(1, G, S5_STATE, S5_GROUP),
        's5_c_re': uncblk(g_creb)[None], 's5_c_im': uncblk(g_cimb)[None],
        's5_d': g_s5d, 's5_b_glu': g_bglu,
        'rk_shift_prev': g_mup[:, :RIN], 'rk_shift_next': g_mun[:, :RIN],
        'rk_k_k': g_kk, 'rk_k_a': g_ka, 'rk_r_k': g_rk_rk.reshape(1, H, RWKV_HEAD),
        'rk_ln_gain': g_lng, 'rk_ln_bias': g_lnb,
    }
    small_shapes = {n: P[n].shape for n in SMALL}
    small_shapes['s5_log_step'] = (N_DIR, G, S5_STATE)
    small_pack = _pack_flat([small_g[n] for n in SMALL])

    def lora_unpad(gp):
        return jnp.stack([gp[d, d * DECAY_LORA:(d + 1) * DECAY_LORA] for d in range(N_DIR)])

    rk_grads = {'rk_w0': g_w0, 'rk_a0': g_a0, 'rk_w_up': lora_unpad(g_wup_p), 'rk_a_up': lora_unpad(g_aup_p),
                'rk_g_up': g_gup_p[:GATE_LORA]}
    rk_gpack = jnp.stack([_pack_rows([_cols_to_slots(rk_grads[n])[j] for n in RKPACK]) for j in range(N_DEV)])
    g_win_s = _cols_to_slots(g_win[:, :PROJ])
    ex = exchange(
        [small_pack, g_win_s, g_wout.reshape(N_DEV, D // N_DEV, D), g_w1, g_w2.reshape(N_DEV, FF // N_DEV, D),
         g_wglu.reshape(N_DEV, S5W // N_DEV, S5W), rk_gpack],
        ['gather'] + ['scatter'] * 6, "comm_grads")
    small_all, win_parts, wout_parts, w1_parts, w2_parts, wglu_parts, rk_parts = ex

    res = {}

    def put(name, g, dl, m2, v2):
        shp = P[name].shape
        res[name] = tuple(t.reshape(shp) for t in (g, dl, m2, v2))

    def adam2d(name, parts):
        shp = P[name].shape
        r2 = (math.prod(shp[:-1]), shp[-1])
        put(name, *adam(parts.reshape((parts.shape[0],) + r2), P[name].reshape(r2), M[name].reshape(r2),
                        V[name].reshape(r2), "adam_" + name))

    adam2d('w_in', win_parts)
    adam2d('w_out', wout_parts)
    adam2d('ffn_w1', w1_parts)
    adam2d('ffn_w2', w2_parts)
    adam2d('s5_w_glu', wglu_parts)
    off = 0
    for n in SMALL:
        if n == 'ada_b':
            break
        off += _round_up(math.prod(small_shapes[n]), SUBLANE * LANE) // LANE
    nrow_b = N_MOD * D // LANE
    dmod_all = small_all[:, off:off + nrow_b].reshape(N_DEV, N_MOD * D)
    dmod_cols = lax.dynamic_slice(dmod_all, (0, me * cs_mod), (N_DEV, cs_mod))
    g_adaw = matmul(c_act, dmod_cols, name="gadaw_mm", ta=True, precise=True)
    adam2d('ada_w', g_adaw[None])
    small_w = dict(P)
    small_m, small_v = dict(M), dict(V)
    rk_res = adam(rk_parts, rk_pack, _pack_rows([M[n][0] for n in RKPACK]), _pack_rows([V[n][0] for n in RKPACK]),
                  "adam_rkpack")
    for name, parts4 in zip(RKPACK, zip(*[_unpack_rows(t, rk_shapes) for t in rk_res])):
        put(name, *parts4)
    return loss, grad_x, res, (small_all, small_shapes, small_w, small_m, small_v)


def _small_update(small_all, small_shapes, P, M, V, res):
    G = (D_MODEL // 2) // S5_GROUP
    names = [n for n in SMALL if n != 's5_log_step']
    shapes = [small_shapes[n] for n in SMALL]
    parts = _unpack_flat_batched(small_all, shapes)
    by = dict(zip(SMALL, parts))
    ls = by['s5_log_step']
    ls = ls.transpose(0, 3, 1, 2).reshape(N_DEV * S5_STATE, N_DIR * G)
    pk = lambda d: _pack_flat([d[n] for n in names])
    packs = jnp.stack([_pack_flat([by[n][j] for n in names]) for j in range(N_DEV)])
    out = adam(packs, pk(P), pk(M), pk(V), "adam_small")
    shp = [P[n].shape for n in names]
    for name, parts4 in zip(names, zip(*[_unpack_flat(t, shp) for t in out])):
        res[name] = parts4
    lsw = lambda d: jnp.pad(d['s5_log_step'].reshape(1, N_DIR * G), ((0, SUBLANE - 1), (0, 0)))
    ls_parts = jnp.pad(ls[:, None, :], ((0, 0), (0, SUBLANE - 1), (0, 0)))
    o = adam(ls_parts, lsw(P), lsw(M), lsw(V), "adam_log_step")
    res['s5_log_step'] = tuple(t[0:1].reshape(P['s5_log_step'].shape) for t in o)


def _unpack_flat_batched(packed, shapes):
    out, r0 = [], 0
    B = packed.shape[0]
    for s in shapes:
        n = math.prod(s)
        nr = _round_up(n, SUBLANE * LANE) // LANE
        out.append(packed[:, r0:r0 + nr].reshape(B, -1)[:, :n].reshape((B,) + tuple(s)))
        r0 += nr
    return out


def kernel(x, c, ada_w, ada_b, norm1_gain, norm2_gain, final_gain, w_in, w_out, s5_lambda_re, s5_lambda_im, s5_log_step, s5_b_re, s5_b_im, s5_c_re, s5_c_im, s5_d, s5_w_glu, s5_b_glu, rk_shift_prev, rk_shift_next, rk_w0, rk_w_up, rk_a0, rk_a_up, rk_g_up, rk_k_k, rk_k_a, rk_r_k, rk_ln_gain, rk_ln_bias, ffn_w1, ffn_w2, loss_target, m_ada_w, m_ada_b, m_norm1_gain, m_norm2_gain, m_final_gain, m_w_in, m_w_out, m_s5_lambda_re, m_s5_lambda_im, m_s5_log_step, m_s5_b_re, m_s5_b_im, m_s5_c_re, m_s5_c_im, m_s5_d, m_s5_w_glu, m_s5_b_glu, m_rk_shift_prev, m_rk_shift_next, m_rk_w0, m_rk_w_up, m_rk_a0, m_rk_a_up, m_rk_g_up, m_rk_k_k, m_rk_k_a, m_rk_r_k, m_rk_ln_gain, m_rk_ln_bias, m_ffn_w1, m_ffn_w2, v_ada_w, v_ada_b, v_norm1_gain, v_norm2_gain, v_final_gain, v_w_in, v_w_out, v_s5_lambda_re, v_s5_lambda_im, v_s5_log_step, v_s5_b_re, v_s5_b_im, v_s5_c_re, v_s5_c_im, v_s5_d, v_s5_w_glu, v_s5_b_glu, v_rk_shift_prev, v_rk_shift_next, v_rk_w0, v_rk_w_up, v_rk_a0, v_rk_a_up, v_rk_g_up, v_rk_k_k, v_rk_k_a, v_rk_r_k, v_rk_ln_gain, v_rk_ln_bias, v_ffn_w1, v_ffn_w2):
    given = dict(locals())
    P = {n: given[n] for n in ['x', 'c', 'loss_target'] + WEIGHTS}
    M = {n: given['m_' + n] for n in WEIGHTS}
    V = {n: given['v_' + n] for n in WEIGHTS}
    loss, grad_x, res, small = _step(P, M, V)
    small_all, small_shapes, _, _, _ = small
    _small_update(small_all, small_shapes, P, M, V, res)
    outs = [loss, grad_x[None]]
    for q in range(4):
        outs += [res[n][q] for n in WEIGHTS]
    return tuple(outs)
```

```python
import functools
import math

import jax
import jax.numpy as jnp
from jax import lax
from jax.experimental import pallas as pl
from jax.experimental.pallas import tpu as pltpu

F32 = jnp.float32
BF16 = jnp.bfloat16
HI = lax.Precision.HIGHEST
MXU_DTYPE = jnp.bfloat16

N_DEV = 8
MESH_AXES = ("x", "y", "c")
D_MODEL = 2048
SEQ = 2048
S5_GROUP = 16
S5_STATE = 64
RWKV_HEAD = 64
DECAY_LORA = 64
GATE_LORA = 160
N_DIR = 2
N_MOD = 6
NORM_EPS = 1e-6
GN_EPS = 64e-5
L2_EPS = 1e-12
ADAM_LR = 0.001
ADAM_B1 = 0.9
ADAM_B2 = 0.999
ADAM_EPS = 1e-08
ADAM_WD = 0.01
ADAM_STEP = 10
LANE = 128
SUBLANE = 8
S5_CHUNK_GROUPS = 8
S5_SCAN_ROWS = 256
VMEM_LIMIT = 56 * 1024 * 1024

WEIGHTS = ['ada_w', 'ada_b', 'norm1_gain', 'norm2_gain', 'final_gain', 'w_in', 'w_out', 's5_lambda_re',
           's5_lambda_im', 's5_log_step', 's5_b_re', 's5_b_im', 's5_c_re', 's5_c_im', 's5_d', 's5_w_glu',
           's5_b_glu', 'rk_shift_prev', 'rk_shift_next', 'rk_w0', 'rk_w_up', 'rk_a0', 'rk_a_up', 'rk_g_up',
           'rk_k_k', 'rk_k_a', 'rk_r_k', 'rk_ln_gain', 'rk_ln_bias', 'ffn_w1', 'ffn_w2']
SMALL = ['ada_b', 'norm1_gain', 'norm2_gain', 'final_gain', 's5_lambda_re', 's5_lambda_im', 's5_log_step',
         's5_b_re', 's5_b_im', 's5_c_re', 's5_c_im', 's5_d', 's5_b_glu', 'rk_shift_prev', 'rk_shift_next',
         'rk_k_k', 'rk_k_a', 'rk_r_k', 'rk_ln_gain', 'rk_ln_bias']
RKPACK = ['rk_w0', 'rk_a0', 'rk_w_up', 'rk_a_up', 'rk_g_up']


def _round_up(n, m):
    return (n + m - 1) // m * m


def _tile(dim, pref, unit=LANE):
    t = min(pref, dim) // unit * unit
    while t >= unit:
        if dim % t == 0:
            return t
        t -= unit
    return dim


def _cparams(sem=None):
    return pltpu.CompilerParams(dimension_semantics=sem, vmem_limit_bytes=VMEM_LIMIT)


def _full_spec(a):
    nd = a.ndim
    return pl.BlockSpec(a.shape, lambda *_: (0,) * nd)


def exchange(arrs, modes, name):
    n = len(arrs)
    out_shape = [jax.ShapeDtypeStruct((N_DEV,) + a.shape if m == 'gather' else a.shape, a.dtype)
                 for a, m in zip(arrs, modes)]

    def body(*refs):
        ins, outs = refs[:n], refs[n:2 * n]
        send_sems, recv_sems, local_sems = refs[2 * n:]
        x, y, c = (lax.axis_index(a) for a in MESH_AXES)
        me = 4 * x + 2 * y + c
        copies = []
        for i in range(n):
            gather = modes[i] == 'gather'
            mine = pltpu.make_async_copy(ins[i] if gather else ins[i].at[me], outs[i].at[me], local_sems.at[i])
            mine.start()
            copies.append(mine)
        remote = []
        for k in range(1, N_DEV):
            px = 1 - x if (k >> 2) & 1 else x
            py = 1 - y if (k >> 1) & 1 else y
            pc = 1 - c if k & 1 else c
            peer = 4 * px + 2 * py + pc
            for i in range(n):
                src = ins[i] if modes[i] == 'gather' else ins[i].at[peer]
                cp = pltpu.make_async_remote_copy(
                    src_ref=src, dst_ref=outs[i].at[me], send_sem=send_sems.at[i, k - 1],
                    recv_sem=recv_sems.at[i, k - 1], device_id=(px, py, pc), device_id_type=pl.DeviceIdType.MESH)
                cp.start()
                remote.append(cp)
        for cp in remote:
            cp.wait_recv()
        for cp in remote:
            cp.wait_send()
        for cp in copies:
            cp.wait()

    any_spec = pl.BlockSpec(memory_space=pl.ANY)
    return pl.pallas_call(
        body, name=name, out_shape=out_shape,
        in_specs=[any_spec] * n, out_specs=[any_spec] * n,
        scratch_shapes=[pltpu.SemaphoreType.DMA((n, N_DEV - 1)), pltpu.SemaphoreType.DMA((n, N_DEV - 1)),
                        pltpu.SemaphoreType.DMA((n,))],
        compiler_params=pltpu.CompilerParams(has_side_effects=True),
    )(*arrs)


def gather_two_level(arrs, name):
    n = len(arrs)
    out_shape = [jax.ShapeDtypeStruct((N_DEV,) + a.shape, a.dtype) for a in arrs]

    def body(*refs):
        ins, outs = refs[:n], refs[n:2 * n]
        send_sems, recv_sems, local_sems = refs[2 * n:]
        x, y, c = (lax.axis_index(a) for a in MESH_AXES)
        me, sibling = (x, y, c), (x, y, 1 - c)
        chips = [(1 - x, y), (x, 1 - y), (1 - x, 1 - y)]

        def slot(px, py, pc):
            return 4 * px + 2 * py + pc

        def copy(i, k, block, to, src=None):
            return pltpu.make_async_remote_copy(
                src_ref=outs[i].at[slot(*block)] if src is None else src, dst_ref=outs[i].at[slot(*block)],
                send_sem=send_sems.at[i, k], recv_sem=recv_sems.at[i, k], device_id=to,
                device_id_type=pl.DeviceIdType.MESH)

        mine = [pltpu.make_async_copy(ins[i], outs[i].at[slot(*me)], local_sems.at[i]) for i in range(n)]
        for cp in mine:
            cp.start()
        started = []
        for i in range(n):
            started.append(copy(i, 0, me, sibling, src=ins[i]))
            started += [copy(i, 1 + j, me, (*chip, c), src=ins[i]) for j, chip in enumerate(chips)]
        for cp in started:
            cp.start()
        for j, chip in enumerate(chips):
            for i in range(n):
                copy(i, 1 + j, (*chip, c), me).wait_recv()
                fwd = copy(i, 4 + j, (*chip, c), sibling)
                fwd.start()
                started.append(fwd)
        for i in range(n):
            copy(i, 0, sibling, me).wait_recv()
        for j, chip in enumerate(chips):
            for i in range(n):
                copy(i, 4 + j, (*chip, 1 - c), me).wait_recv()
        for cp in started:
            cp.wait_send()
        for cp in mine:
            cp.wait()

    any_spec = pl.BlockSpec(memory_space=pl.ANY)
    return pl.pallas_call(
        body, name=name, out_shape=out_shape,
        in_specs=[any_spec] * n, out_specs=[any_spec] * n,
        scratch_shapes=[pltpu.SemaphoreType.DMA((n, N_DEV - 1)), pltpu.SemaphoreType.DMA((n, N_DEV - 1)),
                        pltpu.SemaphoreType.DMA((n,))],
        compiler_params=pltpu.CompilerParams(has_side_effects=True),
    )(*arrs)


def matmul(a, b, *, name, ta=False, tb=False, b_slots=False, out_slots=0, out_dtypes=(F32,), epi=None,
           extras=(), precise=False, tm=512, tn=512, tk=2048):
    if ta:
        K, M = a.shape
    else:
        M, K = a.shape
    if b_slots:
        ns, br, bc = b.shape
        bshape = (br, ns * bc)
    else:
        bshape = b.shape
    N = bshape[0] if tb else bshape[1]
    assert (bshape[1] if tb else bshape[0]) == K, (a.shape, b.shape, ta, tb)
    tm, tn, tk = _tile(M, tm, SUBLANE), _tile(N, tn), _tile(K, tk, SUBLANE if K < LANE else LANE)
    if b_slots and tb:
        tk = _tile(b.shape[2], tk)
    elif b_slots:
        tn = _tile(b.shape[2], tn)
    if out_slots:
        tn = _tile(N // out_slots, tn)
    if b_slots:
        cs = b.shape[2]
        tcol = tk if tb else tn
        assert cs % tcol == 0
        per = cs // tcol
    if out_slots:
        ncs = N // out_slots
        assert ncs % tn == 0
        operc = ncs // tn
    nk = K // tk
    a_spec = pl.BlockSpec((tk, tm), lambda i, j, k: (k, i)) if ta else pl.BlockSpec((tm, tk), lambda i, j, k: (i, k))
    if b_slots:
        if tb:
            b_spec = pl.BlockSpec((None, tn, tk), lambda i, j, k: (k // per, j, k % per))
        else:
            b_spec = pl.BlockSpec((None, tk, tn), lambda i, j, k: (j // per, k, j % per))
    else:
        b_spec = pl.BlockSpec((tn, tk), lambda i, j, k: (j, k)) if tb else pl.BlockSpec((tk, tn), lambda i, j, k: (k, j))
    ex_specs = []
    for arr, kind in extras:
        if kind == 'mn':
            ex_specs.append(pl.BlockSpec((tm, tn), lambda i, j, k: (i, j)))
        else:
            ex_specs.append(pl.BlockSpec((1, tn), lambda i, j, k: (0, j)))
    if out_slots:
        o_spec = pl.BlockSpec((None, tm, tn), lambda i, j, k: (j // operc, i, j % operc))
        o_shape = (out_slots, M, ncs)
    else:
        o_spec = pl.BlockSpec((tm, tn), lambda i, j, k: (i, j))
        o_shape = (M, N)
    ne, no = len(extras), len(out_dtypes)
    dims = (((0 if ta else 1,), (1 if tb else 0,)), ((), ()))
    op_dtype = F32 if precise else MXU_DTYPE

    def body(a_ref, b_ref, *rest):
        ex_refs, out_refs, acc = rest[:ne], rest[ne:ne + no], rest[-1]
        k = pl.program_id(2)
        part = lax.dot_general(a_ref[...].astype(op_dtype), b_ref[...].astype(op_dtype), dims,
                               precision=HI if precise else None, preferred_element_type=F32)

        def finish(total):
            res = epi(total, *[e[...] for e in ex_refs]) if epi is not None else (total,)
            for o, r in zip(out_refs, res):
                o[...] = r.astype(o.dtype)

        if nk == 1:
            finish(part)
        else:
            @pl.when(k == 0)
            def _():
                acc[...] = part

            @pl.when(jnp.logical_and(k > 0, k < nk - 1))
            def _():
                acc[...] += part

            @pl.when(k == nk - 1)
            def _():
                finish(acc[...] + part)

    outs = pl.pallas_call(
        body, name=name, grid=(M // tm, N // tn, nk),
        in_specs=[a_spec, b_spec] + ex_specs, out_specs=[o_spec] * no,
        out_shape=[jax.ShapeDtypeStruct(o_shape, dt) for dt in out_dtypes],
        scratch_shapes=[pltpu.VMEM((tm, tn), F32)],
        compiler_params=_cparams(("parallel", "parallel", "arbitrary")),
    )(a, b, *[e[0] for e in extras])
    return outs[0] if no == 1 else outs


def rowcall(fn, name, tiled, full, tiled_out, acc_out, tt):
    views = [(t, t.shape[1], 0) if not isinstance(t, tuple) else t for t in tiled]
    T = views[0][0].shape[0]
    tt = _tile(T, tt, SUBLANE)
    nt, nf, nto, nao = len(views), len(full), len(tiled_out), len(acc_out)

    def view_spec(w, cb):
        return pl.BlockSpec((tt, w), lambda i: (i, cb))

    in_specs = [view_spec(w, cb) for _, w, cb in views] + [_full_spec(f) for f in full]
    out_specs = [pl.BlockSpec((tt, w), lambda i: (i, 0)) for w, _ in tiled_out]
    out_specs += [pl.BlockSpec(s, lambda i, nd=len(s): (0,) * nd) for s in acc_out]
    out_shape = [jax.ShapeDtypeStruct((T, w), dt) for w, dt in tiled_out]
    out_shape += [jax.ShapeDtypeStruct(s, F32) for s in acc_out]

    def body(*refs):
        tin, fin = refs[:nt], refs[nt:nt + nf]
        tout, aout = refs[nt + nf:nt + nf + nto], refs[nt + nf + nto:]
        touts, aouts = fn(*[r[...] for r in tin], *[r[...] for r in fin])
        for r, v in zip(tout, touts):
            r[...] = v.astype(r.dtype)
        if nao:
            @pl.when(pl.program_id(0) == 0)
            def _():
                for r in aout:
                    r[...] = jnp.zeros_like(r)

            for r, v in zip(aout, aouts):
                r[...] += v.astype(F32)

    outs = pl.pallas_call(
        body, name=name, grid=(T // tt,), in_specs=in_specs, out_specs=out_specs, out_shape=out_shape,
        compiler_params=_cparams(("arbitrary",) if nao else ("parallel",)),
    )(*[v[0] for v in views], *full)
    return outs[:nto], outs[nto:]


def _mm(a, b):
    return jnp.dot(a.astype(MXU_DTYPE), b.astype(MXU_DTYPE), preferred_element_type=F32)


def _rms(x, gain):
    ms = jnp.mean(x * x, axis=-1, keepdims=True)
    return x * lax.rsqrt(ms + NORM_EPS) * gain


def _normmod(x, gain, scale, shift):
    return _rms(x, gain) * (1.0 + scale) + shift


def _gelu_tanh(y):
    return 0.5 * y * (1.0 + jnp.tanh(math.sqrt(2.0 / math.pi) * (y + 0.044715 * (y * y * y))))


def _sigmoid(x):
    return 1.0 / (1.0 + jnp.exp(-x))


def _softplus(x):
    return jnp.maximum(x, 0.0) + jnp.log(1.0 + jnp.exp(-jnp.abs(x)))


def _seg_mats(width, seg):
    r = lax.broadcasted_iota(jnp.int32, (width, LANE), 0) // seg
    c = lax.broadcasted_iota(jnp.int32, (width, LANE), 1)
    s = (r == c).astype(F32)
    rt = lax.broadcasted_iota(jnp.int32, (LANE, width), 0)
    ct = lax.broadcasted_iota(jnp.int32, (LANE, width), 1) // seg
    st = (rt == ct).astype(F32)
    return s, st


def _segsum(x, s):
    return jnp.dot(x, s, precision=HI, preferred_element_type=F32)


def _s5_prep_fn(lre, lim, lstep, bre, bim):
    step = jnp.exp(lstep)
    mag = jnp.exp(lre * step)
    lbr = mag * jnp.cos(lim * step)
    lbi = mag * jnp.sin(lim * step)
    den = lre * lre + lim * lim
    nr = lbr - 1.0
    ni = lbi
    cre = (nr * lre + ni * lim) / den
    cim = (ni * lre - nr * lim) / den
    bbr = jnp.stack([cre[d:d + 1] * bre - cim[d:d + 1] * bim for d in range(N_DIR)])
    bbi = jnp.stack([cre[d:d + 1] * bim + cim[d:d + 1] * bre for d in range(N_DIR)])
    return lbr, lbi, bbr, bbi


def s5_prep(lre, lim, lstep, bre, bim):
    ns = lre.shape[1]

    def body(lre_r, lim_r, ls_r, bre_r, bim_r, lbr_r, lbi_r, bbr_r, bbi_r):
        lbr, lbi, bbr, bbi = _s5_prep_fn(lre_r[...], lim_r[...], ls_r[...], bre_r[...], bim_r[...])
        lbr_r[...] = lbr
        lbi_r[...] = lbi
        bbr_r[...] = bbr
        bbi_r[...] = bbi

    return pl.pallas_call(
        body, name="s5_prep",
        out_shape=[jax.ShapeDtypeStruct((N_DIR, ns), F32)] * 2 + [jax.ShapeDtypeStruct((N_DIR, S5_GROUP, ns), F32)] * 2,
        compiler_params=_cparams(),
    )(lre, lim, lstep, bre, bim)


def s5_prep_bwd(lre, lim, lstep, bre, bim, dlbr, dlbi, dbbr, dbbi):
    ns = lre.shape[1]

    def body(lre_r, lim_r, ls_r, bre_r, bim_r, d1, d2, d3, d4, o1, o2, o3, o4, o5):
        _, vjp = jax.vjp(_s5_prep_fn, lre_r[...], lim_r[...], ls_r[...], bre_r[...], bim_r[...])
        g = vjp((d1[...], d2[...], d3[...], d4[...]))
        for o, v in zip((o1, o2, o3, o4, o5), g):
            o[...] = v

    return pl.pallas_call(
        body, name="s5_prep_bwd",
        out_shape=[jax.ShapeDtypeStruct((N_DIR, ns), F32)] * 3 + [jax.ShapeDtypeStruct((S5_GROUP, ns), F32)] * 2,
        compiler_params=_cparams(),
    )(lre, lim, lstep, bre, bim, dlbr, dlbi, dbbr, dbbi)


def s5_scan(bre, bim, lre, lim, *, reverse, name):
    T, NS = bre.shape
    tt = _tile(T, S5_SCAN_ROWS, SUBLANE)
    wl = _tile(NS, 512)
    nT = T // tt
    ngrp = tt // SUBLANE

    def tmap(j, i):
        return ((nT - 1 - i) if reverse else i, j)

    def body(bre_r, bim_r, lre_r, lim_r, sre_r, sim_r, cre, cim):
        @pl.when(pl.program_id(1) == 0)
        def _():
            cre[...] = jnp.zeros_like(cre)
            cim[...] = jnp.zeros_like(cim)

        lr = jnp.broadcast_to(lre_r[...], (SUBLANE, wl))
        li = jnp.broadcast_to(lim_r[...], (SUBLANE, wl))
        row = lax.broadcasted_iota(jnp.int32, (SUBLANE, wl), 0)
        pows = [(lr, li)]
        for _ in range(3):
            pr, pi = pows[-1]
            pows.append((pr * pr - pi * pi, 2.0 * pr * pi))
        e = (SUBLANE - row) if reverse else (row + 1)
        Pr = jnp.ones((SUBLANE, wl), F32)
        Pi = jnp.zeros((SUBLANE, wl), F32)
        for bit, (qr, qi) in enumerate(pows):
            on = ((e >> bit) & 1) == 1
            nr, ni = Pr * qr - Pi * qi, Pr * qi + Pi * qr
            Pr, Pi = jnp.where(on, nr, Pr), jnp.where(on, ni, Pi)

        def group(g, carry):
            gg = (ngrp - 1 - g) if reverse else g
            rows = pl.ds(pl.multiple_of(gg * SUBLANE, SUBLANE), SUBLANE)
            sr, si = bre_r[rows, :], bim_r[rows, :]
            for lvl, k in enumerate((1, 2, 4)):
                qr, qi = pows[lvl]
                if reverse:
                    shr = pltpu.roll(sr, SUBLANE - k, 0)
                    shi = pltpu.roll(si, SUBLANE - k, 0)
                    keep = row < SUBLANE - k
                else:
                    shr = pltpu.roll(sr, k, 0)
                    shi = pltpu.roll(si, k, 0)
                    keep = row >= k
                shr = jnp.where(keep, shr, 0.0)
                shi = jnp.where(keep, shi, 0.0)
                sr, si = sr + qr * shr - qi * shi, si + qr * shi + qi * shr
            cr, ci = cre[...], cim[...]
            sr, si = sr + Pr * cr - Pi * ci, si + Pr * ci + Pi * cr
            sre_r[rows, :] = sr
            sim_r[rows, :] = si
            last = 0 if reverse else SUBLANE - 1
            cre[...] = jnp.broadcast_to(sr[last:last + 1, :], (SUBLANE, wl))
            cim[...] = jnp.broadcast_to(si[last:last + 1, :], (SUBLANE, wl))
            return carry

        lax.fori_loop(0, ngrp, group, 0)

    blk = pl.BlockSpec((tt, wl), tmap)
    row_spec = pl.BlockSpec((1, wl), lambda j, i: (0, j))
    return pl.pallas_call(
        body, name=name, grid=(NS // wl, nT),
        in_specs=[blk, blk, row_spec, row_spec], out_specs=[blk, blk],
        out_shape=[jax.ShapeDtypeStruct((T, NS), F32)] * 2,
        scratch_shapes=[pltpu.VMEM((SUBLANE, wl), F32)] * 2,
        compiler_params=_cparams(("parallel", "arbitrary")),
    )(bre, bim, lre, lim)


def s5_dlam(sre, sim, gre, gim, *, reverse, name):
    T, NS = sre.shape
    wl = _tile(NS, 256)

    def body(sr_r, si_r, gr_r, gi_r, dr_r, di_r):
        row = lax.broadcasted_iota(jnp.int32, (T, wl), 0)
        if reverse:
            keep = row < T - 1
            pr = jnp.where(keep, pltpu.roll(sr_r[...], T - 1, 0), 0.0)
            pi = jnp.where(keep, pltpu.roll(si_r[...], T - 1, 0), 0.0)
        else:
            keep = row >= 1
            pr = jnp.where(keep, pltpu.roll(sr_r[...], 1, 0), 0.0)
            pi = jnp.where(keep, pltpu.roll(si_r[...], 1, 0), 0.0)
        gr, gi = gr_r[...], gi_r[...]
        dr_r[...] = jnp.sum(pr * gr + pi * gi, axis=0, keepdims=True)
        di_r[...] = jnp.sum(pr * gi - pi * gr, axis=0, keepdims=True)

    blk = pl.BlockSpec((T, wl), lambda j: (0, j))
    o = pl.BlockSpec((1, wl), lambda j: (0, j))
    return pl.pallas_call(
        body, name=name, grid=(NS // wl,), in_specs=[blk] * 4, out_specs=[o, o],
        out_shape=[jax.ShapeDtypeStruct((1, NS), F32)] * 2,
        compiler_params=_cparams(("parallel",)),
    )(sre, sim, gre, gim)


def _s5_bu_fn(u, wblk):
    nch = wblk.shape[0]
    cw = S5_CHUNK_GROUPS * S5_GROUP
    sw = S5_CHUNK_GROUPS * S5_STATE
    parts = [[] for _ in range(4)]
    for ch in range(nch):
        res = _mm(u[:, ch * cw:(ch + 1) * cw], wblk[ch])
        for q in range(4):
            parts[q].append(res[:, q * sw:(q + 1) * sw])
    return tuple(jnp.concatenate(p, axis=1) if nch > 1 else p[0] for p in parts)


def _s5_out_fn(x0r, x0i, x1r, x1i, u, cre, cim, dsk, wglu, bglu):
    xr, xi = x0r + x1r, x0i + x1i
    nch = cre.shape[0]
    sw = S5_CHUNK_GROUPS * S5_STATE
    ys = [_mm(xr[:, ch * sw:(ch + 1) * sw], cre[ch]) - _mm(xi[:, ch * sw:(ch + 1) * sw], cim[ch]) for ch in range(nch)]
    y = jnp.concatenate(ys, axis=1) if nch > 1 else ys[0]
    z = _gelu_tanh(y + dsk * u)
    gate = _sigmoid(_mm(z, wglu) + bglu)
    return z * gate


def _rk_dims():
    C = D_MODEL // 2
    LW = N_DIR * DECAY_LORA
    GP = _round_up(GATE_LORA, LANE)
    return C, LW, GP


def _rk_pre_fn(ps, w0, wup0, wup1, a0, aup0, aup1, gup, k_k, k_a):
    C, LW, GP = _rk_dims()
    r, k, v = ps[:, 0:C], ps[:, C:2 * C], ps[:, 2 * C:3 * C]
    wdn = ps[:, 3 * C:3 * C + LW]
    adn = ps[:, 3 * C + LW:3 * C + 2 * LW]
    gdn = ps[:, 3 * C + 2 * LW:3 * C + 2 * LW + GP]
    s, st = _seg_mats(C, RWKV_HEAD)
    kk = k * k_k
    n2 = _segsum(kk * kk, s)
    n2 = jnp.where(n2 > 0.0, n2, 1.0)
    inv = 1.0 / jnp.maximum(jnp.sqrt(n2), L2_EPS)
    kkn = kk * _segsum(inv, st)
    tw = jnp.tanh(wdn)
    wup, aup = (wup0, wup1), (aup0, aup1)
    ws, ks, bs = [], [], []
    for d in range(N_DIR):
        wraw = w0[d:d + 1] + _mm(tw, wup[d])
        w = -_softplus(-wraw) - 0.5
        ws.append(jnp.exp(-jnp.exp(w)))
        a = _sigmoid(a0[d:d + 1] + _mm(adn, aup[d]))
        ks.append(k * (1.0 + (a - 1.0) * k_a))
        bs.append(kkn * a)
    g = _mm(_sigmoid(gdn), gup)
    return r, v, kkn, ws[0], ws[1], ks[0], ks[1], bs[0], bs[1], g


def _rk_post_fn(y0, y1, r, v, k0, k1, g, r_k, lng, lnb):
    C = r.shape[1]
    s, st = _seg_mats(C, RWKV_HEAD)
    y = y0 + y1
    mu = _segsum(_segsum(y, s) * (1.0 / RWKV_HEAD), st)
    yc = y - mu
    var = _segsum(_segsum(yc * yc, s) * (1.0 / RWKV_HEAD), st)
    yn = yc * lax.rsqrt(var + GN_EPS) * lng + lnb
    bonus = _segsum(_segsum(r * (k0 + k1) * r_k, s), st)
    return (yn + bonus * v) * g


def rk_shift(proj, mp, mn, col0):
    T = proj.shape[0]
    W = mp.shape[1]
    wl = _tile(math.gcd(W, col0), 256)
    cb0 = col0 // wl

    def body(p_r, mp_r, mn_r, o_r):
        p = p_r[...]
        row = lax.broadcasted_iota(jnp.int32, (T, wl), 0)
        prev = jnp.where(row >= 1, pltpu.roll(p, 1, 0), 0.0)
        nxt = jnp.where(row < T - 1, pltpu.roll(p, T - 1, 0), 0.0)
        o_r[...] = p + mp_r[...] * (prev - p) + mn_r[...] * (nxt - p)

    rs = pl.BlockSpec((1, wl), lambda j: (0, j))
    return pl.pallas_call(
        body, name="rk_shift", grid=(W // wl,),
        in_specs=[pl.BlockSpec((T, wl), lambda j: (0, cb0 + j)), rs, rs],
        out_specs=pl.BlockSpec((T, wl), lambda j: (0, j)),
        out_shape=jax.ShapeDtypeStruct((T, W), F32),
        compiler_params=_cparams(("parallel",)),
    )(proj, mp, mn)


def rk_shift_bwd(dps, proj, mp, mn, col0):
    T, W = dps.shape
    wl = _tile(math.gcd(W, col0), 256)
    cb0 = col0 // wl

    def body(d_r, p_r, mp_r, mn_r, dp_r, dmp_r, dmn_r):
        d, p = d_r[...], p_r[...]
        mpv, mnv = mp_r[...], mn_r[...]
        row = lax.broadcasted_iota(jnp.int32, (T, wl), 0)
        first, last = row >= 1, row < T - 1
        prev = jnp.where(first, pltpu.roll(p, 1, 0), 0.0)
        nxt = jnp.where(last, pltpu.roll(p, T - 1, 0), 0.0)
        dmp_r[...] = jnp.sum(d * (prev - p), axis=0, keepdims=True)
        dmn_r[...] = jnp.sum(d * (nxt - p), axis=0, keepdims=True)
        dp_r[...] = (d * (1.0 - mpv - mnv) + jnp.where(last, pltpu.roll(d * mpv, T - 1, 0), 0.0)
                     + jnp.where(first, pltpu.roll(d * mnv, 1, 0), 0.0))

    rs = pl.BlockSpec((1, wl), lambda j: (0, j))
    blk = pl.BlockSpec((T, wl), lambda j: (0, j))
    return pl.pallas_call(
        body, name="rk_shift_bwd", grid=(W // wl,),
        in_specs=[blk, pl.BlockSpec((T, wl), lambda j: (0, cb0 + j)), rs, rs],
        out_specs=[blk, rs, rs],
        out_shape=[jax.ShapeDtypeStruct((T, W), F32), jax.ShapeDtypeStruct((1, W), F32), jax.ShapeDtypeStruct((1, W), F32)],
        compiler_params=_cparams(("parallel",)),
    )(dps, proj, mp, mn)


RK_FWD_PAIRS = 4
RK_BWD_PAIRS = 2
RK_SPREAD_PAIRS = 2
RK_TIME_BLOCK = 32
RK_LANE_BLOCK = 128


def _rk_blocks(T, C, N, order_reversed, pairs):
    pw = 2 * N
    pp = min(pairs, C // pw)
    tb = min(RK_TIME_BLOCK, T)
    lb = min(RK_LANE_BLOCK, T)
    nb, per = T // tb, lb // tb

    def tix(i):
        return (nb - 1 - i) if order_reversed else i

    rows = pl.BlockSpec((tb, pp * pw), lambda g, i: (tix(i), g))
    cols = pl.BlockSpec((2 * pp, N, lb), lambda g, i: (g, 0, tix(i) // per))
    hist = pl.BlockSpec((tb, pp, N, pw), lambda g, i: (tix(i), g, 0, 0))
    return pp, pw, tb, lb, nb, per, tix, rows, cols, hist


def rk_spread(xT, name):
    H, N, T = xT.shape
    pw = 2 * N
    lb = min(RK_LANE_BLOCK, T)
    pp = min(RK_SPREAD_PAIRS, H // 2)

    def body(x_r, o_r):
        lane = lax.broadcasted_iota(jnp.int32, (N, lb), 1)
        first = lax.broadcasted_iota(jnp.int32, (N, pw), 1) < N
        tiles = [x_r[h] for h in range(2 * pp)]

        def step(t, carry):
            for p in range(pp):
                c = [jnp.sum(jnp.where(lane == t, tiles[2 * p + q], 0.0), axis=1, keepdims=True) for q in range(2)]
                o_r[t, p] = jnp.where(first, c[0], c[1])
            return carry

        lax.fori_loop(0, lb, step, 0, unroll=8)

    return pl.pallas_call(
        body, name=name, grid=(H // (2 * pp), T // lb),
        in_specs=[pl.BlockSpec((2 * pp, N, lb), lambda g, i: (g, 0, i))],
        out_specs=pl.BlockSpec((lb, pp, N, pw), lambda g, i: (i, g, 0, 0)),
        out_shape=jax.ShapeDtypeStruct((T, H // 2, N, pw), F32),
        compiler_params=_cparams(("parallel", "parallel")),
    )(xT)


def _half_sums(x, first):
    return (jnp.sum(jnp.where(first, x, 0.0), axis=1, keepdims=True),
            jnp.sum(jnp.where(first, 0.0, x), axis=1, keepdims=True))


def rk_scan(r, kk, w, k, b, vc, *, reverse, name):
    T, C = r.shape
    N = vc.shape[2]
    H = C // N
    pp, pw, tb, lb, nb, per, tix, rows, cols, hist_spec = _rk_blocks(T, C, N, reverse, RK_FWD_PAIRS)

    def body(r_r, kk_r, w_r, k_r, b_r, VC, yT_r, hist_r, S, YA):
        i = pl.program_id(1)

        @pl.when(i == 0)
        def _():
            S[...] = jnp.zeros_like(S)

        @pl.when(i % per == 0)
        def _():
            yT_r[...] = jnp.zeros_like(yT_r)

        off = (tix(i) % per) * tb
        first = lax.broadcasted_iota(jnp.int32, (N, pw), 1) < N
        YA[...] = jnp.zeros_like(YA)
        st = [S[p] for p in range(pp)]
        seg = [slice(p * pw, (p + 1) * pw) for p in range(pp)]
        for s in range(tb):
            t = (tb - 1 - s) if reverse else s
            row = slice(t, t + 1)
            sk = [_half_sums(st[p] * kk_r[row, seg[p]], first) for p in range(pp)]
            for p in range(pp):
                hist_r[t, p] = st[p]
                skp = jnp.where(first, sk[p][0], sk[p][1])
                st[p] = st[p] * w_r[row, seg[p]] - skp * b_r[row, seg[p]] + VC[t, p] * k_r[row, seg[p]]
            ys = [_half_sums(st[p] * r_r[row, seg[p]], first) for p in range(pp)]
            for p in range(pp):
                for q in range(2):
                    YA[2 * p + q, :, row] = ys[p][q]
        for p in range(pp):
            S[p] = st[p]
        for h in range(2 * pp):
            yT_r[h] = yT_r[h] + (pltpu.roll(YA[h], off, 1) if per > 1 else YA[h])

    return pl.pallas_call(
        body, name=name, grid=(C // (pp * pw), nb),
        in_specs=[rows] * 5 + [hist_spec], out_specs=[cols, hist_spec],
        out_shape=[jax.ShapeDtypeStruct((H, N, T), F32), jax.ShapeDtypeStruct((T, H // 2, N, pw), F32)],
        scratch_shapes=[pltpu.VMEM((pp, N, pw), F32), pltpu.VMEM((2 * pp, N, lb), F32)],
        compiler_params=_cparams(("parallel", "arbitrary")),
    )(r, kk, w, k, b, vc)


def rk_scan_bwd(r, kk, w, k, b, vc, dc, hist, *, reverse, name):
    T, C = r.shape
    N = vc.shape[2]
    H = C // N
    pp, pw, tb, lb, nb, per, tix, rows, cols, hist_spec = _rk_blocks(T, C, N, not reverse, RK_BWD_PAIRS)

    def body(r_r, kk_r, w_r, k_r, b_r, VC, DC, hist_r, dr_r, dkk_r, dw_r, dk_r, db_r, dvT_r, G, YA):
        i = pl.program_id(1)

        @pl.when(i == 0)
        def _():
            G[...] = jnp.zeros_like(G)

        @pl.when(i % per == 0)
        def _():
            dvT_r[...] = jnp.zeros_like(dvT_r)

        off = (tix(i) % per) * tb
        first = lax.broadcasted_iota(jnp.int32, (N, pw), 1) < N
        YA[...] = jnp.zeros_like(YA)
        gs = [G[p] for p in range(pp)]
        seg = [slice(p * pw, (p + 1) * pw) for p in range(pp)]
        for s in range(tb):
            t = s if reverse else (tb - 1 - s)
            row = slice(t, t + 1)
            g = [gs[p] + DC[t, p] * r_r[row, seg[p]] for p in range(pp)]
            sk = [_half_sums(hist_r[t, p] * kk_r[row, seg[p]], first) for p in range(pp)]
            gb = [_half_sums(g[p] * b_r[row, seg[p]], first) for p in range(pp)]
            gk = [_half_sums(g[p] * k_r[row, seg[p]], first) for p in range(pp)]
            for p in range(pp):
                sp = hist_r[t, p]
                kkv, wv, kv, bv = kk_r[row, seg[p]], w_r[row, seg[p]], k_r[row, seg[p]], b_r[row, seg[p]]
                vcol, dycol = VC[t, p], DC[t, p]
                sa = -jnp.where(first, sk[p][0], sk[p][1])
                dsa = jnp.where(first, gb[p][0], gb[p][1])
                sn = sp * wv + sa * bv + vcol * kv
                dr_r[row, seg[p]] = jnp.sum(sn * dycol, axis=0, keepdims=True)
                dw_r[row, seg[p]] = jnp.sum(g[p] * sp, axis=0, keepdims=True)
                db_r[row, seg[p]] = jnp.sum(g[p] * sa, axis=0, keepdims=True)
                dk_r[row, seg[p]] = jnp.sum(g[p] * vcol, axis=0, keepdims=True)
                dkk_r[row, seg[p]] = -jnp.sum(sp * dsa, axis=0, keepdims=True)
                gs[p] = g[p] * wv - dsa * kkv
                for q in range(2):
                    YA[2 * p + q, :, row] = gk[p][q]
        for p in range(pp):
            G[p] = gs[p]
        for h in range(2 * pp):
            dvT_r[h] = dvT_r[h] + (pltpu.roll(YA[h], off, 1) if per > 1 else YA[h])

    return pl.pallas_call(
        body, name=name, grid=(C // (pp * pw), nb),
        in_specs=[rows] * 5 + [hist_spec] * 3, out_specs=[rows] * 5 + [cols],
        out_shape=[jax.ShapeDtypeStruct((T, C), F32)] * 5 + [jax.ShapeDtypeStruct((H, N, T), F32)],
        scratch_shapes=[pltpu.VMEM((pp, N, pw), F32), pltpu.VMEM((2 * pp, N, lb), F32)],
        compiler_params=_cparams(("parallel", "arbitrary")),
    )(r, kk, w, k, b, vc, dc, hist)


def adam(parts, w, m, v, name):
    P, R, C = parts.shape
    tr = _tile(R, max(SUBLANE, (1 << 19) // max(C, 1) // SUBLANE * SUBLANE), SUBLANE)
    c1 = 1.0 / (1.0 - ADAM_B1 ** ADAM_STEP)
    c2 = 1.0 / (1.0 - ADAM_B2 ** ADAM_STEP)

    def body(p_r, w_r, m_r, v_r, g_o, d_o, m_o, v_o):
        g = p_r[0].astype(F32)
        for q in range(1, P):
            g = g + p_r[q].astype(F32)
        m2 = ADAM_B1 * m_r[...] + (1.0 - ADAM_B1) * g
        v2 = ADAM_B2 * v_r[...] + (1.0 - ADAM_B2) * (g * g)
        g_o[...] = g
        m_o[...] = m2
        v_o[...] = v2
        d_o[...] = -ADAM_LR * ((m2 * c1) / (jnp.sqrt(v2 * c2) + ADAM_EPS) + ADAM_WD * w_r[...])

    blk = pl.BlockSpec((tr, C), lambda i: (i, 0))
    return pl.pallas_call(
        body, name=name, grid=(R // tr,),
        in_specs=[pl.BlockSpec((P, tr, C), lambda i: (0, i, 0)), blk, blk, blk], out_specs=[blk] * 4,
        out_shape=[jax.ShapeDtypeStruct((R, C), F32)] * 4,
        compiler_params=_cparams(("parallel",)),
    )(parts, w, m, v)


def _pack_flat(arrs):
    rows = []
    for a in arrs:
        f = a.reshape(-1).astype(F32)
        n = _round_up(f.shape[0], SUBLANE * LANE)
        rows.append(jnp.pad(f, (0, n - f.shape[0])).reshape(-1, LANE))
    return jnp.concatenate(rows, axis=0)


def _unpack_flat(packed, shapes):
    out, r0 = [], 0
    for s in shapes:
        n = math.prod(s)
        nr = _round_up(n, SUBLANE * LANE) // LANE
        out.append(packed[r0:r0 + nr].reshape(-1)[:n].reshape(s))
        r0 += nr
    return out


def _pack_rows(arrs):
    rows = []
    for a in arrs:
        f = a.reshape(-1, a.shape[-1]).astype(F32)
        n = _round_up(f.shape[0], SUBLANE)
        rows.append(jnp.pad(f, ((0, n - f.shape[0]), (0, 0))))
    return jnp.concatenate(rows, axis=0)


def _unpack_rows(packed, shapes):
    out, r0 = [], 0
    for s in shapes:
        nr = math.prod(s[:-1])
        out.append(packed[r0:r0 + nr].reshape(s))
        r0 += _round_up(nr, SUBLANE)
    return out


def _cols_from_slots(g):
    return jnp.moveaxis(g, 0, -2).reshape(g.shape[1:-1] + (N_DEV * g.shape[-1],))


def _cols_to_slots(a):
    cs = a.shape[-1] // N_DEV
    return jnp.moveaxis(a.reshape(a.shape[:-1] + (N_DEV, cs)), -2, 0)


def _step(P, M, V):
    D, T = D_MODEL, SEQ
    S5W = D // 2
    C, LW, GP = _rk_dims()
    H = C // RWKV_HEAD
    G = S5W // S5_GROUP
    NS = G * S5_STATE
    NCH = G // S5_CHUNK_GROUPS
    SW = S5_CHUNK_GROUPS * S5_STATE
    RIN = 3 * C + 2 * LW + GATE_LORA
    RINP = 3 * C + 2 * LW + GP
    PROJ = S5W + RIN
    PROJP = S5W + RINP
    FF = 4 * D
    me = 4 * lax.axis_index("x") + 2 * lax.axis_index("y") + lax.axis_index("c")
    cs_mod = N_MOD * D // N_DEV
    eye = jnp.eye(S5_CHUNK_GROUPS, dtype=F32)

    x = P['x'][0]
    target = P['loss_target'][0]

    (c_all,) = exchange([P['c']], ['gather'], "comm_gather_c")
    c_all = c_all.reshape(N_DEV, D)
    (c_act,), _ = rowcall(lambda cv: ((cv * _sigmoid(cv),), ()), "silu_c", [c_all], [], [(D, F32)], [], N_DEV)
    ada_b_loc = lax.dynamic_slice(P['ada_b'], (0, me * cs_mod), (1, cs_mod))
    mod_loc = matmul(c_act, P['ada_w'][0], name="mod_mm", precise=True, extras=[(ada_b_loc, 'n')],
                     epi=lambda acc, bias: (acc + bias,))

    rk_shapes = [P[n][0].shape for n in RKPACK]
    rk_pack = _pack_rows([P[n][0] for n in RKPACK])
    gathered = gather_two_level(
        [mod_loc, P['w_in'][0].astype(BF16), P['w_out'][0].astype(BF16), P['ffn_w1'][0].astype(BF16),
         P['ffn_w2'][0].astype(BF16), P['s5_w_glu'][0].astype(BF16), rk_pack], "comm_gather_weights")
    mod_all, w_in_g, w_out_g, w1_g, w2_g, wglu_g, rk_g = gathered
    mod_me = lax.dynamic_index_in_dim(mod_all, me, axis=1, keepdims=False).reshape(N_MOD, 1, D)
    shift1, scale1, gate1, shift2, scale2, gate2 = (mod_me[i] for i in range(N_MOD))
    w_in = jnp.pad(_cols_from_slots(w_in_g), ((0, 0), (0, PROJP - PROJ)))
    w_out = w_out_g.reshape(D, D)
    w2 = w2_g.reshape(FF, D)
    wglu = wglu_g.reshape(S5W, S5W)
    rk_full = _unpack_rows(_cols_from_slots(rk_g), [s[:-1] + (C,) for s in rk_shapes])
    rk_w0, rk_a0, rk_wup, rk_aup, rk_gup = rk_full

    def lora_pad(up):
        z = jnp.zeros((N_DIR, LW, C), F32)
        for d in range(N_DIR):
            z = z.at[d, d * DECAY_LORA:(d + 1) * DECAY_LORA].set(up[d])
        return z

    wup_p, aup_p = lora_pad(rk_wup), lora_pad(rk_aup)
    gup_p = jnp.pad(rk_gup, ((0, GP - GATE_LORA), (0, 0)))
    mu_prev = jnp.pad(P['rk_shift_prev'], ((0, 0), (0, RINP - RIN)))
    mu_next = jnp.pad(P['rk_shift_next'], ((0, 0), (0, RINP - RIN)))
    r_k = P['rk_r_k'].reshape(1, C)
    fgain = P['final_gain'].reshape(1, D)

    TT = 256
    (h1,), _ = rowcall(lambda xv, g, sc, sh: ((_normmod(xv, g, sc, sh),), ()), "norm1",
                       [x], [P['norm1_gain'], scale1, shift1], [(D, BF16)], [], TT)
    proj = matmul(h1, w_in, name="proj_mm")

    lre = P['s5_lambda_re'][0].reshape(N_DIR, NS)
    lim = P['s5_lambda_im'][0].reshape(N_DIR, NS)
    lstep = jnp.broadcast_to(P['s5_log_step'][0][:, :, None], (N_DIR, G, S5_STATE)).reshape(N_DIR, NS)
    bre = P['s5_b_re'][0].reshape(NS, S5_GROUP).T
    bim = P['s5_b_im'][0].reshape(NS, S5_GROUP).T
    lbr, lbi, bbr, bbi = s5_prep(lre, lim, lstep, bre, bim)
    bbar = jnp.stack([bbr, bbi], axis=1).reshape(N_DIR, 2, S5_GROUP, NCH, S5_CHUNK_GROUPS, S5_STATE)
    wblk = jnp.einsum('drhcgp,gk->cghdrkp', bbar, eye).reshape(NCH, S5_CHUNK_GROUPS * S5_GROUP, 4 * SW)
    wblk = wblk.astype(MXU_DTYPE)
    u_view = (proj, S5W, 0)
    bus, _ = rowcall(lambda uv, wb: (_s5_bu_fn(uv, wb), ()), "s5_bu", [u_view], [wblk], [(NS, F32)] * 4, [], TT)
    s0r, s0i = s5_scan(bus[0], bus[1], lbr[0:1], lbi[0:1], reverse=False, name="s5_scan_f0")
    s1r, s1i = s5_scan(bus[2], bus[3], lbr[1:2], lbi[1:2], reverse=True, name="s5_scan_f1")

    def cblk(cm):
        c4 = cm.reshape(NCH, S5_CHUNK_GROUPS, S5_GROUP, S5_STATE)
        return jnp.einsum('cghp,gk->cgpkh', c4, eye).reshape(NCH, SW, S5_CHUNK_GROUPS * S5_GROUP)

    cre_b = cblk(P['s5_c_re'][0]).astype(MXU_DTYPE)
    cim_b = cblk(P['s5_c_im'][0]).astype(MXU_DTYPE)
    s5_full = [cre_b, cim_b, P['s5_d'], wglu, P['s5_b_glu']]
    TS = 128
    (y_s5,), _ = rowcall(lambda *a: ((_s5_out_fn(*a),), ()), "s5_out", [s0r, s0i, s1r, s1i, u_view], s5_full,
                         [(S5W, BF16)], [], TS)

    ps = rk_shift(proj, mu_prev, mu_next, S5W)
    pre_full = [rk_w0, wup_p[0], wup_p[1], rk_a0, aup_p[0], aup_p[1], gup_p, P['rk_k_k'], P['rk_k_a']]
    pre_out, _ = rowcall(lambda *a: (_rk_pre_fn(*a)[2:], ()), "rk_pre", [ps], pre_full, [(C, F32)] * 8, [], TS)
    kkn, w_0, w_1, k_0, k_1, b_0, b_1, g_gate = pre_out
    r_t, v_t = ps[:, 0:C], ps[:, 2 * C:3 * C]

    def hmT(a):
        return a.reshape(T, H, RWKV_HEAD).transpose(1, 2, 0)

    def unT(a):
        return a.transpose(2, 0, 1).reshape(T, C)

    v_cols = rk_spread(hmT(v_t), "rk_spread_v")
    dir_rows = [(w_0, k_0, b_0), (w_1, k_1, b_1)]
    yT, hist = [], []
    for d in range(N_DIR):
        wd, kd, bd = dir_rows[d]
        yd, hd = rk_scan(r_t, kkn, wd, kd, bd, v_cols, reverse=(d == 1), name=f"rk_scan_f{d}")
        yT.append(yd)
        hist.append(hd)
    y_0, y_1 = unT(yT[0]), unT(yT[1])
    post_full = [r_k, P['rk_ln_gain'], P['rk_ln_bias']]
    post_tiled = [y_0, y_1, (ps, C, 0), (ps, C, 2), k_0, k_1, g_gate]
    (y_rk,), _ = rowcall(lambda *a: ((_rk_post_fn(*a),), ()), "rk_post", post_tiled, post_full, [(C, BF16)], [], TS)

    ycat = jnp.concatenate([y_s5, y_rk], axis=1)
    mixed = matmul(ycat, w_out, name="out_mm")

    def res_norm(xv, mv, gate, g, sc, sh):
        x1v = xv + gate * mv
        return x1v, _normmod(x1v, g, sc, sh)

    (x1, h2), _ = rowcall(lambda *a: (res_norm(*a), ()), "norm2", [x, mixed], [gate1, P['norm2_gain'], scale2, shift2],
                          [(D, F32), (D, BF16)], [], TT)
    a_ff, hh_ff = matmul(h2, w1_g, name="ffn1_mm", b_slots=True, out_dtypes=(F32, BF16),
                         epi=lambda acc: (acc, jnp.square(jnp.maximum(acc, 0.0))))
    ffn = matmul(hh_ff, w2, name="ffn2_mm")

    def loss_fn(x1v, fv, tg, gate, fg):
        def f(x1_, f_, gate_, fg_):
            out = _rms(x1_ + gate_ * f_, fg_)
            err = out - tg
            return 0.5 * jnp.sum(jnp.sum(err * err, axis=1, keepdims=True), axis=0, keepdims=True) * (1.0 / D)
        lv, vjp = jax.vjp(f, x1v, fv, gate, fg)
        dx1, dff, dgate, dfg = vjp(jnp.ones((1, 1), F32))
        return (dx1, dff), (jnp.broadcast_to(lv, (SUBLANE, LANE)), dgate, dfg)

    (dx2, dffn), (loss_t, dgate2, dfgain) = rowcall(
        loss_fn, "loss", [x1, ffn, target], [gate2, fgain], [(D, F32), (D, BF16)], [(SUBLANE, LANE), (1, D), (1, D)], TT)
    loss = lax.psum(loss_t[0, 0], MESH_AXES)

    da = matmul(dffn, w2, name="dffn2_mm", tb=True, out_dtypes=(BF16,), extras=[(a_ff, 'mn')],
                epi=lambda acc, av: (acc * (2.0 * jnp.maximum(av, 0.0)),))
    g_w2 = matmul(hh_ff, dffn, name="gw2_mm", ta=True, out_dtypes=(BF16,))
    dh2 = matmul(da, w1_g, name="dh2_mm", tb=True, b_slots=True)
    g_w1 = matmul(h2, da, name="gw1_mm", ta=True, out_slots=N_DEV, out_dtypes=(BF16,))

    def res_norm_bwd(dx2v, dh2v, xv, mv, gate, g, sc, sh):
        _, vjp = jax.vjp(res_norm, xv, mv, gate, g, sc, sh)
        dx, dm, dgate, dg, dsc, dsh = vjp((dx2v, dh2v))
        return (dx, dm), (dgate, dg, dsc, dsh)

    (dx1, dmixed), (dgate1, dgain2, dscale2, dshift2) = rowcall(
        res_norm_bwd, "norm2_bwd", [dx2, dh2, x, mixed], [gate1, P['norm2_gain'], scale2, shift2],
        [(D, F32), (D, BF16)], [(1, D)] * 4, TT)

    dycat = matmul(dmixed, w_out, name="dycat_mm", tb=True)
    g_wout = matmul(ycat, dmixed, name="gwout_mm", ta=True, out_dtypes=(BF16,))

    def post_bwd(dy, *a):
        _, vjp = jax.vjp(_rk_post_fn, *a)
        gy0, gy1, gr, gv, gk0, gk1, gg, grk, glg, glb = vjp(dy)
        return (gy0, gr, gv, gk0, gk1, gg), (grk, glg, glb)

    cb_rk = S5W // C if C else 0
    (dy_rk, dr_p, dv_p, dk0_p, dk1_p, dg_p), (g_rk_rk, g_lng, g_lnb) = rowcall(
        post_bwd, "rk_post_bwd", [(dycat, C, cb_rk)] + post_tiled, post_full, [(C, F32)] * 6, [(1, C)] * 3, TS)
    dy_cols = rk_spread(hmT(dy_rk), "rk_spread_dy")
    scan_g = []
    for d in range(N_DIR):
        wd, kd, bd = dir_rows[d]
        scan_g.append(rk_scan_bwd(r_t, kkn, wd, kd, bd, v_cols, dy_cols, hist[d], reverse=(d == 1), name=f"rk_scan_b{d}"))
    cot = [dr_p, scan_g[0][0], scan_g[1][0],
           dv_p, unT(scan_g[0][5]), unT(scan_g[1][5]),
           scan_g[0][1], scan_g[1][1],
           scan_g[0][2], scan_g[1][2],
           dk0_p, scan_g[0][3], dk1_p, scan_g[1][3],
           scan_g[0][4], scan_g[1][4],
           dg_p]

    def pre_bwd(psv, r0, r1, r2, v0, v1, v2, q0, q1, dw0, dw1, k0a, k0b, k1a, k1b, db0, db1, dgv, *params):
        _, vjp = jax.vjp(_rk_pre_fn, psv, *params)
        grads = vjp((r0 + r1 + r2, v0 + v1 + v2, q0 + q1, dw0, dw1, k0a + k0b, k1a + k1b, db0, db1, dgv))
        return (grads[0],), tuple(grads[1:])

    (dps,), pre_g = rowcall(pre_bwd, "rk_pre_bwd", [ps] + cot, pre_full, [(RINP, F32)],
                            [f.shape for f in pre_full], TS)
    g_w0, g_wup0, g_wup1, g_a0, g_aup0, g_aup1, g_gup_p, g_kk, g_ka = pre_g
    g_wup_p, g_aup_p = jnp.stack([g_wup0, g_wup1]), jnp.stack([g_aup0, g_aup1])
    dp_rk, g_mup, g_mun = rk_shift_bwd(dps, proj, mu_prev, mu_next, S5W)

    def s5_out_bwd(dy, *a):
        a = [t.astype(F32) for t in a]
        _, vjp = jax.vjp(_s5_out_fn, *a)
        g = vjp(dy)
        return (g[0], g[1], g[4]), tuple(g[5:])

    (dxr, dxi, du_a), s5_pg = rowcall(
        s5_out_bwd, "s5_out_bwd", [(dycat, S5W, 0), s0r, s0i, s1r, s1i, u_view], s5_full,
        [(NS, F32), (NS, F32), (S5W, F32)], [f.shape for f in s5_full], TS)
    g_creb, g_cimb, g_s5d, g_wglu, g_bglu = s5_pg
    l0r, l0i = s5_scan(dxr, dxi, lbr[0:1], -lbi[0:1], reverse=True, name="s5_scan_b0")
    l1r, l1i = s5_scan(dxr, dxi, lbr[1:2], -lbi[1:2], reverse=False, name="s5_scan_b1")
    dl0r, dl0i = s5_dlam(s0r, s0i, l0r, l0i, reverse=False, name="s5_dlam0")
    dl1r, dl1i = s5_dlam(s1r, s1i, l1r, l1i, reverse=True, name="s5_dlam1")

    def bu_bwd(uv, g0, g1, g2, g3, wb):
        _, vjp = jax.vjp(_s5_bu_fn, uv, wb.astype(F32))
        du, dwb = vjp((g0, g1, g2, g3))
        return (du,), (dwb,)

    (du_b,), (g_wblk,) = rowcall(bu_bwd, "s5_bu_bwd", [u_view, l0r, l0i, l1r, l1i], [wblk], [(S5W, F32)],
                                 [wblk.shape], TS)
    g_bbar = jnp.einsum('cghdrkp,gk->drhcgp',
                        g_wblk.reshape(NCH, S5_CHUNK_GROUPS, S5_GROUP, N_DIR, 2, S5_CHUNK_GROUPS, S5_STATE), eye)
    g_bbar = g_bbar.reshape(N_DIR, 2, S5_GROUP, NS)
    g_lre, g_lim, g_lstep, g_bre, g_bim = s5_prep_bwd(
        lre, lim, lstep, bre, bim, jnp.concatenate([dl0r, dl1r], 0), jnp.concatenate([dl0i, dl1i], 0),
        g_bbar[:, 0], g_bbar[:, 1])

    def uncblk(gb):
        g5 = gb.reshape(NCH, S5_CHUNK_GROUPS, S5_STATE, S5_CHUNK_GROUPS, S5_GROUP)
        return jnp.einsum('cgpkh,gk->cghp', g5, eye).reshape(G, S5_GROUP, S5_STATE)

    (du_tot,), _ = rowcall(lambda a, b_: ((a + b_,), ()), "s5_du_sum", [du_a, du_b], [], [(S5W, BF16)], [], TT)
    dproj = jnp.concatenate([du_tot, dp_rk.astype(BF16)], axis=1)
    dh1 = matmul(dproj, w_in, name="dh1_mm", tb=True)
    g_win = matmul(h1, dproj, name="gwin_mm", ta=True, out_dtypes=(BF16,))

    def norm1_bwd(dx1v, dh1v, xv, g, sc, sh):
        _, vjp = jax.vjp(_normmod, xv, g, sc, sh)
        dx, dg, dsc, dsh = vjp(dh1v)
        return (dx1v + dx,), (dg, dsc, dsh)

    (grad_x,), (dgain1, dscale1, dshift1) = rowcall(
        norm1_bwd, "norm1_bwd", [dx1, dh1, x], [P['norm1_gain'], scale1, shift1], [(D, F32)], [(1, D)] * 3, TT)

    dmod = jnp.concatenate([dshift1, dscale1, dgate1, dshift2, dscale2, dgate2], axis=1)
    lstep_g = g_lstep.reshape(N_DIR, G, S5_STATE)
    small_g = {
        'ada_b': dmod, 'norm1_gain': dgain1, 'norm2_gain': dgain2, 'final_gain': dfgain.reshape(D),
        's5_lambda_re': g_lre.reshape(1, N_DIR, G, S5_STATE), 's5_lambda_im': g_lim.reshape(1, N_DIR, G, S5_STATE),
        's5_log_step': lstep_g,
        's5_b_re': g_bre.T.reshape(1, G, S5_STATE, S5_GROUP), 's5_b_im': g_bim.T.reshape(1, G, S5_STATE, S5_GROUP),
        's5_c_re': uncblk(g_creb)[None], 's5_c_im': uncblk(g_cimb)[None],
        's5_d': g_s5d, 's5_b_glu': g_bglu,
        'rk_shift_prev': g_mup[:, :RIN], 'rk_shift_next': g_mun[:, :RIN],
        'rk_k_k': g_kk, 'rk_k_a': g_ka, 'rk_r_k': g_rk_rk.reshape(1, H, RWKV_HEAD),
        'rk_ln_gain': g_lng, 'rk_ln_bias': g_lnb,
    }
    small_shapes = {n: P[n].shape for n in SMALL}
    small_shapes['s5_log_step'] = (N_DIR, G, S5_STATE)
    small_pack = _pack_flat([small_g[n] for n in SMALL])

    def lora_unpad(gp):
        return jnp.stack([gp[d, d * DECAY_LORA:(d + 1) * DECAY_LORA] for d in range(N_DIR)])

    rk_grads = {'rk_w0': g_w0, 'rk_a0': g_a0, 'rk_w_up': lora_unpad(g_wup_p), 'rk_a_up': lora_unpad(g_aup_p),
                'rk_g_up': g_gup_p[:GATE_LORA]}
    rk_gpack = jnp.stack([_pack_rows([_cols_to_slots(rk_grads[n])[j] for n in RKPACK]) for j in range(N_DEV)])
    g_win_s = _cols_to_slots(g_win[:, :PROJ])
    ex = exchange(
        [small_pack, g_win_s, g_wout.reshape(N_DEV, D // N_DEV, D), g_w1, g_w2.reshape(N_DEV, FF // N_DEV, D),
         g_wglu.reshape(N_DEV, S5W // N_DEV, S5W), rk_gpack],
        ['gather'] + ['scatter'] * 6, "comm_grads")
    small_all, win_parts, wout_parts, w1_parts, w2_parts, wglu_parts, rk_parts = ex

    res = {}

    def put(name, g, dl, m2, v2):
        shp = P[name].shape
        res[name] = tuple(t.reshape(shp) for t in (g, dl, m2, v2))

    def adam2d(name, parts):
        shp = P[name].shape
        r2 = (math.prod(shp[:-1]), shp[-1])
        put(name, *adam(parts.reshape((parts.shape[0],) + r2), P[name].reshape(r2), M[name].reshape(r2),
                        V[name].reshape(r2), "adam_" + name))

    adam2d('w_in', win_parts)
    adam2d('w_out', wout_parts)
    adam2d('ffn_w1', w1_parts)
    adam2d('ffn_w2', w2_parts)
    adam2d('s5_w_glu', wglu_parts)
    off = 0
    for n in SMALL:
        if n == 'ada_b':
            break
        off += _round_up(math.prod(small_shapes[n]), SUBLANE * LANE) // LANE
    nrow_b = N_MOD * D // LANE
    dmod_all = small_all[:, off:off + nrow_b].reshape(N_DEV, N_MOD * D)
    dmod_cols = lax.dynamic_slice(dmod_all, (0, me * cs_mod), (N_DEV, cs_mod))
    g_adaw = matmul(c_act, dmod_cols, name="gadaw_mm", ta=True, precise=True)
    adam2d('ada_w', g_adaw[None])
    small_w = dict(P)
    small_m, small_v = dict(M), dict(V)
    rk_res = adam(rk_parts, rk_pack, _pack_rows([M[n][0] for n in RKPACK]), _pack_rows([V[n][0] for n in RKPACK]),
                  "adam_rkpack")
    for name, parts4 in zip(RKPACK, zip(*[_unpack_rows(t, rk_shapes) for t in rk_res])):
        put(name, *parts4)
    return loss, grad_x, res, (small_all, small_shapes, small_w, small_m, small_v)


def _small_update(small_all, small_shapes, P, M, V, res):
    G = (D_MODEL // 2) // S5_GROUP
    names = [n for n in SMALL if n != 's5_log_step']
    shapes = [small_shapes[n] for n in SMALL]
    parts = _unpack_flat_batched(small_all, shapes)
    by = dict(zip(SMALL, parts))
    ls = by['s5_log_step']
    ls = ls.transpose(0, 3, 1, 2).reshape(N_DEV * S5_STATE, N_DIR * G)
    pk = lambda d: _pack_flat([d[n] for n in names])
    packs = jnp.stack([_pack_flat([by[n][j] for n in names]) for j in range(N_DEV)])
    out = adam(packs, pk(P), pk(M), pk(V), "adam_small")
    shp = [P[n].shape for n in names]
    for name, parts4 in zip(names, zip(*[_unpack_flat(t, shp) for t in out])):
        res[name] = parts4
    lsw = lambda d: jnp.pad(d['s5_log_step'].reshape(1, N_DIR * G), ((0, SUBLANE - 1), (0, 0)))
    ls_parts = jnp.pad(ls[:, None, :], ((0, 0), (0, SUBLANE - 1), (0, 0)))
    o = adam(ls_parts, lsw(P), lsw(M), lsw(V), "adam_log_step")
    res['s5_log_step'] = tuple(t[0:1].reshape(P['s5_log_step'].shape) for t in o)


def _unpack_flat_batched(packed, shapes):
    out, r0 = [], 0
    B = packed.shape[0]
    for s in shapes:
        n = math.prod(s)
        nr = _round_up(n, SUBLANE * LANE) // LANE
        out.append(packed[:, r0:r0 + nr].reshape(B, -1)[:, :n].reshape((B,) + tuple(s)))
        r0 += nr
    return out


def kernel(x, c, ada_w, ada_b, norm1_gain, norm2_gain, final_gain, w_in, w_out, s5_lambda_re, s5_lambda_im, s5_log_step, s5_b_re, s5_b_im, s5_c_re, s5_c_im, s5_d, s5_w_glu, s5_b_glu, rk_shift_prev, rk_shift_next, rk_w0, rk_w_up, rk_a0, rk_a_up, rk_g_up, rk_k_k, rk_k_a, rk_r_k, rk_ln_gain, rk_ln_bias, ffn_w1, ffn_w2, loss_target, m_ada_w, m_ada_b, m_norm1_gain, m_norm2_gain, m_final_gain, m_w_in, m_w_out, m_s5_lambda_re, m_s5_lambda_im, m_s5_log_step, m_s5_b_re, m_s5_b_im, m_s5_c_re, m_s5_c_im, m_s5_d, m_s5_w_glu, m_s5_b_glu, m_rk_shift_prev, m_rk_shift_next, m_rk_w0, m_rk_w_up, m_rk_a0, m_rk_a_up, m_rk_g_up, m_rk_k_k, m_rk_k_a, m_rk_r_k, m_rk_ln_gain, m_rk_ln_bias, m_ffn_w1, m_ffn_w2, v_ada_w, v_ada_b, v_norm1_gain, v_norm2_gain, v_final_gain, v_w_in, v_w_out, v_s5_lambda_re, v_s5_lambda_im, v_s5_log_step, v_s5_b_re, v_s5_b_im, v_s5_c_re, v_s5_c_im, v_s5_d, v_s5_w_glu, v_s5_b_glu, v_rk_shift_prev, v_rk_shift_next, v_rk_w0, v_rk_w_up, v_rk_a0, v_rk_a_up, v_rk_g_up, v_rk_k_k, v_rk_k_a, v_rk_r_k, v_rk_ln_gain, v_rk_ln_bias, v_ffn_w1, v_ffn_w2):
    given = dict(locals())
    P = {n: given[n] for n in ['x', 'c', 'loss_target'] + WEIGHTS}
    M = {n: given['m_' + n] for n in WEIGHTS}
    V = {n: given['v_' + n] for n in WEIGHTS}
    loss, grad_x, res, small = _step(P, M, V)
    small_all, small_shapes, _, _, _ = small
    _small_update(small_all, small_shapes, P, M, V, res)
    outs = [loss, grad_x[None]]
    for q in range(4):
        outs += [res[n][q] for n in WEIGHTS]
    return tuple(outs)
```

```python
import functools
import math

import jax
import jax.numpy as jnp
from jax import lax
from jax.experimental import pallas as pl
from jax.experimental.pallas import tpu as pltpu

F32 = jnp.float32
BF16 = jnp.bfloat16
HI = lax.Precision.HIGHEST
MXU_DTYPE = jnp.bfloat16

N_DEV = 8
MESH_AXES = ("x", "y", "c")
D_MODEL = 2048
SEQ = 2048
S5_GROUP = 16
S5_STATE = 64
RWKV_HEAD = 64
DECAY_LORA = 64
GATE_LORA = 160
N_DIR = 2
N_MOD = 6
NORM_EPS = 1e-6
GN_EPS = 64e-5
L2_EPS = 1e-12
ADAM_LR = 0.001
ADAM_B1 = 0.9
ADAM_B2 = 0.999
ADAM_EPS = 1e-08
ADAM_WD = 0.01
ADAM_STEP = 10
LANE = 128
SUBLANE = 8
S5_CHUNK_GROUPS = 8
S5_SCAN_ROWS = 256
VMEM_LIMIT = 56 * 1024 * 1024

WEIGHTS = ['ada_w', 'ada_b', 'norm1_gain', 'norm2_gain', 'final_gain', 'w_in', 'w_out', 's5_lambda_re',
           's5_lambda_im', 's5_log_step', 's5_b_re', 's5_b_im', 's5_c_re', 's5_c_im', 's5_d', 's5_w_glu',
           's5_b_glu', 'rk_shift_prev', 'rk_shift_next', 'rk_w0', 'rk_w_up', 'rk_a0', 'rk_a_up', 'rk_g_up',
           'rk_k_k', 'rk_k_a', 'rk_r_k', 'rk_ln_gain', 'rk_ln_bias', 'ffn_w1', 'ffn_w2']
SMALL = ['ada_b', 'norm1_gain', 'norm2_gain', 'final_gain', 's5_lambda_re', 's5_lambda_im', 's5_log_step',
         's5_b_re', 's5_b_im', 's5_c_re', 's5_c_im', 's5_d', 's5_b_glu', 'rk_shift_prev', 'rk_shift_next',
         'rk_k_k', 'rk_k_a', 'rk_r_k', 'rk_ln_gain', 'rk_ln_bias']
RKPACK = ['rk_w0', 'rk_a0', 'rk_w_up', 'rk_a_up', 'rk_g_up']


def _round_up(n, m):
    return (n + m - 1) // m * m


def _tile(dim, pref, unit=LANE):
    t = min(pref, dim) // unit * unit
    while t >= unit:
        if dim % t == 0:
            return t
        t -= unit
    return dim


def _cparams(sem=None):
    return pltpu.CompilerParams(dimension_semantics=sem, vmem_limit_bytes=VMEM_LIMIT)


def _full_spec(a):
    nd = a.ndim
    return pl.BlockSpec(a.shape, lambda *_: (0,) * nd)


def exchange(arrs, modes, name):
    n = len(arrs)
    out_shape = [jax.ShapeDtypeStruct((N_DEV,) + a.shape if m == 'gather' else a.shape, a.dtype)
                 for a, m in zip(arrs, modes)]

    def body(*refs):
        ins, outs = refs[:n], refs[n:2 * n]
        send_sems, recv_sems, local_sems = refs[2 * n:]
        x, y, c = (lax.axis_index(a) for a in MESH_AXES)
        me = 4 * x + 2 * y + c
        copies = []
        for i in range(n):
            gather = modes[i] == 'gather'
            mine = pltpu.make_async_copy(ins[i] if gather else ins[i].at[me], outs[i].at[me], local_sems.at[i])
            mine.start()
            copies.append(mine)
        remote = []
        for k in range(1, N_DEV):
            px = 1 - x if (k >> 2) & 1 else x
            py = 1 - y if (k >> 1) & 1 else y
            pc = 1 - c if k & 1 else c
            peer = 4 * px + 2 * py + pc
            for i in range(n):
                src = ins[i] if modes[i] == 'gather' else ins[i].at[peer]
                cp = pltpu.make_async_remote_copy(
                    src_ref=src, dst_ref=outs[i].at[me], send_sem=send_sems.at[i, k - 1],
                    recv_sem=recv_sems.at[i, k - 1], device_id=(px, py, pc), device_id_type=pl.DeviceIdType.MESH)
                cp.start()
                remote.append(cp)
        for cp in remote:
            cp.wait_recv()
        for cp in remote:
            cp.wait_send()
        for cp in copies:
            cp.wait()

    any_spec = pl.BlockSpec(memory_space=pl.ANY)
    return pl.pallas_call(
        body, name=name, out_shape=out_shape,
        in_specs=[any_spec] * n, out_specs=[any_spec] * n,
        scratch_shapes=[pltpu.SemaphoreType.DMA((n, N_DEV - 1)), pltpu.SemaphoreType.DMA((n, N_DEV - 1)),
                        pltpu.SemaphoreType.DMA((n,))],
        compiler_params=pltpu.CompilerParams(has_side_effects=True),
    )(*arrs)


def gather_two_level(arrs, name):
    n = len(arrs)
    out_shape = [jax.ShapeDtypeStruct((N_DEV,) + a.shape, a.dtype) for a in arrs]

    def body(*refs):
        ins, outs = refs[:n], refs[n:2 * n]
        send_sems, recv_sems, local_sems = refs[2 * n:]
        x, y, c = (lax.axis_index(a) for a in MESH_AXES)
        me, sibling = (x, y, c), (x, y, 1 - c)
        chips = [(1 - x, y), (x, 1 - y), (1 - x, 1 - y)]

        def slot(px, py, pc):
            return 4 * px + 2 * py + pc

        def copy(i, k, block, to, src=None):
            return pltpu.make_async_remote_copy(
                src_ref=outs[i].at[slot(*block)] if src is None else src, dst_ref=outs[i].at[slot(*block)],
                send_sem=send_sems.at[i, k], recv_sem=recv_sems.at[i, k], device_id=to,
                device_id_type=pl.DeviceIdType.MESH)

        mine = [pltpu.make_async_copy(ins[i], outs[i].at[slot(*me)], local_sems.at[i]) for i in range(n)]
        for cp in mine:
            cp.start()
        started = []
        for i in range(n):
            started.append(copy(i, 0, me, sibling, src=ins[i]))
            started += [copy(i, 1 + j, me, (*chip, c), src=ins[i]) for j, chip in enumerate(chips)]
        for cp in started:
            cp.start()
        for j, chip in enumerate(chips):
            for i in range(n):
                copy(i, 1 + j, (*chip, c), me).wait_recv()
                fwd = copy(i, 4 + j, (*chip, c), sibling)
                fwd.start()
                started.append(fwd)
        for i in range(n):
            copy(i, 0, sibling, me).wait_recv()
        for j, chip in enumerate(chips):
            for i in range(n):
                copy(i, 4 + j, (*chip, 1 - c), me).wait_recv()
        for cp in started:
            cp.wait_send()
        for cp in mine:
            cp.wait()

    any_spec = pl.BlockSpec(memory_space=pl.ANY)
    return pl.pallas_call(
        body, name=name, out_shape=out_shape,
        in_specs=[any_spec] * n, out_specs=[any_spec] * n,
        scratch_shapes=[pltpu.SemaphoreType.DMA((n, N_DEV - 1)), pltpu.SemaphoreType.DMA((n, N_DEV - 1)),
                        pltpu.SemaphoreType.DMA((n,))],
        compiler_params=pltpu.CompilerParams(has_side_effects=True),
    )(*arrs)


def swap_sibling(arrs, name):
    n = len(arrs)

    def body(*refs):
        ins, outs = refs[:n], refs[n:2 * n]
        send_sems, recv_sems = refs[2 * n:]
        x, y, c = (lax.axis_index(a) for a in MESH_AXES)
        cps = [pltpu.make_async_remote_copy(src_ref=ins[i], dst_ref=outs[i], send_sem=send_sems.at[i],
                                            recv_sem=recv_sems.at[i], device_id=(x, y, 1 - c),
                                            device_id_type=pl.DeviceIdType.MESH) for i in range(n)]
        for cp in cps:
            cp.start()
        for cp in cps:
            cp.wait()

    any_spec = pl.BlockSpec(memory_space=pl.ANY)
    return pl.pallas_call(
        body, name=name, out_shape=[jax.ShapeDtypeStruct(a.shape, a.dtype) for a in arrs],
        in_specs=[any_spec] * n, out_specs=[any_spec] * n,
        scratch_shapes=[pltpu.SemaphoreType.DMA((n,)), pltpu.SemaphoreType.DMA((n,))],
        compiler_params=pltpu.CompilerParams(has_side_effects=True),
    )(*arrs)


def exchange_chips(arrs, name):
    n = len(arrs)

    def body(*refs):
        ins, outs = refs[:n], refs[n:2 * n]
        send_sems, recv_sems, local_sems = refs[2 * n:]
        x, y, c = (lax.axis_index(a) for a in MESH_AXES)
        mine = 2 * x + y
        chips = [(1 - x, y), (x, 1 - y), (1 - x, 1 - y)]
        local = [pltpu.make_async_copy(ins[i].at[mine], outs[i].at[mine], local_sems.at[i]) for i in range(n)]
        for cp in local:
            cp.start()
        remote = []
        for j, (px, py) in enumerate(chips):
            for i in range(n):
                cp = pltpu.make_async_remote_copy(
                    src_ref=ins[i].at[2 * px + py], dst_ref=outs[i].at[mine], send_sem=send_sems.at[i, j],
                    recv_sem=recv_sems.at[i, j], device_id=(px, py, c), device_id_type=pl.DeviceIdType.MESH)
                cp.start()
                remote.append(cp)
        for cp in remote:
            cp.wait_recv()
        for cp in remote:
            cp.wait_send()
        for cp in local:
            cp.wait()

    any_spec = pl.BlockSpec(memory_space=pl.ANY)
    return pl.pallas_call(
        body, name=name, out_shape=[jax.ShapeDtypeStruct(a.shape, a.dtype) for a in arrs],
        in_specs=[any_spec] * n, out_specs=[any_spec] * n,
        scratch_shapes=[pltpu.SemaphoreType.DMA((n, 3)), pltpu.SemaphoreType.DMA((n, 3)), pltpu.SemaphoreType.DMA((n,))],
        compiler_params=pltpu.CompilerParams(has_side_effects=True),
    )(*arrs)


def matmul(a, b, *, name, ta=False, tb=False, b_slots=False, out_slots=0, out_dtypes=(F32,), epi=None,
           extras=(), precise=False, tm=512, tn=512, tk=2048):
    if ta:
        K, M = a.shape
    else:
        M, K = a.shape
    if b_slots:
        ns, br, bc = b.shape
        bshape = (br, ns * bc)
    else:
        bshape = b.shape
    N = bshape[0] if tb else bshape[1]
    assert (bshape[1] if tb else bshape[0]) == K, (a.shape, b.shape, ta, tb)
    tm, tn, tk = _tile(M, tm, SUBLANE), _tile(N, tn), _tile(K, tk, SUBLANE if K < LANE else LANE)
    if b_slots and tb:
        tk = _tile(b.shape[2], tk)
    elif b_slots:
        tn = _tile(b.shape[2], tn)
    if out_slots:
        tn = _tile(N // out_slots, tn)
    if b_slots:
        cs = b.shape[2]
        tcol = tk if tb else tn
        assert cs % tcol == 0
        per = cs // tcol
    if out_slots:
        ncs = N // out_slots
        assert ncs % tn == 0
        operc = ncs // tn
    nk = K // tk
    a_spec = pl.BlockSpec((tk, tm), lambda i, j, k: (k, i)) if ta else pl.BlockSpec((tm, tk), lambda i, j, k: (i, k))
    if b_slots:
        if tb:
            b_spec = pl.BlockSpec((None, tn, tk), lambda i, j, k: (k // per, j, k % per))
        else:
            b_spec = pl.BlockSpec((None, tk, tn), lambda i, j, k: (j // per, k, j % per))
    else:
        b_spec = pl.BlockSpec((tn, tk), lambda i, j, k: (j, k)) if tb else pl.BlockSpec((tk, tn), lambda i, j, k: (k, j))
    ex_specs = []
    for arr, kind in extras:
        if kind == 'mn':
            ex_specs.append(pl.BlockSpec((tm, tn), lambda i, j, k: (i, j)))
        else:
            ex_specs.append(pl.BlockSpec((1, tn), lambda i, j, k: (0, j)))
    if out_slots:
        o_spec = pl.BlockSpec((None, tm, tn), lambda i, j, k: (j // operc, i, j % operc))
        o_shape = (out_slots, M, ncs)
    else:
        o_spec = pl.BlockSpec((tm, tn), lambda i, j, k: (i, j))
        o_shape = (M, N)
    ne, no = len(extras), len(out_dtypes)
    dims = (((0 if ta else 1,), (1 if tb else 0,)), ((), ()))
    op_dtype = F32 if precise else MXU_DTYPE

    def body(a_ref, b_ref, *rest):
        ex_refs, out_refs, acc = rest[:ne], rest[ne:ne + no], rest[-1]
        k = pl.program_id(2)
        part = lax.dot_general(a_ref[...].astype(op_dtype), b_ref[...].astype(op_dtype), dims,
                               precision=HI if precise else None, preferred_element_type=F32)

        def finish(total):
            res = epi(total, *[e[...] for e in ex_refs]) if epi is not None else (total,)
            for o, r in zip(out_refs, res):
                o[...] = r.astype(o.dtype)

        if nk == 1:
            finish(part)
        else:
            @pl.when(k == 0)
            def _():
                acc[...] = part

            @pl.when(jnp.logical_and(k > 0, k < nk - 1))
            def _():
                acc[...] += part

            @pl.when(k == nk - 1)
            def _():
                finish(acc[...] + part)

    outs = pl.pallas_call(
        body, name=name, grid=(M // tm, N // tn, nk),
        in_specs=[a_spec, b_spec] + ex_specs, out_specs=[o_spec] * no,
        out_shape=[jax.ShapeDtypeStruct(o_shape, dt) for dt in out_dtypes],
        scratch_shapes=[pltpu.VMEM((tm, tn), F32)],
        compiler_params=_cparams(("parallel", "parallel", "arbitrary")),
    )(a, b, *[e[0] for e in extras])
    return outs[0] if no == 1 else outs


def rowcall(fn, name, tiled, full, tiled_out, acc_out, tt):
    views = [(t, t.shape[1], 0) if not isinstance(t, tuple) else t for t in tiled]
    T = views[0][0].shape[0]
    tt = _tile(T, tt, SUBLANE)
    nt, nf, nto, nao = len(views), len(full), len(tiled_out), len(acc_out)

    def view_spec(w, cb):
        return pl.BlockSpec((tt, w), lambda i: (i, cb))

    in_specs = [view_spec(w, cb) for _, w, cb in views] + [_full_spec(f) for f in full]
    out_specs = [pl.BlockSpec((tt, w), lambda i: (i, 0)) for w, _ in tiled_out]
    out_specs += [pl.BlockSpec(s, lambda i, nd=len(s): (0,) * nd) for s in acc_out]
    out_shape = [jax.ShapeDtypeStruct((T, w), dt) for w, dt in tiled_out]
    out_shape += [jax.ShapeDtypeStruct(s, F32) for s in acc_out]

    def body(*refs):
        tin, fin = refs[:nt], refs[nt:nt + nf]
        tout, aout = refs[nt + nf:nt + nf + nto], refs[nt + nf + nto:]
        touts, aouts = fn(*[r[...] for r in tin], *[r[...] for r in fin])
        for r, v in zip(tout, touts):
            r[...] = v.astype(r.dtype)
        if nao:
            @pl.when(pl.program_id(0) == 0)
            def _():
                for r in aout:
                    r[...] = jnp.zeros_like(r)

            for r, v in zip(aout, aouts):
                r[...] += v.astype(F32)

    outs = pl.pallas_call(
        body, name=name, grid=(T // tt,), in_specs=in_specs, out_specs=out_specs, out_shape=out_shape,
        compiler_params=_cparams(("arbitrary",) if nao else ("parallel",)),
    )(*[v[0] for v in views], *full)
    return outs[:nto], outs[nto:]


def _mm(a, b):
    return jnp.dot(a.astype(MXU_DTYPE), b.astype(MXU_DTYPE), preferred_element_type=F32)


def _rms(x, gain):
    ms = jnp.mean(x * x, axis=-1, keepdims=True)
    return x * lax.rsqrt(ms + NORM_EPS) * gain


def _normmod(x, gain, scale, shift):
    return _rms(x, gain) * (1.0 + scale) + shift


def _gelu_tanh(y):
    return 0.5 * y * (1.0 + jnp.tanh(math.sqrt(2.0 / math.pi) * (y + 0.044715 * (y * y * y))))


def _sigmoid(x):
    return 1.0 / (1.0 + jnp.exp(-x))


def _softplus(x):
    return jnp.maximum(x, 0.0) + jnp.log(1.0 + jnp.exp(-jnp.abs(x)))


def _seg_mats(width, seg):
    r = lax.broadcasted_iota(jnp.int32, (width, LANE), 0) // seg
    c = lax.broadcasted_iota(jnp.int32, (width, LANE), 1)
    s = (r == c).astype(F32)
    rt = lax.broadcasted_iota(jnp.int32, (LANE, width), 0)
    ct = lax.broadcasted_iota(jnp.int32, (LANE, width), 1) // seg
    st = (rt == ct).astype(F32)
    return s, st


def _segsum(x, s):
    return jnp.dot(x, s, precision=HI, preferred_element_type=F32)


def _s5_prep_fn(lre, lim, lstep, bre, bim):
    step = jnp.exp(lstep)
    mag = jnp.exp(lre * step)
    lbr = mag * jnp.cos(lim * step)
    lbi = mag * jnp.sin(lim * step)
    den = lre * lre + lim * lim
    nr = lbr - 1.0
    ni = lbi
    cre = (nr * lre + ni * lim) / den
    cim = (ni * lre - nr * lim) / den
    bbr = jnp.stack([cre[d:d + 1] * bre - cim[d:d + 1] * bim for d in range(N_DIR)])
    bbi = jnp.stack([cre[d:d + 1] * bim + cim[d:d + 1] * bre for d in range(N_DIR)])
    return lbr, lbi, bbr, bbi


def s5_prep(lre, lim, lstep, bre, bim):
    ns = lre.shape[1]

    def body(lre_r, lim_r, ls_r, bre_r, bim_r, lbr_r, lbi_r, bbr_r, bbi_r):
        lbr, lbi, bbr, bbi = _s5_prep_fn(lre_r[...], lim_r[...], ls_r[...], bre_r[...], bim_r[...])
        lbr_r[...] = lbr
        lbi_r[...] = lbi
        bbr_r[...] = bbr
        bbi_r[...] = bbi

    return pl.pallas_call(
        body, name="s5_prep",
        out_shape=[jax.ShapeDtypeStruct((N_DIR, ns), F32)] * 2 + [jax.ShapeDtypeStruct((N_DIR, S5_GROUP, ns), F32)] * 2,
        compiler_params=_cparams(),
    )(lre, lim, lstep, bre, bim)


def s5_prep_bwd(lre, lim, lstep, bre, bim, dlbr, dlbi, dbbr, dbbi):
    ns = lre.shape[1]

    def body(lre_r, lim_r, ls_r, bre_r, bim_r, d1, d2, d3, d4, o1, o2, o3, o4, o5):
        _, vjp = jax.vjp(_s5_prep_fn, lre_r[...], lim_r[...], ls_r[...], bre_r[...], bim_r[...])
        g = vjp((d1[...], d2[...], d3[...], d4[...]))
        for o, v in zip((o1, o2, o3, o4, o5), g):
            o[...] = v

    return pl.pallas_call(
        body, name="s5_prep_bwd",
        out_shape=[jax.ShapeDtypeStruct((N_DIR, ns), F32)] * 3 + [jax.ShapeDtypeStruct((S5_GROUP, ns), F32)] * 2,
        compiler_params=_cparams(),
    )(lre, lim, lstep, bre, bim, dlbr, dlbi, dbbr, dbbi)


def s5_scan(bre, bim, lre, lim, *, reverse, name):
    T, NS = bre.shape
    tt = _tile(T, S5_SCAN_ROWS, SUBLANE)
    wl = _tile(NS, 512)
    nT = T // tt
    ngrp = tt // SUBLANE

    def tmap(j, i):
        return ((nT - 1 - i) if reverse else i, j)

    def body(bre_r, bim_r, lre_r, lim_r, sre_r, sim_r, cre, cim):
        @pl.when(pl.program_id(1) == 0)
        def _():
            cre[...] = jnp.zeros_like(cre)
            cim[...] = jnp.zeros_like(cim)

        lr = jnp.broadcast_to(lre_r[...], (SUBLANE, wl))
        li = jnp.broadcast_to(lim_r[...], (SUBLANE, wl))
        row = lax.broadcasted_iota(jnp.int32, (SUBLANE, wl), 0)
        pows = [(lr, li)]
        for _ in range(3):
            pr, pi = pows[-1]
            pows.append((pr * pr - pi * pi, 2.0 * pr * pi))
        e = (SUBLANE - row) if reverse else (row + 1)
        Pr = jnp.ones((SUBLANE, wl), F32)
        Pi = jnp.zeros((SUBLANE, wl), F32)
        for bit, (qr, qi) in enumerate(pows):
            on = ((e >> bit) & 1) == 1
            nr, ni = Pr * qr - Pi * qi, Pr * qi + Pi * qr
            Pr, Pi = jnp.where(on, nr, Pr), jnp.where(on, ni, Pi)

        def group(g, carry):
            gg = (ngrp - 1 - g) if reverse else g
            rows = pl.ds(pl.multiple_of(gg * SUBLANE, SUBLANE), SUBLANE)
            sr, si = bre_r[rows, :], bim_r[rows, :]
            for lvl, k in enumerate((1, 2, 4)):
                qr, qi = pows[lvl]
                if reverse:
                    shr = pltpu.roll(sr, SUBLANE - k, 0)
                    shi = pltpu.roll(si, SUBLANE - k, 0)
                    keep = row < SUBLANE - k
                else:
                    shr = pltpu.roll(sr, k, 0)
                    shi = pltpu.roll(si, k, 0)
                    keep = row >= k
                shr = jnp.where(keep, shr, 0.0)
                shi = jnp.where(keep, shi, 0.0)
                sr, si = sr + qr * shr - qi * shi, si + qr * shi + qi * shr
            cr, ci = cre[...], cim[...]
            sr, si = sr + Pr * cr - Pi * ci, si + Pr * ci + Pi * cr
            sre_r[rows, :] = sr
            sim_r[rows, :] = si
            last = 0 if reverse else SUBLANE - 1
            cre[...] = jnp.broadcast_to(sr[last:last + 1, :], (SUBLANE, wl))
            cim[...] = jnp.broadcast_to(si[last:last + 1, :], (SUBLANE, wl))
            return carry

        lax.fori_loop(0, ngrp, group, 0)

    blk = pl.BlockSpec((tt, wl), tmap)
    row_spec = pl.BlockSpec((1, wl), lambda j, i: (0, j))
    return pl.pallas_call(
        body, name=name, grid=(NS // wl, nT),
        in_specs=[blk, blk, row_spec, row_spec], out_specs=[blk, blk],
        out_shape=[jax.ShapeDtypeStruct((T, NS), F32)] * 2,
        scratch_shapes=[pltpu.VMEM((SUBLANE, wl), F32)] * 2,
        compiler_params=_cparams(("parallel", "arbitrary")),
    )(bre, bim, lre, lim)


def s5_dlam(sre, sim, gre, gim, *, reverse, name):
    T, NS = sre.shape
    wl = _tile(NS, 256)

    def body(sr_r, si_r, gr_r, gi_r, dr_r, di_r):
        row = lax.broadcasted_iota(jnp.int32, (T, wl), 0)
        if reverse:
            keep = row < T - 1
            pr = jnp.where(keep, pltpu.roll(sr_r[...], T - 1, 0), 0.0)
            pi = jnp.where(keep, pltpu.roll(si_r[...], T - 1, 0), 0.0)
        else:
            keep = row >= 1
            pr = jnp.where(keep, pltpu.roll(sr_r[...], 1, 0), 0.0)
            pi = jnp.where(keep, pltpu.roll(si_r[...], 1, 0), 0.0)
        gr, gi = gr_r[...], gi_r[...]
        dr_r[...] = jnp.sum(pr * gr + pi * gi, axis=0, keepdims=True)
        di_r[...] = jnp.sum(pr * gi - pi * gr, axis=0, keepdims=True)

    blk = pl.BlockSpec((T, wl), lambda j: (0, j))
    o = pl.BlockSpec((1, wl), lambda j: (0, j))
    return pl.pallas_call(
        body, name=name, grid=(NS // wl,), in_specs=[blk] * 4, out_specs=[o, o],
        out_shape=[jax.ShapeDtypeStruct((1, NS), F32)] * 2,
        compiler_params=_cparams(("parallel",)),
    )(sre, sim, gre, gim)


def _s5_bu_fn(u, wblk):
    nch = wblk.shape[0]
    cw = S5_CHUNK_GROUPS * S5_GROUP
    sw = S5_CHUNK_GROUPS * S5_STATE
    parts = [[] for _ in range(4)]
    for ch in range(nch):
        res = _mm(u[:, ch * cw:(ch + 1) * cw], wblk[ch])
        for q in range(4):
            parts[q].append(res[:, q * sw:(q + 1) * sw])
    return tuple(jnp.concatenate(p, axis=1) if nch > 1 else p[0] for p in parts)


def _s5_out_fn(x0r, x0i, x1r, x1i, u, cre, cim, dsk, wglu, bglu):
    xr, xi = x0r + x1r, x0i + x1i
    nch = cre.shape[0]
    sw = S5_CHUNK_GROUPS * S5_STATE
    ys = [_mm(xr[:, ch * sw:(ch + 1) * sw], cre[ch]) - _mm(xi[:, ch * sw:(ch + 1) * sw], cim[ch]) for ch in range(nch)]
    y = jnp.concatenate(ys, axis=1) if nch > 1 else ys[0]
    z = _gelu_tanh(y + dsk * u)
    gate = _sigmoid(_mm(z, wglu) + bglu)
    return z * gate


def _rk_dims():
    C = D_MODEL // 2
    LW = N_DIR * DECAY_LORA
    GP = _round_up(GATE_LORA, LANE)
    return C, LW, GP


def _rk_pre_fn(ps, w0, wup0, wup1, a0, aup0, aup1, gup, k_k, k_a):
    C, LW, GP = _rk_dims()
    r, k, v = ps[:, 0:C], ps[:, C:2 * C], ps[:, 2 * C:3 * C]
    wdn = ps[:, 3 * C:3 * C + LW]
    adn = ps[:, 3 * C + LW:3 * C + 2 * LW]
    gdn = ps[:, 3 * C + 2 * LW:3 * C + 2 * LW + GP]
    s, st = _seg_mats(C, RWKV_HEAD)
    kk = k * k_k
    n2 = _segsum(kk * kk, s)
    n2 = jnp.where(n2 > 0.0, n2, 1.0)
    inv = 1.0 / jnp.maximum(jnp.sqrt(n2), L2_EPS)
    kkn = kk * _segsum(inv, st)
    tw = jnp.tanh(wdn)
    wup, aup = (wup0, wup1), (aup0, aup1)
    ws, ks, bs = [], [], []
    for d in range(N_DIR):
        wraw = w0[d:d + 1] + _mm(tw, wup[d])
        w = -_softplus(-wraw) - 0.5
        ws.append(jnp.exp(-jnp.exp(w)))
        a = _sigmoid(a0[d:d + 1] + _mm(adn, aup[d]))
        ks.append(k * (1.0 + (a - 1.0) * k_a))
        bs.append(kkn * a)
    g = _mm(_sigmoid(gdn), gup)
    return r, v, kkn, ws[0], ws[1], ks[0], ks[1], bs[0], bs[1], g


def _rk_post_fn(y0, y1, r, v, k0, k1, g, r_k, lng, lnb):
    C = r.shape[1]
    s, st = _seg_mats(C, RWKV_HEAD)
    y = y0 + y1
    mu = _segsum(_segsum(y, s) * (1.0 / RWKV_HEAD), st)
    yc = y - mu
    var = _segsum(_segsum(yc * yc, s) * (1.0 / RWKV_HEAD), st)
    yn = yc * lax.rsqrt(var + GN_EPS) * lng + lnb
    bonus = _segsum(_segsum(r * (k0 + k1) * r_k, s), st)
    return (yn + bonus * v) * g


def rk_shift(proj, mp, mn, col0):
    T = proj.shape[0]
    W = mp.shape[1]
    wl = _tile(math.gcd(W, col0), 256)
    cb0 = col0 // wl

    def body(p_r, mp_r, mn_r, o_r):
        p = p_r[...]
        row = lax.broadcasted_iota(jnp.int32, (T, wl), 0)
        prev = jnp.where(row >= 1, pltpu.roll(p, 1, 0), 0.0)
        nxt = jnp.where(row < T - 1, pltpu.roll(p, T - 1, 0), 0.0)
        o_r[...] = p + mp_r[...] * (prev - p) + mn_r[...] * (nxt - p)

    rs = pl.BlockSpec((1, wl), lambda j: (0, j))
    return pl.pallas_call(
        body, name="rk_shift", grid=(W // wl,),
        in_specs=[pl.BlockSpec((T, wl), lambda j: (0, cb0 + j)), rs, rs],
        out_specs=pl.BlockSpec((T, wl), lambda j: (0, j)),
        out_shape=jax.ShapeDtypeStruct((T, W), F32),
        compiler_params=_cparams(("parallel",)),
    )(proj, mp, mn)


def rk_shift_bwd(dps, proj, mp, mn, col0):
    T, W = dps.shape
    wl = _tile(math.gcd(W, col0), 256)
    cb0 = col0 // wl

    def body(d_r, p_r, mp_r, mn_r, dp_r, dmp_r, dmn_r):
        d, p = d_r[...], p_r[...]
        mpv, mnv = mp_r[...], mn_r[...]
        row = lax.broadcasted_iota(jnp.int32, (T, wl), 0)
        first, last = row >= 1, row < T - 1
        prev = jnp.where(first, pltpu.roll(p, 1, 0), 0.0)
        nxt = jnp.where(last, pltpu.roll(p, T - 1, 0), 0.0)
        dmp_r[...] = jnp.sum(d * (prev - p), axis=0, keepdims=True)
        dmn_r[...] = jnp.sum(d * (nxt - p), axis=0, keepdims=True)
        dp_r[...] = (d * (1.0 - mpv - mnv) + jnp.where(last, pltpu.roll(d * mpv, T - 1, 0), 0.0)
                     + jnp.where(first, pltpu.roll(d * mnv, 1, 0), 0.0))

    rs = pl.BlockSpec((1, wl), lambda j: (0, j))
    blk = pl.BlockSpec((T, wl), lambda j: (0, j))
    return pl.pallas_call(
        body, name="rk_shift_bwd", grid=(W // wl,),
        in_specs=[blk, pl.BlockSpec((T, wl), lambda j: (0, cb0 + j)), rs, rs],
        out_specs=[blk, rs, rs],
        out_shape=[jax.ShapeDtypeStruct((T, W), F32), jax.ShapeDtypeStruct((1, W), F32), jax.ShapeDtypeStruct((1, W), F32)],
        compiler_params=_cparams(("parallel",)),
    )(dps, proj, mp, mn)


RK_FWD_PAIRS = 4
RK_BWD_PAIRS = 2
RK_SPREAD_PAIRS = 2
RK_TIME_BLOCK = 32
RK_LANE_BLOCK = 128


def _rk_blocks(T, C, N, order_reversed, pairs):
    pw = 2 * N
    pp = min(pairs, C // pw)
    tb = min(RK_TIME_BLOCK, T)
    lb = min(RK_LANE_BLOCK, T)
    nb, per = T // tb, lb // tb

    def tix(i):
        return (nb - 1 - i) if order_reversed else i

    rows = pl.BlockSpec((tb, pp * pw), lambda g, i: (tix(i), g))
    cols = pl.BlockSpec((2 * pp, N, lb), lambda g, i: (g, 0, tix(i) // per))
    hist = pl.BlockSpec((tb, pp, N, pw), lambda g, i: (tix(i), g, 0, 0))
    return pp, pw, tb, lb, nb, per, tix, rows, cols, hist


def rk_spread(xT, name):
    H, N, T = xT.shape
    pw = 2 * N
    lb = min(RK_LANE_BLOCK, T)
    pp = min(RK_SPREAD_PAIRS, H // 2)

    def body(x_r, o_r):
        lane = lax.broadcasted_iota(jnp.int32, (N, lb), 1)
        first = lax.broadcasted_iota(jnp.int32, (N, pw), 1) < N
        tiles = [x_r[h] for h in range(2 * pp)]

        def step(t, carry):
            for p in range(pp):
                c = [jnp.sum(jnp.where(lane == t, tiles[2 * p + q], 0.0), axis=1, keepdims=True) for q in range(2)]
                o_r[t, p] = jnp.where(first, c[0], c[1])
            return carry

        lax.fori_loop(0, lb, step, 0, unroll=8)

    return pl.pallas_call(
        body, name=name, grid=(H // (2 * pp), T // lb),
        in_specs=[pl.BlockSpec((2 * pp, N, lb), lambda g, i: (g, 0, i))],
        out_specs=pl.BlockSpec((lb, pp, N, pw), lambda g, i: (i, g, 0, 0)),
        out_shape=jax.ShapeDtypeStruct((T, H // 2, N, pw), F32),
        compiler_params=_cparams(("parallel", "parallel")),
    )(xT)


def _half_sums(x, first):
    return (jnp.sum(jnp.where(first, x, 0.0), axis=1, keepdims=True),
            jnp.sum(jnp.where(first, 0.0, x), axis=1, keepdims=True))


def rk_scan(r, kk, w, k, b, vc, *, reverse, name):
    T, C = r.shape
    N = vc.shape[2]
    H = C // N
    pp, pw, tb, lb, nb, per, tix, rows, cols, hist_spec = _rk_blocks(T, C, N, reverse, RK_FWD_PAIRS)

    def body(r_r, kk_r, w_r, k_r, b_r, VC, yT_r, hist_r, S, YA):
        i = pl.program_id(1)

        @pl.when(i == 0)
        def _():
            S[...] = jnp.zeros_like(S)

        @pl.when(i % per == 0)
        def _():
            yT_r[...] = jnp.zeros_like(yT_r)

        off = (tix(i) % per) * tb
        first = lax.broadcasted_iota(jnp.int32, (N, pw), 1) < N
        YA[...] = jnp.zeros_like(YA)
        st = [S[p] for p in range(pp)]
        seg = [slice(p * pw, (p + 1) * pw) for p in range(pp)]
        for s in range(tb):
            t = (tb - 1 - s) if reverse else s
            row = slice(t, t + 1)
            sk = [_half_sums(st[p] * kk_r[row, seg[p]], first) for p in range(pp)]
            for p in range(pp):
                hist_r[t, p] = st[p]
                skp = jnp.where(first, sk[p][0], sk[p][1])
                st[p] = st[p] * w_r[row, seg[p]] - skp * b_r[row, seg[p]] + VC[t, p] * k_r[row, seg[p]]
            ys = [_half_sums(st[p] * r_r[row, seg[p]], first) for p in range(pp)]
            for p in range(pp):
                for q in range(2):
                    YA[2 * p + q, :, row] = ys[p][q]
        for p in range(pp):
            S[p] = st[p]
        for h in range(2 * pp):
            yT_r[h] = yT_r[h] + (pltpu.roll(YA[h], off, 1) if per > 1 else YA[h])

    return pl.pallas_call(
        body, name=name, grid=(C // (pp * pw), nb),
        in_specs=[rows] * 5 + [hist_spec], out_specs=[cols, hist_spec],
        out_shape=[jax.ShapeDtypeStruct((H, N, T), F32), jax.ShapeDtypeStruct((T, H // 2, N, pw), F32)],
        scratch_shapes=[pltpu.VMEM((pp, N, pw), F32), pltpu.VMEM((2 * pp, N, lb), F32)],
        compiler_params=_cparams(("parallel", "arbitrary")),
    )(r, kk, w, k, b, vc)


def rk_scan_bwd(r, kk, w, k, b, vc, dc, hist, *, reverse, name):
    T, C = r.shape
    N = vc.shape[2]
    H = C // N
    pp, pw, tb, lb, nb, per, tix, rows, cols, hist_spec = _rk_blocks(T, C, N, not reverse, RK_BWD_PAIRS)

    def body(r_r, kk_r, w_r, k_r, b_r, VC, DC, hist_r, dr_r, dkk_r, dw_r, dk_r, db_r, dvT_r, G, YA):
        i = pl.program_id(1)

        @pl.when(i == 0)
        def _():
            G[...] = jnp.zeros_like(G)

        @pl.when(i % per == 0)
        def _():
            dvT_r[...] = jnp.zeros_like(dvT_r)

        off = (tix(i) % per) * tb
        first = lax.broadcasted_iota(jnp.int32, (N, pw), 1) < N
        YA[...] = jnp.zeros_like(YA)
        gs = [G[p] for p in range(pp)]
        seg = [slice(p * pw, (p + 1) * pw) for p in range(pp)]
        for s in range(tb):
            t = s if reverse else (tb - 1 - s)
            row = slice(t, t + 1)
            g = [gs[p] + DC[t, p] * r_r[row, seg[p]] for p in range(pp)]
            sk = [_half_sums(hist_r[t, p] * kk_r[row, seg[p]], first) for p in range(pp)]
            gb = [_half_sums(g[p] * b_r[row, seg[p]], first) for p in range(pp)]
            gk = [_half_sums(g[p] * k_r[row, seg[p]], first) for p in range(pp)]
            for p in range(pp):
                sp = hist_r[t, p]
                kkv, wv, kv, bv = kk_r[row, seg[p]], w_r[row, seg[p]], k_r[row, seg[p]], b_r[row, seg[p]]
                vcol, dycol = VC[t, p], DC[t, p]
                sa = -jnp.where(first, sk[p][0], sk[p][1])
                dsa = jnp.where(first, gb[p][0], gb[p][1])
                sn = sp * wv + sa * bv + vcol * kv
                dr_r[row, seg[p]] = jnp.sum(sn * dycol, axis=0, keepdims=True)
                dw_r[row, seg[p]] = jnp.sum(g[p] * sp, axis=0, keepdims=True)
                db_r[row, seg[p]] = jnp.sum(g[p] * sa, axis=0, keepdims=True)
                dk_r[row, seg[p]] = jnp.sum(g[p] * vcol, axis=0, keepdims=True)
                dkk_r[row, seg[p]] = -jnp.sum(sp * dsa, axis=0, keepdims=True)
                gs[p] = g[p] * wv - dsa * kkv
                for q in range(2):
                    YA[2 * p + q, :, row] = gk[p][q]
        for p in range(pp):
            G[p] = gs[p]
        for h in range(2 * pp):
            dvT_r[h] = dvT_r[h] + (pltpu.roll(YA[h], off, 1) if per > 1 else YA[h])

    return pl.pallas_call(
        body, name=name, grid=(C // (pp * pw), nb),
        in_specs=[rows] * 5 + [hist_spec] * 3, out_specs=[rows] * 5 + [cols],
        out_shape=[jax.ShapeDtypeStruct((T, C), F32)] * 5 + [jax.ShapeDtypeStruct((H, N, T), F32)],
        scratch_shapes=[pltpu.VMEM((pp, N, pw), F32), pltpu.VMEM((2 * pp, N, lb), F32)],
        compiler_params=_cparams(("parallel", "arbitrary")),
    )(r, kk, w, k, b, vc, dc, hist)


def adam(parts, w, m, v, name):
    P, R, C = parts.shape
    tr = _tile(R, max(SUBLANE, (1 << 19) // max(C, 1) // SUBLANE * SUBLANE), SUBLANE)
    c1 = 1.0 / (1.0 - ADAM_B1 ** ADAM_STEP)
    c2 = 1.0 / (1.0 - ADAM_B2 ** ADAM_STEP)

    def body(p_r, w_r, m_r, v_r, g_o, d_o, m_o, v_o):
        g = p_r[0].astype(F32)
        for q in range(1, P):
            g = g + p_r[q].astype(F32)
        m2 = ADAM_B1 * m_r[...] + (1.0 - ADAM_B1) * g
        v2 = ADAM_B2 * v_r[...] + (1.0 - ADAM_B2) * (g * g)
        g_o[...] = g
        m_o[...] = m2
        v_o[...] = v2
        d_o[...] = -ADAM_LR * ((m2 * c1) / (jnp.sqrt(v2 * c2) + ADAM_EPS) + ADAM_WD * w_r[...])

    blk = pl.BlockSpec((tr, C), lambda i: (i, 0))
    return pl.pallas_call(
        body, name=name, grid=(R // tr,),
        in_specs=[pl.BlockSpec((P, tr, C), lambda i: (0, i, 0)), blk, blk, blk], out_specs=[blk] * 4,
        out_shape=[jax.ShapeDtypeStruct((R, C), F32)] * 4,
        compiler_params=_cparams(("parallel",)),
    )(parts, w, m, v)


def _pack_flat(arrs):
    rows = []
    for a in arrs:
        f = a.reshape(-1).astype(F32)
        n = _round_up(f.shape[0], SUBLANE * LANE)
        rows.append(jnp.pad(f, (0, n - f.shape[0])).reshape(-1, LANE))
    return jnp.concatenate(rows, axis=0)


def _unpack_flat(packed, shapes):
    out, r0 = [], 0
    for s in shapes:
        n = math.prod(s)
        nr = _round_up(n, SUBLANE * LANE) // LANE
        out.append(packed[r0:r0 + nr].reshape(-1)[:n].reshape(s))
        r0 += nr
    return out


def _pack_rows(arrs):
    rows = []
    for a in arrs:
        f = a.reshape(-1, a.shape[-1]).astype(F32)
        n = _round_up(f.shape[0], SUBLANE)
        rows.append(jnp.pad(f, ((0, n - f.shape[0]), (0, 0))))
    return jnp.concatenate(rows, axis=0)


def _unpack_rows(packed, shapes):
    out, r0 = [], 0
    for s in shapes:
        nr = math.prod(s[:-1])
        out.append(packed[r0:r0 + nr].reshape(s))
        r0 += _round_up(nr, SUBLANE)
    return out


def _cols_from_slots(g):
    return jnp.moveaxis(g, 0, -2).reshape(g.shape[1:-1] + (N_DEV * g.shape[-1],))


def _cols_to_slots(a):
    cs = a.shape[-1] // N_DEV
    return jnp.moveaxis(a.reshape(a.shape[:-1] + (N_DEV, cs)), -2, 0)


def _step(P, M, V):
    D, T = D_MODEL, SEQ
    S5W = D // 2
    C, LW, GP = _rk_dims()
    H = C // RWKV_HEAD
    G = S5W // S5_GROUP
    NS = G * S5_STATE
    NCH = G // S5_CHUNK_GROUPS
    SW = S5_CHUNK_GROUPS * S5_STATE
    RIN = 3 * C + 2 * LW + GATE_LORA
    RINP = 3 * C + 2 * LW + GP
    PROJ = S5W + RIN
    PROJP = S5W + RINP
    FF = 4 * D
    me = 4 * lax.axis_index("x") + 2 * lax.axis_index("y") + lax.axis_index("c")
    cs_mod = N_MOD * D // N_DEV
    eye = jnp.eye(S5_CHUNK_GROUPS, dtype=F32)

    x = P['x'][0]
    target = P['loss_target'][0]

    (c_all,) = exchange([P['c']], ['gather'], "comm_gather_c")
    c_all = c_all.reshape(N_DEV, D)
    (c_act,), _ = rowcall(lambda cv: ((cv * _sigmoid(cv),), ()), "silu_c", [c_all], [], [(D, F32)], [], N_DEV)
    ada_b_loc = lax.dynamic_slice(P['ada_b'], (0, me * cs_mod), (1, cs_mod))
    mod_loc = matmul(c_act, P['ada_w'][0], name="mod_mm", precise=True, extras=[(ada_b_loc, 'n')],
                     epi=lambda acc, bias: (acc + bias,))

    rk_shapes = [P[n][0].shape for n in RKPACK]
    rk_pack = _pack_rows([P[n][0] for n in RKPACK])
    gathered = gather_two_level(
        [mod_loc, P['w_in'][0].astype(BF16), P['w_out'][0].astype(BF16), P['ffn_w1'][0].astype(BF16),
         P['ffn_w2'][0].astype(BF16), P['s5_w_glu'][0].astype(BF16), rk_pack], "comm_gather_weights")
    mod_all, w_in_g, w_out_g, w1_g, w2_g, wglu_g, rk_g = gathered
    mod_me = lax.dynamic_index_in_dim(mod_all, me, axis=1, keepdims=False).reshape(N_MOD, 1, D)
    shift1, scale1, gate1, shift2, scale2, gate2 = (mod_me[i] for i in range(N_MOD))
    w_in = jnp.pad(_cols_from_slots(w_in_g), ((0, 0), (0, PROJP - PROJ)))
    w_out = w_out_g.reshape(D, D)
    w2 = w2_g.reshape(FF, D)
    wglu = wglu_g.reshape(S5W, S5W)
    rk_full = _unpack_rows(_cols_from_slots(rk_g), [s[:-1] + (C,) for s in rk_shapes])
    rk_w0, rk_a0, rk_wup, rk_aup, rk_gup = rk_full

    def lora_pad(up):
        z = jnp.zeros((N_DIR, LW, C), F32)
        for d in range(N_DIR):
            z = z.at[d, d * DECAY_LORA:(d + 1) * DECAY_LORA].set(up[d])
        return z

    wup_p, aup_p = lora_pad(rk_wup), lora_pad(rk_aup)
    gup_p = jnp.pad(rk_gup, ((0, GP - GATE_LORA), (0, 0)))
    mu_prev = jnp.pad(P['rk_shift_prev'], ((0, 0), (0, RINP - RIN)))
    mu_next = jnp.pad(P['rk_shift_next'], ((0, 0), (0, RINP - RIN)))
    r_k = P['rk_r_k'].reshape(1, C)
    fgain = P['final_gain'].reshape(1, D)

    TT = 256
    (h1,), _ = rowcall(lambda xv, g, sc, sh: ((_normmod(xv, g, sc, sh),), ()), "norm1",
                       [x], [P['norm1_gain'], scale1, shift1], [(D, BF16)], [], TT)
    proj = matmul(h1, w_in, name="proj_mm")

    lre = P['s5_lambda_re'][0].reshape(N_DIR, NS)
    lim = P['s5_lambda_im'][0].reshape(N_DIR, NS)
    lstep = jnp.broadcast_to(P['s5_log_step'][0][:, :, None], (N_DIR, G, S5_STATE)).reshape(N_DIR, NS)
    bre = P['s5_b_re'][0].reshape(NS, S5_GROUP).T
    bim = P['s5_b_im'][0].reshape(NS, S5_GROUP).T
    lbr, lbi, bbr, bbi = s5_prep(lre, lim, lstep, bre, bim)
    bbar = jnp.stack([bbr, bbi], axis=1).reshape(N_DIR, 2, S5_GROUP, NCH, S5_CHUNK_GROUPS, S5_STATE)
    wblk = jnp.einsum('drhcgp,gk->cghdrkp', bbar, eye).reshape(NCH, S5_CHUNK_GROUPS * S5_GROUP, 4 * SW)
    wblk = wblk.astype(MXU_DTYPE)
    u_view = (proj, S5W, 0)
    bus, _ = rowcall(lambda uv, wb: (_s5_bu_fn(uv, wb), ()), "s5_bu", [u_view], [wblk], [(NS, F32)] * 4, [], TT)
    s0r, s0i = s5_scan(bus[0], bus[1], lbr[0:1], lbi[0:1], reverse=False, name="s5_scan_f0")
    s1r, s1i = s5_scan(bus[2], bus[3], lbr[1:2], lbi[1:2], reverse=True, name="s5_scan_f1")

    def cblk(cm):
        c4 = cm.reshape(NCH, S5_CHUNK_GROUPS, S5_GROUP, S5_STATE)
        return jnp.einsum('cghp,gk->cgpkh', c4, eye).reshape(NCH, SW, S5_CHUNK_GROUPS * S5_GROUP)

    cre_b = cblk(P['s5_c_re'][0]).astype(MXU_DTYPE)
    cim_b = cblk(P['s5_c_im'][0]).astype(MXU_DTYPE)
    s5_full = [cre_b, cim_b, P['s5_d'], wglu, P['s5_b_glu']]
    TS = 128
    (y_s5,), _ = rowcall(lambda *a: ((_s5_out_fn(*a),), ()), "s5_out", [s0r, s0i, s1r, s1i, u_view], s5_full,
                         [(S5W, BF16)], [], TS)

    ps = rk_shift(proj, mu_prev, mu_next, S5W)
    pre_full = [rk_w0, wup_p[0], wup_p[1], rk_a0, aup_p[0], aup_p[1], gup_p, P['rk_k_k'], P['rk_k_a']]
    pre_out, _ = rowcall(lambda *a: (_rk_pre_fn(*a)[2:], ()), "rk_pre", [ps], pre_full, [(C, F32)] * 8, [], TS)
    kkn, w_0, w_1, k_0, k_1, b_0, b_1, g_gate = pre_out
    r_t, v_t = ps[:, 0:C], ps[:, 2 * C:3 * C]

    def hmT(a):
        return a.reshape(T, H, RWKV_HEAD).transpose(1, 2, 0)

    def unT(a):
        return a.transpose(2, 0, 1).reshape(T, C)

    v_cols = rk_spread(hmT(v_t), "rk_spread_v")
    dir_rows = [(w_0, k_0, b_0), (w_1, k_1, b_1)]
    yT, hist = [], []
    for d in range(N_DIR):
        wd, kd, bd = dir_rows[d]
        yd, hd = rk_scan(r_t, kkn, wd, kd, bd, v_cols, reverse=(d == 1), name=f"rk_scan_f{d}")
        yT.append(yd)
        hist.append(hd)
    y_0, y_1 = unT(yT[0]), unT(yT[1])
    post_full = [r_k, P['rk_ln_gain'], P['rk_ln_bias']]
    post_tiled = [y_0, y_1, (ps, C, 0), (ps, C, 2), k_0, k_1, g_gate]
    (y_rk,), _ = rowcall(lambda *a: ((_rk_post_fn(*a),), ()), "rk_post", post_tiled, post_full, [(C, BF16)], [], TS)

    ycat = jnp.concatenate([y_s5, y_rk], axis=1)
    mixed = matmul(ycat, w_out, name="out_mm")

    def res_norm(xv, mv, gate, g, sc, sh):
        x1v = xv + gate * mv
        return x1v, _normmod(x1v, g, sc, sh)

    (x1, h2), _ = rowcall(lambda *a: (res_norm(*a), ()), "norm2", [x, mixed], [gate1, P['norm2_gain'], scale2, shift2],
                          [(D, F32), (D, BF16)], [], TT)
    a_ff, hh_ff = matmul(h2, w1_g, name="ffn1_mm", b_slots=True, out_dtypes=(F32, BF16),
                         epi=lambda acc: (acc, jnp.square(jnp.maximum(acc, 0.0))))
    ffn = matmul(hh_ff, w2, name="ffn2_mm")

    def loss_fn(x1v, fv, tg, gate, fg):
        def f(x1_, f_, gate_, fg_):
            out = _rms(x1_ + gate_ * f_, fg_)
            err = out - tg
            return 0.5 * jnp.sum(jnp.sum(err * err, axis=1, keepdims=True), axis=0, keepdims=True) * (1.0 / D)
        lv, vjp = jax.vjp(f, x1v, fv, gate, fg)
        dx1, dff, dgate, dfg = vjp(jnp.ones((1, 1), F32))
        return (dx1, dff), (jnp.broadcast_to(lv, (SUBLANE, LANE)), dgate, dfg)

    (dx2, dffn), (loss_t, dgate2, dfgain) = rowcall(
        loss_fn, "loss", [x1, ffn, target], [gate2, fgain], [(D, F32), (D, BF16)], [(SUBLANE, LANE), (1, D), (1, D)], TT)
    loss = lax.psum(loss_t[0, 0], MESH_AXES)

    da = matmul(dffn, w2, name="dffn2_mm", tb=True, out_dtypes=(BF16,), extras=[(a_ff, 'mn')],
                epi=lambda acc, av: (acc * (2.0 * jnp.maximum(av, 0.0)),))
    g_w2 = matmul(hh_ff, dffn, name="gw2_mm", ta=True, out_dtypes=(BF16,))
    dh2 = matmul(da, w1_g, name="dh2_mm", tb=True, b_slots=True)
    g_w1 = matmul(h2, da, name="gw1_mm", ta=True, out_slots=N_DEV, out_dtypes=(BF16,))

    def res_norm_bwd(dx2v, dh2v, xv, mv, gate, g, sc, sh):
        _, vjp = jax.vjp(res_norm, xv, mv, gate, g, sc, sh)
        dx, dm, dgate, dg, dsc, dsh = vjp((dx2v, dh2v))
        return (dx, dm), (dgate, dg, dsc, dsh)

    (dx1, dmixed), (dgate1, dgain2, dscale2, dshift2) = rowcall(
        res_norm_bwd, "norm2_bwd", [dx2, dh2, x, mixed], [gate1, P['norm2_gain'], scale2, shift2],
        [(D, F32), (D, BF16)], [(1, D)] * 4, TT)

    dycat = matmul(dmixed, w_out, name="dycat_mm", tb=True)
    g_wout = matmul(ycat, dmixed, name="gwout_mm", ta=True, out_dtypes=(BF16,))

    def post_bwd(dy, *a):
        _, vjp = jax.vjp(_rk_post_fn, *a)
        gy0, gy1, gr, gv, gk0, gk1, gg, grk, glg, glb = vjp(dy)
        return (gy0, gr, gv, gk0, gk1, gg), (grk, glg, glb)

    cb_rk = S5W // C if C else 0
    (dy_rk, dr_p, dv_p, dk0_p, dk1_p, dg_p), (g_rk_rk, g_lng, g_lnb) = rowcall(
        post_bwd, "rk_post_bwd", [(dycat, C, cb_rk)] + post_tiled, post_full, [(C, F32)] * 6, [(1, C)] * 3, TS)
    dy_cols = rk_spread(hmT(dy_rk), "rk_spread_dy")
    scan_g = []
    for d in range(N_DIR):
        wd, kd, bd = dir_rows[d]
        scan_g.append(rk_scan_bwd(r_t, kkn, wd, kd, bd, v_cols, dy_cols, hist[d], reverse=(d == 1), name=f"rk_scan_b{d}"))
    cot = [dr_p, scan_g[0][0], scan_g[1][0],
           dv_p, unT(scan_g[0][5]), unT(scan_g[1][5]),
           scan_g[0][1], scan_g[1][1],
           scan_g[0][2], scan_g[1][2],
           dk0_p, scan_g[0][3], dk1_p, scan_g[1][3],
           scan_g[0][4], scan_g[1][4],
           dg_p]

    def pre_bwd(psv, r0, r1, r2, v0, v1, v2, q0, q1, dw0, dw1, k0a, k0b, k1a, k1b, db0, db1, dgv, *params):
        _, vjp = jax.vjp(_rk_pre_fn, psv, *params)
        grads = vjp((r0 + r1 + r2, v0 + v1 + v2, q0 + q1, dw0, dw1, k0a + k0b, k1a + k1b, db0, db1, dgv))
        return (grads[0],), tuple(grads[1:])

    (dps,), pre_g = rowcall(pre_bwd, "rk_pre_bwd", [ps] + cot, pre_full, [(RINP, F32)],
                            [f.shape for f in pre_full], TS)
    g_w0, g_wup0, g_wup1, g_a0, g_aup0, g_aup1, g_gup_p, g_kk, g_ka = pre_g
    g_wup_p, g_aup_p = jnp.stack([g_wup0, g_wup1]), jnp.stack([g_aup0, g_aup1])
    dp_rk, g_mup, g_mun = rk_shift_bwd(dps, proj, mu_prev, mu_next, S5W)

    def s5_out_bwd(dy, *a):
        a = [t.astype(F32) for t in a]
        _, vjp = jax.vjp(_s5_out_fn, *a)
        g = vjp(dy)
        return (g[0], g[1], g[4]), tuple(g[5:])

    (dxr, dxi, du_a), s5_pg = rowcall(
        s5_out_bwd, "s5_out_bwd", [(dycat, S5W, 0), s0r, s0i, s1r, s1i, u_view], s5_full,
        [(NS, F32), (NS, F32), (S5W, F32)], [f.shape for f in s5_full], TS)
    g_creb, g_cimb, g_s5d, g_wglu, g_bglu = s5_pg
    l0r, l0i = s5_scan(dxr, dxi, lbr[0:1], -lbi[0:1], reverse=True, name="s5_scan_b0")
    l1r, l1i = s5_scan(dxr, dxi, lbr[1:2], -lbi[1:2], reverse=False, name="s5_scan_b1")
    dl0r, dl0i = s5_dlam(s0r, s0i, l0r, l0i, reverse=False, name="s5_dlam0")
    dl1r, dl1i = s5_dlam(s1r, s1i, l1r, l1i, reverse=True, name="s5_dlam1")

    def bu_bwd(uv, g0, g1, g2, g3, wb):
        _, vjp = jax.vjp(_s5_bu_fn, uv, wb.astype(F32))
        du, dwb = vjp((g0, g1, g2, g3))
        return (du,), (dwb,)

    (du_b,), (g_wblk,) = rowcall(bu_bwd, "s5_bu_bwd", [u_view, l0r, l0i, l1r, l1i], [wblk], [(S5W, F32)],
                                 [wblk.shape], TS)
    g_bbar = jnp.einsum('cghdrkp,gk->drhcgp',
                        g_wblk.reshape(NCH, S5_CHUNK_GROUPS, S5_GROUP, N_DIR, 2, S5_CHUNK_GROUPS, S5_STATE), eye)
    g_bbar = g_bbar.reshape(N_DIR, 2, S5_GROUP, NS)
    g_lre, g_lim, g_lstep, g_bre, g_bim = s5_prep_bwd(
        lre, lim, lstep, bre, bim, jnp.concatenate([dl0r, dl1r], 0), jnp.concatenate([dl0i, dl1i], 0),
        g_bbar[:, 0], g_bbar[:, 1])

    def uncblk(gb):
        g5 = gb.reshape(NCH, S5_CHUNK_GROUPS, S5_STATE, S5_CHUNK_GROUPS, S5_GROUP)
        return jnp.einsum('cgpkh,gk->cghp', g5, eye).reshape(G, S5_GROUP, S5_STATE)

    (du_tot,), _ = rowcall(lambda a, b_: ((a + b_,), ()), "s5_du_sum", [du_a, du_b], [], [(S5W, BF16)], [], TT)
    dproj = jnp.concatenate([du_tot, dp_rk.astype(BF16)], axis=1)
    dh1 = matmul(dproj, w_in, name="dh1_mm", tb=True)
    g_win = matmul(h1, dproj, name="gwin_mm", ta=True, out_dtypes=(BF16,))

    def norm1_bwd(dx1v, dh1v, xv, g, sc, sh):
        _, vjp = jax.vjp(_normmod, xv, g, sc, sh)
        dx, dg, dsc, dsh = vjp(dh1v)
        return (dx1v + dx,), (dg, dsc, dsh)

    (grad_x,), (dgain1, dscale1, dshift1) = rowcall(
        norm1_bwd, "norm1_bwd", [dx1, dh1, x], [P['norm1_gain'], scale1, shift1], [(D, F32)], [(1, D)] * 3, TT)

    dmod = jnp.concatenate([dshift1, dscale1, dgate1, dshift2, dscale2, dgate2], axis=1)
    lstep_g = g_lstep.reshape(N_DIR, G, S5_STATE)
    small_g = {
        'ada_b': dmod, 'norm1_gain': dgain1, 'norm2_gain': dgain2, 'final_gain': dfgain.reshape(D),
        's5_lambda_re': g_lre.reshape(1, N_DIR, G, S5_STATE), 's5_lambda_im': g_lim.reshape(1, N_DIR, G, S5_STATE),
        's5_log_step': lstep_g,
        's5_b_re': g_bre.T.reshape(1, G, S5_STATE, S5_GROUP), 's5_b_im': g_bim.T.reshape(1, G, S5_STATE, S5_GROUP),
        's5_c_re': uncblk(g_creb)[None], 's5_c_im': uncblk(g_cimb)[None],
        's5_d': g_s5d, 's5_b_glu': g_bglu,
        'rk_shift_prev': g_mup[:, :RIN], 'rk_shift_next': g_mun[:, :RIN],
        'rk_k_k': g_kk, 'rk_k_a': g_ka, 'rk_r_k': g_rk_rk.reshape(1, H, RWKV_HEAD),
        'rk_ln_gain': g_lng, 'rk_ln_bias': g_lnb,
    }
    small_shapes = {n: P[n].shape for n in SMALL}
    small_shapes['s5_log_step'] = (N_DIR, G, S5_STATE)
    small_pack = _pack_flat([small_g[n] for n in SMALL])

    def lora_unpad(gp):
        return jnp.stack([gp[d, d * DECAY_LORA:(d + 1) * DECAY_LORA] for d in range(N_DIR)])

    rk_grads = {'rk_w0': g_w0, 'rk_a0': g_a0, 'rk_w_up': lora_unpad(g_wup_p), 'rk_a_up': lora_unpad(g_aup_p),
                'rk_g_up': g_gup_p[:GATE_LORA]}
    rk_gpack = jnp.stack([_pack_rows([_cols_to_slots(rk_grads[n])[j] for n in RKPACK]) for j in range(N_DEV)])
    g_win_s = _cols_to_slots(g_win[:, :PROJ])
    (small_all,) = gather_two_level([small_pack], "comm_gather_small_grads")
    slots = [g_win_s, g_wout.reshape(N_DEV, D // N_DEV, D), g_w1, g_w2.reshape(N_DEV, FF // N_DEV, D),
             g_wglu.reshape(N_DEV, S5W // N_DEV, S5W), rk_gpack]
    my_c = lax.axis_index("c")

    def core_half(a, core):
        return lax.dynamic_index_in_dim(a.reshape((N_DEV // 2, 2) + a.shape[1:]), core, axis=1, keepdims=False)

    from_sibling = swap_sibling([core_half(a, 1 - my_c) for a in slots], "comm_swap_grads")
    chip_sums = []
    for a, got, nm in zip(slots, from_sibling, ['w_in', 'w_out', 'ffn_w1', 'ffn_w2', 's5_w_glu', 'rkpack']):
        own = core_half(a, my_c)
        flat = (own.shape[0] * math.prod(own.shape[1:-1]), own.shape[-1])
        (sm,), _ = rowcall(lambda u, v_: ((u.astype(F32) + v_.astype(F32),), ()), "chip_sum_" + nm,
                           [own.reshape(flat), got.reshape(flat)], [], [(flat[1], a.dtype)], [], 512)
        chip_sums.append(sm.reshape(own.shape))
    win_parts, wout_parts, w1_parts, w2_parts, wglu_parts, rk_parts = exchange_chips(chip_sums, "comm_grads")

    res = {}

    def put(name, g, dl, m2, v2):
        shp = P[name].shape
        res[name] = tuple(t.reshape(shp) for t in (g, dl, m2, v2))

    def adam2d(name, parts):
        shp = P[name].shape
        r2 = (math.prod(shp[:-1]), shp[-1])
        put(name, *adam(parts.reshape((parts.shape[0],) + r2), P[name].reshape(r2), M[name].reshape(r2),
                        V[name].reshape(r2), "adam_" + name))

    adam2d('w_in', win_parts)
    adam2d('w_out', wout_parts)
    adam2d('ffn_w1', w1_parts)
    adam2d('ffn_w2', w2_parts)
    adam2d('s5_w_glu', wglu_parts)
    off = 0
    for n in SMALL:
        if n == 'ada_b':
            break
        off += _round_up(math.prod(small_shapes[n]), SUBLANE * LANE) // LANE
    nrow_b = N_MOD * D // LANE
    dmod_all = small_all[:, off:off + nrow_b].reshape(N_DEV, N_MOD * D)
    dmod_cols = lax.dynamic_slice(dmod_all, (0, me * cs_mod), (N_DEV, cs_mod))
    g_adaw = matmul(c_act, dmod_cols, name="gadaw_mm", ta=True, precise=True)
    adam2d('ada_w', g_adaw[None])
    small_w = dict(P)
    small_m, small_v = dict(M), dict(V)
    rk_res = adam(rk_parts, rk_pack, _pack_rows([M[n][0] for n in RKPACK]), _pack_rows([V[n][0] for n in RKPACK]),
                  "adam_rkpack")
    for name, parts4 in zip(RKPACK, zip(*[_unpack_rows(t, rk_shapes) for t in rk_res])):
        put(name, *parts4)
    return loss, grad_x, res, (small_all, small_shapes, small_w, small_m, small_v)


def _small_update(small_all, small_shapes, P, M, V, res):
    G = (D_MODEL // 2) // S5_GROUP
    names = [n for n in SMALL if n != 's5_log_step']
    shapes = [small_shapes[n] for n in SMALL]
    parts = _unpack_flat_batched(small_all, shapes)
    by = dict(zip(SMALL, parts))
    ls = by['s5_log_step']
    ls = ls.transpose(0, 3, 1, 2).reshape(N_DEV * S5_STATE, N_DIR * G)
    pk = lambda d: _pack_flat([d[n] for n in names])
    packs = jnp.stack([_pack_flat([by[n][j] for n in names]) for j in range(N_DEV)])
    out = adam(packs, pk(P), pk(M), pk(V), "adam_small")
    shp = [P[n].shape for n in names]
    for name, parts4 in zip(names, zip(*[_unpack_flat(t, shp) for t in out])):
        res[name] = parts4
    lsw = lambda d: jnp.pad(d['s5_log_step'].reshape(1, N_DIR * G), ((0, SUBLANE - 1), (0, 0)))
    ls_parts = jnp.pad(ls[:, None, :], ((0, 0), (0, SUBLANE - 1), (0, 0)))
    o = adam(ls_parts, lsw(P), lsw(M), lsw(V), "adam_log_step")
    res['s5_log_step'] = tuple(t[0:1].reshape(P['s5_log_step'].shape) for t in o)


def _unpack_flat_batched(packed, shapes):
    out, r0 = [], 0
    B = packed.shape[0]
    for s in shapes:
        n = math.prod(s)
        nr = _round_up(n, SUBLANE * LANE) // LANE
        out.append(packed[:, r0:r0 + nr].reshape(B, -1)[:, :n].reshape((B,) + tuple(s)))
        r0 += nr
    return out


def kernel(x, c, ada_w, ada_b, norm1_gain, norm2_gain, final_gain, w_in, w_out, s5_lambda_re, s5_lambda_im, s5_log_step, s5_b_re, s5_b_im, s5_c_re, s5_c_im, s5_d, s5_w_glu, s5_b_glu, rk_shift_prev, rk_shift_next, rk_w0, rk_w_up, rk_a0, rk_a_up, rk_g_up, rk_k_k, rk_k_a, rk_r_k, rk_ln_gain, rk_ln_bias, ffn_w1, ffn_w2, loss_target, m_ada_w, m_ada_b, m_norm1_gain, m_norm2_gain, m_final_gain, m_w_in, m_w_out, m_s5_lambda_re, m_s5_lambda_im, m_s5_log_step, m_s5_b_re, m_s5_b_im, m_s5_c_re, m_s5_c_im, m_s5_d, m_s5_w_glu, m_s5_b_glu, m_rk_shift_prev, m_rk_shift_next, m_rk_w0, m_rk_w_up, m_rk_a0, m_rk_a_up, m_rk_g_up, m_rk_k_k, m_rk_k_a, m_rk_r_k, m_rk_ln_gain, m_rk_ln_bias, m_ffn_w1, m_ffn_w2, v_ada_w, v_ada_b, v_norm1_gain, v_norm2_gain, v_final_gain, v_w_in, v_w_out, v_s5_lambda_re, v_s5_lambda_im, v_s5_log_step, v_s5_b_re, v_s5_b_im, v_s5_c_re, v_s5_c_im, v_s5_d, v_s5_w_glu, v_s5_b_glu, v_rk_shift_prev, v_rk_shift_next, v_rk_w0, v_rk_w_up, v_rk_a0, v_rk_a_up, v_rk_g_up, v_rk_k_k, v_rk_k_a, v_rk_r_k, v_rk_ln_gain, v_rk_ln_bias, v_ffn_w1, v_ffn_w2):
    given = dict(locals())
    P = {n: given[n] for n in ['x', 'c', 'loss_target'] + WEIGHTS}
    M = {n: given['m_' + n] for n in WEIGHTS}
    V = {n: given['v_' + n] for n in WEIGHTS}
    loss, grad_x, res, small = _step(P, M, V)
    small_all, small_shapes, _, _, _ = small
    _small_update(small_all, small_shapes, P, M, V, res)
    outs = [loss, grad_x[None]]
    for q in range(4):
        outs += [res[n][q] for n in WEIGHTS]
    return tuple(outs)
```

```python
import functools
import math

import jax
import jax.numpy as jnp
from jax import lax
from jax.experimental import pallas as pl
from jax.experimental.pallas import tpu as pltpu

F32 = jnp.float32
BF16 = jnp.bfloat16
HI = lax.Precision.HIGHEST
MXU_DTYPE = jnp.bfloat16

N_DEV = 8
MESH_AXES = ("x", "y", "c")
D_MODEL = 2048
SEQ = 2048
S5_GROUP = 16
S5_STATE = 64
RWKV_HEAD = 64
DECAY_LORA = 64
GATE_LORA = 160
N_DIR = 2
N_MOD = 6
NORM_EPS = 1e-6
GN_EPS = 64e-5
L2_EPS = 1e-12
ADAM_LR = 0.001
ADAM_B1 = 0.9
ADAM_B2 = 0.999
ADAM_EPS = 1e-08
ADAM_WD = 0.01
ADAM_STEP = 10
LANE = 128
SUBLANE = 8
S5_CHUNK_GROUPS = 8
S5_SCAN_ROWS = 256
VMEM_LIMIT = 56 * 1024 * 1024

WEIGHTS = ['ada_w', 'ada_b', 'norm1_gain', 'norm2_gain', 'final_gain', 'w_in', 'w_out', 's5_lambda_re',
           's5_lambda_im', 's5_log_step', 's5_b_re', 's5_b_im', 's5_c_re', 's5_c_im', 's5_d', 's5_w_glu',
           's5_b_glu', 'rk_shift_prev', 'rk_shift_next', 'rk_w0', 'rk_w_up', 'rk_a0', 'rk_a_up', 'rk_g_up',
           'rk_k_k', 'rk_k_a', 'rk_r_k', 'rk_ln_gain', 'rk_ln_bias', 'ffn_w1', 'ffn_w2']
SMALL = ['ada_b', 'norm1_gain', 'norm2_gain', 'final_gain', 's5_lambda_re', 's5_lambda_im', 's5_log_step',
         's5_b_re', 's5_b_im', 's5_c_re', 's5_c_im', 's5_d', 's5_b_glu', 'rk_shift_prev', 'rk_shift_next',
         'rk_k_k', 'rk_k_a', 'rk_r_k', 'rk_ln_gain', 'rk_ln_bias']
RKPACK = ['rk_w0', 'rk_a0', 'rk_w_up', 'rk_a_up', 'rk_g_up']


def _round_up(n, m):
    return (n + m - 1) // m * m


def _tile(dim, pref, unit=LANE):
    t = min(pref, dim) // unit * unit
    while t >= unit:
        if dim % t == 0:
            return t
        t -= unit
    return dim


def _cparams(sem=None):
    return pltpu.CompilerParams(dimension_semantics=sem, vmem_limit_bytes=VMEM_LIMIT)


def _full_spec(a):
    nd = a.ndim
    return pl.BlockSpec(a.shape, lambda *_: (0,) * nd)


def exchange(arrs, modes, name):
    n = len(arrs)
    out_shape = [jax.ShapeDtypeStruct((N_DEV,) + a.shape if m == 'gather' else a.shape, a.dtype)
                 for a, m in zip(arrs, modes)]

    def body(*refs):
        ins, outs = refs[:n], refs[n:2 * n]
        send_sems, recv_sems, local_sems = refs[2 * n:]
        x, y, c = (lax.axis_index(a) for a in MESH_AXES)
        me = 4 * x + 2 * y + c
        copies = []
        for i in range(n):
            gather = modes[i] == 'gather'
            mine = pltpu.make_async_copy(ins[i] if gather else ins[i].at[me], outs[i].at[me], local_sems.at[i])
            mine.start()
            copies.append(mine)
        remote = []
        for k in range(1, N_DEV):
            px = 1 - x if (k >> 2) & 1 else x
            py = 1 - y if (k >> 1) & 1 else y
            pc = 1 - c if k & 1 else c
            peer = 4 * px + 2 * py + pc
            for i in range(n):
                src = ins[i] if modes[i] == 'gather' else ins[i].at[peer]
                cp = pltpu.make_async_remote_copy(
                    src_ref=src, dst_ref=outs[i].at[me], send_sem=send_sems.at[i, k - 1],
                    recv_sem=recv_sems.at[i, k - 1], device_id=(px, py, pc), device_id_type=pl.DeviceIdType.MESH)
                cp.start()
                remote.append(cp)
        for cp in remote:
            cp.wait_recv()
        for cp in remote:
            cp.wait_send()
        for cp in copies:
            cp.wait()

    any_spec = pl.BlockSpec(memory_space=pl.ANY)
    return pl.pallas_call(
        body, name=name, out_shape=out_shape,
        in_specs=[any_spec] * n, out_specs=[any_spec] * n,
        scratch_shapes=[pltpu.SemaphoreType.DMA((n, N_DEV - 1)), pltpu.SemaphoreType.DMA((n, N_DEV - 1)),
                        pltpu.SemaphoreType.DMA((n,))],
        compiler_params=pltpu.CompilerParams(has_side_effects=True),
    )(*arrs)


def gather_two_level(arrs, name):
    n = len(arrs)
    out_shape = [jax.ShapeDtypeStruct((N_DEV,) + a.shape, a.dtype) for a in arrs]

    def body(*refs):
        ins, outs = refs[:n], refs[n:2 * n]
        send_sems, recv_sems, local_sems = refs[2 * n:]
        x, y, c = (lax.axis_index(a) for a in MESH_AXES)
        me, sibling = (x, y, c), (x, y, 1 - c)
        chips = [(1 - x, y), (x, 1 - y), (1 - x, 1 - y)]

        def slot(px, py, pc):
            return 4 * px + 2 * py + pc

        def copy(i, k, block, to, src=None):
            return pltpu.make_async_remote_copy(
                src_ref=outs[i].at[slot(*block)] if src is None else src, dst_ref=outs[i].at[slot(*block)],
                send_sem=send_sems.at[i, k], recv_sem=recv_sems.at[i, k], device_id=to,
                device_id_type=pl.DeviceIdType.MESH)

        mine = [pltpu.make_async_copy(ins[i], outs[i].at[slot(*me)], local_sems.at[i]) for i in range(n)]
        for cp in mine:
            cp.start()
        started = []
        for i in range(n):
            started.append(copy(i, 0, me, sibling, src=ins[i]))
            started += [copy(i, 1 + j, me, (*chip, c), src=ins[i]) for j, chip in enumerate(chips)]
        for cp in started:
            cp.start()
        for j, chip in enumerate(chips):
            for i in range(n):
                copy(i, 1 + j, (*chip, c), me).wait_recv()
                fwd = copy(i, 4 + j, (*chip, c), sibling)
                fwd.start()
                started.append(fwd)
        for i in range(n):
            copy(i, 0, sibling, me).wait_recv()
        for j, chip in enumerate(chips):
            for i in range(n):
                copy(i, 4 + j, (*chip, 1 - c), me).wait_recv()
        for cp in started:
            cp.wait_send()
        for cp in mine:
            cp.wait()

    any_spec = pl.BlockSpec(memory_space=pl.ANY)
    return pl.pallas_call(
        body, name=name, out_shape=out_shape,
        in_specs=[any_spec] * n, out_specs=[any_spec] * n,
        scratch_shapes=[pltpu.SemaphoreType.DMA((n, N_DEV - 1)), pltpu.SemaphoreType.DMA((n, N_DEV - 1)),
                        pltpu.SemaphoreType.DMA((n,))],
        compiler_params=pltpu.CompilerParams(has_side_effects=True),
    )(*arrs)


def swap_sibling(arrs, name):
    n = len(arrs)

    def body(*refs):
        ins, outs = refs[:n], refs[n:2 * n]
        send_sems, recv_sems = refs[2 * n:]
        x, y, c = (lax.axis_index(a) for a in MESH_AXES)
        cps = [pltpu.make_async_remote_copy(src_ref=ins[i], dst_ref=outs[i], send_sem=send_sems.at[i],
                                            recv_sem=recv_sems.at[i], device_id=(x, y, 1 - c),
                                            device_id_type=pl.DeviceIdType.MESH) for i in range(n)]
        for cp in cps:
            cp.start()
        for cp in cps:
            cp.wait()

    any_spec = pl.BlockSpec(memory_space=pl.ANY)
    return pl.pallas_call(
        body, name=name, out_shape=[jax.ShapeDtypeStruct(a.shape, a.dtype) for a in arrs],
        in_specs=[any_spec] * n, out_specs=[any_spec] * n,
        scratch_shapes=[pltpu.SemaphoreType.DMA((n,)), pltpu.SemaphoreType.DMA((n,))],
        compiler_params=pltpu.CompilerParams(has_side_effects=True),
    )(*arrs)


def exchange_chips(arrs, name):
    n = len(arrs)

    def body(*refs):
        ins, outs = refs[:n], refs[n:2 * n]
        send_sems, recv_sems, local_sems = refs[2 * n:]
        x, y, c = (lax.axis_index(a) for a in MESH_AXES)
        mine = 2 * x + y
        chips = [(1 - x, y), (x, 1 - y), (1 - x, 1 - y)]
        local = [pltpu.make_async_copy(ins[i].at[mine], outs[i].at[mine], local_sems.at[i]) for i in range(n)]
        for cp in local:
            cp.start()
        remote = []
        for j, (px, py) in enumerate(chips):
            for i in range(n):
                cp = pltpu.make_async_remote_copy(
                    src_ref=ins[i].at[2 * px + py], dst_ref=outs[i].at[mine], send_sem=send_sems.at[i, j],
                    recv_sem=recv_sems.at[i, j], device_id=(px, py, c), device_id_type=pl.DeviceIdType.MESH)
                cp.start()
                remote.append(cp)
        for cp in remote:
            cp.wait_recv()
        for cp in remote:
            cp.wait_send()
        for cp in local:
            cp.wait()

    any_spec = pl.BlockSpec(memory_space=pl.ANY)
    return pl.pallas_call(
        body, name=name, out_shape=[jax.ShapeDtypeStruct(a.shape, a.dtype) for a in arrs],
        in_specs=[any_spec] * n, out_specs=[any_spec] * n,
        scratch_shapes=[pltpu.SemaphoreType.DMA((n, 3)), pltpu.SemaphoreType.DMA((n, 3)), pltpu.SemaphoreType.DMA((n,))],
        compiler_params=pltpu.CompilerParams(has_side_effects=True),
    )(*arrs)


def matmul(a, b, *, name, ta=False, tb=False, b_slots=False, out_slots=0, out_dtypes=(F32,), epi=None,
           extras=(), precise=False, tm=512, tn=512, tk=2048):
    if ta:
        K, M = a.shape
    else:
        M, K = a.shape
    if b_slots:
        ns, br, bc = b.shape
        bshape = (br, ns * bc)
    else:
        bshape = b.shape
    N = bshape[0] if tb else bshape[1]
    assert (bshape[1] if tb else bshape[0]) == K, (a.shape, b.shape, ta, tb)
    tm, tn, tk = _tile(M, tm, SUBLANE), _tile(N, tn), _tile(K, tk, SUBLANE if K < LANE else LANE)
    if b_slots and tb:
        tk = _tile(b.shape[2], tk)
    elif b_slots:
        tn = _tile(b.shape[2], tn)
    if out_slots:
        tn = _tile(N // out_slots, tn)
    if b_slots:
        cs = b.shape[2]
        tcol = tk if tb else tn
        assert cs % tcol == 0
        per = cs // tcol
    if out_slots:
        ncs = N // out_slots
        assert ncs % tn == 0
        operc = ncs // tn
    nk = K // tk
    a_spec = pl.BlockSpec((tk, tm), lambda i, j, k: (k, i)) if ta else pl.BlockSpec((tm, tk), lambda i, j, k: (i, k))
    if b_slots:
        if tb:
            b_spec = pl.BlockSpec((None, tn, tk), lambda i, j, k: (k // per, j, k % per))
        else:
            b_spec = pl.BlockSpec((None, tk, tn), lambda i, j, k: (j // per, k, j % per))
    else:
        b_spec = pl.BlockSpec((tn, tk), lambda i, j, k: (j, k)) if tb else pl.BlockSpec((tk, tn), lambda i, j, k: (k, j))
    ex_specs = []
    for arr, kind in extras:
        if kind == 'mn':
            ex_specs.append(pl.BlockSpec((tm, tn), lambda i, j, k: (i, j)))
        else:
            ex_specs.append(pl.BlockSpec((1, tn), lambda i, j, k: (0, j)))
    if out_slots:
        o_spec = pl.BlockSpec((None, tm, tn), lambda i, j, k: (j // operc, i, j % operc))
        o_shape = (out_slots, M, ncs)
    else:
        o_spec = pl.BlockSpec((tm, tn), lambda i, j, k: (i, j))
        o_shape = (M, N)
    ne, no = len(extras), len(out_dtypes)
    dims = (((0 if ta else 1,), (1 if tb else 0,)), ((), ()))
    op_dtype = F32 if precise else MXU_DTYPE

    def body(a_ref, b_ref, *rest):
        ex_refs, out_refs, acc = rest[:ne], rest[ne:ne + no], rest[-1]
        k = pl.program_id(2)
        part = lax.dot_general(a_ref[...].astype(op_dtype), b_ref[...].astype(op_dtype), dims,
                               precision=HI if precise else None, preferred_element_type=F32)

        def finish(total):
            res = epi(total, *[e[...] for e in ex_refs]) if epi is not None else (total,)
            for o, r in zip(out_refs, res):
                o[...] = r.astype(o.dtype)

        if nk == 1:
            finish(part)
        else:
            @pl.when(k == 0)
            def _():
                acc[...] = part

            @pl.when(jnp.logical_and(k > 0, k < nk - 1))
            def _():
                acc[...] += part

            @pl.when(k == nk - 1)
            def _():
                finish(acc[...] + part)

    outs = pl.pallas_call(
        body, name=name, grid=(M // tm, N // tn, nk),
        in_specs=[a_spec, b_spec] + ex_specs, out_specs=[o_spec] * no,
        out_shape=[jax.ShapeDtypeStruct(o_shape, dt) for dt in out_dtypes],
        scratch_shapes=[pltpu.VMEM((tm, tn), F32)],
        compiler_params=_cparams(("parallel", "parallel", "arbitrary")),
    )(a, b, *[e[0] for e in extras])
    return outs[0] if no == 1 else outs


def rowcall(fn, name, tiled, full, tiled_out, acc_out, tt):
    views = [(t, t.shape[1], 0) if not isinstance(t, tuple) else t for t in tiled]
    T = views[0][0].shape[0]
    tt = _tile(T, tt, SUBLANE)
    nt, nf, nto, nao = len(views), len(full), len(tiled_out), len(acc_out)

    def view_spec(w, cb):
        return pl.BlockSpec((tt, w), lambda i: (i, cb))

    in_specs = [view_spec(w, cb) for _, w, cb in views] + [_full_spec(f) for f in full]
    out_specs = [pl.BlockSpec((tt, w), lambda i: (i, 0)) for w, _ in tiled_out]
    out_specs += [pl.BlockSpec(s, lambda i, nd=len(s): (0,) * nd) for s in acc_out]
    out_shape = [jax.ShapeDtypeStruct((T, w), dt) for w, dt in tiled_out]
    out_shape += [jax.ShapeDtypeStruct(s, F32) for s in acc_out]

    def body(*refs):
        tin, fin = refs[:nt], refs[nt:nt + nf]
        tout, aout = refs[nt + nf:nt + nf + nto], refs[nt + nf + nto:]
        touts, aouts = fn(*[r[...] for r in tin], *[r[...] for r in fin])
        for r, v in zip(tout, touts):
            r[...] = v.astype(r.dtype)
        if nao:
            @pl.when(pl.program_id(0) == 0)
            def _():
                for r in aout:
                    r[...] = jnp.zeros_like(r)

            for r, v in zip(aout, aouts):
                r[...] += v.astype(F32)

    outs = pl.pallas_call(
        body, name=name, grid=(T // tt,), in_specs=in_specs, out_specs=out_specs, out_shape=out_shape,
        compiler_params=_cparams(("arbitrary",) if nao else ("parallel",)),
    )(*[v[0] for v in views], *full)
    return outs[:nto], outs[nto:]


def _mm(a, b):
    return jnp.dot(a.astype(MXU_DTYPE), b.astype(MXU_DTYPE), preferred_element_type=F32)


def _rms(x, gain):
    ms = jnp.mean(x * x, axis=-1, keepdims=True)
    return x * lax.rsqrt(ms + NORM_EPS) * gain


def _normmod(x, gain, scale, shift):
    return _rms(x, gain) * (1.0 + scale) + shift


def _gelu_tanh(y):
    return 0.5 * y * (1.0 + jnp.tanh(math.sqrt(2.0 / math.pi) * (y + 0.044715 * (y * y * y))))


def _sigmoid(x):
    return 1.0 / (1.0 + jnp.exp(-x))


def _softplus(x):
    return jnp.maximum(x, 0.0) + jnp.log(1.0 + jnp.exp(-jnp.abs(x)))


def _seg_mats(width, seg):
    r = lax.broadcasted_iota(jnp.int32, (width, LANE), 0) // seg
    c = lax.broadcasted_iota(jnp.int32, (width, LANE), 1)
    s = (r == c).astype(F32)
    rt = lax.broadcasted_iota(jnp.int32, (LANE, width), 0)
    ct = lax.broadcasted_iota(jnp.int32, (LANE, width), 1) // seg
    st = (rt == ct).astype(F32)
    return s, st


def _segsum(x, s):
    return jnp.dot(x, s, precision=HI, preferred_element_type=F32)


def _s5_prep_fn(lre, lim, lstep, bre, bim):
    step = jnp.exp(lstep)
    mag = jnp.exp(lre * step)
    lbr = mag * jnp.cos(lim * step)
    lbi = mag * jnp.sin(lim * step)
    den = lre * lre + lim * lim
    nr = lbr - 1.0
    ni = lbi
    cre = (nr * lre + ni * lim) / den
    cim = (ni * lre - nr * lim) / den
    bbr = jnp.stack([cre[d:d + 1] * bre - cim[d:d + 1] * bim for d in range(N_DIR)])
    bbi = jnp.stack([cre[d:d + 1] * bim + cim[d:d + 1] * bre for d in range(N_DIR)])
    return lbr, lbi, bbr, bbi


def s5_prep(lre, lim, lstep, bre, bim):
    ns = lre.shape[1]

    def body(lre_r, lim_r, ls_r, bre_r, bim_r, lbr_r, lbi_r, bbr_r, bbi_r):
        lbr, lbi, bbr, bbi = _s5_prep_fn(lre_r[...], lim_r[...], ls_r[...], bre_r[...], bim_r[...])
        lbr_r[...] = lbr
        lbi_r[...] = lbi
        bbr_r[...] = bbr
        bbi_r[...] = bbi

    return pl.pallas_call(
        body, name="s5_prep",
        out_shape=[jax.ShapeDtypeStruct((N_DIR, ns), F32)] * 2 + [jax.ShapeDtypeStruct((N_DIR, S5_GROUP, ns), F32)] * 2,
        compiler_params=_cparams(),
    )(lre, lim, lstep, bre, bim)


def s5_prep_bwd(lre, lim, lstep, bre, bim, dlbr, dlbi, dbbr, dbbi):
    ns = lre.shape[1]

    def body(lre_r, lim_r, ls_r, bre_r, bim_r, d1, d2, d3, d4, o1, o2, o3, o4, o5):
        _, vjp = jax.vjp(_s5_prep_fn, lre_r[...], lim_r[...], ls_r[...], bre_r[...], bim_r[...])
        g = vjp((d1[...], d2[...], d3[...], d4[...]))
        for o, v in zip((o1, o2, o3, o4, o5), g):
            o[...] = v

    return pl.pallas_call(
        body, name="s5_prep_bwd",
        out_shape=[jax.ShapeDtypeStruct((N_DIR, ns), F32)] * 3 + [jax.ShapeDtypeStruct((S5_GROUP, ns), F32)] * 2,
        compiler_params=_cparams(),
    )(lre, lim, lstep, bre, bim, dlbr, dlbi, dbbr, dbbi)


def s5_scan(bre, bim, lre, lim, *, reverse, name):
    T, NS = bre.shape
    tt = _tile(T, S5_SCAN_ROWS, SUBLANE)
    wl = _tile(NS, 512)
    nT = T // tt
    ngrp = tt // SUBLANE

    def tmap(j, i):
        return ((nT - 1 - i) if reverse else i, j)

    def body(bre_r, bim_r, lre_r, lim_r, sre_r, sim_r, cre, cim):
        @pl.when(pl.program_id(1) == 0)
        def _():
            cre[...] = jnp.zeros_like(cre)
            cim[...] = jnp.zeros_like(cim)

        lr = jnp.broadcast_to(lre_r[...], (SUBLANE, wl))
        li = jnp.broadcast_to(lim_r[...], (SUBLANE, wl))
        row = lax.broadcasted_iota(jnp.int32, (SUBLANE, wl), 0)
        pows = [(lr, li)]
        for _ in range(3):
            pr, pi = pows[-1]
            pows.append((pr * pr - pi * pi, 2.0 * pr * pi))
        e = (SUBLANE - row) if reverse else (row + 1)
        Pr = jnp.ones((SUBLANE, wl), F32)
        Pi = jnp.zeros((SUBLANE, wl), F32)
        for bit, (qr, qi) in enumerate(pows):
            on = ((e >> bit) & 1) == 1
            nr, ni = Pr * qr - Pi * qi, Pr * qi + Pi * qr
            Pr, Pi = jnp.where(on, nr, Pr), jnp.where(on, ni, Pi)

        def group(g, carry):
            gg = (ngrp - 1 - g) if reverse else g
            rows = pl.ds(pl.multiple_of(gg * SUBLANE, SUBLANE), SUBLANE)
            sr, si = bre_r[rows, :], bim_r[rows, :]
            for lvl, k in enumerate((1, 2, 4)):
                qr, qi = pows[lvl]
                if reverse:
                    shr = pltpu.roll(sr, SUBLANE - k, 0)
                    shi = pltpu.roll(si, SUBLANE - k, 0)
                    keep = row < SUBLANE - k
                else:
                    shr = pltpu.roll(sr, k, 0)
                    shi = pltpu.roll(si, k, 0)
                    keep = row >= k
                shr = jnp.where(keep, shr, 0.0)
                shi = jnp.where(keep, shi, 0.0)
                sr, si = sr + qr * shr - qi * shi, si + qr * shi + qi * shr
            cr, ci = cre[...], cim[...]
            sr, si = sr + Pr * cr - Pi * ci, si + Pr * ci + Pi * cr
            sre_r[rows, :] = sr
            sim_r[rows, :] = si
            last = 0 if reverse else SUBLANE - 1
            cre[...] = jnp.broadcast_to(sr[last:last + 1, :], (SUBLANE, wl))
            cim[...] = jnp.broadcast_to(si[last:last + 1, :], (SUBLANE, wl))
            return carry

        lax.fori_loop(0, ngrp, group, 0)

    blk = pl.BlockSpec((tt, wl), tmap)
    row_spec = pl.BlockSpec((1, wl), lambda j, i: (0, j))
    return pl.pallas_call(
        body, name=name, grid=(NS // wl, nT),
        in_specs=[blk, blk, row_spec, row_spec], out_specs=[blk, blk],
        out_shape=[jax.ShapeDtypeStruct((T, NS), F32)] * 2,
        scratch_shapes=[pltpu.VMEM((SUBLANE, wl), F32)] * 2,
        compiler_params=_cparams(("parallel", "arbitrary")),
    )(bre, bim, lre, lim)


def s5_dlam(sre, sim, gre, gim, *, reverse, name):
    T, NS = sre.shape
    wl = _tile(NS, 256)

    def body(sr_r, si_r, gr_r, gi_r, dr_r, di_r):
        row = lax.broadcasted_iota(jnp.int32, (T, wl), 0)
        if reverse:
            keep = row < T - 1
            pr = jnp.where(keep, pltpu.roll(sr_r[...], T - 1, 0), 0.0)
            pi = jnp.where(keep, pltpu.roll(si_r[...], T - 1, 0), 0.0)
        else:
            keep = row >= 1
            pr = jnp.where(keep, pltpu.roll(sr_r[...], 1, 0), 0.0)
            pi = jnp.where(keep, pltpu.roll(si_r[...], 1, 0), 0.0)
        gr, gi = gr_r[...], gi_r[...]
        dr_r[...] = jnp.sum(pr * gr + pi * gi, axis=0, keepdims=True)
        di_r[...] = jnp.sum(pr * gi - pi * gr, axis=0, keepdims=True)

    blk = pl.BlockSpec((T, wl), lambda j: (0, j))
    o = pl.BlockSpec((1, wl), lambda j: (0, j))
    return pl.pallas_call(
        body, name=name, grid=(NS // wl,), in_specs=[blk] * 4, out_specs=[o, o],
        out_shape=[jax.ShapeDtypeStruct((1, NS), F32)] * 2,
        compiler_params=_cparams(("parallel",)),
    )(sre, sim, gre, gim)


def _s5_bu_fn(u, wblk):
    nch = wblk.shape[0]
    cw = S5_CHUNK_GROUPS * S5_GROUP
    sw = S5_CHUNK_GROUPS * S5_STATE
    parts = [[] for _ in range(4)]
    for ch in range(nch):
        res = _mm(u[:, ch * cw:(ch + 1) * cw], wblk[ch])
        for q in range(4):
            parts[q].append(res[:, q * sw:(q + 1) * sw])
    return tuple(jnp.concatenate(p, axis=1) if nch > 1 else p[0] for p in parts)


def _s5_out_fn(x0r, x0i, x1r, x1i, u, cre, cim, dsk, wglu, bglu):
    xr, xi = x0r + x1r, x0i + x1i
    nch = cre.shape[0]
    sw = S5_CHUNK_GROUPS * S5_STATE
    ys = [_mm(xr[:, ch * sw:(ch + 1) * sw], cre[ch]) - _mm(xi[:, ch * sw:(ch + 1) * sw], cim[ch]) for ch in range(nch)]
    y = jnp.concatenate(ys, axis=1) if nch > 1 else ys[0]
    z = _gelu_tanh(y + dsk * u)
    gate = _sigmoid(_mm(z, wglu) + bglu)
    return z * gate


def _rk_dims():
    C = D_MODEL // 2
    LW = N_DIR * DECAY_LORA
    GP = _round_up(GATE_LORA, LANE)
    return C, LW, GP


def _rk_pre_fn(ps, w0, wup0, wup1, a0, aup0, aup1, gup, k_k, k_a):
    C, LW, GP = _rk_dims()
    r, k, v = ps[:, 0:C], ps[:, C:2 * C], ps[:, 2 * C:3 * C]
    wdn = ps[:, 3 * C:3 * C + LW]
    adn = ps[:, 3 * C + LW:3 * C + 2 * LW]
    gdn = ps[:, 3 * C + 2 * LW:3 * C + 2 * LW + GP]
    s, st = _seg_mats(C, RWKV_HEAD)
    kk = k * k_k
    n2 = _segsum(kk * kk, s)
    n2 = jnp.where(n2 > 0.0, n2, 1.0)
    inv = 1.0 / jnp.maximum(jnp.sqrt(n2), L2_EPS)
    kkn = kk * _segsum(inv, st)
    tw = jnp.tanh(wdn)
    wup, aup = (wup0, wup1), (aup0, aup1)
    ws, ks, bs = [], [], []
    for d in range(N_DIR):
        wraw = w0[d:d + 1] + _mm(tw, wup[d])
        w = -_softplus(-wraw) - 0.5
        ws.append(jnp.exp(-jnp.exp(w)))
        a = _sigmoid(a0[d:d + 1] + _mm(adn, aup[d]))
        ks.append(k * (1.0 + (a - 1.0) * k_a))
        bs.append(kkn * a)
    g = _mm(_sigmoid(gdn), gup)
    return r, v, kkn, ws[0], ws[1], ks[0], ks[1], bs[0], bs[1], g


def _rk_post_fn(y0, y1, r, v, k0, k1, g, r_k, lng, lnb):
    C = r.shape[1]
    s, st = _seg_mats(C, RWKV_HEAD)
    y = y0 + y1
    mu = _segsum(_segsum(y, s) * (1.0 / RWKV_HEAD), st)
    yc = y - mu
    var = _segsum(_segsum(yc * yc, s) * (1.0 / RWKV_HEAD), st)
    yn = yc * lax.rsqrt(var + GN_EPS) * lng + lnb
    bonus = _segsum(_segsum(r * (k0 + k1) * r_k, s), st)
    return (yn + bonus * v) * g


def rk_shift(proj, mp, mn, col0):
    T = proj.shape[0]
    W = mp.shape[1]
    wl = _tile(math.gcd(W, col0), 256)
    cb0 = col0 // wl

    def body(p_r, mp_r, mn_r, o_r):
        p = p_r[...]
        row = lax.broadcasted_iota(jnp.int32, (T, wl), 0)
        prev = jnp.where(row >= 1, pltpu.roll(p, 1, 0), 0.0)
        nxt = jnp.where(row < T - 1, pltpu.roll(p, T - 1, 0), 0.0)
        o_r[...] = p + mp_r[...] * (prev - p) + mn_r[...] * (nxt - p)

    rs = pl.BlockSpec((1, wl), lambda j: (0, j))
    return pl.pallas_call(
        body, name="rk_shift", grid=(W // wl,),
        in_specs=[pl.BlockSpec((T, wl), lambda j: (0, cb0 + j)), rs, rs],
        out_specs=pl.BlockSpec((T, wl), lambda j: (0, j)),
        out_shape=jax.ShapeDtypeStruct((T, W), F32),
        compiler_params=_cparams(("parallel",)),
    )(proj, mp, mn)


def rk_shift_bwd(dps, proj, mp, mn, col0):
    T, W = dps.shape
    wl = _tile(math.gcd(W, col0), 256)
    cb0 = col0 // wl

    def body(d_r, p_r, mp_r, mn_r, dp_r, dmp_r, dmn_r):
        d, p = d_r[...], p_r[...]
        mpv, mnv = mp_r[...], mn_r[...]
        row = lax.broadcasted_iota(jnp.int32, (T, wl), 0)
        first, last = row >= 1, row < T - 1
        prev = jnp.where(first, pltpu.roll(p, 1, 0), 0.0)
        nxt = jnp.where(last, pltpu.roll(p, T - 1, 0), 0.0)
        dmp_r[...] = jnp.sum(d * (prev - p), axis=0, keepdims=True)
        dmn_r[...] = jnp.sum(d * (nxt - p), axis=0, keepdims=True)
        dp_r[...] = (d * (1.0 - mpv - mnv) + jnp.where(last, pltpu.roll(d * mpv, T - 1, 0), 0.0)
                     + jnp.where(first, pltpu.roll(d * mnv, 1, 0), 0.0))

    rs = pl.BlockSpec((1, wl), lambda j: (0, j))
    blk = pl.BlockSpec((T, wl), lambda j: (0, j))
    return pl.pallas_call(
        body, name="rk_shift_bwd", grid=(W // wl,),
        in_specs=[blk, pl.BlockSpec((T, wl), lambda j: (0, cb0 + j)), rs, rs],
        out_specs=[blk, rs, rs],
        out_shape=[jax.ShapeDtypeStruct((T, W), F32), jax.ShapeDtypeStruct((1, W), F32), jax.ShapeDtypeStruct((1, W), F32)],
        compiler_params=_cparams(("parallel",)),
    )(dps, proj, mp, mn)


RK_FWD_PAIRS = 8
RK_BWD_PAIRS = 4
RK_SPREAD_PAIRS = 2
RK_TIME_BLOCK = 32
RK_LANE_BLOCK = 128


def _rk_blocks(T, C, N, order_reversed, pairs):
    pw = 2 * N
    pp = min(pairs, C // pw)
    tb = min(RK_TIME_BLOCK, T)
    lb = min(RK_LANE_BLOCK, T)
    nb, per = T // tb, lb // tb

    def tix(i):
        return (nb - 1 - i) if order_reversed else i

    rows = pl.BlockSpec((tb, pp * pw), lambda g, i: (tix(i), g))
    cols = pl.BlockSpec((2 * pp, N, lb), lambda g, i: (g, 0, tix(i) // per))
    hist = pl.BlockSpec((tb, pp, N, pw), lambda g, i: (tix(i), g, 0, 0))
    return pp, pw, tb, lb, nb, per, tix, rows, cols, hist


def rk_spread(xT, name):
    H, N, T = xT.shape
    pw = 2 * N
    lb = min(RK_LANE_BLOCK, T)
    pp = min(RK_SPREAD_PAIRS, H // 2)

    def body(x_r, o_r):
        lane = lax.broadcasted_iota(jnp.int32, (N, lb), 1)
        first = lax.broadcasted_iota(jnp.int32, (N, pw), 1) < N
        tiles = [x_r[h] for h in range(2 * pp)]

        def step(t, carry):
            for p in range(pp):
                c = [jnp.sum(jnp.where(lane == t, tiles[2 * p + q], 0.0), axis=1, keepdims=True) for q in range(2)]
                o_r[t, p] = jnp.where(first, c[0], c[1])
            return carry

        lax.fori_loop(0, lb, step, 0, unroll=8)

    return pl.pallas_call(
        body, name=name, grid=(H // (2 * pp), T // lb),
        in_specs=[pl.BlockSpec((2 * pp, N, lb), lambda g, i: (g, 0, i))],
        out_specs=pl.BlockSpec((lb, pp, N, pw), lambda g, i: (i, g, 0, 0)),
        out_shape=jax.ShapeDtypeStruct((T, H // 2, N, pw), F32),
        compiler_params=_cparams(("parallel", "parallel")),
    )(xT)


def _half_sums(x, first):
    return (jnp.sum(jnp.where(first, x, 0.0), axis=1, keepdims=True),
            jnp.sum(jnp.where(first, 0.0, x), axis=1, keepdims=True))


def rk_scan(r, kk, w, k, b, vc, *, reverse, name):
    T, C = r.shape
    N = vc.shape[2]
    H = C // N
    pp, pw, tb, lb, nb, per, tix, rows, cols, hist_spec = _rk_blocks(T, C, N, reverse, RK_FWD_PAIRS)

    def body(r_r, kk_r, w_r, k_r, b_r, VC, yT_r, hist_r, S, YA):
        i = pl.program_id(1)

        @pl.when(i == 0)
        def _():
            S[...] = jnp.zeros_like(S)

        @pl.when(i % per == 0)
        def _():
            yT_r[...] = jnp.zeros_like(yT_r)

        off = (tix(i) % per) * tb
        first = lax.broadcasted_iota(jnp.int32, (N, pw), 1) < N
        YA[...] = jnp.zeros_like(YA)
        st = [S[p] for p in range(pp)]
        seg = [slice(p * pw, (p + 1) * pw) for p in range(pp)]
        for s in range(tb):
            t = (tb - 1 - s) if reverse else s
            row = slice(t, t + 1)
            sk = [_half_sums(st[p] * kk_r[row, seg[p]], first) for p in range(pp)]
            for p in range(pp):
                hist_r[t, p] = st[p]
                skp = jnp.where(first, sk[p][0], sk[p][1])
                st[p] = st[p] * w_r[row, seg[p]] - skp * b_r[row, seg[p]] + VC[t, p] * k_r[row, seg[p]]
            ys = [_half_sums(st[p] * r_r[row, seg[p]], first) for p in range(pp)]
            for p in range(pp):
                for q in range(2):
                    YA[2 * p + q, :, row] = ys[p][q]
        for p in range(pp):
            S[p] = st[p]
        for h in range(2 * pp):
            yT_r[h] = yT_r[h] + (pltpu.roll(YA[h], off, 1) if per > 1 else YA[h])

    return pl.pallas_call(
        body, name=name, grid=(C // (pp * pw), nb),
        in_specs=[rows] * 5 + [hist_spec], out_specs=[cols, hist_spec],
        out_shape=[jax.ShapeDtypeStruct((H, N, T), F32), jax.ShapeDtypeStruct((T, H // 2, N, pw), F32)],
        scratch_shapes=[pltpu.VMEM((pp, N, pw), F32), pltpu.VMEM((2 * pp, N, lb), F32)],
        compiler_params=_cparams(("parallel", "arbitrary")),
    )(r, kk, w, k, b, vc)


def rk_scan_bwd(r, kk, w, k, b, vc, dc, hist, *, reverse, name):
    T, C = r.shape
    N = vc.shape[2]
    H = C // N
    pp, pw, tb, lb, nb, per, tix, rows, cols, hist_spec = _rk_blocks(T, C, N, not reverse, RK_BWD_PAIRS)

    def body(r_r, kk_r, w_r, k_r, b_r, VC, DC, hist_r, dr_r, dkk_r, dw_r, dk_r, db_r, dvT_r, G, YA):
        i = pl.program_id(1)

        @pl.when(i == 0)
        def _():
            G[...] = jnp.zeros_like(G)

        @pl.when(i % per == 0)
        def _():
            dvT_r[...] = jnp.zeros_like(dvT_r)

        off = (tix(i) % per) * tb
        first = lax.broadcasted_iota(jnp.int32, (N, pw), 1) < N
        YA[...] = jnp.zeros_like(YA)
        gs = [G[p] for p in range(pp)]
        seg = [slice(p * pw, (p + 1) * pw) for p in range(pp)]
        for s in range(tb):
            t = s if reverse else (tb - 1 - s)
            row = slice(t, t + 1)
            g = [gs[p] + DC[t, p] * r_r[row, seg[p]] for p in range(pp)]
            sk = [_half_sums(hist_r[t, p] * kk_r[row, seg[p]], first) for p in range(pp)]
            gb = [_half_sums(g[p] * b_r[row, seg[p]], first) for p in range(pp)]
            gk = [_half_sums(g[p] * k_r[row, seg[p]], first) for p in range(pp)]
            for p in range(pp):
                sp = hist_r[t, p]
                kkv, wv, kv, bv = kk_r[row, seg[p]], w_r[row, seg[p]], k_r[row, seg[p]], b_r[row, seg[p]]
                vcol, dycol = VC[t, p], DC[t, p]
                sa = -jnp.where(first, sk[p][0], sk[p][1])
                dsa = jnp.where(first, gb[p][0], gb[p][1])
                sn = sp * wv + sa * bv + vcol * kv
                dr_r[row, seg[p]] = jnp.sum(sn * dycol, axis=0, keepdims=True)
                dw_r[row, seg[p]] = jnp.sum(g[p] * sp, axis=0, keepdims=True)
                db_r[row, seg[p]] = jnp.sum(g[p] * sa, axis=0, keepdims=True)
                dk_r[row, seg[p]] = jnp.sum(g[p] * vcol, axis=0, keepdims=True)
                dkk_r[row, seg[p]] = -jnp.sum(sp * dsa, axis=0, keepdims=True)
                gs[p] = g[p] * wv - dsa * kkv
                for q in range(2):
                    YA[2 * p + q, :, row] = gk[p][q]
        for p in range(pp):
            G[p] = gs[p]
        for h in range(2 * pp):
            dvT_r[h] = dvT_r[h] + (pltpu.roll(YA[h], off, 1) if per > 1 else YA[h])

    return pl.pallas_call(
        body, name=name, grid=(C // (pp * pw), nb),
        in_specs=[rows] * 5 + [hist_spec] * 3, out_specs=[rows] * 5 + [cols],
        out_shape=[jax.ShapeDtypeStruct((T, C), F32)] * 5 + [jax.ShapeDtypeStruct((H, N, T), F32)],
        scratch_shapes=[pltpu.VMEM((pp, N, pw), F32), pltpu.VMEM((2 * pp, N, lb), F32)],
        compiler_params=_cparams(("parallel", "arbitrary")),
    )(r, kk, w, k, b, vc, dc, hist)


def adam(parts, w, m, v, name):
    P, R, C = parts.shape
    tr = _tile(R, max(SUBLANE, (1 << 19) // max(C, 1) // SUBLANE * SUBLANE), SUBLANE)
    c1 = 1.0 / (1.0 - ADAM_B1 ** ADAM_STEP)
    c2 = 1.0 / (1.0 - ADAM_B2 ** ADAM_STEP)

    def body(p_r, w_r, m_r, v_r, g_o, d_o, m_o, v_o):
        g = p_r[0].astype(F32)
        for q in range(1, P):
            g = g + p_r[q].astype(F32)
        m2 = ADAM_B1 * m_r[...] + (1.0 - ADAM_B1) * g
        v2 = ADAM_B2 * v_r[...] + (1.0 - ADAM_B2) * (g * g)
        g_o[...] = g
        m_o[...] = m2
        v_o[...] = v2
        d_o[...] = -ADAM_LR * ((m2 * c1) / (jnp.sqrt(v2 * c2) + ADAM_EPS) + ADAM_WD * w_r[...])

    blk = pl.BlockSpec((tr, C), lambda i: (i, 0))
    return pl.pallas_call(
        body, name=name, grid=(R // tr,),
        in_specs=[pl.BlockSpec((P, tr, C), lambda i: (0, i, 0)), blk, blk, blk], out_specs=[blk] * 4,
        out_shape=[jax.ShapeDtypeStruct((R, C), F32)] * 4,
        compiler_params=_cparams(("parallel",)),
    )(parts, w, m, v)


def _pack_flat(arrs):
    rows = []
    for a in arrs:
        f = a.reshape(-1).astype(F32)
        n = _round_up(f.shape[0], SUBLANE * LANE)
        rows.append(jnp.pad(f, (0, n - f.shape[0])).reshape(-1, LANE))
    return jnp.concatenate(rows, axis=0)


def _unpack_flat(packed, shapes):
    out, r0 = [], 0
    for s in shapes:
        n = math.prod(s)
        nr = _round_up(n, SUBLANE * LANE) // LANE
        out.append(packed[r0:r0 + nr].reshape(-1)[:n].reshape(s))
        r0 += nr
    return out


def _pack_rows(arrs):
    rows = []
    for a in arrs:
        f = a.reshape(-1, a.shape[-1]).astype(F32)
        n = _round_up(f.shape[0], SUBLANE)
        rows.append(jnp.pad(f, ((0, n - f.shape[0]), (0, 0))))
    return jnp.concatenate(rows, axis=0)


def _unpack_rows(packed, shapes):
    out, r0 = [], 0
    for s in shapes:
        nr = math.prod(s[:-1])
        out.append(packed[r0:r0 + nr].reshape(s))
        r0 += _round_up(nr, SUBLANE)
    return out


def _cols_from_slots(g):
    return jnp.moveaxis(g, 0, -2).reshape(g.shape[1:-1] + (N_DEV * g.shape[-1],))


def _cols_to_slots(a):
    cs = a.shape[-1] // N_DEV
    return jnp.moveaxis(a.reshape(a.shape[:-1] + (N_DEV, cs)), -2, 0)


def _step(P, M, V):
    D, T = D_MODEL, SEQ
    S5W = D // 2
    C, LW, GP = _rk_dims()
    H = C // RWKV_HEAD
    G = S5W // S5_GROUP
    NS = G * S5_STATE
    NCH = G // S5_CHUNK_GROUPS
    SW = S5_CHUNK_GROUPS * S5_STATE
    RIN = 3 * C + 2 * LW + GATE_LORA
    RINP = 3 * C + 2 * LW + GP
    PROJ = S5W + RIN
    PROJP = S5W + RINP
    FF = 4 * D
    me = 4 * lax.axis_index("x") + 2 * lax.axis_index("y") + lax.axis_index("c")
    cs_mod = N_MOD * D // N_DEV
    eye = jnp.eye(S5_CHUNK_GROUPS, dtype=F32)

    x = P['x'][0]
    target = P['loss_target'][0]

    (c_all,) = exchange([P['c']], ['gather'], "comm_gather_c")
    c_all = c_all.reshape(N_DEV, D)
    (c_act,), _ = rowcall(lambda cv: ((cv * _sigmoid(cv),), ()), "silu_c", [c_all], [], [(D, F32)], [], N_DEV)
    ada_b_loc = lax.dynamic_slice(P['ada_b'], (0, me * cs_mod), (1, cs_mod))
    mod_loc = matmul(c_act, P['ada_w'][0], name="mod_mm", precise=True, extras=[(ada_b_loc, 'n')],
                     epi=lambda acc, bias: (acc + bias,))

    rk_shapes = [P[n][0].shape for n in RKPACK]
    rk_pack = _pack_rows([P[n][0] for n in RKPACK])
    gathered = gather_two_level(
        [mod_loc, P['w_in'][0].astype(BF16), P['w_out'][0].astype(BF16), P['ffn_w1'][0].astype(BF16),
         P['ffn_w2'][0].astype(BF16), P['s5_w_glu'][0].astype(BF16), rk_pack], "comm_gather_weights")
    mod_all, w_in_g, w_out_g, w1_g, w2_g, wglu_g, rk_g = gathered
    mod_me = lax.dynamic_index_in_dim(mod_all, me, axis=1, keepdims=False).reshape(N_MOD, 1, D)
    shift1, scale1, gate1, shift2, scale2, gate2 = (mod_me[i] for i in range(N_MOD))
    w_in = jnp.pad(_cols_from_slots(w_in_g), ((0, 0), (0, PROJP - PROJ)))
    w_out = w_out_g.reshape(D, D)
    w2 = w2_g.reshape(FF, D)
    wglu = wglu_g.reshape(S5W, S5W)
    rk_full = _unpack_rows(_cols_from_slots(rk_g), [s[:-1] + (C,) for s in rk_shapes])
    rk_w0, rk_a0, rk_wup, rk_aup, rk_gup = rk_full

    def lora_pad(up):
        z = jnp.zeros((N_DIR, LW, C), F32)
        for d in range(N_DIR):
            z = z.at[d, d * DECAY_LORA:(d + 1) * DECAY_LORA].set(up[d])
        return z

    wup_p, aup_p = lora_pad(rk_wup), lora_pad(rk_aup)
    gup_p = jnp.pad(rk_gup, ((0, GP - GATE_LORA), (0, 0)))
    mu_prev = jnp.pad(P['rk_shift_prev'], ((0, 0), (0, RINP - RIN)))
    mu_next = jnp.pad(P['rk_shift_next'], ((0, 0), (0, RINP - RIN)))
    r_k = P['rk_r_k'].reshape(1, C)
    fgain = P['final_gain'].reshape(1, D)

    TT = 256
    (h1,), _ = rowcall(lambda xv, g, sc, sh: ((_normmod(xv, g, sc, sh),), ()), "norm1",
                       [x], [P['norm1_gain'], scale1, shift1], [(D, BF16)], [], TT)
    proj = matmul(h1, w_in, name="proj_mm")

    lre = P['s5_lambda_re'][0].reshape(N_DIR, NS)
    lim = P['s5_lambda_im'][0].reshape(N_DIR, NS)
    lstep = jnp.broadcast_to(P['s5_log_step'][0][:, :, None], (N_DIR, G, S5_STATE)).reshape(N_DIR, NS)
    bre = P['s5_b_re'][0].reshape(NS, S5_GROUP).T
    bim = P['s5_b_im'][0].reshape(NS, S5_GROUP).T
    lbr, lbi, bbr, bbi = s5_prep(lre, lim, lstep, bre, bim)
    bbar = jnp.stack([bbr, bbi], axis=1).reshape(N_DIR, 2, S5_GROUP, NCH, S5_CHUNK_GROUPS, S5_STATE)
    wblk = jnp.einsum('drhcgp,gk->cghdrkp', bbar, eye).reshape(NCH, S5_CHUNK_GROUPS * S5_GROUP, 4 * SW)
    wblk = wblk.astype(MXU_DTYPE)
    u_view = (proj, S5W, 0)
    bus, _ = rowcall(lambda uv, wb: (_s5_bu_fn(uv, wb), ()), "s5_bu", [u_view], [wblk], [(NS, F32)] * 4, [], TT)
    s0r, s0i = s5_scan(bus[0], bus[1], lbr[0:1], lbi[0:1], reverse=False, name="s5_scan_f0")
    s1r, s1i = s5_scan(bus[2], bus[3], lbr[1:2], lbi[1:2], reverse=True, name="s5_scan_f1")

    def cblk(cm):
        c4 = cm.reshape(NCH, S5_CHUNK_GROUPS, S5_GROUP, S5_STATE)
        return jnp.einsum('cghp,gk->cgpkh', c4, eye).reshape(NCH, SW, S5_CHUNK_GROUPS * S5_GROUP)

    cre_b = cblk(P['s5_c_re'][0]).astype(MXU_DTYPE)
    cim_b = cblk(P['s5_c_im'][0]).astype(MXU_DTYPE)
    s5_full = [cre_b, cim_b, P['s5_d'], wglu, P['s5_b_glu']]
    TS = 128
    (y_s5,), _ = rowcall(lambda *a: ((_s5_out_fn(*a),), ()), "s5_out", [s0r, s0i, s1r, s1i, u_view], s5_full,
                         [(S5W, BF16)], [], TS)

    ps = rk_shift(proj, mu_prev, mu_next, S5W)
    pre_full = [rk_w0, wup_p[0], wup_p[1], rk_a0, aup_p[0], aup_p[1], gup_p, P['rk_k_k'], P['rk_k_a']]
    pre_out, _ = rowcall(lambda *a: (_rk_pre_fn(*a)[2:], ()), "rk_pre", [ps], pre_full, [(C, F32)] * 8, [], TS)
    kkn, w_0, w_1, k_0, k_1, b_0, b_1, g_gate = pre_out
    r_t, v_t = ps[:, 0:C], ps[:, 2 * C:3 * C]

    def hmT(a):
        return a.reshape(T, H, RWKV_HEAD).transpose(1, 2, 0)

    def unT(a):
        return a.transpose(2, 0, 1).reshape(T, C)

    v_cols = rk_spread(hmT(v_t), "rk_spread_v")
    dir_rows = [(w_0, k_0, b_0), (w_1, k_1, b_1)]
    yT, hist = [], []
    for d in range(N_DIR):
        wd, kd, bd = dir_rows[d]
        yd, hd = rk_scan(r_t, kkn, wd, kd, bd, v_cols, reverse=(d == 1), name=f"rk_scan_f{d}")
        yT.append(yd)
        hist.append(hd)
    y_0, y_1 = unT(yT[0]), unT(yT[1])
    post_full = [r_k, P['rk_ln_gain'], P['rk_ln_bias']]
    post_tiled = [y_0, y_1, (ps, C, 0), (ps, C, 2), k_0, k_1, g_gate]
    (y_rk,), _ = rowcall(lambda *a: ((_rk_post_fn(*a),), ()), "rk_post", post_tiled, post_full, [(C, BF16)], [], TS)

    ycat = jnp.concatenate([y_s5, y_rk], axis=1)
    mixed = matmul(ycat, w_out, name="out_mm")

    def res_norm(xv, mv, gate, g, sc, sh):
        x1v = xv + gate * mv
        return x1v, _normmod(x1v, g, sc, sh)

    (x1, h2), _ = rowcall(lambda *a: (res_norm(*a), ()), "norm2", [x, mixed], [gate1, P['norm2_gain'], scale2, shift2],
                          [(D, F32), (D, BF16)], [], TT)
    a_ff, hh_ff = matmul(h2, w1_g, name="ffn1_mm", b_slots=True, out_dtypes=(F32, BF16),
                         epi=lambda acc: (acc, jnp.square(jnp.maximum(acc, 0.0))))
    ffn = matmul(hh_ff, w2, name="ffn2_mm")

    def loss_fn(x1v, fv, tg, gate, fg):
        def f(x1_, f_, gate_, fg_):
            out = _rms(x1_ + gate_ * f_, fg_)
            err = out - tg
            return 0.5 * jnp.sum(jnp.sum(err * err, axis=1, keepdims=True), axis=0, keepdims=True) * (1.0 / D)
        lv, vjp = jax.vjp(f, x1v, fv, gate, fg)
        dx1, dff, dgate, dfg = vjp(jnp.ones((1, 1), F32))
        return (dx1, dff), (jnp.broadcast_to(lv, (SUBLANE, LANE)), dgate, dfg)

    (dx2, dffn), (loss_t, dgate2, dfgain) = rowcall(
        loss_fn, "loss", [x1, ffn, target], [gate2, fgain], [(D, F32), (D, BF16)], [(SUBLANE, LANE), (1, D), (1, D)], TT)
    loss = lax.psum(loss_t[0, 0], MESH_AXES)

    da = matmul(dffn, w2, name="dffn2_mm", tb=True, out_dtypes=(BF16,), extras=[(a_ff, 'mn')],
                epi=lambda acc, av: (acc * (2.0 * jnp.maximum(av, 0.0)),))
    g_w2 = matmul(hh_ff, dffn, name="gw2_mm", ta=True, out_dtypes=(BF16,))
    dh2 = matmul(da, w1_g, name="dh2_mm", tb=True, b_slots=True)
    g_w1 = matmul(h2, da, name="gw1_mm", ta=True, out_slots=N_DEV, out_dtypes=(BF16,))

    def res_norm_bwd(dx2v, dh2v, xv, mv, gate, g, sc, sh):
        _, vjp = jax.vjp(res_norm, xv, mv, gate, g, sc, sh)
        dx, dm, dgate, dg, dsc, dsh = vjp((dx2v, dh2v))
        return (dx, dm), (dgate, dg, dsc, dsh)

    (dx1, dmixed), (dgate1, dgain2, dscale2, dshift2) = rowcall(
        res_norm_bwd, "norm2_bwd", [dx2, dh2, x, mixed], [gate1, P['norm2_gain'], scale2, shift2],
        [(D, F32), (D, BF16)], [(1, D)] * 4, TT)

    dycat = matmul(dmixed, w_out, name="dycat_mm", tb=True)
    g_wout = matmul(ycat, dmixed, name="gwout_mm", ta=True, out_dtypes=(BF16,))

    def post_bwd(dy, *a):
        _, vjp = jax.vjp(_rk_post_fn, *a)
        gy0, gy1, gr, gv, gk0, gk1, gg, grk, glg, glb = vjp(dy)
        return (gy0, gr, gv, gk0, gk1, gg), (grk, glg, glb)

    cb_rk = S5W // C if C else 0
    (dy_rk, dr_p, dv_p, dk0_p, dk1_p, dg_p), (g_rk_rk, g_lng, g_lnb) = rowcall(
        post_bwd, "rk_post_bwd", [(dycat, C, cb_rk)] + post_tiled, post_full, [(C, F32)] * 6, [(1, C)] * 3, TS)
    dy_cols = rk_spread(hmT(dy_rk), "rk_spread_dy")
    scan_g = []
    for d in range(N_DIR):
        wd, kd, bd = dir_rows[d]
        scan_g.append(rk_scan_bwd(r_t, kkn, wd, kd, bd, v_cols, dy_cols, hist[d], reverse=(d == 1), name=f"rk_scan_b{d}"))
    cot = [dr_p, scan_g[0][0], scan_g[1][0],
           dv_p, unT(scan_g[0][5]), unT(scan_g[1][5]),
           scan_g[0][1], scan_g[1][1],
           scan_g[0][2], scan_g[1][2],
           dk0_p, scan_g[0][3], dk1_p, scan_g[1][3],
           scan_g[0][4], scan_g[1][4],
           dg_p]

    def pre_bwd(psv, r0, r1, r2, v0, v1, v2, q0, q1, dw0, dw1, k0a, k0b, k1a, k1b, db0, db1, dgv, *params):
        _, vjp = jax.vjp(_rk_pre_fn, psv, *params)
        grads = vjp((r0 + r1 + r2, v0 + v1 + v2, q0 + q1, dw0, dw1, k0a + k0b, k1a + k1b, db0, db1, dgv))
        return (grads[0],), tuple(grads[1:])

    (dps,), pre_g = rowcall(pre_bwd, "rk_pre_bwd", [ps] + cot, pre_full, [(RINP, F32)],
                            [f.shape for f in pre_full], TS)
    g_w0, g_wup0, g_wup1, g_a0, g_aup0, g_aup1, g_gup_p, g_kk, g_ka = pre_g
    g_wup_p, g_aup_p = jnp.stack([g_wup0, g_wup1]), jnp.stack([g_aup0, g_aup1])
    dp_rk, g_mup, g_mun = rk_shift_bwd(dps, proj, mu_prev, mu_next, S5W)

    def s5_out_bwd(dy, *a):
        a = [t.astype(F32) for t in a]
        _, vjp = jax.vjp(_s5_out_fn, *a)
        g = vjp(dy)
        return (g[0], g[1], g[4]), tuple(g[5:])

    (dxr, dxi, du_a), s5_pg = rowcall(
        s5_out_bwd, "s5_out_bwd", [(dycat, S5W, 0), s0r, s0i, s1r, s1i, u_view], s5_full,
        [(NS, F32), (NS, F32), (S5W, F32)], [f.shape for f in s5_full], TS)
    g_creb, g_cimb, g_s5d, g_wglu, g_bglu = s5_pg
    l0r, l0i = s5_scan(dxr, dxi, lbr[0:1], -lbi[0:1], reverse=True, name="s5_scan_b0")
    l1r, l1i = s5_scan(dxr, dxi, lbr[1:2], -lbi[1:2], reverse=False, name="s5_scan_b1")
    dl0r, dl0i = s5_dlam(s0r, s0i, l0r, l0i, reverse=False, name="s5_dlam0")
    dl1r, dl1i = s5_dlam(s1r, s1i, l1r, l1i, reverse=True, name="s5_dlam1")

    def bu_bwd(uv, g0, g1, g2, g3, wb):
        _, vjp = jax.vjp(_s5_bu_fn, uv, wb.astype(F32))
        du, dwb = vjp((g0, g1, g2, g3))
        return (du,), (dwb,)

    (du_b,), (g_wblk,) = rowcall(bu_bwd, "s5_bu_bwd", [u_view, l0r, l0i, l1r, l1i], [wblk], [(S5W, F32)],
                                 [wblk.shape], TS)
    g_bbar = jnp.einsum('cghdrkp,gk->drhcgp',
                        g_wblk.reshape(NCH, S5_CHUNK_GROUPS, S5_GROUP, N_DIR, 2, S5_CHUNK_GROUPS, S5_STATE), eye)
    g_bbar = g_bbar.reshape(N_DIR, 2, S5_GROUP, NS)
    g_lre, g_lim, g_lstep, g_bre, g_bim = s5_prep_bwd(
        lre, lim, lstep, bre, bim, jnp.concatenate([dl0r, dl1r], 0), jnp.concatenate([dl0i, dl1i], 0),
        g_bbar[:, 0], g_bbar[:, 1])

    def uncblk(gb):
        g5 = gb.reshape(NCH, S5_CHUNK_GROUPS, S5_STATE, S5_CHUNK_GROUPS, S5_GROUP)
        return jnp.einsum('cgpkh,gk->cghp', g5, eye).reshape(G, S5_GROUP, S5_STATE)

    (du_tot,), _ = rowcall(lambda a, b_: ((a + b_,), ()), "s5_du_sum", [du_a, du_b], [], [(S5W, BF16)], [], TT)
    dproj = jnp.concatenate([du_tot, dp_rk.astype(BF16)], axis=1)
    dh1 = matmul(dproj, w_in, name="dh1_mm", tb=True)
    g_win = matmul(h1, dproj, name="gwin_mm", ta=True, out_dtypes=(BF16,))

    def norm1_bwd(dx1v, dh1v, xv, g, sc, sh):
        _, vjp = jax.vjp(_normmod, xv, g, sc, sh)
        dx, dg, dsc, dsh = vjp(dh1v)
        return (dx1v + dx,), (dg, dsc, dsh)

    (grad_x,), (dgain1, dscale1, dshift1) = rowcall(
        norm1_bwd, "norm1_bwd", [dx1, dh1, x], [P['norm1_gain'], scale1, shift1], [(D, F32)], [(1, D)] * 3, TT)

    dmod = jnp.concatenate([dshift1, dscale1, dgate1, dshift2, dscale2, dgate2], axis=1)
    lstep_g = g_lstep.reshape(N_DIR, G, S5_STATE)
    small_g = {
        'ada_b': dmod, 'norm1_gain': dgain1, 'norm2_gain': dgain2, 'final_gain': dfgain.reshape(D),
        's5_lambda_re': g_lre.reshape(1, N_DIR, G, S5_STATE), 's5_lambda_im': g_lim.reshape(1, N_DIR, G, S5_STATE),
        's5_log_step': lstep_g,
        's5_b_re': g_bre.T.reshape(1, G, S5_STATE, S5_GROUP), 's5_b_im': g_bim.T.reshape(1, G, S5_STATE, S5_GROUP),
        's5_c_re': uncblk(g_creb)[None], 's5_c_im': uncblk(g_cimb)[None],
        's5_d': g_s5d, 's5_b_glu': g_bglu,
        'rk_shift_prev': g_mup[:, :RIN], 'rk_shift_next': g_mun[:, :RIN],
        'rk_k_k': g_kk, 'rk_k_a': g_ka, 'rk_r_k': g_rk_rk.reshape(1, H, RWKV_HEAD),
        'rk_ln_gain': g_lng, 'rk_ln_bias': g_lnb,
    }
    small_shapes = {n: P[n].shape for n in SMALL}
    small_shapes['s5_log_step'] = (N_DIR, G, S5_STATE)
    small_pack = _pack_flat([small_g[n] for n in SMALL])

    def lora_unpad(gp):
        return jnp.stack([gp[d, d * DECAY_LORA:(d + 1) * DECAY_LORA] for d in range(N_DIR)])

    rk_grads = {'rk_w0': g_w0, 'rk_a0': g_a0, 'rk_w_up': lora_unpad(g_wup_p), 'rk_a_up': lora_unpad(g_aup_p),
                'rk_g_up': g_gup_p[:GATE_LORA]}
    rk_gpack = jnp.stack([_pack_rows([_cols_to_slots(rk_grads[n])[j] for n in RKPACK]) for j in range(N_DEV)])
    g_win_s = _cols_to_slots(g_win[:, :PROJ])
    (small_all,) = gather_two_level([small_pack], "comm_gather_small_grads")
    slots = [g_win_s, g_wout.reshape(N_DEV, D // N_DEV, D), g_w1, g_w2.reshape(N_DEV, FF // N_DEV, D),
             g_wglu.reshape(N_DEV, S5W // N_DEV, S5W), rk_gpack]
    my_c = lax.axis_index("c")

    def core_half(a, core):
        return lax.dynamic_index_in_dim(a.reshape((N_DEV // 2, 2) + a.shape[1:]), core, axis=1, keepdims=False)

    from_sibling = swap_sibling([core_half(a, 1 - my_c) for a in slots], "comm_swap_grads")
    chip_sums = []
    for a, got, nm in zip(slots, from_sibling, ['w_in', 'w_out', 'ffn_w1', 'ffn_w2', 's5_w_glu', 'rkpack']):
        own = core_half(a, my_c)
        flat = (own.shape[0] * math.prod(own.shape[1:-1]), own.shape[-1])
        (sm,), _ = rowcall(lambda u, v_: ((u.astype(F32) + v_.astype(F32),), ()), "chip_sum_" + nm,
                           [own.reshape(flat), got.reshape(flat)], [], [(flat[1], a.dtype)], [], 512)
        chip_sums.append(sm.reshape(own.shape))
    win_parts, wout_parts, w1_parts, w2_parts, wglu_parts, rk_parts = exchange_chips(chip_sums, "comm_grads")

    res = {}

    def put(name, g, dl, m2, v2):
        shp = P[name].shape
        res[name] = tuple(t.reshape(shp) for t in (g, dl, m2, v2))

    def adam2d(name, parts):
        shp = P[name].shape
        r2 = (math.prod(shp[:-1]), shp[-1])
        put(name, *adam(parts.reshape((parts.shape[0],) + r2), P[name].reshape(r2), M[name].reshape(r2),
                        V[name].reshape(r2), "adam_" + name))

    adam2d('w_in', win_parts)
    adam2d('w_out', wout_parts)
    adam2d('ffn_w1', w1_parts)
    adam2d('ffn_w2', w2_parts)
    adam2d('s5_w_glu', wglu_parts)
    off = 0
    for n in SMALL:
        if n == 'ada_b':
            break
        off += _round_up(math.prod(small_shapes[n]), SUBLANE * LANE) // LANE
    nrow_b = N_MOD * D // LANE
    dmod_all = small_all[:, off:off + nrow_b].reshape(N_DEV, N_MOD * D)
    dmod_cols = lax.dynamic_slice(dmod_all, (0, me * cs_mod), (N_DEV, cs_mod))
    g_adaw = matmul(c_act, dmod_cols, name="gadaw_mm", ta=True, precise=True)
    adam2d('ada_w', g_adaw[None])
    small_w = dict(P)
    small_m, small_v = dict(M), dict(V)
    rk_res = adam(rk_parts, rk_pack, _pack_rows([M[n][0] for n in RKPACK]), _pack_rows([V[n][0] for n in RKPACK]),
                  "adam_rkpack")
    for name, parts4 in zip(RKPACK, zip(*[_unpack_rows(t, rk_shapes) for t in rk_res])):
        put(name, *parts4)
    return loss, grad_x, res, (small_all, small_shapes, small_w, small_m, small_v)


def _small_update(small_all, small_shapes, P, M, V, res):
    G = (D_MODEL // 2) // S5_GROUP
    names = [n for n in SMALL if n != 's5_log_step']
    shapes = [small_shapes[n] for n in SMALL]
    parts = _unpack_flat_batched(small_all, shapes)
    by = dict(zip(SMALL, parts))
    ls = by['s5_log_step']
    ls = ls.transpose(0, 3, 1, 2).reshape(N_DEV * S5_STATE, N_DIR * G)
    pk = lambda d: _pack_flat([d[n] for n in names])
    packs = jnp.stack([_pack_flat([by[n][j] for n in names]) for j in range(N_DEV)])
    out = adam(packs, pk(P), pk(M), pk(V), "adam_small")
    shp = [P[n].shape for n in names]
    for name, parts4 in zip(names, zip(*[_unpack_flat(t, shp) for t in out])):
        res[name] = parts4
    lsw = lambda d: jnp.pad(d['s5_log_step'].reshape(1, N_DIR * G), ((0, SUBLANE - 1), (0, 0)))
    ls_parts = jnp.pad(ls[:, None, :], ((0, 0), (0, SUBLANE - 1), (0, 0)))
    o = adam(ls_parts, lsw(P), lsw(M), lsw(V), "adam_log_step")
    res['s5_log_step'] = tuple(t[0:1].reshape(P['s5_log_step'].shape) for t in o)


def _unpack_flat_batched(packed, shapes):
    out, r0 = [], 0
    B = packed.shape[0]
    for s in shapes:
        n = math.prod(s)
        nr = _round_up(n, SUBLANE * LANE) // LANE
        out.append(packed[:, r0:r0 + nr].reshape(B, -1)[:, :n].reshape((B,) + tuple(s)))
        r0 += nr
    return out


def kernel(x, c, ada_w, ada_b, norm1_gain, norm2_gain, final_gain, w_in, w_out, s5_lambda_re, s5_lambda_im, s5_log_step, s5_b_re, s5_b_im, s5_c_re, s5_c_im, s5_d, s5_w_glu, s5_b_glu, rk_shift_prev, rk_shift_next, rk_w0, rk_w_up, rk_a0, rk_a_up, rk_g_up, rk_k_k, rk_k_a, rk_r_k, rk_ln_gain, rk_ln_bias, ffn_w1, ffn_w2, loss_target, m_ada_w, m_ada_b, m_norm1_gain, m_norm2_gain, m_final_gain, m_w_in, m_w_out, m_s5_lambda_re, m_s5_lambda_im, m_s5_log_step, m_s5_b_re, m_s5_b_im, m_s5_c_re, m_s5_c_im, m_s5_d, m_s5_w_glu, m_s5_b_glu, m_rk_shift_prev, m_rk_shift_next, m_rk_w0, m_rk_w_up, m_rk_a0, m_rk_a_up, m_rk_g_up, m_rk_k_k, m_rk_k_a, m_rk_r_k, m_rk_ln_gain, m_rk_ln_bias, m_ffn_w1, m_ffn_w2, v_ada_w, v_ada_b, v_norm1_gain, v_norm2_gain, v_final_gain, v_w_in, v_w_out, v_s5_lambda_re, v_s5_lambda_im, v_s5_log_step, v_s5_b_re, v_s5_b_im, v_s5_c_re, v_s5_c_im, v_s5_d, v_s5_w_glu, v_s5_b_glu, v_rk_shift_prev, v_rk_shift_next, v_rk_w0, v_rk_w_up, v_rk_a0, v_rk_a_up, v_rk_g_up, v_rk_k_k, v_rk_k_a, v_rk_r_k, v_rk_ln_gain, v_rk_ln_bias, v_ffn_w1, v_ffn_w2):
    given = dict(locals())
    P = {n: given[n] for n in ['x', 'c', 'loss_target'] + WEIGHTS}
    M = {n: given['m_' + n] for n in WEIGHTS}
    V = {n: given['v_' + n] for n in WEIGHTS}
    loss, grad_x, res, small = _step(P, M, V)
    small_all, small_shapes, _, _, _ = small
    _small_update(small_all, small_shapes, P, M, V, res)
    outs = [loss, grad_x[None]]
    for q in range(4):
        outs += [res[n][q] for n in WEIGHTS]
    return tuple(outs)
```

```python
import functools
import math

import jax
import jax.numpy as jnp
from jax import lax
from jax.experimental import pallas as pl
from jax.experimental.pallas import tpu as pltpu

F32 = jnp.float32
BF16 = jnp.bfloat16
HI = lax.Precision.HIGHEST
MXU_DTYPE = jnp.bfloat16

N_DEV = 8
MESH_AXES = ("x", "y", "c")
D_MODEL = 2048
SEQ = 2048
S5_GROUP = 16
S5_STATE = 64
RWKV_HEAD = 64
DECAY_LORA = 64
GATE_LORA = 160
N_DIR = 2
N_MOD = 6
NORM_EPS = 1e-6
GN_EPS = 64e-5
L2_EPS = 1e-12
ADAM_LR = 0.001
ADAM_B1 = 0.9
ADAM_B2 = 0.999
ADAM_EPS = 1e-08
ADAM_WD = 0.01
ADAM_STEP = 10
LANE = 128
SUBLANE = 8
S5_CHUNK_GROUPS = 8
S5_SCAN_ROWS = 256
VMEM_LIMIT = 56 * 1024 * 1024

WEIGHTS = ['ada_w', 'ada_b', 'norm1_gain', 'norm2_gain', 'final_gain', 'w_in', 'w_out', 's5_lambda_re',
           's5_lambda_im', 's5_log_step', 's5_b_re', 's5_b_im', 's5_c_re', 's5_c_im', 's5_d', 's5_w_glu',
           's5_b_glu', 'rk_shift_prev', 'rk_shift_next', 'rk_w0', 'rk_w_up', 'rk_a0', 'rk_a_up', 'rk_g_up',
           'rk_k_k', 'rk_k_a', 'rk_r_k', 'rk_ln_gain', 'rk_ln_bias', 'ffn_w1', 'ffn_w2']
SMALL = ['ada_b', 'norm1_gain', 'norm2_gain', 'final_gain', 's5_lambda_re', 's5_lambda_im', 's5_log_step',
         's5_b_re', 's5_b_im', 's5_c_re', 's5_c_im', 's5_d', 's5_b_glu', 'rk_shift_prev', 'rk_shift_next',
         'rk_k_k', 'rk_k_a', 'rk_r_k', 'rk_ln_gain', 'rk_ln_bias']
RKPACK = ['rk_w0', 'rk_a0', 'rk_w_up', 'rk_a_up', 'rk_g_up']


def _round_up(n, m):
    return (n + m - 1) // m * m


def _tile(dim, pref, unit=LANE):
    t = min(pref, dim) // unit * unit
    while t >= unit:
        if dim % t == 0:
            return t
        t -= unit
    return dim


def _cparams(sem=None):
    return pltpu.CompilerParams(dimension_semantics=sem, vmem_limit_bytes=VMEM_LIMIT)


def _full_spec(a):
    nd = a.ndim
    return pl.BlockSpec(a.shape, lambda *_: (0,) * nd)


def exchange(arrs, modes, name):
    n = len(arrs)
    out_shape = [jax.ShapeDtypeStruct((N_DEV,) + a.shape if m == 'gather' else a.shape, a.dtype)
                 for a, m in zip(arrs, modes)]

    def body(*refs):
        ins, outs = refs[:n], refs[n:2 * n]
        send_sems, recv_sems, local_sems = refs[2 * n:]
        x, y, c = (lax.axis_index(a) for a in MESH_AXES)
        me = 4 * x + 2 * y + c
        copies = []
        for i in range(n):
            gather = modes[i] == 'gather'
            mine = pltpu.make_async_copy(ins[i] if gather else ins[i].at[me], outs[i].at[me], local_sems.at[i])
            mine.start()
            copies.append(mine)
        remote = []
        for k in range(1, N_DEV):
            px = 1 - x if (k >> 2) & 1 else x
            py = 1 - y if (k >> 1) & 1 else y
            pc = 1 - c if k & 1 else c
            peer = 4 * px + 2 * py + pc
            for i in range(n):
                src = ins[i] if modes[i] == 'gather' else ins[i].at[peer]
                cp = pltpu.make_async_remote_copy(
                    src_ref=src, dst_ref=outs[i].at[me], send_sem=send_sems.at[i, k - 1],
                    recv_sem=recv_sems.at[i, k - 1], device_id=(px, py, pc), device_id_type=pl.DeviceIdType.MESH)
                cp.start()
                remote.append(cp)
        for cp in remote:
            cp.wait_recv()
        for cp in remote:
            cp.wait_send()
        for cp in copies:
            cp.wait()

    any_spec = pl.BlockSpec(memory_space=pl.ANY)
    return pl.pallas_call(
        body, name=name, out_shape=out_shape,
        in_specs=[any_spec] * n, out_specs=[any_spec] * n,
        scratch_shapes=[pltpu.SemaphoreType.DMA((n, N_DEV - 1)), pltpu.SemaphoreType.DMA((n, N_DEV - 1)),
                        pltpu.SemaphoreType.DMA((n,))],
        compiler_params=pltpu.CompilerParams(has_side_effects=True),
    )(*arrs)


def gather_two_level(arrs, name):
    n = len(arrs)
    out_shape = [jax.ShapeDtypeStruct((N_DEV,) + a.shape, a.dtype) for a in arrs]

    def body(*refs):
        ins, outs = refs[:n], refs[n:2 * n]
        send_sems, recv_sems, local_sems = refs[2 * n:]
        x, y, c = (lax.axis_index(a) for a in MESH_AXES)
        me, sibling = (x, y, c), (x, y, 1 - c)
        chips = [(1 - x, y), (x, 1 - y), (1 - x, 1 - y)]

        def slot(px, py, pc):
            return 4 * px + 2 * py + pc

        def copy(i, k, block, to, src=None):
            return pltpu.make_async_remote_copy(
                src_ref=outs[i].at[slot(*block)] if src is None else src, dst_ref=outs[i].at[slot(*block)],
                send_sem=send_sems.at[i, k], recv_sem=recv_sems.at[i, k], device_id=to,
                device_id_type=pl.DeviceIdType.MESH)

        mine = [pltpu.make_async_copy(ins[i], outs[i].at[slot(*me)], local_sems.at[i]) for i in range(n)]
        for cp in mine:
            cp.start()
        started = []
        for i in range(n):
            started.append(copy(i, 0, me, sibling, src=ins[i]))
            started += [copy(i, 1 + j, me, (*chip, c), src=ins[i]) for j, chip in enumerate(chips)]
        for cp in started:
            cp.start()
        for j, chip in enumerate(chips):
            for i in range(n):
                copy(i, 1 + j, (*chip, c), me).wait_recv()
                fwd = copy(i, 4 + j, (*chip, c), sibling)
                fwd.start()
                started.append(fwd)
        for i in range(n):
            copy(i, 0, sibling, me).wait_recv()
        for j, chip in enumerate(chips):
            for i in range(n):
                copy(i, 4 + j, (*chip, 1 - c), me).wait_recv()
        for cp in started:
            cp.wait_send()
        for cp in mine:
            cp.wait()

    any_spec = pl.BlockSpec(memory_space=pl.ANY)
    return pl.pallas_call(
        body, name=name, out_shape=out_shape,
        in_specs=[any_spec] * n, out_specs=[any_spec] * n,
        scratch_shapes=[pltpu.SemaphoreType.DMA((n, N_DEV - 1)), pltpu.SemaphoreType.DMA((n, N_DEV - 1)),
                        pltpu.SemaphoreType.DMA((n,))],
        compiler_params=pltpu.CompilerParams(has_side_effects=True),
    )(*arrs)


def swap_sibling(arrs, name):
    n = len(arrs)

    def body(*refs):
        ins, outs = refs[:n], refs[n:2 * n]
        send_sems, recv_sems = refs[2 * n:]
        x, y, c = (lax.axis_index(a) for a in MESH_AXES)
        cps = [pltpu.make_async_remote_copy(src_ref=ins[i], dst_ref=outs[i], send_sem=send_sems.at[i],
                                            recv_sem=recv_sems.at[i], device_id=(x, y, 1 - c),
                                            device_id_type=pl.DeviceIdType.MESH) for i in range(n)]
        for cp in cps:
            cp.start()
        for cp in cps:
            cp.wait()

    any_spec = pl.BlockSpec(memory_space=pl.ANY)
    return pl.pallas_call(
        body, name=name, out_shape=[jax.ShapeDtypeStruct(a.shape, a.dtype) for a in arrs],
        in_specs=[any_spec] * n, out_specs=[any_spec] * n,
        scratch_shapes=[pltpu.SemaphoreType.DMA((n,)), pltpu.SemaphoreType.DMA((n,))],
        compiler_params=pltpu.CompilerParams(has_side_effects=True),
    )(*arrs)


def exchange_chips(arrs, name):
    n = len(arrs)

    def body(*refs):
        ins, outs = refs[:n], refs[n:2 * n]
        send_sems, recv_sems, local_sems = refs[2 * n:]
        x, y, c = (lax.axis_index(a) for a in MESH_AXES)
        mine = 2 * x + y
        chips = [(1 - x, y), (x, 1 - y), (1 - x, 1 - y)]
        local = [pltpu.make_async_copy(ins[i].at[mine], outs[i].at[mine], local_sems.at[i]) for i in range(n)]
        for cp in local:
            cp.start()
        remote = []
        for j, (px, py) in enumerate(chips):
            for i in range(n):
                cp = pltpu.make_async_remote_copy(
                    src_ref=ins[i].at[2 * px + py], dst_ref=outs[i].at[mine], send_sem=send_sems.at[i, j],
                    recv_sem=recv_sems.at[i, j], device_id=(px, py, c), device_id_type=pl.DeviceIdType.MESH)
                cp.start()
                remote.append(cp)
        for cp in remote:
            cp.wait_recv()
        for cp in remote:
            cp.wait_send()
        for cp in local:
            cp.wait()

    any_spec = pl.BlockSpec(memory_space=pl.ANY)
    return pl.pallas_call(
        body, name=name, out_shape=[jax.ShapeDtypeStruct(a.shape, a.dtype) for a in arrs],
        in_specs=[any_spec] * n, out_specs=[any_spec] * n,
        scratch_shapes=[pltpu.SemaphoreType.DMA((n, 3)), pltpu.SemaphoreType.DMA((n, 3)), pltpu.SemaphoreType.DMA((n,))],
        compiler_params=pltpu.CompilerParams(has_side_effects=True),
    )(*arrs)


def matmul(a, b, *, name, ta=False, tb=False, b_slots=False, out_slots=0, out_dtypes=(F32,), epi=None,
           extras=(), precise=False, tm=512, tn=512, tk=2048):
    if ta:
        K, M = a.shape
    else:
        M, K = a.shape
    if b_slots:
        ns, br, bc = b.shape
        bshape = (br, ns * bc)
    else:
        bshape = b.shape
    N = bshape[0] if tb else bshape[1]
    assert (bshape[1] if tb else bshape[0]) == K, (a.shape, b.shape, ta, tb)
    tm, tn, tk = _tile(M, tm, SUBLANE), _tile(N, tn), _tile(K, tk, SUBLANE if K < LANE else LANE)
    if b_slots and tb:
        tk = _tile(b.shape[2], tk)
    elif b_slots:
        tn = _tile(b.shape[2], tn)
    if out_slots:
        tn = _tile(N // out_slots, tn)
    if b_slots:
        cs = b.shape[2]
        tcol = tk if tb else tn
        assert cs % tcol == 0
        per = cs // tcol
    if out_slots:
        ncs = N // out_slots
        assert ncs % tn == 0
        operc = ncs // tn
    nk = K // tk
    a_spec = pl.BlockSpec((tk, tm), lambda i, j, k: (k, i)) if ta else pl.BlockSpec((tm, tk), lambda i, j, k: (i, k))
    if b_slots:
        if tb:
            b_spec = pl.BlockSpec((None, tn, tk), lambda i, j, k: (k // per, j, k % per))
        else:
            b_spec = pl.BlockSpec((None, tk, tn), lambda i, j, k: (j // per, k, j % per))
    else:
        b_spec = pl.BlockSpec((tn, tk), lambda i, j, k: (j, k)) if tb else pl.BlockSpec((tk, tn), lambda i, j, k: (k, j))
    ex_specs = []
    for arr, kind in extras:
        if kind == 'mn':
            ex_specs.append(pl.BlockSpec((tm, tn), lambda i, j, k: (i, j)))
        else:
            ex_specs.append(pl.BlockSpec((1, tn), lambda i, j, k: (0, j)))
    if out_slots:
        o_spec = pl.BlockSpec((None, tm, tn), lambda i, j, k: (j // operc, i, j % operc))
        o_shape = (out_slots, M, ncs)
    else:
        o_spec = pl.BlockSpec((tm, tn), lambda i, j, k: (i, j))
        o_shape = (M, N)
    ne, no = len(extras), len(out_dtypes)
    dims = (((0 if ta else 1,), (1 if tb else 0,)), ((), ()))
    op_dtype = F32 if precise else MXU_DTYPE

    def body(a_ref, b_ref, *rest):
        ex_refs, out_refs, acc = rest[:ne], rest[ne:ne + no], rest[-1]
        k = pl.program_id(2)
        part = lax.dot_general(a_ref[...].astype(op_dtype), b_ref[...].astype(op_dtype), dims,
                               precision=HI if precise else None, preferred_element_type=F32)

        def finish(total):
            res = epi(total, *[e[...] for e in ex_refs]) if epi is not None else (total,)
            for o, r in zip(out_refs, res):
                o[...] = r.astype(o.dtype)

        if nk == 1:
            finish(part)
        else:
            @pl.when(k == 0)
            def _():
                acc[...] = part

            @pl.when(jnp.logical_and(k > 0, k < nk - 1))
            def _():
                acc[...] += part

            @pl.when(k == nk - 1)
            def _():
                finish(acc[...] + part)

    outs = pl.pallas_call(
        body, name=name, grid=(M // tm, N // tn, nk),
        in_specs=[a_spec, b_spec] + ex_specs, out_specs=[o_spec] * no,
        out_shape=[jax.ShapeDtypeStruct(o_shape, dt) for dt in out_dtypes],
        scratch_shapes=[pltpu.VMEM((tm, tn), F32)],
        compiler_params=_cparams(("parallel", "parallel", "arbitrary")),
    )(a, b, *[e[0] for e in extras])
    return outs[0] if no == 1 else outs


def rowcall(fn, name, tiled, full, tiled_out, acc_out, tt):
    views = [(t, t.shape[1], 0) if not isinstance(t, tuple) else t for t in tiled]
    T = views[0][0].shape[0]
    tt = _tile(T, tt, SUBLANE)
    nt, nf, nto, nao = len(views), len(full), len(tiled_out), len(acc_out)

    def view_spec(w, cb):
        return pl.BlockSpec((tt, w), lambda i: (i, cb))

    in_specs = [view_spec(w, cb) for _, w, cb in views] + [_full_spec(f) for f in full]
    out_specs = [pl.BlockSpec((tt, w), lambda i: (i, 0)) for w, _ in tiled_out]
    out_specs += [pl.BlockSpec(s, lambda i, nd=len(s): (0,) * nd) for s in acc_out]
    out_shape = [jax.ShapeDtypeStruct((T, w), dt) for w, dt in tiled_out]
    out_shape += [jax.ShapeDtypeStruct(s, F32) for s in acc_out]

    def body(*refs):
        tin, fin = refs[:nt], refs[nt:nt + nf]
        tout, aout = refs[nt + nf:nt + nf + nto], refs[nt + nf + nto:]
        touts, aouts = fn(*[r[...] for r in tin], *[r[...] for r in fin])
        for r, v in zip(tout, touts):
            r[...] = v.astype(r.dtype)
        if nao:
            @pl.when(pl.program_id(0) == 0)
            def _():
                for r in aout:
                    r[...] = jnp.zeros_like(r)

            for r, v in zip(aout, aouts):
                r[...] += v.astype(F32)

    outs = pl.pallas_call(
        body, name=name, grid=(T // tt,), in_specs=in_specs, out_specs=out_specs, out_shape=out_shape,
        compiler_params=_cparams(("arbitrary",) if nao else ("parallel",)),
    )(*[v[0] for v in views], *full)
    return outs[:nto], outs[nto:]


def _mm(a, b):
    return jnp.dot(a.astype(MXU_DTYPE), b.astype(MXU_DTYPE), preferred_element_type=F32)


def _rms(x, gain):
    ms = jnp.mean(x * x, axis=-1, keepdims=True)
    return x * lax.rsqrt(ms + NORM_EPS) * gain


def _normmod(x, gain, scale, shift):
    return _rms(x, gain) * (1.0 + scale) + shift


def _gelu_tanh(y):
    return 0.5 * y * (1.0 + jnp.tanh(math.sqrt(2.0 / math.pi) * (y + 0.044715 * (y * y * y))))


def _sigmoid(x):
    return 1.0 / (1.0 + jnp.exp(-x))


def _softplus(x):
    return jnp.maximum(x, 0.0) + jnp.log(1.0 + jnp.exp(-jnp.abs(x)))


def _seg_mats(width, seg):
    r = lax.broadcasted_iota(jnp.int32, (width, LANE), 0) // seg
    c = lax.broadcasted_iota(jnp.int32, (width, LANE), 1)
    s = (r == c).astype(F32)
    rt = lax.broadcasted_iota(jnp.int32, (LANE, width), 0)
    ct = lax.broadcasted_iota(jnp.int32, (LANE, width), 1) // seg
    st = (rt == ct).astype(F32)
    return s, st


def _segsum(x, s):
    return jnp.dot(x, s, precision=HI, preferred_element_type=F32)


def _s5_prep_fn(lre, lim, lstep, bre, bim):
    step = jnp.exp(lstep)
    mag = jnp.exp(lre * step)
    lbr = mag * jnp.cos(lim * step)
    lbi = mag * jnp.sin(lim * step)
    den = lre * lre + lim * lim
    nr = lbr - 1.0
    ni = lbi
    cre = (nr * lre + ni * lim) / den
    cim = (ni * lre - nr * lim) / den
    bbr = jnp.stack([cre[d:d + 1] * bre - cim[d:d + 1] * bim for d in range(N_DIR)])
    bbi = jnp.stack([cre[d:d + 1] * bim + cim[d:d + 1] * bre for d in range(N_DIR)])
    return lbr, lbi, bbr, bbi


def s5_prep(lre, lim, lstep, bre, bim):
    ns = lre.shape[1]

    def body(lre_r, lim_r, ls_r, bre_r, bim_r, lbr_r, lbi_r, bbr_r, bbi_r):
        lbr, lbi, bbr, bbi = _s5_prep_fn(lre_r[...], lim_r[...], ls_r[...], bre_r[...], bim_r[...])
        lbr_r[...] = lbr
        lbi_r[...] = lbi
        bbr_r[...] = bbr
        bbi_r[...] = bbi

    return pl.pallas_call(
        body, name="s5_prep",
        out_shape=[jax.ShapeDtypeStruct((N_DIR, ns), F32)] * 2 + [jax.ShapeDtypeStruct((N_DIR, S5_GROUP, ns), F32)] * 2,
        compiler_params=_cparams(),
    )(lre, lim, lstep, bre, bim)


def s5_prep_bwd(lre, lim, lstep, bre, bim, dlbr, dlbi, dbbr, dbbi):
    ns = lre.shape[1]

    def body(lre_r, lim_r, ls_r, bre_r, bim_r, d1, d2, d3, d4, o1, o2, o3, o4, o5):
        _, vjp = jax.vjp(_s5_prep_fn, lre_r[...], lim_r[...], ls_r[...], bre_r[...], bim_r[...])
        g = vjp((d1[...], d2[...], d3[...], d4[...]))
        for o, v in zip((o1, o2, o3, o4, o5), g):
            o[...] = v

    return pl.pallas_call(
        body, name="s5_prep_bwd",
        out_shape=[jax.ShapeDtypeStruct((N_DIR, ns), F32)] * 3 + [jax.ShapeDtypeStruct((S5_GROUP, ns), F32)] * 2,
        compiler_params=_cparams(),
    )(lre, lim, lstep, bre, bim, dlbr, dlbi, dbbr, dbbi)


def s5_scan(bre, bim, lre, lim, *, reverse, name):
    T, NS = bre.shape
    tt = _tile(T, S5_SCAN_ROWS, SUBLANE)
    wl = _tile(NS, 512)
    nT = T // tt
    ngrp = tt // SUBLANE

    def tmap(j, i):
        return ((nT - 1 - i) if reverse else i, j)

    def body(bre_r, bim_r, lre_r, lim_r, sre_r, sim_r, cre, cim):
        @pl.when(pl.program_id(1) == 0)
        def _():
            cre[...] = jnp.zeros_like(cre)
            cim[...] = jnp.zeros_like(cim)

        lr = jnp.broadcast_to(lre_r[...], (SUBLANE, wl))
        li = jnp.broadcast_to(lim_r[...], (SUBLANE, wl))
        row = lax.broadcasted_iota(jnp.int32, (SUBLANE, wl), 0)
        pows = [(lr, li)]
        for _ in range(3):
            pr, pi = pows[-1]
            pows.append((pr * pr - pi * pi, 2.0 * pr * pi))
        e = (SUBLANE - row) if reverse else (row + 1)
        Pr = jnp.ones((SUBLANE, wl), F32)
        Pi = jnp.zeros((SUBLANE, wl), F32)
        for bit, (qr, qi) in enumerate(pows):
            on = ((e >> bit) & 1) == 1
            nr, ni = Pr * qr - Pi * qi, Pr * qi + Pi * qr
            Pr, Pi = jnp.where(on, nr, Pr), jnp.where(on, ni, Pi)

        def group(g, carry):
            gg = (ngrp - 1 - g) if reverse else g
            rows = pl.ds(pl.multiple_of(gg * SUBLANE, SUBLANE), SUBLANE)
            sr, si = bre_r[rows, :], bim_r[rows, :]
            for lvl, k in enumerate((1, 2, 4)):
                qr, qi = pows[lvl]
                if reverse:
                    shr = pltpu.roll(sr, SUBLANE - k, 0)
                    shi = pltpu.roll(si, SUBLANE - k, 0)
                    keep = row < SUBLANE - k
                else:
                    shr = pltpu.roll(sr, k, 0)
                    shi = pltpu.roll(si, k, 0)
                    keep = row >= k
                shr = jnp.where(keep, shr, 0.0)
                shi = jnp.where(keep, shi, 0.0)
                sr, si = sr + qr * shr - qi * shi, si + qr * shi + qi * shr
            cr, ci = carry
            sr, si = sr + Pr * cr - Pi * ci, si + Pr * ci + Pi * cr
            sre_r[rows, :] = sr
            sim_r[rows, :] = si
            last = 0 if reverse else SUBLANE - 1
            return (jnp.broadcast_to(sr[last:last + 1, :], (SUBLANE, wl)),
                    jnp.broadcast_to(si[last:last + 1, :], (SUBLANE, wl)))

        cr, ci = lax.fori_loop(0, ngrp, group, (cre[...], cim[...]), unroll=4 if ngrp % 4 == 0 else 1)
        cre[...] = cr
        cim[...] = ci

    blk = pl.BlockSpec((tt, wl), tmap)
    row_spec = pl.BlockSpec((1, wl), lambda j, i: (0, j))
    return pl.pallas_call(
        body, name=name, grid=(NS // wl, nT),
        in_specs=[blk, blk, row_spec, row_spec], out_specs=[blk, blk],
        out_shape=[jax.ShapeDtypeStruct((T, NS), F32)] * 2,
        scratch_shapes=[pltpu.VMEM((SUBLANE, wl), F32)] * 2,
        compiler_params=_cparams(("parallel", "arbitrary")),
    )(bre, bim, lre, lim)


def s5_dlam(sre, sim, gre, gim, *, reverse, name):
    T, NS = sre.shape
    wl = _tile(NS, 256)

    def body(sr_r, si_r, gr_r, gi_r, dr_r, di_r):
        row = lax.broadcasted_iota(jnp.int32, (T, wl), 0)
        if reverse:
            keep = row < T - 1
            pr = jnp.where(keep, pltpu.roll(sr_r[...], T - 1, 0), 0.0)
            pi = jnp.where(keep, pltpu.roll(si_r[...], T - 1, 0), 0.0)
        else:
            keep = row >= 1
            pr = jnp.where(keep, pltpu.roll(sr_r[...], 1, 0), 0.0)
            pi = jnp.where(keep, pltpu.roll(si_r[...], 1, 0), 0.0)
        gr, gi = gr_r[...], gi_r[...]
        dr_r[...] = jnp.sum(pr * gr + pi * gi, axis=0, keepdims=True)
        di_r[...] = jnp.sum(pr * gi - pi * gr, axis=0, keepdims=True)

    blk = pl.BlockSpec((T, wl), lambda j: (0, j))
    o = pl.BlockSpec((1, wl), lambda j: (0, j))
    return pl.pallas_call(
        body, name=name, grid=(NS // wl,), in_specs=[blk] * 4, out_specs=[o, o],
        out_shape=[jax.ShapeDtypeStruct((1, NS), F32)] * 2,
        compiler_params=_cparams(("parallel",)),
    )(sre, sim, gre, gim)


def _s5_bu_fn(u, wblk):
    nch = wblk.shape[0]
    cw = S5_CHUNK_GROUPS * S5_GROUP
    sw = S5_CHUNK_GROUPS * S5_STATE
    parts = [[] for _ in range(4)]
    for ch in range(nch):
        res = _mm(u[:, ch * cw:(ch + 1) * cw], wblk[ch])
        for q in range(4):
            parts[q].append(res[:, q * sw:(q + 1) * sw])
    return tuple(jnp.concatenate(p, axis=1) if nch > 1 else p[0] for p in parts)


def _s5_out_fn(x0r, x0i, x1r, x1i, u, cre, cim, dsk, wglu, bglu):
    xr, xi = x0r + x1r, x0i + x1i
    nch = cre.shape[0]
    sw = S5_CHUNK_GROUPS * S5_STATE
    ys = [_mm(xr[:, ch * sw:(ch + 1) * sw], cre[ch]) - _mm(xi[:, ch * sw:(ch + 1) * sw], cim[ch]) for ch in range(nch)]
    y = jnp.concatenate(ys, axis=1) if nch > 1 else ys[0]
    z = _gelu_tanh(y + dsk * u)
    gate = _sigmoid(_mm(z, wglu) + bglu)
    return z * gate


def _rk_dims():
    C = D_MODEL // 2
    LW = N_DIR * DECAY_LORA
    GP = _round_up(GATE_LORA, LANE)
    return C, LW, GP


def _rk_pre_fn(ps, w0, wup0, wup1, a0, aup0, aup1, gup, k_k, k_a):
    C, LW, GP = _rk_dims()
    r, k, v = ps[:, 0:C], ps[:, C:2 * C], ps[:, 2 * C:3 * C]
    wdn = ps[:, 3 * C:3 * C + LW]
    adn = ps[:, 3 * C + LW:3 * C + 2 * LW]
    gdn = ps[:, 3 * C + 2 * LW:3 * C + 2 * LW + GP]
    s, st = _seg_mats(C, RWKV_HEAD)
    kk = k * k_k
    n2 = _segsum(kk * kk, s)
    n2 = jnp.where(n2 > 0.0, n2, 1.0)
    inv = 1.0 / jnp.maximum(jnp.sqrt(n2), L2_EPS)
    kkn = kk * _segsum(inv, st)
    tw = jnp.tanh(wdn)
    wup, aup = (wup0, wup1), (aup0, aup1)
    ws, ks, bs = [], [], []
    for d in range(N_DIR):
        wraw = w0[d:d + 1] + _mm(tw, wup[d])
        w = -_softplus(-wraw) - 0.5
        ws.append(jnp.exp(-jnp.exp(w)))
        a = _sigmoid(a0[d:d + 1] + _mm(adn, aup[d]))
        ks.append(k * (1.0 + (a - 1.0) * k_a))
        bs.append(kkn * a)
    g = _mm(_sigmoid(gdn), gup)
    return r, v, kkn, ws[0], ws[1], ks[0], ks[1], bs[0], bs[1], g


def _rk_post_fn(y0, y1, r, v, k0, k1, g, r_k, lng, lnb):
    C = r.shape[1]
    s, st = _seg_mats(C, RWKV_HEAD)
    y = y0 + y1
    mu = _segsum(_segsum(y, s) * (1.0 / RWKV_HEAD), st)
    yc = y - mu
    var = _segsum(_segsum(yc * yc, s) * (1.0 / RWKV_HEAD), st)
    yn = yc * lax.rsqrt(var + GN_EPS) * lng + lnb
    bonus = _segsum(_segsum(r * (k0 + k1) * r_k, s), st)
    return (yn + bonus * v) * g


def rk_shift(proj, mp, mn, col0):
    T = proj.shape[0]
    W = mp.shape[1]
    wl = _tile(math.gcd(W, col0), 256)
    cb0 = col0 // wl

    def body(p_r, mp_r, mn_r, o_r):
        p = p_r[...]
        row = lax.broadcasted_iota(jnp.int32, (T, wl), 0)
        prev = jnp.where(row >= 1, pltpu.roll(p, 1, 0), 0.0)
        nxt = jnp.where(row < T - 1, pltpu.roll(p, T - 1, 0), 0.0)
        o_r[...] = p + mp_r[...] * (prev - p) + mn_r[...] * (nxt - p)

    rs = pl.BlockSpec((1, wl), lambda j: (0, j))
    return pl.pallas_call(
        body, name="rk_shift", grid=(W // wl,),
        in_specs=[pl.BlockSpec((T, wl), lambda j: (0, cb0 + j)), rs, rs],
        out_specs=pl.BlockSpec((T, wl), lambda j: (0, j)),
        out_shape=jax.ShapeDtypeStruct((T, W), F32),
        compiler_params=_cparams(("parallel",)),
    )(proj, mp, mn)


def rk_shift_bwd(dps, proj, mp, mn, col0):
    T, W = dps.shape
    wl = _tile(math.gcd(W, col0), 256)
    cb0 = col0 // wl

    def body(d_r, p_r, mp_r, mn_r, dp_r, dmp_r, dmn_r):
        d, p = d_r[...], p_r[...]
        mpv, mnv = mp_r[...], mn_r[...]
        row = lax.broadcasted_iota(jnp.int32, (T, wl), 0)
        first, last = row >= 1, row < T - 1
        prev = jnp.where(first, pltpu.roll(p, 1, 0), 0.0)
        nxt = jnp.where(last, pltpu.roll(p, T - 1, 0), 0.0)
        dmp_r[...] = jnp.sum(d * (prev - p), axis=0, keepdims=True)
        dmn_r[...] = jnp.sum(d * (nxt - p), axis=0, keepdims=True)
        dp_r[...] = (d * (1.0 - mpv - mnv) + jnp.where(last, pltpu.roll(d * mpv, T - 1, 0), 0.0)
                     + jnp.where(first, pltpu.roll(d * mnv, 1, 0), 0.0))

    rs = pl.BlockSpec((1, wl), lambda j: (0, j))
    blk = pl.BlockSpec((T, wl), lambda j: (0, j))
    return pl.pallas_call(
        body, name="rk_shift_bwd", grid=(W // wl,),
        in_specs=[blk, pl.BlockSpec((T, wl), lambda j: (0, cb0 + j)), rs, rs],
        out_specs=[blk, rs, rs],
        out_shape=[jax.ShapeDtypeStruct((T, W), F32), jax.ShapeDtypeStruct((1, W), F32), jax.ShapeDtypeStruct((1, W), F32)],
        compiler_params=_cparams(("parallel",)),
    )(dps, proj, mp, mn)


RK_FWD_PAIRS = 8
RK_BWD_PAIRS = 4
RK_SPREAD_PAIRS = 2
RK_TIME_BLOCK = 32
RK_LANE_BLOCK = 128


def _rk_blocks(T, C, N, order_reversed, pairs):
    pw = 2 * N
    pp = min(pairs, C // pw)
    tb = min(RK_TIME_BLOCK, T)
    lb = min(RK_LANE_BLOCK, T)
    nb, per = T // tb, lb // tb

    def tix(i):
        return (nb - 1 - i) if order_reversed else i

    rows = pl.BlockSpec((tb, pp * pw), lambda g, i: (tix(i), g))
    cols = pl.BlockSpec((2 * pp, N, lb), lambda g, i: (g, 0, tix(i) // per))
    hist = pl.BlockSpec((tb, pp, N, pw), lambda g, i: (tix(i), g, 0, 0))
    return pp, pw, tb, lb, nb, per, tix, rows, cols, hist


def rk_spread(xT, name):
    H, N, T = xT.shape
    pw = 2 * N
    lb = min(RK_LANE_BLOCK, T)
    pp = min(RK_SPREAD_PAIRS, H // 2)

    def body(x_r, o_r):
        lane = lax.broadcasted_iota(jnp.int32, (N, lb), 1)
        first = lax.broadcasted_iota(jnp.int32, (N, pw), 1) < N
        tiles = [x_r[h] for h in range(2 * pp)]

        def step(t, carry):
            for p in range(pp):
                c = [jnp.sum(jnp.where(lane == t, tiles[2 * p + q], 0.0), axis=1, keepdims=True) for q in range(2)]
                o_r[t, p] = jnp.where(first, c[0], c[1])
            return carry

        lax.fori_loop(0, lb, step, 0, unroll=8)

    return pl.pallas_call(
        body, name=name, grid=(H // (2 * pp), T // lb),
        in_specs=[pl.BlockSpec((2 * pp, N, lb), lambda g, i: (g, 0, i))],
        out_specs=pl.BlockSpec((lb, pp, N, pw), lambda g, i: (i, g, 0, 0)),
        out_shape=jax.ShapeDtypeStruct((T, H // 2, N, pw), F32),
        compiler_params=_cparams(("parallel", "parallel")),
    )(xT)


def _half_sums(x, first):
    return (jnp.sum(jnp.where(first, x, 0.0), axis=1, keepdims=True),
            jnp.sum(jnp.where(first, 0.0, x), axis=1, keepdims=True))


def rk_scan(r, kk, w, k, b, vc, *, reverse, name):
    T, C = r.shape
    N = vc.shape[2]
    H = C // N
    pp, pw, tb, lb, nb, per, tix, rows, cols, hist_spec = _rk_blocks(T, C, N, reverse, RK_FWD_PAIRS)

    def body(r_r, kk_r, w_r, k_r, b_r, VC, yT_r, hist_r, S, YA):
        i = pl.program_id(1)

        @pl.when(i == 0)
        def _():
            S[...] = jnp.zeros_like(S)

        @pl.when(i % per == 0)
        def _():
            yT_r[...] = jnp.zeros_like(yT_r)

        off = (tix(i) % per) * tb
        first = lax.broadcasted_iota(jnp.int32, (N, pw), 1) < N
        YA[...] = jnp.zeros_like(YA)
        st = [S[p] for p in range(pp)]
        seg = [slice(p * pw, (p + 1) * pw) for p in range(pp)]
        for s in range(tb):
            t = (tb - 1 - s) if reverse else s
            row = slice(t, t + 1)
            sk = [_half_sums(st[p] * kk_r[row, seg[p]], first) for p in range(pp)]
            for p in range(pp):
                hist_r[t, p] = st[p]
                skp = jnp.where(first, sk[p][0], sk[p][1])
                st[p] = st[p] * w_r[row, seg[p]] - skp * b_r[row, seg[p]] + VC[t, p] * k_r[row, seg[p]]
            ys = [_half_sums(st[p] * r_r[row, seg[p]], first) for p in range(pp)]
            for p in range(pp):
                for q in range(2):
                    YA[2 * p + q, :, row] = ys[p][q]
        for p in range(pp):
            S[p] = st[p]
        for h in range(2 * pp):
            yT_r[h] = yT_r[h] + (pltpu.roll(YA[h], off, 1) if per > 1 else YA[h])

    return pl.pallas_call(
        body, name=name, grid=(C // (pp * pw), nb),
        in_specs=[rows] * 5 + [hist_spec], out_specs=[cols, hist_spec],
        out_shape=[jax.ShapeDtypeStruct((H, N, T), F32), jax.ShapeDtypeStruct((T, H // 2, N, pw), F32)],
        scratch_shapes=[pltpu.VMEM((pp, N, pw), F32), pltpu.VMEM((2 * pp, N, lb), F32)],
        compiler_params=_cparams(("parallel", "arbitrary")),
    )(r, kk, w, k, b, vc)


def rk_scan_bwd(r, kk, w, k, b, vc, dc, hist, *, reverse, name):
    T, C = r.shape
    N = vc.shape[2]
    H = C // N
    pp, pw, tb, lb, nb, per, tix, rows, cols, hist_spec = _rk_blocks(T, C, N, not reverse, RK_BWD_PAIRS)

    def body(r_r, kk_r, w_r, k_r, b_r, VC, DC, hist_r, dr_r, dkk_r, dw_r, dk_r, db_r, dvT_r, G, YA):
        i = pl.program_id(1)

        @pl.when(i == 0)
        def _():
            G[...] = jnp.zeros_like(G)

        @pl.when(i % per == 0)
        def _():
            dvT_r[...] = jnp.zeros_like(dvT_r)

        off = (tix(i) % per) * tb
        first = lax.broadcasted_iota(jnp.int32, (N, pw), 1) < N
        YA[...] = jnp.zeros_like(YA)
        gs = [G[p] for p in range(pp)]
        seg = [slice(p * pw, (p + 1) * pw) for p in range(pp)]
        for s in range(tb):
            t = s if reverse else (tb - 1 - s)
            row = slice(t, t + 1)
            g = [gs[p] + DC[t, p] * r_r[row, seg[p]] for p in range(pp)]
            sk = [_half_sums(hist_r[t, p] * kk_r[row, seg[p]], first) for p in range(pp)]
            gb = [_half_sums(g[p] * b_r[row, seg[p]], first) for p in range(pp)]
            gk = [_half_sums(g[p] * k_r[row, seg[p]], first) for p in range(pp)]
            for p in range(pp):
                sp = hist_r[t, p]
                kkv, wv, kv, bv = kk_r[row, seg[p]], w_r[row, seg[p]], k_r[row, seg[p]], b_r[row, seg[p]]
                vcol, dycol = VC[t, p], DC[t, p]
                sa = -jnp.where(first, sk[p][0], sk[p][1])
                dsa = jnp.where(first, gb[p][0], gb[p][1])
                sn = sp * wv + sa * bv + vcol * kv
                dr_r[row, seg[p]] = jnp.sum(sn * dycol, axis=0, keepdims=True)
                dw_r[row, seg[p]] = jnp.sum(g[p] * sp, axis=0, keepdims=True)
                db_r[row, seg[p]] = jnp.sum(g[p] * sa, axis=0, keepdims=True)
                dk_r[row, seg[p]] = jnp.sum(g[p] * vcol, axis=0, keepdims=True)
                dkk_r[row, seg[p]] = -jnp.sum(sp * dsa, axis=0, keepdims=True)
                gs[p] = g[p] * wv - dsa * kkv
                for q in range(2):
                    YA[2 * p + q, :, row] = gk[p][q]
        for p in range(pp):
            G[p] = gs[p]
        for h in range(2 * pp):
            dvT_r[h] = dvT_r[h] + (pltpu.roll(YA[h], off, 1) if per > 1 else YA[h])

    return pl.pallas_call(
        body, name=name, grid=(C // (pp * pw), nb),
        in_specs=[rows] * 5 + [hist_spec] * 3, out_specs=[rows] * 5 + [cols],
        out_shape=[jax.ShapeDtypeStruct((T, C), F32)] * 5 + [jax.ShapeDtypeStruct((H, N, T), F32)],
        scratch_shapes=[pltpu.VMEM((pp, N, pw), F32), pltpu.VMEM((2 * pp, N, lb), F32)],
        compiler_params=_cparams(("parallel", "arbitrary")),
    )(r, kk, w, k, b, vc, dc, hist)


def adam(parts, w, m, v, name):
    P, R, C = parts.shape
    tr = _tile(R, max(SUBLANE, (1 << 19) // max(C, 1) // SUBLANE * SUBLANE), SUBLANE)
    c1 = 1.0 / (1.0 - ADAM_B1 ** ADAM_STEP)
    c2 = 1.0 / (1.0 - ADAM_B2 ** ADAM_STEP)

    def body(p_r, w_r, m_r, v_r, g_o, d_o, m_o, v_o):
        g = p_r[0].astype(F32)
        for q in range(1, P):
            g = g + p_r[q].astype(F32)
        m2 = ADAM_B1 * m_r[...] + (1.0 - ADAM_B1) * g
        v2 = ADAM_B2 * v_r[...] + (1.0 - ADAM_B2) * (g * g)
        g_o[...] = g
        m_o[...] = m2
        v_o[...] = v2
        d_o[...] = -ADAM_LR * ((m2 * c1) / (jnp.sqrt(v2 * c2) + ADAM_EPS) + ADAM_WD * w_r[...])

    blk = pl.BlockSpec((tr, C), lambda i: (i, 0))
    return pl.pallas_call(
        body, name=name, grid=(R // tr,),
        in_specs=[pl.BlockSpec((P, tr, C), lambda i: (0, i, 0)), blk, blk, blk], out_specs=[blk] * 4,
        out_shape=[jax.ShapeDtypeStruct((R, C), F32)] * 4,
        compiler_params=_cparams(("parallel",)),
    )(parts, w, m, v)


def _pack_flat(arrs):
    rows = []
    for a in arrs:
        f = a.reshape(-1).astype(F32)
        n = _round_up(f.shape[0], SUBLANE * LANE)
        rows.append(jnp.pad(f, (0, n - f.shape[0])).reshape(-1, LANE))
    return jnp.concatenate(rows, axis=0)


def _unpack_flat(packed, shapes):
    out, r0 = [], 0
    for s in shapes:
        n = math.prod(s)
        nr = _round_up(n, SUBLANE * LANE) // LANE
        out.append(packed[r0:r0 + nr].reshape(-1)[:n].reshape(s))
        r0 += nr
    return out


def _pack_rows(arrs):
    rows = []
    for a in arrs:
        f = a.reshape(-1, a.shape[-1]).astype(F32)
        n = _round_up(f.shape[0], SUBLANE)
        rows.append(jnp.pad(f, ((0, n - f.shape[0]), (0, 0))))
    return jnp.concatenate(rows, axis=0)


def _unpack_rows(packed, shapes):
    out, r0 = [], 0
    for s in shapes:
        nr = math.prod(s[:-1])
        out.append(packed[r0:r0 + nr].reshape(s))
        r0 += _round_up(nr, SUBLANE)
    return out


def _cols_from_slots(g):
    return jnp.moveaxis(g, 0, -2).reshape(g.shape[1:-1] + (N_DEV * g.shape[-1],))


def _cols_to_slots(a):
    cs = a.shape[-1] // N_DEV
    return jnp.moveaxis(a.reshape(a.shape[:-1] + (N_DEV, cs)), -2, 0)


def _step(P, M, V):
    D, T = D_MODEL, SEQ
    S5W = D // 2
    C, LW, GP = _rk_dims()
    H = C // RWKV_HEAD
    G = S5W // S5_GROUP
    NS = G * S5_STATE
    NCH = G // S5_CHUNK_GROUPS
    SW = S5_CHUNK_GROUPS * S5_STATE
    RIN = 3 * C + 2 * LW + GATE_LORA
    RINP = 3 * C + 2 * LW + GP
    PROJ = S5W + RIN
    PROJP = S5W + RINP
    FF = 4 * D
    me = 4 * lax.axis_index("x") + 2 * lax.axis_index("y") + lax.axis_index("c")
    cs_mod = N_MOD * D // N_DEV
    eye = jnp.eye(S5_CHUNK_GROUPS, dtype=F32)

    x = P['x'][0]
    target = P['loss_target'][0]

    (c_all,) = exchange([P['c']], ['gather'], "comm_gather_c")
    c_all = c_all.reshape(N_DEV, D)
    (c_act,), _ = rowcall(lambda cv: ((cv * _sigmoid(cv),), ()), "silu_c", [c_all], [], [(D, F32)], [], N_DEV)
    ada_b_loc = lax.dynamic_slice(P['ada_b'], (0, me * cs_mod), (1, cs_mod))
    mod_loc = matmul(c_act, P['ada_w'][0], name="mod_mm", precise=True, extras=[(ada_b_loc, 'n')],
                     epi=lambda acc, bias: (acc + bias,))

    rk_shapes = [P[n][0].shape for n in RKPACK]
    rk_pack = _pack_rows([P[n][0] for n in RKPACK])
    gathered = gather_two_level(
        [mod_loc, P['w_in'][0].astype(BF16), P['w_out'][0].astype(BF16), P['ffn_w1'][0].astype(BF16),
         P['ffn_w2'][0].astype(BF16), P['s5_w_glu'][0].astype(BF16), rk_pack], "comm_gather_weights")
    mod_all, w_in_g, w_out_g, w1_g, w2_g, wglu_g, rk_g = gathered
    mod_me = lax.dynamic_index_in_dim(mod_all, me, axis=1, keepdims=False).reshape(N_MOD, 1, D)
    shift1, scale1, gate1, shift2, scale2, gate2 = (mod_me[i] for i in range(N_MOD))
    w_in = jnp.pad(_cols_from_slots(w_in_g), ((0, 0), (0, PROJP - PROJ)))
    w_out = w_out_g.reshape(D, D)
    w2 = w2_g.reshape(FF, D)
    wglu = wglu_g.reshape(S5W, S5W)
    rk_full = _unpack_rows(_cols_from_slots(rk_g), [s[:-1] + (C,) for s in rk_shapes])
    rk_w0, rk_a0, rk_wup, rk_aup, rk_gup = rk_full

    def lora_pad(up):
        z = jnp.zeros((N_DIR, LW, C), F32)
        for d in range(N_DIR):
            z = z.at[d, d * DECAY_LORA:(d + 1) * DECAY_LORA].set(up[d])
        return z

    wup_p, aup_p = lora_pad(rk_wup), lora_pad(rk_aup)
    gup_p = jnp.pad(rk_gup, ((0, GP - GATE_LORA), (0, 0)))
    mu_prev = jnp.pad(P['rk_shift_prev'], ((0, 0), (0, RINP - RIN)))
    mu_next = jnp.pad(P['rk_shift_next'], ((0, 0), (0, RINP - RIN)))
    r_k = P['rk_r_k'].reshape(1, C)
    fgain = P['final_gain'].reshape(1, D)

    TT = 256
    (h1,), _ = rowcall(lambda xv, g, sc, sh: ((_normmod(xv, g, sc, sh),), ()), "norm1",
                       [x], [P['norm1_gain'], scale1, shift1], [(D, BF16)], [], TT)
    proj = matmul(h1, w_in, name="proj_mm")

    lre = P['s5_lambda_re'][0].reshape(N_DIR, NS)
    lim = P['s5_lambda_im'][0].reshape(N_DIR, NS)
    lstep = jnp.broadcast_to(P['s5_log_step'][0][:, :, None], (N_DIR, G, S5_STATE)).reshape(N_DIR, NS)
    bre = P['s5_b_re'][0].reshape(NS, S5_GROUP).T
    bim = P['s5_b_im'][0].reshape(NS, S5_GROUP).T
    lbr, lbi, bbr, bbi = s5_prep(lre, lim, lstep, bre, bim)
    bbar = jnp.stack([bbr, bbi], axis=1).reshape(N_DIR, 2, S5_GROUP, NCH, S5_CHUNK_GROUPS, S5_STATE)
    wblk = jnp.einsum('drhcgp,gk->cghdrkp', bbar, eye).reshape(NCH, S5_CHUNK_GROUPS * S5_GROUP, 4 * SW)
    wblk = wblk.astype(MXU_DTYPE)
    u_view = (proj, S5W, 0)
    bus, _ = rowcall(lambda uv, wb: (_s5_bu_fn(uv, wb), ()), "s5_bu", [u_view], [wblk], [(NS, F32)] * 4, [], TT)
    s0r, s0i = s5_scan(bus[0], bus[1], lbr[0:1], lbi[0:1], reverse=False, name="s5_scan_f0")
    s1r, s1i = s5_scan(bus[2], bus[3], lbr[1:2], lbi[1:2], reverse=True, name="s5_scan_f1")

    def cblk(cm):
        c4 = cm.reshape(NCH, S5_CHUNK_GROUPS, S5_GROUP, S5_STATE)
        return jnp.einsum('cghp,gk->cgpkh', c4, eye).reshape(NCH, SW, S5_CHUNK_GROUPS * S5_GROUP)

    cre_b = cblk(P['s5_c_re'][0]).astype(MXU_DTYPE)
    cim_b = cblk(P['s5_c_im'][0]).astype(MXU_DTYPE)
    s5_full = [cre_b, cim_b, P['s5_d'], wglu, P['s5_b_glu']]
    TS = 128
    (y_s5,), _ = rowcall(lambda *a: ((_s5_out_fn(*a),), ()), "s5_out", [s0r, s0i, s1r, s1i, u_view], s5_full,
                         [(S5W, BF16)], [], TS)

    ps = rk_shift(proj, mu_prev, mu_next, S5W)
    pre_full = [rk_w0, wup_p[0], wup_p[1], rk_a0, aup_p[0], aup_p[1], gup_p, P['rk_k_k'], P['rk_k_a']]
    pre_out, _ = rowcall(lambda *a: (_rk_pre_fn(*a)[2:], ()), "rk_pre", [ps], pre_full, [(C, F32)] * 8, [], TS)
    kkn, w_0, w_1, k_0, k_1, b_0, b_1, g_gate = pre_out
    r_t, v_t = ps[:, 0:C], ps[:, 2 * C:3 * C]

    def hmT(a):
        return a.reshape(T, H, RWKV_HEAD).transpose(1, 2, 0)

    def unT(a):
        return a.transpose(2, 0, 1).reshape(T, C)

    v_cols = rk_spread(hmT(v_t), "rk_spread_v")
    dir_rows = [(w_0, k_0, b_0), (w_1, k_1, b_1)]
    yT, hist = [], []
    for d in range(N_DIR):
        wd, kd, bd = dir_rows[d]
        yd, hd = rk_scan(r_t, kkn, wd, kd, bd, v_cols, reverse=(d == 1), name=f"rk_scan_f{d}")
        yT.append(yd)
        hist.append(hd)
    y_0, y_1 = unT(yT[0]), unT(yT[1])
    post_full = [r_k, P['rk_ln_gain'], P['rk_ln_bias']]
    post_tiled = [y_0, y_1, (ps, C, 0), (ps, C, 2), k_0, k_1, g_gate]
    (y_rk,), _ = rowcall(lambda *a: ((_rk_post_fn(*a),), ()), "rk_post", post_tiled, post_full, [(C, BF16)], [], TS)

    ycat = jnp.concatenate([y_s5, y_rk], axis=1)
    mixed = matmul(ycat, w_out, name="out_mm")

    def res_norm(xv, mv, gate, g, sc, sh):
        x1v = xv + gate * mv
        return x1v, _normmod(x1v, g, sc, sh)

    (x1, h2), _ = rowcall(lambda *a: (res_norm(*a), ()), "norm2", [x, mixed], [gate1, P['norm2_gain'], scale2, shift2],
                          [(D, F32), (D, BF16)], [], TT)
    a_ff, hh_ff = matmul(h2, w1_g, name="ffn1_mm", b_slots=True, out_dtypes=(F32, BF16),
                         epi=lambda acc: (acc, jnp.square(jnp.maximum(acc, 0.0))))
    ffn = matmul(hh_ff, w2, name="ffn2_mm")

    def loss_fn(x1v, fv, tg, gate, fg):
        def f(x1_, f_, gate_, fg_):
            out = _rms(x1_ + gate_ * f_, fg_)
            err = out - tg
            return 0.5 * jnp.sum(jnp.sum(err * err, axis=1, keepdims=True), axis=0, keepdims=True) * (1.0 / D)
        lv, vjp = jax.vjp(f, x1v, fv, gate, fg)
        dx1, dff, dgate, dfg = vjp(jnp.ones((1, 1), F32))
        return (dx1, dff), (jnp.broadcast_to(lv, (SUBLANE, LANE)), dgate, dfg)

    (dx2, dffn), (loss_t, dgate2, dfgain) = rowcall(
        loss_fn, "loss", [x1, ffn, target], [gate2, fgain], [(D, F32), (D, BF16)], [(SUBLANE, LANE), (1, D), (1, D)], TT)
    loss = lax.psum(loss_t[0, 0], MESH_AXES)

    da = matmul(dffn, w2, name="dffn2_mm", tb=True, out_dtypes=(BF16,), extras=[(a_ff, 'mn')],
                epi=lambda acc, av: (acc * (2.0 * jnp.maximum(av, 0.0)),))
    g_w2 = matmul(hh_ff, dffn, name="gw2_mm", ta=True, out_dtypes=(BF16,))
    dh2 = matmul(da, w1_g, name="dh2_mm", tb=True, b_slots=True)
    g_w1 = matmul(h2, da, name="gw1_mm", ta=True, out_slots=N_DEV, out_dtypes=(BF16,))

    def res_norm_bwd(dx2v, dh2v, xv, mv, gate, g, sc, sh):
        _, vjp = jax.vjp(res_norm, xv, mv, gate, g, sc, sh)
        dx, dm, dgate, dg, dsc, dsh = vjp((dx2v, dh2v))
        return (dx, dm), (dgate, dg, dsc, dsh)

    (dx1, dmixed), (dgate1, dgain2, dscale2, dshift2) = rowcall(
        res_norm_bwd, "norm2_bwd", [dx2, dh2, x, mixed], [gate1, P['norm2_gain'], scale2, shift2],
        [(D, F32), (D, BF16)], [(1, D)] * 4, TT)

    dycat = matmul(dmixed, w_out, name="dycat_mm", tb=True)
    g_wout = matmul(ycat, dmixed, name="gwout_mm", ta=True, out_dtypes=(BF16,))

    def post_bwd(dy, *a):
        _, vjp = jax.vjp(_rk_post_fn, *a)
        gy0, gy1, gr, gv, gk0, gk1, gg, grk, glg, glb = vjp(dy)
        return (gy0, gr, gv, gk0, gk1, gg), (grk, glg, glb)

    cb_rk = S5W // C if C else 0
    (dy_rk, dr_p, dv_p, dk0_p, dk1_p, dg_p), (g_rk_rk, g_lng, g_lnb) = rowcall(
        post_bwd, "rk_post_bwd", [(dycat, C, cb_rk)] + post_tiled, post_full, [(C, F32)] * 6, [(1, C)] * 3, TS)
    dy_cols = rk_spread(hmT(dy_rk), "rk_spread_dy")
    scan_g = []
    for d in range(N_DIR):
        wd, kd, bd = dir_rows[d]
        scan_g.append(rk_scan_bwd(r_t, kkn, wd, kd, bd, v_cols, dy_cols, hist[d], reverse=(d == 1), name=f"rk_scan_b{d}"))
    cot = [dr_p, scan_g[0][0], scan_g[1][0],
           dv_p, unT(scan_g[0][5]), unT(scan_g[1][5]),
           scan_g[0][1], scan_g[1][1],
           scan_g[0][2], scan_g[1][2],
           dk0_p, scan_g[0][3], dk1_p, scan_g[1][3],
           scan_g[0][4], scan_g[1][4],
           dg_p]

    def pre_bwd(psv, r0, r1, r2, v0, v1, v2, q0, q1, dw0, dw1, k0a, k0b, k1a, k1b, db0, db1, dgv, *params):
        _, vjp = jax.vjp(_rk_pre_fn, psv, *params)
        grads = vjp((r0 + r1 + r2, v0 + v1 + v2, q0 + q1, dw0, dw1, k0a + k0b, k1a + k1b, db0, db1, dgv))
        return (grads[0],), tuple(grads[1:])

    (dps,), pre_g = rowcall(pre_bwd, "rk_pre_bwd", [ps] + cot, pre_full, [(RINP, F32)],
                            [f.shape for f in pre_full], TS)
    g_w0, g_wup0, g_wup1, g_a0, g_aup0, g_aup1, g_gup_p, g_kk, g_ka = pre_g
    g_wup_p, g_aup_p = jnp.stack([g_wup0, g_wup1]), jnp.stack([g_aup0, g_aup1])
    dp_rk, g_mup, g_mun = rk_shift_bwd(dps, proj, mu_prev, mu_next, S5W)

    def s5_out_bwd(dy, *a):
        a = [t.astype(F32) for t in a]
        _, vjp = jax.vjp(_s5_out_fn, *a)
        g = vjp(dy)
        return (g[0], g[1], g[4]), tuple(g[5:])

    (dxr, dxi, du_a), s5_pg = rowcall(
        s5_out_bwd, "s5_out_bwd", [(dycat, S5W, 0), s0r, s0i, s1r, s1i, u_view], s5_full,
        [(NS, F32), (NS, F32), (S5W, F32)], [f.shape for f in s5_full], TS)
    g_creb, g_cimb, g_s5d, g_wglu, g_bglu = s5_pg
    l0r, l0i = s5_scan(dxr, dxi, lbr[0:1], -lbi[0:1], reverse=True, name="s5_scan_b0")
    l1r, l1i = s5_scan(dxr, dxi, lbr[1:2], -lbi[1:2], reverse=False, name="s5_scan_b1")
    dl0r, dl0i = s5_dlam(s0r, s0i, l0r, l0i, reverse=False, name="s5_dlam0")
    dl1r, dl1i = s5_dlam(s1r, s1i, l1r, l1i, reverse=True, name="s5_dlam1")

    def bu_bwd(uv, g0, g1, g2, g3, wb):
        _, vjp = jax.vjp(_s5_bu_fn, uv, wb.astype(F32))
        du, dwb = vjp((g0, g1, g2, g3))
        return (du,), (dwb,)

    (du_b,), (g_wblk,) = rowcall(bu_bwd, "s5_bu_bwd", [u_view, l0r, l0i, l1r, l1i], [wblk], [(S5W, F32)],
                                 [wblk.shape], TS)
    g_bbar = jnp.einsum('cghdrkp,gk->drhcgp',
                        g_wblk.reshape(NCH, S5_CHUNK_GROUPS, S5_GROUP, N_DIR, 2, S5_CHUNK_GROUPS, S5_STATE), eye)
    g_bbar = g_bbar.reshape(N_DIR, 2, S5_GROUP, NS)
    g_lre, g_lim, g_lstep, g_bre, g_bim = s5_prep_bwd(
        lre, lim, lstep, bre, bim, jnp.concatenate([dl0r, dl1r], 0), jnp.concatenate([dl0i, dl1i], 0),
        g_bbar[:, 0], g_bbar[:, 1])

    def uncblk(gb):
        g5 = gb.reshape(NCH, S5_CHUNK_GROUPS, S5_STATE, S5_CHUNK_GROUPS, S5_GROUP)
        return jnp.einsum('cgpkh,gk->cghp', g5, eye).reshape(G, S5_GROUP, S5_STATE)

    (du_tot,), _ = rowcall(lambda a, b_: ((a + b_,), ()), "s5_du_sum", [du_a, du_b], [], [(S5W, BF16)], [], TT)
    dproj = jnp.concatenate([du_tot, dp_rk.astype(BF16)], axis=1)
    dh1 = matmul(dproj, w_in, name="dh1_mm", tb=True)
    g_win = matmul(h1, dproj, name="gwin_mm", ta=True, out_dtypes=(BF16,))

    def norm1_bwd(dx1v, dh1v, xv, g, sc, sh):
        _, vjp = jax.vjp(_normmod, xv, g, sc, sh)
        dx, dg, dsc, dsh = vjp(dh1v)
        return (dx1v + dx,), (dg, dsc, dsh)

    (grad_x,), (dgain1, dscale1, dshift1) = rowcall(
        norm1_bwd, "norm1_bwd", [dx1, dh1, x], [P['norm1_gain'], scale1, shift1], [(D, F32)], [(1, D)] * 3, TT)

    dmod = jnp.concatenate([dshift1, dscale1, dgate1, dshift2, dscale2, dgate2], axis=1)
    lstep_g = g_lstep.reshape(N_DIR, G, S5_STATE)
    small_g = {
        'ada_b': dmod, 'norm1_gain': dgain1, 'norm2_gain': dgain2, 'final_gain': dfgain.reshape(D),
        's5_lambda_re': g_lre.reshape(1, N_DIR, G, S5_STATE), 's5_lambda_im': g_lim.reshape(1, N_DIR, G, S5_STATE),
        's5_log_step': lstep_g,
        's5_b_re': g_bre.T.reshape(1, G, S5_STATE, S5_GROUP), 's5_b_im': g_bim.T.reshape(1, G, S5_STATE, S5_GROUP),
        's5_c_re': uncblk(g_creb)[None], 's5_c_im': uncblk(g_cimb)[None],
        's5_d': g_s5d, 's5_b_glu': g_bglu,
        'rk_shift_prev': g_mup[:, :RIN], 'rk_shift_next': g_mun[:, :RIN],
        'rk_k_k': g_kk, 'rk_k_a': g_ka, 'rk_r_k': g_rk_rk.reshape(1, H, RWKV_HEAD),
        'rk_ln_gain': g_lng, 'rk_ln_bias': g_lnb,
    }
    small_shapes = {n: P[n].shape for n in SMALL}
    small_shapes['s5_log_step'] = (N_DIR, G, S5_STATE)
    small_pack = _pack_flat([small_g[n] for n in SMALL])

    def lora_unpad(gp):
        return jnp.stack([gp[d, d * DECAY_LORA:(d + 1) * DECAY_LORA] for d in range(N_DIR)])

    rk_grads = {'rk_w0': g_w0, 'rk_a0': g_a0, 'rk_w_up': lora_unpad(g_wup_p), 'rk_a_up': lora_unpad(g_aup_p),
                'rk_g_up': g_gup_p[:GATE_LORA]}
    rk_gpack = jnp.stack([_pack_rows([_cols_to_slots(rk_grads[n])[j] for n in RKPACK]) for j in range(N_DEV)])
    g_win_s = _cols_to_slots(g_win[:, :PROJ])
    (small_all,) = gather_two_level([small_pack], "comm_gather_small_grads")
    slots = [g_win_s, g_wout.reshape(N_DEV, D // N_DEV, D), g_w1, g_w2.reshape(N_DEV, FF // N_DEV, D),
             g_wglu.reshape(N_DEV, S5W // N_DEV, S5W), rk_gpack]
    my_c = lax.axis_index("c")

    def core_half(a, core):
        return lax.dynamic_index_in_dim(a.reshape((N_DEV // 2, 2) + a.shape[1:]), core, axis=1, keepdims=False)

    from_sibling = swap_sibling([core_half(a, 1 - my_c) for a in slots], "comm_swap_grads")
    chip_sums = []
    for a, got, nm in zip(slots, from_sibling, ['w_in', 'w_out', 'ffn_w1', 'ffn_w2', 's5_w_glu', 'rkpack']):
        own = core_half(a, my_c)
        flat = (own.shape[0] * math.prod(own.shape[1:-1]), own.shape[-1])
        (sm,), _ = rowcall(lambda u, v_: ((u.astype(F32) + v_.astype(F32),), ()), "chip_sum_" + nm,
                           [own.reshape(flat), got.reshape(flat)], [], [(flat[1], a.dtype)], [], 512)
        chip_sums.append(sm.reshape(own.shape))
    win_parts, wout_parts, w1_parts, w2_parts, wglu_parts, rk_parts = exchange_chips(chip_sums, "comm_grads")

    res = {}

    def put(name, g, dl, m2, v2):
        shp = P[name].shape
        res[name] = tuple(t.reshape(shp) for t in (g, dl, m2, v2))

    def adam2d(name, parts):
        shp = P[name].shape
        r2 = (math.prod(shp[:-1]), shp[-1])
        put(name, *adam(parts.reshape((parts.shape[0],) + r2), P[name].reshape(r2), M[name].reshape(r2),
                        V[name].reshape(r2), "adam_" + name))

    adam2d('w_in', win_parts)
    adam2d('w_out', wout_parts)
    adam2d('ffn_w1', w1_parts)
    adam2d('ffn_w2', w2_parts)
    adam2d('s5_w_glu', wglu_parts)
    off = 0
    for n in SMALL:
        if n == 'ada_b':
            break
        off += _round_up(math.prod(small_shapes[n]), SUBLANE * LANE) // LANE
    nrow_b = N_MOD * D // LANE
    dmod_all = small_all[:, off:off + nrow_b].reshape(N_DEV, N_MOD * D)
    dmod_cols = lax.dynamic_slice(dmod_all, (0, me * cs_mod), (N_DEV, cs_mod))
    g_adaw = matmul(c_act, dmod_cols, name="gadaw_mm", ta=True, precise=True)
    adam2d('ada_w', g_adaw[None])
    small_w = dict(P)
    small_m, small_v = dict(M), dict(V)
    rk_res = adam(rk_parts, rk_pack, _pack_rows([M[n][0] for n in RKPACK]), _pack_rows([V[n][0] for n in RKPACK]),
                  "adam_rkpack")
    for name, parts4 in zip(RKPACK, zip(*[_unpack_rows(t, rk_shapes) for t in rk_res])):
        put(name, *parts4)
    return loss, grad_x, res, (small_all, small_shapes, small_w, small_m, small_v)


def _small_update(small_all, small_shapes, P, M, V, res):
    G = (D_MODEL // 2) // S5_GROUP
    names = [n for n in SMALL if n != 's5_log_step']
    shapes = [small_shapes[n] for n in SMALL]
    parts = _unpack_flat_batched(small_all, shapes)
    by = dict(zip(SMALL, parts))
    ls = by['s5_log_step']
    ls = ls.transpose(0, 3, 1, 2).reshape(N_DEV * S5_STATE, N_DIR * G)
    pk = lambda d: _pack_flat([d[n] for n in names])
    packs = jnp.stack([_pack_flat([by[n][j] for n in names]) for j in range(N_DEV)])
    out = adam(packs, pk(P), pk(M), pk(V), "adam_small")
    shp = [P[n].shape for n in names]
    for name, parts4 in zip(names, zip(*[_unpack_flat(t, shp) for t in out])):
        res[name] = parts4
    lsw = lambda d: jnp.pad(d['s5_log_step'].reshape(1, N_DIR * G), ((0, SUBLANE - 1), (0, 0)))
    ls_parts = jnp.pad(ls[:, None, :], ((0, 0), (0, SUBLANE - 1), (0, 0)))
    o = adam(ls_parts, lsw(P), lsw(M), lsw(V), "adam_log_step")
    res['s5_log_step'] = tuple(t[0:1].reshape(P['s5_log_step'].shape) for t in o)


def _unpack_flat_batched(packed, shapes):
    out, r0 = [], 0
    B = packed.shape[0]
    for s in shapes:
        n = math.prod(s)
        nr = _round_up(n, SUBLANE * LANE) // LANE
        out.append(packed[:, r0:r0 + nr].reshape(B, -1)[:, :n].reshape((B,) + tuple(s)))
        r0 += nr
    return out


def kernel(x, c, ada_w, ada_b, norm1_gain, norm2_gain, final_gain, w_in, w_out, s5_lambda_re, s5_lambda_im, s5_log_step, s5_b_re, s5_b_im, s5_c_re, s5_c_im, s5_d, s5_w_glu, s5_b_glu, rk_shift_prev, rk_shift_next, rk_w0, rk_w_up, rk_a0, rk_a_up, rk_g_up, rk_k_k, rk_k_a, rk_r_k, rk_ln_gain, rk_ln_bias, ffn_w1, ffn_w2, loss_target, m_ada_w, m_ada_b, m_norm1_gain, m_norm2_gain, m_final_gain, m_w_in, m_w_out, m_s5_lambda_re, m_s5_lambda_im, m_s5_log_step, m_s5_b_re, m_s5_b_im, m_s5_c_re, m_s5_c_im, m_s5_d, m_s5_w_glu, m_s5_b_glu, m_rk_shift_prev, m_rk_shift_next, m_rk_w0, m_rk_w_up, m_rk_a0, m_rk_a_up, m_rk_g_up, m_rk_k_k, m_rk_k_a, m_rk_r_k, m_rk_ln_gain, m_rk_ln_bias, m_ffn_w1, m_ffn_w2, v_ada_w, v_ada_b, v_norm1_gain, v_norm2_gain, v_final_gain, v_w_in, v_w_out, v_s5_lambda_re, v_s5_lambda_im, v_s5_log_step, v_s5_b_re, v_s5_b_im, v_s5_c_re, v_s5_c_im, v_s5_d, v_s5_w_glu, v_s5_b_glu, v_rk_shift_prev, v_rk_shift_next, v_rk_w0, v_rk_w_up, v_rk_a0, v_rk_a_up, v_rk_g_up, v_rk_k_k, v_rk_k_a, v_rk_r_k, v_rk_ln_gain, v_rk_ln_bias, v_ffn_w1, v_ffn_w2):
    given = dict(locals())
    P = {n: given[n] for n in ['x', 'c', 'loss_target'] + WEIGHTS}
    M = {n: given['m_' + n] for n in WEIGHTS}
    V = {n: given['v_' + n] for n in WEIGHTS}
    loss, grad_x, res, small = _step(P, M, V)
    small_all, small_shapes, _, _, _ = small
    _small_update(small_all, small_shapes, P, M, V, res)
    outs = [loss, grad_x[None]]
    for q in range(4):
        outs += [res[n][q] for n in WEIGHTS]
    return tuple(outs)
```

```python
import functools
import math

import jax
import jax.numpy as jnp
from jax import lax
from jax.experimental import pallas as pl
from jax.experimental.pallas import tpu as pltpu

F32 = jnp.float32
BF16 = jnp.bfloat16
HI = lax.Precision.HIGHEST
MXU_DTYPE = jnp.bfloat16

N_DEV = 8
MESH_AXES = ("x", "y", "c")
D_MODEL = 2048
SEQ = 2048
S5_GROUP = 16
S5_STATE = 64
RWKV_HEAD = 64
DECAY_LORA = 64
GATE_LORA = 160
N_DIR = 2
N_MOD = 6
NORM_EPS = 1e-6
GN_EPS = 64e-5
L2_EPS = 1e-12
ADAM_LR = 0.001
ADAM_B1 = 0.9
ADAM_B2 = 0.999
ADAM_EPS = 1e-08
ADAM_WD = 0.01
ADAM_STEP = 10
LANE = 128
SUBLANE = 8
S5_CHUNK_GROUPS = 8
S5_SCAN_ROWS = 256
VMEM_LIMIT = 56 * 1024 * 1024

WEIGHTS = ['ada_w', 'ada_b', 'norm1_gain', 'norm2_gain', 'final_gain', 'w_in', 'w_out', 's5_lambda_re',
           's5_lambda_im', 's5_log_step', 's5_b_re', 's5_b_im', 's5_c_re', 's5_c_im', 's5_d', 's5_w_glu',
           's5_b_glu', 'rk_shift_prev', 'rk_shift_next', 'rk_w0', 'rk_w_up', 'rk_a0', 'rk_a_up', 'rk_g_up',
           'rk_k_k', 'rk_k_a', 'rk_r_k', 'rk_ln_gain', 'rk_ln_bias', 'ffn_w1', 'ffn_w2']
SMALL = ['ada_b', 'norm1_gain', 'norm2_gain', 'final_gain', 's5_lambda_re', 's5_lambda_im', 's5_log_step',
         's5_b_re', 's5_b_im', 's5_c_re', 's5_c_im', 's5_d', 's5_b_glu', 'rk_shift_prev', 'rk_shift_next',
         'rk_k_k', 'rk_k_a', 'rk_r_k', 'rk_ln_gain', 'rk_ln_bias']
RKPACK = ['rk_w0', 'rk_a0', 'rk_w_up', 'rk_a_up', 'rk_g_up']


def _round_up(n, m):
    return (n + m - 1) // m * m


def _tile(dim, pref, unit=LANE):
    t = min(pref, dim) // unit * unit
    while t >= unit:
        if dim % t == 0:
            return t
        t -= unit
    return dim


def _cparams(sem=None):
    return pltpu.CompilerParams(dimension_semantics=sem, vmem_limit_bytes=VMEM_LIMIT)


def _full_spec(a):
    nd = a.ndim
    return pl.BlockSpec(a.shape, lambda *_: (0,) * nd)


def exchange(arrs, modes, name):
    n = len(arrs)
    out_shape = [jax.ShapeDtypeStruct((N_DEV,) + a.shape if m == 'gather' else a.shape, a.dtype)
                 for a, m in zip(arrs, modes)]

    def body(*refs):
        ins, outs = refs[:n], refs[n:2 * n]
        send_sems, recv_sems, local_sems = refs[2 * n:]
        x, y, c = (lax.axis_index(a) for a in MESH_AXES)
        me = 4 * x + 2 * y + c
        copies = []
        for i in range(n):
            gather = modes[i] == 'gather'
            mine = pltpu.make_async_copy(ins[i] if gather else ins[i].at[me], outs[i].at[me], local_sems.at[i])
            mine.start()
            copies.append(mine)
        remote = []
        for k in range(1, N_DEV):
            px = 1 - x if (k >> 2) & 1 else x
            py = 1 - y if (k >> 1) & 1 else y
            pc = 1 - c if k & 1 else c
            peer = 4 * px + 2 * py + pc
            for i in range(n):
                src = ins[i] if modes[i] == 'gather' else ins[i].at[peer]
                cp = pltpu.make_async_remote_copy(
                    src_ref=src, dst_ref=outs[i].at[me], send_sem=send_sems.at[i, k - 1],
                    recv_sem=recv_sems.at[i, k - 1], device_id=(px, py, pc), device_id_type=pl.DeviceIdType.MESH)
                cp.start()
                remote.append(cp)
        for cp in remote:
            cp.wait_recv()
        for cp in remote:
            cp.wait_send()
        for cp in copies:
            cp.wait()

    any_spec = pl.BlockSpec(memory_space=pl.ANY)
    return pl.pallas_call(
        body, name=name, out_shape=out_shape,
        in_specs=[any_spec] * n, out_specs=[any_spec] * n,
        scratch_shapes=[pltpu.SemaphoreType.DMA((n, N_DEV - 1)), pltpu.SemaphoreType.DMA((n, N_DEV - 1)),
                        pltpu.SemaphoreType.DMA((n,))],
        compiler_params=pltpu.CompilerParams(has_side_effects=True),
    )(*arrs)


def gather_two_level(arrs, name):
    n = len(arrs)
    out_shape = [jax.ShapeDtypeStruct((N_DEV,) + a.shape, a.dtype) for a in arrs]

    def body(*refs):
        ins, outs = refs[:n], refs[n:2 * n]
        send_sems, recv_sems, local_sems = refs[2 * n:]
        x, y, c = (lax.axis_index(a) for a in MESH_AXES)
        me, sibling = (x, y, c), (x, y, 1 - c)
        chips = [(1 - x, y), (x, 1 - y), (1 - x, 1 - y)]

        def slot(px, py, pc):
            return 4 * px + 2 * py + pc

        def copy(i, k, block, to, src=None):
            return pltpu.make_async_remote_copy(
                src_ref=outs[i].at[slot(*block)] if src is None else src, dst_ref=outs[i].at[slot(*block)],
                send_sem=send_sems.at[i, k], recv_sem=recv_sems.at[i, k], device_id=to,
                device_id_type=pl.DeviceIdType.MESH)

        mine = [pltpu.make_async_copy(ins[i], outs[i].at[slot(*me)], local_sems.at[i]) for i in range(n)]
        for cp in mine:
            cp.start()
        started = []
        for i in range(n):
            started.append(copy(i, 0, me, sibling, src=ins[i]))
            started += [copy(i, 1 + j, me, (*chip, c), src=ins[i]) for j, chip in enumerate(chips)]
        for cp in started:
            cp.start()
        for j, chip in enumerate(chips):
            for i in range(n):
                copy(i, 1 + j, (*chip, c), me).wait_recv()
                fwd = copy(i, 4 + j, (*chip, c), sibling)
                fwd.start()
                started.append(fwd)
        for i in range(n):
            copy(i, 0, sibling, me).wait_recv()
        for j, chip in enumerate(chips):
            for i in range(n):
                copy(i, 4 + j, (*chip, 1 - c), me).wait_recv()
        for cp in started:
            cp.wait_send()
        for cp in mine:
            cp.wait()

    any_spec = pl.BlockSpec(memory_space=pl.ANY)
    return pl.pallas_call(
        body, name=name, out_shape=out_shape,
        in_specs=[any_spec] * n, out_specs=[any_spec] * n,
        scratch_shapes=[pltpu.SemaphoreType.DMA((n, N_DEV - 1)), pltpu.SemaphoreType.DMA((n, N_DEV - 1)),
                        pltpu.SemaphoreType.DMA((n,))],
        compiler_params=pltpu.CompilerParams(has_side_effects=True),
    )(*arrs)


def swap_sibling(arrs, name):
    n = len(arrs)

    def body(*refs):
        ins, outs = refs[:n], refs[n:2 * n]
        send_sems, recv_sems = refs[2 * n:]
        x, y, c = (lax.axis_index(a) for a in MESH_AXES)
        cps = [pltpu.make_async_remote_copy(src_ref=ins[i], dst_ref=outs[i], send_sem=send_sems.at[i],
                                            recv_sem=recv_sems.at[i], device_id=(x, y, 1 - c),
                                            device_id_type=pl.DeviceIdType.MESH) for i in range(n)]
        for cp in cps:
            cp.start()
        for cp in cps:
            cp.wait()

    any_spec = pl.BlockSpec(memory_space=pl.ANY)
    return pl.pallas_call(
        body, name=name, out_shape=[jax.ShapeDtypeStruct(a.shape, a.dtype) for a in arrs],
        in_specs=[any_spec] * n, out_specs=[any_spec] * n,
        scratch_shapes=[pltpu.SemaphoreType.DMA((n,)), pltpu.SemaphoreType.DMA((n,))],
        compiler_params=pltpu.CompilerParams(has_side_effects=True),
    )(*arrs)


def exchange_chips(arrs, name):
    n = len(arrs)

    def body(*refs):
        ins, outs = refs[:n], refs[n:2 * n]
        send_sems, recv_sems, local_sems = refs[2 * n:]
        x, y, c = (lax.axis_index(a) for a in MESH_AXES)
        mine = 2 * x + y
        chips = [(1 - x, y), (x, 1 - y), (1 - x, 1 - y)]
        local = [pltpu.make_async_copy(ins[i].at[mine], outs[i].at[mine], local_sems.at[i]) for i in range(n)]
        for cp in local:
            cp.start()
        remote = []
        for j, (px, py) in enumerate(chips):
            for i in range(n):
                cp = pltpu.make_async_remote_copy(
                    src_ref=ins[i].at[2 * px + py], dst_ref=outs[i].at[mine], send_sem=send_sems.at[i, j],
                    recv_sem=recv_sems.at[i, j], device_id=(px, py, c), device_id_type=pl.DeviceIdType.MESH)
                cp.start()
                remote.append(cp)
        for cp in remote:
            cp.wait_recv()
        for cp in remote:
            cp.wait_send()
        for cp in local:
            cp.wait()

    any_spec = pl.BlockSpec(memory_space=pl.ANY)
    return pl.pallas_call(
        body, name=name, out_shape=[jax.ShapeDtypeStruct(a.shape, a.dtype) for a in arrs],
        in_specs=[any_spec] * n, out_specs=[any_spec] * n,
        scratch_shapes=[pltpu.SemaphoreType.DMA((n, 3)), pltpu.SemaphoreType.DMA((n, 3)), pltpu.SemaphoreType.DMA((n,))],
        compiler_params=pltpu.CompilerParams(has_side_effects=True),
    )(*arrs)


def matmul(a, b, *, name, ta=False, tb=False, b_slots=False, out_slots=0, out_dtypes=(F32,), epi=None,
           extras=(), precise=False, tm=512, tn=1024, tk=2048):
    if ta:
        K, M = a.shape
    else:
        M, K = a.shape
    if b_slots:
        ns, br, bc = b.shape
        bshape = (br, ns * bc)
    else:
        bshape = b.shape
    N = bshape[0] if tb else bshape[1]
    assert (bshape[1] if tb else bshape[0]) == K, (a.shape, b.shape, ta, tb)
    tm, tn, tk = _tile(M, tm, SUBLANE), _tile(N, tn), _tile(K, tk, SUBLANE if K < LANE else LANE)
    if b_slots and tb:
        tk = _tile(b.shape[2], tk)
    elif b_slots:
        tn = _tile(b.shape[2], tn)
    if out_slots:
        tn = _tile(N // out_slots, tn)
    if b_slots:
        cs = b.shape[2]
        tcol = tk if tb else tn
        assert cs % tcol == 0
        per = cs // tcol
    if out_slots:
        ncs = N // out_slots
        assert ncs % tn == 0
        operc = ncs // tn
    nk = K // tk
    a_spec = pl.BlockSpec((tk, tm), lambda i, j, k: (k, i)) if ta else pl.BlockSpec((tm, tk), lambda i, j, k: (i, k))
    if b_slots:
        if tb:
            b_spec = pl.BlockSpec((None, tn, tk), lambda i, j, k: (k // per, j, k % per))
        else:
            b_spec = pl.BlockSpec((None, tk, tn), lambda i, j, k: (j // per, k, j % per))
    else:
        b_spec = pl.BlockSpec((tn, tk), lambda i, j, k: (j, k)) if tb else pl.BlockSpec((tk, tn), lambda i, j, k: (k, j))
    ex_specs = []
    for arr, kind in extras:
        if kind == 'mn':
            ex_specs.append(pl.BlockSpec((tm, tn), lambda i, j, k: (i, j)))
        else:
            ex_specs.append(pl.BlockSpec((1, tn), lambda i, j, k: (0, j)))
    if out_slots:
        o_spec = pl.BlockSpec((None, tm, tn), lambda i, j, k: (j // operc, i, j % operc))
        o_shape = (out_slots, M, ncs)
    else:
        o_spec = pl.BlockSpec((tm, tn), lambda i, j, k: (i, j))
        o_shape = (M, N)
    ne, no = len(extras), len(out_dtypes)
    dims = (((0 if ta else 1,), (1 if tb else 0,)), ((), ()))
    op_dtype = F32 if precise else MXU_DTYPE

    def body(a_ref, b_ref, *rest):
        ex_refs, out_refs, acc = rest[:ne], rest[ne:ne + no], rest[-1]
        k = pl.program_id(2)
        part = lax.dot_general(a_ref[...].astype(op_dtype), b_ref[...].astype(op_dtype), dims,
                               precision=HI if precise else None, preferred_element_type=F32)

        def finish(total):
            res = epi(total, *[e[...] for e in ex_refs]) if epi is not None else (total,)
            for o, r in zip(out_refs, res):
                o[...] = r.astype(o.dtype)

        if nk == 1:
            finish(part)
        else:
            @pl.when(k == 0)
            def _():
                acc[...] = part

            @pl.when(jnp.logical_and(k > 0, k < nk - 1))
            def _():
                acc[...] += part

            @pl.when(k == nk - 1)
            def _():
                finish(acc[...] + part)

    outs = pl.pallas_call(
        body, name=name, grid=(M // tm, N // tn, nk),
        in_specs=[a_spec, b_spec] + ex_specs, out_specs=[o_spec] * no,
        out_shape=[jax.ShapeDtypeStruct(o_shape, dt) for dt in out_dtypes],
        scratch_shapes=[pltpu.VMEM((tm, tn), F32)],
        compiler_params=_cparams(("parallel", "parallel", "arbitrary")),
    )(a, b, *[e[0] for e in extras])
    return outs[0] if no == 1 else outs


def rowcall(fn, name, tiled, full, tiled_out, acc_out, tt):
    views = [(t, t.shape[1], 0) if not isinstance(t, tuple) else t for t in tiled]
    T = views[0][0].shape[0]
    tt = _tile(T, tt, SUBLANE)
    nt, nf, nto, nao = len(views), len(full), len(tiled_out), len(acc_out)

    def view_spec(w, cb):
        return pl.BlockSpec((tt, w), lambda i: (i, cb))

    in_specs = [view_spec(w, cb) for _, w, cb in views] + [_full_spec(f) for f in full]
    out_specs = [pl.BlockSpec((tt, w), lambda i: (i, 0)) for w, _ in tiled_out]
    out_specs += [pl.BlockSpec(s, lambda i, nd=len(s): (0,) * nd) for s in acc_out]
    out_shape = [jax.ShapeDtypeStruct((T, w), dt) for w, dt in tiled_out]
    out_shape += [jax.ShapeDtypeStruct(s, F32) for s in acc_out]

    def body(*refs):
        tin, fin = refs[:nt], refs[nt:nt + nf]
        tout, aout = refs[nt + nf:nt + nf + nto], refs[nt + nf + nto:]
        touts, aouts = fn(*[r[...] for r in tin], *[r[...] for r in fin])
        for r, v in zip(tout, touts):
            r[...] = v.astype(r.dtype)
        if nao:
            @pl.when(pl.program_id(0) == 0)
            def _():
                for r in aout:
                    r[...] = jnp.zeros_like(r)

            for r, v in zip(aout, aouts):
                r[...] += v.astype(F32)

    outs = pl.pallas_call(
        body, name=name, grid=(T // tt,), in_specs=in_specs, out_specs=out_specs, out_shape=out_shape,
        compiler_params=_cparams(("arbitrary",) if nao else ("parallel",)),
    )(*[v[0] for v in views], *full)
    return outs[:nto], outs[nto:]


def _mm(a, b):
    return jnp.dot(a.astype(MXU_DTYPE), b.astype(MXU_DTYPE), preferred_element_type=F32)


def _rms(x, gain):
    ms = jnp.mean(x * x, axis=-1, keepdims=True)
    return x * lax.rsqrt(ms + NORM_EPS) * gain


def _normmod(x, gain, scale, shift):
    return _rms(x, gain) * (1.0 + scale) + shift


def _gelu_tanh(y):
    return 0.5 * y * (1.0 + jnp.tanh(math.sqrt(2.0 / math.pi) * (y + 0.044715 * (y * y * y))))


def _sigmoid(x):
    return 1.0 / (1.0 + jnp.exp(-x))


def _softplus(x):
    return jnp.maximum(x, 0.0) + jnp.log(1.0 + jnp.exp(-jnp.abs(x)))


def _seg_mats(width, seg):
    r = lax.broadcasted_iota(jnp.int32, (width, LANE), 0) // seg
    c = lax.broadcasted_iota(jnp.int32, (width, LANE), 1)
    s = (r == c).astype(F32)
    rt = lax.broadcasted_iota(jnp.int32, (LANE, width), 0)
    ct = lax.broadcasted_iota(jnp.int32, (LANE, width), 1) // seg
    st = (rt == ct).astype(F32)
    return s, st


def _segsum(x, s):
    return jnp.dot(x, s, precision=HI, preferred_element_type=F32)


def _s5_prep_fn(lre, lim, lstep, bre, bim):
    step = jnp.exp(lstep)
    mag = jnp.exp(lre * step)
    lbr = mag * jnp.cos(lim * step)
    lbi = mag * jnp.sin(lim * step)
    den = lre * lre + lim * lim
    nr = lbr - 1.0
    ni = lbi
    cre = (nr * lre + ni * lim) / den
    cim = (ni * lre - nr * lim) / den
    bbr = jnp.stack([cre[d:d + 1] * bre - cim[d:d + 1] * bim for d in range(N_DIR)])
    bbi = jnp.stack([cre[d:d + 1] * bim + cim[d:d + 1] * bre for d in range(N_DIR)])
    return lbr, lbi, bbr, bbi


def s5_prep(lre, lim, lstep, bre, bim):
    ns = lre.shape[1]

    def body(lre_r, lim_r, ls_r, bre_r, bim_r, lbr_r, lbi_r, bbr_r, bbi_r):
        lbr, lbi, bbr, bbi = _s5_prep_fn(lre_r[...], lim_r[...], ls_r[...], bre_r[...], bim_r[...])
        lbr_r[...] = lbr
        lbi_r[...] = lbi
        bbr_r[...] = bbr
        bbi_r[...] = bbi

    return pl.pallas_call(
        body, name="s5_prep",
        out_shape=[jax.ShapeDtypeStruct((N_DIR, ns), F32)] * 2 + [jax.ShapeDtypeStruct((N_DIR, S5_GROUP, ns), F32)] * 2,
        compiler_params=_cparams(),
    )(lre, lim, lstep, bre, bim)


def s5_prep_bwd(lre, lim, lstep, bre, bim, dlbr, dlbi, dbbr, dbbi):
    ns = lre.shape[1]

    def body(lre_r, lim_r, ls_r, bre_r, bim_r, d1, d2, d3, d4, o1, o2, o3, o4, o5):
        _, vjp = jax.vjp(_s5_prep_fn, lre_r[...], lim_r[...], ls_r[...], bre_r[...], bim_r[...])
        g = vjp((d1[...], d2[...], d3[...], d4[...]))
        for o, v in zip((o1, o2, o3, o4, o5), g):
            o[...] = v

    return pl.pallas_call(
        body, name="s5_prep_bwd",
        out_shape=[jax.ShapeDtypeStruct((N_DIR, ns), F32)] * 3 + [jax.ShapeDtypeStruct((S5_GROUP, ns), F32)] * 2,
        compiler_params=_cparams(),
    )(lre, lim, lstep, bre, bim, dlbr, dlbi, dbbr, dbbi)


def s5_scan(bre, bim, lre, lim, *, reverse, name):
    T, NS = bre.shape
    tt = _tile(T, S5_SCAN_ROWS, SUBLANE)
    wl = _tile(NS, 512)
    nT = T // tt
    ngrp = tt // SUBLANE

    def tmap(j, i):
        return ((nT - 1 - i) if reverse else i, j)

    def body(bre_r, bim_r, lre_r, lim_r, sre_r, sim_r, cre, cim):
        @pl.when(pl.program_id(1) == 0)
        def _():
            cre[...] = jnp.zeros_like(cre)
            cim[...] = jnp.zeros_like(cim)

        lr = jnp.broadcast_to(lre_r[...], (SUBLANE, wl))
        li = jnp.broadcast_to(lim_r[...], (SUBLANE, wl))
        row = lax.broadcasted_iota(jnp.int32, (SUBLANE, wl), 0)
        pows = [(lr, li)]
        for _ in range(3):
            pr, pi = pows[-1]
            pows.append((pr * pr - pi * pi, 2.0 * pr * pi))
        e = (SUBLANE - row) if reverse else (row + 1)
        Pr = jnp.ones((SUBLANE, wl), F32)
        Pi = jnp.zeros((SUBLANE, wl), F32)
        for bit, (qr, qi) in enumerate(pows):
            on = ((e >> bit) & 1) == 1
            nr, ni = Pr * qr - Pi * qi, Pr * qi + Pi * qr
            Pr, Pi = jnp.where(on, nr, Pr), jnp.where(on, ni, Pi)

        def group(g, carry):
            gg = (ngrp - 1 - g) if reverse else g
            rows = pl.ds(pl.multiple_of(gg * SUBLANE, SUBLANE), SUBLANE)
            sr, si = bre_r[rows, :], bim_r[rows, :]
            for lvl, k in enumerate((1, 2, 4)):
                qr, qi = pows[lvl]
                if reverse:
                    shr = pltpu.roll(sr, SUBLANE - k, 0)
                    shi = pltpu.roll(si, SUBLANE - k, 0)
                    keep = row < SUBLANE - k
                else:
                    shr = pltpu.roll(sr, k, 0)
                    shi = pltpu.roll(si, k, 0)
                    keep = row >= k
                shr = jnp.where(keep, shr, 0.0)
                shi = jnp.where(keep, shi, 0.0)
                sr, si = sr + qr * shr - qi * shi, si + qr * shi + qi * shr
            cr, ci = carry
            sr, si = sr + Pr * cr - Pi * ci, si + Pr * ci + Pi * cr
            sre_r[rows, :] = sr
            sim_r[rows, :] = si
            last = 0 if reverse else SUBLANE - 1
            return (jnp.broadcast_to(sr[last:last + 1, :], (SUBLANE, wl)),
                    jnp.broadcast_to(si[last:last + 1, :], (SUBLANE, wl)))

        cr, ci = lax.fori_loop(0, ngrp, group, (cre[...], cim[...]), unroll=4 if ngrp % 4 == 0 else 1)
        cre[...] = cr
        cim[...] = ci

    blk = pl.BlockSpec((tt, wl), tmap)
    row_spec = pl.BlockSpec((1, wl), lambda j, i: (0, j))
    return pl.pallas_call(
        body, name=name, grid=(NS // wl, nT),
        in_specs=[blk, blk, row_spec, row_spec], out_specs=[blk, blk],
        out_shape=[jax.ShapeDtypeStruct((T, NS), F32)] * 2,
        scratch_shapes=[pltpu.VMEM((SUBLANE, wl), F32)] * 2,
        compiler_params=_cparams(("parallel", "arbitrary")),
    )(bre, bim, lre, lim)


def s5_dlam(sre, sim, gre, gim, *, reverse, name):
    T, NS = sre.shape
    wl = _tile(NS, 256)

    def body(sr_r, si_r, gr_r, gi_r, dr_r, di_r):
        row = lax.broadcasted_iota(jnp.int32, (T, wl), 0)
        if reverse:
            keep = row < T - 1
            pr = jnp.where(keep, pltpu.roll(sr_r[...], T - 1, 0), 0.0)
            pi = jnp.where(keep, pltpu.roll(si_r[...], T - 1, 0), 0.0)
        else:
            keep = row >= 1
            pr = jnp.where(keep, pltpu.roll(sr_r[...], 1, 0), 0.0)
            pi = jnp.where(keep, pltpu.roll(si_r[...], 1, 0), 0.0)
        gr, gi = gr_r[...], gi_r[...]
        dr_r[...] = jnp.sum(pr * gr + pi * gi, axis=0, keepdims=True)
        di_r[...] = jnp.sum(pr * gi - pi * gr, axis=0, keepdims=True)

    blk = pl.BlockSpec((T, wl), lambda j: (0, j))
    o = pl.BlockSpec((1, wl), lambda j: (0, j))
    return pl.pallas_call(
        body, name=name, grid=(NS // wl,), in_specs=[blk] * 4, out_specs=[o, o],
        out_shape=[jax.ShapeDtypeStruct((1, NS), F32)] * 2,
        compiler_params=_cparams(("parallel",)),
    )(sre, sim, gre, gim)


def _s5_bu_fn(u, wblk):
    nch = wblk.shape[0]
    cw = S5_CHUNK_GROUPS * S5_GROUP
    sw = S5_CHUNK_GROUPS * S5_STATE
    parts = [[] for _ in range(4)]
    for ch in range(nch):
        res = _mm(u[:, ch * cw:(ch + 1) * cw], wblk[ch])
        for q in range(4):
            parts[q].append(res[:, q * sw:(q + 1) * sw])
    return tuple(jnp.concatenate(p, axis=1) if nch > 1 else p[0] for p in parts)


def _s5_out_fn(x0r, x0i, x1r, x1i, u, cre, cim, dsk, wglu, bglu):
    xr, xi = x0r + x1r, x0i + x1i
    nch = cre.shape[0]
    sw = S5_CHUNK_GROUPS * S5_STATE
    ys = [_mm(xr[:, ch * sw:(ch + 1) * sw], cre[ch]) - _mm(xi[:, ch * sw:(ch + 1) * sw], cim[ch]) for ch in range(nch)]
    y = jnp.concatenate(ys, axis=1) if nch > 1 else ys[0]
    z = _gelu_tanh(y + dsk * u)
    gate = _sigmoid(_mm(z, wglu) + bglu)
    return z * gate


def _rk_dims():
    C = D_MODEL // 2
    LW = N_DIR * DECAY_LORA
    GP = _round_up(GATE_LORA, LANE)
    return C, LW, GP


def _rk_pre_fn(ps, w0, wup0, wup1, a0, aup0, aup1, gup, k_k, k_a):
    C, LW, GP = _rk_dims()
    r, k, v = ps[:, 0:C], ps[:, C:2 * C], ps[:, 2 * C:3 * C]
    wdn = ps[:, 3 * C:3 * C + LW]
    adn = ps[:, 3 * C + LW:3 * C + 2 * LW]
    gdn = ps[:, 3 * C + 2 * LW:3 * C + 2 * LW + GP]
    s, st = _seg_mats(C, RWKV_HEAD)
    kk = k * k_k
    n2 = _segsum(kk * kk, s)
    n2 = jnp.where(n2 > 0.0, n2, 1.0)
    inv = 1.0 / jnp.maximum(jnp.sqrt(n2), L2_EPS)
    kkn = kk * _segsum(inv, st)
    tw = jnp.tanh(wdn)
    wup, aup = (wup0, wup1), (aup0, aup1)
    ws, ks, bs = [], [], []
    for d in range(N_DIR):
        wraw = w0[d:d + 1] + _mm(tw, wup[d])
        w = -_softplus(-wraw) - 0.5
        ws.append(jnp.exp(-jnp.exp(w)))
        a = _sigmoid(a0[d:d + 1] + _mm(adn, aup[d]))
        ks.append(k * (1.0 + (a - 1.0) * k_a))
        bs.append(kkn * a)
    g = _mm(_sigmoid(gdn), gup)
    return r, v, kkn, ws[0], ws[1], ks[0], ks[1], bs[0], bs[1], g


def _rk_post_fn(y0, y1, r, v, k0, k1, g, r_k, lng, lnb):
    C = r.shape[1]
    s, st = _seg_mats(C, RWKV_HEAD)
    y = y0 + y1
    mu = _segsum(_segsum(y, s) * (1.0 / RWKV_HEAD), st)
    yc = y - mu
    var = _segsum(_segsum(yc * yc, s) * (1.0 / RWKV_HEAD), st)
    yn = yc * lax.rsqrt(var + GN_EPS) * lng + lnb
    bonus = _segsum(_segsum(r * (k0 + k1) * r_k, s), st)
    return (yn + bonus * v) * g


def rk_shift(proj, mp, mn, col0):
    T = proj.shape[0]
    W = mp.shape[1]
    wl = _tile(math.gcd(W, col0), 256)
    cb0 = col0 // wl

    def body(p_r, mp_r, mn_r, o_r):
        p = p_r[...]
        row = lax.broadcasted_iota(jnp.int32, (T, wl), 0)
        prev = jnp.where(row >= 1, pltpu.roll(p, 1, 0), 0.0)
        nxt = jnp.where(row < T - 1, pltpu.roll(p, T - 1, 0), 0.0)
        o_r[...] = p + mp_r[...] * (prev - p) + mn_r[...] * (nxt - p)

    rs = pl.BlockSpec((1, wl), lambda j: (0, j))
    return pl.pallas_call(
        body, name="rk_shift", grid=(W // wl,),
        in_specs=[pl.BlockSpec((T, wl), lambda j: (0, cb0 + j)), rs, rs],
        out_specs=pl.BlockSpec((T, wl), lambda j: (0, j)),
        out_shape=jax.ShapeDtypeStruct((T, W), F32),
        compiler_params=_cparams(("parallel",)),
    )(proj, mp, mn)


def rk_shift_bwd(dps, proj, mp, mn, col0):
    T, W = dps.shape
    wl = _tile(math.gcd(W, col0), 256)
    cb0 = col0 // wl

    def body(d_r, p_r, mp_r, mn_r, dp_r, dmp_r, dmn_r):
        d, p = d_r[...], p_r[...]
        mpv, mnv = mp_r[...], mn_r[...]
        row = lax.broadcasted_iota(jnp.int32, (T, wl), 0)
        first, last = row >= 1, row < T - 1
        prev = jnp.where(first, pltpu.roll(p, 1, 0), 0.0)
        nxt = jnp.where(last, pltpu.roll(p, T - 1, 0), 0.0)
        dmp_r[...] = jnp.sum(d * (prev - p), axis=0, keepdims=True)
        dmn_r[...] = jnp.sum(d * (nxt - p), axis=0, keepdims=True)
        dp_r[...] = (d * (1.0 - mpv - mnv) + jnp.where(last, pltpu.roll(d * mpv, T - 1, 0), 0.0)
                     + jnp.where(first, pltpu.roll(d * mnv, 1, 0), 0.0))

    rs = pl.BlockSpec((1, wl), lambda j: (0, j))
    blk = pl.BlockSpec((T, wl), lambda j: (0, j))
    return pl.pallas_call(
        body, name="rk_shift_bwd", grid=(W // wl,),
        in_specs=[blk, pl.BlockSpec((T, wl), lambda j: (0, cb0 + j)), rs, rs],
        out_specs=[blk, rs, rs],
        out_shape=[jax.ShapeDtypeStruct((T, W), F32), jax.ShapeDtypeStruct((1, W), F32), jax.ShapeDtypeStruct((1, W), F32)],
        compiler_params=_cparams(("parallel",)),
    )(dps, proj, mp, mn)


RK_FWD_PAIRS = 8
RK_BWD_PAIRS = 4
RK_SPREAD_PAIRS = 2
RK_TIME_BLOCK = 32
RK_LANE_BLOCK = 128


def _rk_blocks(T, C, N, order_reversed, pairs):
    pw = 2 * N
    pp = min(pairs, C // pw)
    tb = min(RK_TIME_BLOCK, T)
    lb = min(RK_LANE_BLOCK, T)
    nb, per = T // tb, lb // tb

    def tix(i):
        return (nb - 1 - i) if order_reversed else i

    rows = pl.BlockSpec((tb, pp * pw), lambda g, i: (tix(i), g))
    cols = pl.BlockSpec((2 * pp, N, lb), lambda g, i: (g, 0, tix(i) // per))
    hist = pl.BlockSpec((tb, pp, N, pw), lambda g, i: (tix(i), g, 0, 0))
    return pp, pw, tb, lb, nb, per, tix, rows, cols, hist


def rk_spread(xT, name):
    H, N, T = xT.shape
    pw = 2 * N
    lb = min(RK_LANE_BLOCK, T)
    pp = min(RK_SPREAD_PAIRS, H // 2)

    def body(x_r, o_r):
        lane = lax.broadcasted_iota(jnp.int32, (N, lb), 1)
        first = lax.broadcasted_iota(jnp.int32, (N, pw), 1) < N
        tiles = [x_r[h] for h in range(2 * pp)]

        def step(t, carry):
            for p in range(pp):
                c = [jnp.sum(jnp.where(lane == t, tiles[2 * p + q], 0.0), axis=1, keepdims=True) for q in range(2)]
                o_r[t, p] = jnp.where(first, c[0], c[1])
            return carry

        lax.fori_loop(0, lb, step, 0, unroll=8)

    return pl.pallas_call(
        body, name=name, grid=(H // (2 * pp), T // lb),
        in_specs=[pl.BlockSpec((2 * pp, N, lb), lambda g, i: (g, 0, i))],
        out_specs=pl.BlockSpec((lb, pp, N, pw), lambda g, i: (i, g, 0, 0)),
        out_shape=jax.ShapeDtypeStruct((T, H // 2, N, pw), F32),
        compiler_params=_cparams(("parallel", "parallel")),
    )(xT)


def _half_sums(x, first):
    return (jnp.sum(jnp.where(first, x, 0.0), axis=1, keepdims=True),
            jnp.sum(jnp.where(first, 0.0, x), axis=1, keepdims=True))


def rk_scan(r, kk, w, k, b, vc, *, reverse, name):
    T, C = r.shape
    N = vc.shape[2]
    H = C // N
    pp, pw, tb, lb, nb, per, tix, rows, cols, hist_spec = _rk_blocks(T, C, N, reverse, RK_FWD_PAIRS)

    def body(r_r, kk_r, w_r, k_r, b_r, VC, yT_r, hist_r, S, YA):
        i = pl.program_id(1)

        @pl.when(i == 0)
        def _():
            S[...] = jnp.zeros_like(S)

        @pl.when(i % per == 0)
        def _():
            yT_r[...] = jnp.zeros_like(yT_r)

        off = (tix(i) % per) * tb
        first = lax.broadcasted_iota(jnp.int32, (N, pw), 1) < N
        YA[...] = jnp.zeros_like(YA)
        st = [S[p] for p in range(pp)]
        seg = [slice(p * pw, (p + 1) * pw) for p in range(pp)]
        for s in range(tb):
            t = (tb - 1 - s) if reverse else s
            row = slice(t, t + 1)
            sk = [_half_sums(st[p] * kk_r[row, seg[p]], first) for p in range(pp)]
            for p in range(pp):
                hist_r[t, p] = st[p]
                skp = jnp.where(first, sk[p][0], sk[p][1])
                st[p] = st[p] * w_r[row, seg[p]] - skp * b_r[row, seg[p]] + VC[t, p] * k_r[row, seg[p]]
            ys = [_half_sums(st[p] * r_r[row, seg[p]], first) for p in range(pp)]
            for p in range(pp):
                for q in range(2):
                    YA[2 * p + q, :, row] = ys[p][q]
        for p in range(pp):
            S[p] = st[p]
        for h in range(2 * pp):
            yT_r[h] = yT_r[h] + (pltpu.roll(YA[h], off, 1) if per > 1 else YA[h])

    return pl.pallas_call(
        body, name=name, grid=(C // (pp * pw), nb),
        in_specs=[rows] * 5 + [hist_spec], out_specs=[cols, hist_spec],
        out_shape=[jax.ShapeDtypeStruct((H, N, T), F32), jax.ShapeDtypeStruct((T, H // 2, N, pw), F32)],
        scratch_shapes=[pltpu.VMEM((pp, N, pw), F32), pltpu.VMEM((2 * pp, N, lb), F32)],
        compiler_params=_cparams(("parallel", "arbitrary")),
    )(r, kk, w, k, b, vc)


def rk_scan_bwd(r, kk, w, k, b, vc, dc, hist, *, reverse, name):
    T, C = r.shape
    N = vc.shape[2]
    H = C // N
    pp, pw, tb, lb, nb, per, tix, rows, cols, hist_spec = _rk_blocks(T, C, N, not reverse, RK_BWD_PAIRS)

    def body(r_r, kk_r, w_r, k_r, b_r, VC, DC, hist_r, dr_r, dkk_r, dw_r, dk_r, db_r, dvT_r, G, YA):
        i = pl.program_id(1)

        @pl.when(i == 0)
        def _():
            G[...] = jnp.zeros_like(G)

        @pl.when(i % per == 0)
        def _():
            dvT_r[...] = jnp.zeros_like(dvT_r)

        off = (tix(i) % per) * tb
        first = lax.broadcasted_iota(jnp.int32, (N, pw), 1) < N
        YA[...] = jnp.zeros_like(YA)
        gs = [G[p] for p in range(pp)]
        seg = [slice(p * pw, (p + 1) * pw) for p in range(pp)]
        for s in range(tb):
            t = s if reverse else (tb - 1 - s)
            row = slice(t, t + 1)
            g = [gs[p] + DC[t, p] * r_r[row, seg[p]] for p in range(pp)]
            sk = [_half_sums(hist_r[t, p] * kk_r[row, seg[p]], first) for p in range(pp)]
            gb = [_half_sums(g[p] * b_r[row, seg[p]], first) for p in range(pp)]
            gk = [_half_sums(g[p] * k_r[row, seg[p]], first) for p in range(pp)]
            for p in range(pp):
                sp = hist_r[t, p]
                kkv, wv, kv, bv = kk_r[row, seg[p]], w_r[row, seg[p]], k_r[row, seg[p]], b_r[row, seg[p]]
                vcol, dycol = VC[t, p], DC[t, p]
                sa = -jnp.where(first, sk[p][0], sk[p][1])
                dsa = jnp.where(first, gb[p][0], gb[p][1])
                sn = sp * wv + sa * bv + vcol * kv
                dr_r[row, seg[p]] = jnp.sum(sn * dycol, axis=0, keepdims=True)
                dw_r[row, seg[p]] = jnp.sum(g[p] * sp, axis=0, keepdims=True)
                db_r[row, seg[p]] = jnp.sum(g[p] * sa, axis=0, keepdims=True)
                dk_r[row, seg[p]] = jnp.sum(g[p] * vcol, axis=0, keepdims=True)
                dkk_r[row, seg[p]] = -jnp.sum(sp * dsa, axis=0, keepdims=True)
                gs[p] = g[p] * wv - dsa * kkv
                for q in range(2):
                    YA[2 * p + q, :, row] = gk[p][q]
        for p in range(pp):
            G[p] = gs[p]
        for h in range(2 * pp):
            dvT_r[h] = dvT_r[h] + (pltpu.roll(YA[h], off, 1) if per > 1 else YA[h])

    return pl.pallas_call(
        body, name=name, grid=(C // (pp * pw), nb),
        in_specs=[rows] * 5 + [hist_spec] * 3, out_specs=[rows] * 5 + [cols],
        out_shape=[jax.ShapeDtypeStruct((T, C), F32)] * 5 + [jax.ShapeDtypeStruct((H, N, T), F32)],
        scratch_shapes=[pltpu.VMEM((pp, N, pw), F32), pltpu.VMEM((2 * pp, N, lb), F32)],
        compiler_params=_cparams(("parallel", "arbitrary")),
    )(r, kk, w, k, b, vc, dc, hist)


def adam(parts, w, m, v, name):
    P, R, C = parts.shape
    tr = _tile(R, max(SUBLANE, (1 << 19) // max(C, 1) // SUBLANE * SUBLANE), SUBLANE)
    c1 = 1.0 / (1.0 - ADAM_B1 ** ADAM_STEP)
    c2 = 1.0 / (1.0 - ADAM_B2 ** ADAM_STEP)

    def body(p_r, w_r, m_r, v_r, g_o, d_o, m_o, v_o):
        g = p_r[0].astype(F32)
        for q in range(1, P):
            g = g + p_r[q].astype(F32)
        m2 = ADAM_B1 * m_r[...] + (1.0 - ADAM_B1) * g
        v2 = ADAM_B2 * v_r[...] + (1.0 - ADAM_B2) * (g * g)
        g_o[...] = g
        m_o[...] = m2
        v_o[...] = v2
        d_o[...] = -ADAM_LR * ((m2 * c1) / (jnp.sqrt(v2 * c2) + ADAM_EPS) + ADAM_WD * w_r[...])

    blk = pl.BlockSpec((tr, C), lambda i: (i, 0))
    return pl.pallas_call(
        body, name=name, grid=(R // tr,),
        in_specs=[pl.BlockSpec((P, tr, C), lambda i: (0, i, 0)), blk, blk, blk], out_specs=[blk] * 4,
        out_shape=[jax.ShapeDtypeStruct((R, C), F32)] * 4,
        compiler_params=_cparams(("parallel",)),
    )(parts, w, m, v)


def _pack_flat(arrs):
    rows = []
    for a in arrs:
        f = a.reshape(-1).astype(F32)
        n = _round_up(f.shape[0], SUBLANE * LANE)
        rows.append(jnp.pad(f, (0, n - f.shape[0])).reshape(-1, LANE))
    return jnp.concatenate(rows, axis=0)


def _unpack_flat(packed, shapes):
    out, r0 = [], 0
    for s in shapes:
        n = math.prod(s)
        nr = _round_up(n, SUBLANE * LANE) // LANE
        out.append(packed[r0:r0 + nr].reshape(-1)[:n].reshape(s))
        r0 += nr
    return out


def _pack_rows(arrs):
    rows = []
    for a in arrs:
        f = a.reshape(-1, a.shape[-1]).astype(F32)
        n = _round_up(f.shape[0], SUBLANE)
        rows.append(jnp.pad(f, ((0, n - f.shape[0]), (0, 0))))
    return jnp.concatenate(rows, axis=0)


def _unpack_rows(packed, shapes):
    out, r0 = [], 0
    for s in shapes:
        nr = math.prod(s[:-1])
        out.append(packed[r0:r0 + nr].reshape(s))
        r0 += _round_up(nr, SUBLANE)
    return out


def _cols_from_slots(g):
    return jnp.moveaxis(g, 0, -2).reshape(g.shape[1:-1] + (N_DEV * g.shape[-1],))


def _cols_to_slots(a):
    cs = a.shape[-1] // N_DEV
    return jnp.moveaxis(a.reshape(a.shape[:-1] + (N_DEV, cs)), -2, 0)


def _step(P, M, V):
    D, T = D_MODEL, SEQ
    S5W = D // 2
    C, LW, GP = _rk_dims()
    H = C // RWKV_HEAD
    G = S5W // S5_GROUP
    NS = G * S5_STATE
    NCH = G // S5_CHUNK_GROUPS
    SW = S5_CHUNK_GROUPS * S5_STATE
    RIN = 3 * C + 2 * LW + GATE_LORA
    RINP = 3 * C + 2 * LW + GP
    PROJ = S5W + RIN
    PROJP = S5W + RINP
    FF = 4 * D
    me = 4 * lax.axis_index("x") + 2 * lax.axis_index("y") + lax.axis_index("c")
    cs_mod = N_MOD * D // N_DEV
    eye = jnp.eye(S5_CHUNK_GROUPS, dtype=F32)

    x = P['x'][0]
    target = P['loss_target'][0]

    (c_all,) = exchange([P['c']], ['gather'], "comm_gather_c")
    c_all = c_all.reshape(N_DEV, D)
    (c_act,), _ = rowcall(lambda cv: ((cv * _sigmoid(cv),), ()), "silu_c", [c_all], [], [(D, F32)], [], N_DEV)
    ada_b_loc = lax.dynamic_slice(P['ada_b'], (0, me * cs_mod), (1, cs_mod))
    mod_loc = matmul(c_act, P['ada_w'][0], name="mod_mm", precise=True, extras=[(ada_b_loc, 'n')],
                     epi=lambda acc, bias: (acc + bias,))

    rk_shapes = [P[n][0].shape for n in RKPACK]
    rk_pack = _pack_rows([P[n][0] for n in RKPACK])
    gathered = gather_two_level(
        [mod_loc, P['w_in'][0].astype(BF16), P['w_out'][0].astype(BF16), P['ffn_w1'][0].astype(BF16),
         P['ffn_w2'][0].astype(BF16), P['s5_w_glu'][0].astype(BF16), rk_pack], "comm_gather_weights")
    mod_all, w_in_g, w_out_g, w1_g, w2_g, wglu_g, rk_g = gathered
    mod_me = lax.dynamic_index_in_dim(mod_all, me, axis=1, keepdims=False).reshape(N_MOD, 1, D)
    shift1, scale1, gate1, shift2, scale2, gate2 = (mod_me[i] for i in range(N_MOD))
    w_in = jnp.pad(_cols_from_slots(w_in_g), ((0, 0), (0, PROJP - PROJ)))
    w_out = w_out_g.reshape(D, D)
    w2 = w2_g.reshape(FF, D)
    wglu = wglu_g.reshape(S5W, S5W)
    rk_full = _unpack_rows(_cols_from_slots(rk_g), [s[:-1] + (C,) for s in rk_shapes])
    rk_w0, rk_a0, rk_wup, rk_aup, rk_gup = rk_full

    def lora_pad(up):
        z = jnp.zeros((N_DIR, LW, C), F32)
        for d in range(N_DIR):
            z = z.at[d, d * DECAY_LORA:(d + 1) * DECAY_LORA].set(up[d])
        return z

    wup_p, aup_p = lora_pad(rk_wup), lora_pad(rk_aup)
    gup_p = jnp.pad(rk_gup, ((0, GP - GATE_LORA), (0, 0)))
    mu_prev = jnp.pad(P['rk_shift_prev'], ((0, 0), (0, RINP - RIN)))
    mu_next = jnp.pad(P['rk_shift_next'], ((0, 0), (0, RINP - RIN)))
    r_k = P['rk_r_k'].reshape(1, C)
    fgain = P['final_gain'].reshape(1, D)

    TT = 256
    (h1,), _ = rowcall(lambda xv, g, sc, sh: ((_normmod(xv, g, sc, sh),), ()), "norm1",
                       [x], [P['norm1_gain'], scale1, shift1], [(D, BF16)], [], TT)
    proj = matmul(h1, w_in, name="proj_mm")

    lre = P['s5_lambda_re'][0].reshape(N_DIR, NS)
    lim = P['s5_lambda_im'][0].reshape(N_DIR, NS)
    lstep = jnp.broadcast_to(P['s5_log_step'][0][:, :, None], (N_DIR, G, S5_STATE)).reshape(N_DIR, NS)
    bre = P['s5_b_re'][0].reshape(NS, S5_GROUP).T
    bim = P['s5_b_im'][0].reshape(NS, S5_GROUP).T
    lbr, lbi, bbr, bbi = s5_prep(lre, lim, lstep, bre, bim)
    bbar = jnp.stack([bbr, bbi], axis=1).reshape(N_DIR, 2, S5_GROUP, NCH, S5_CHUNK_GROUPS, S5_STATE)
    wblk = jnp.einsum('drhcgp,gk->cghdrkp', bbar, eye).reshape(NCH, S5_CHUNK_GROUPS * S5_GROUP, 4 * SW)
    wblk = wblk.astype(MXU_DTYPE)
    u_view = (proj, S5W, 0)
    bus, _ = rowcall(lambda uv, wb: (_s5_bu_fn(uv, wb), ()), "s5_bu", [u_view], [wblk], [(NS, F32)] * 4, [], TT)
    s0r, s0i = s5_scan(bus[0], bus[1], lbr[0:1], lbi[0:1], reverse=False, name="s5_scan_f0")
    s1r, s1i = s5_scan(bus[2], bus[3], lbr[1:2], lbi[1:2], reverse=True, name="s5_scan_f1")

    def cblk(cm):
        c4 = cm.reshape(NCH, S5_CHUNK_GROUPS, S5_GROUP, S5_STATE)
        return jnp.einsum('cghp,gk->cgpkh', c4, eye).reshape(NCH, SW, S5_CHUNK_GROUPS * S5_GROUP)

    cre_b = cblk(P['s5_c_re'][0]).astype(MXU_DTYPE)
    cim_b = cblk(P['s5_c_im'][0]).astype(MXU_DTYPE)
    s5_full = [cre_b, cim_b, P['s5_d'], wglu, P['s5_b_glu']]
    TS = 128
    (y_s5,), _ = rowcall(lambda *a: ((_s5_out_fn(*a),), ()), "s5_out", [s0r, s0i, s1r, s1i, u_view], s5_full,
                         [(S5W, BF16)], [], TS)

    ps = rk_shift(proj, mu_prev, mu_next, S5W)
    pre_full = [rk_w0, wup_p[0], wup_p[1], rk_a0, aup_p[0], aup_p[1], gup_p, P['rk_k_k'], P['rk_k_a']]
    pre_out, _ = rowcall(lambda *a: (_rk_pre_fn(*a)[2:], ()), "rk_pre", [ps], pre_full, [(C, F32)] * 8, [], TS)
    kkn, w_0, w_1, k_0, k_1, b_0, b_1, g_gate = pre_out
    r_t, v_t = ps[:, 0:C], ps[:, 2 * C:3 * C]

    def hmT(a):
        return a.reshape(T, H, RWKV_HEAD).transpose(1, 2, 0)

    def unT(a):
        return a.transpose(2, 0, 1).reshape(T, C)

    v_cols = rk_spread(hmT(v_t), "rk_spread_v")
    dir_rows = [(w_0, k_0, b_0), (w_1, k_1, b_1)]
    yT, hist = [], []
    for d in range(N_DIR):
        wd, kd, bd = dir_rows[d]
        yd, hd = rk_scan(r_t, kkn, wd, kd, bd, v_cols, reverse=(d == 1), name=f"rk_scan_f{d}")
        yT.append(yd)
        hist.append(hd)
    y_0, y_1 = unT(yT[0]), unT(yT[1])
    post_full = [r_k, P['rk_ln_gain'], P['rk_ln_bias']]
    post_tiled = [y_0, y_1, (ps, C, 0), (ps, C, 2), k_0, k_1, g_gate]
    (y_rk,), _ = rowcall(lambda *a: ((_rk_post_fn(*a),), ()), "rk_post", post_tiled, post_full, [(C, BF16)], [], TS)

    ycat = jnp.concatenate([y_s5, y_rk], axis=1)
    mixed = matmul(ycat, w_out, name="out_mm")

    def res_norm(xv, mv, gate, g, sc, sh):
        x1v = xv + gate * mv
        return x1v, _normmod(x1v, g, sc, sh)

    (x1, h2), _ = rowcall(lambda *a: (res_norm(*a), ()), "norm2", [x, mixed], [gate1, P['norm2_gain'], scale2, shift2],
                          [(D, F32), (D, BF16)], [], TT)
    a_ff, hh_ff = matmul(h2, w1_g, name="ffn1_mm", b_slots=True, out_dtypes=(F32, BF16),
                         epi=lambda acc: (acc, jnp.square(jnp.maximum(acc, 0.0))))
    ffn = matmul(hh_ff, w2, name="ffn2_mm")

    def loss_fn(x1v, fv, tg, gate, fg):
        def f(x1_, f_, gate_, fg_):
            out = _rms(x1_ + gate_ * f_, fg_)
            err = out - tg
            return 0.5 * jnp.sum(jnp.sum(err * err, axis=1, keepdims=True), axis=0, keepdims=True) * (1.0 / D)
        lv, vjp = jax.vjp(f, x1v, fv, gate, fg)
        dx1, dff, dgate, dfg = vjp(jnp.ones((1, 1), F32))
        return (dx1, dff), (jnp.broadcast_to(lv, (SUBLANE, LANE)), dgate, dfg)

    (dx2, dffn), (loss_t, dgate2, dfgain) = rowcall(
        loss_fn, "loss", [x1, ffn, target], [gate2, fgain], [(D, F32), (D, BF16)], [(SUBLANE, LANE), (1, D), (1, D)], TT)
    loss = lax.psum(loss_t[0, 0], MESH_AXES)

    da = matmul(dffn, w2, name="dffn2_mm", tb=True, out_dtypes=(BF16,), extras=[(a_ff, 'mn')],
                epi=lambda acc, av: (acc * (2.0 * jnp.maximum(av, 0.0)),))
    g_w2 = matmul(hh_ff, dffn, name="gw2_mm", ta=True, out_dtypes=(BF16,))
    dh2 = matmul(da, w1_g, name="dh2_mm", tb=True, b_slots=True)
    g_w1 = matmul(h2, da, name="gw1_mm", ta=True, out_slots=N_DEV, out_dtypes=(BF16,))

    def res_norm_bwd(dx2v, dh2v, xv, mv, gate, g, sc, sh):
        _, vjp = jax.vjp(res_norm, xv, mv, gate, g, sc, sh)
        dx, dm, dgate, dg, dsc, dsh = vjp((dx2v, dh2v))
        return (dx, dm), (dgate, dg, dsc, dsh)

    (dx1, dmixed), (dgate1, dgain2, dscale2, dshift2) = rowcall(
        res_norm_bwd, "norm2_bwd", [dx2, dh2, x, mixed], [gate1, P['norm2_gain'], scale2, shift2],
        [(D, F32), (D, BF16)], [(1, D)] * 4, TT)

    dycat = matmul(dmixed, w_out, name="dycat_mm", tb=True)
    g_wout = matmul(ycat, dmixed, name="gwout_mm", ta=True, out_dtypes=(BF16,))

    def post_bwd(dy, *a):
        _, vjp = jax.vjp(_rk_post_fn, *a)
        gy0, gy1, gr, gv, gk0, gk1, gg, grk, glg, glb = vjp(dy)
        return (gy0, gr, gv, gk0, gk1, gg), (grk, glg, glb)

    cb_rk = S5W // C if C else 0
    (dy_rk, dr_p, dv_p, dk0_p, dk1_p, dg_p), (g_rk_rk, g_lng, g_lnb) = rowcall(
        post_bwd, "rk_post_bwd", [(dycat, C, cb_rk)] + post_tiled, post_full, [(C, F32)] * 6, [(1, C)] * 3, TS)
    dy_cols = rk_spread(hmT(dy_rk), "rk_spread_dy")
    scan_g = []
    for d in range(N_DIR):
        wd, kd, bd = dir_rows[d]
        scan_g.append(rk_scan_bwd(r_t, kkn, wd, kd, bd, v_cols, dy_cols, hist[d], reverse=(d == 1), name=f"rk_scan_b{d}"))
    cot = [dr_p, scan_g[0][0], scan_g[1][0],
           dv_p, unT(scan_g[0][5]), unT(scan_g[1][5]),
           scan_g[0][1], scan_g[1][1],
           scan_g[0][2], scan_g[1][2],
           dk0_p, scan_g[0][3], dk1_p, scan_g[1][3],
           scan_g[0][4], scan_g[1][4],
           dg_p]

    def pre_bwd(psv, r0, r1, r2, v0, v1, v2, q0, q1, dw0, dw1, k0a, k0b, k1a, k1b, db0, db1, dgv, *params):
        _, vjp = jax.vjp(_rk_pre_fn, psv, *params)
        grads = vjp((r0 + r1 + r2, v0 + v1 + v2, q0 + q1, dw0, dw1, k0a + k0b, k1a + k1b, db0, db1, dgv))
        return (grads[0],), tuple(grads[1:])

    (dps,), pre_g = rowcall(pre_bwd, "rk_pre_bwd", [ps] + cot, pre_full, [(RINP, F32)],
                            [f.shape for f in pre_full], TS)
    g_w0, g_wup0, g_wup1, g_a0, g_aup0, g_aup1, g_gup_p, g_kk, g_ka = pre_g
    g_wup_p, g_aup_p = jnp.stack([g_wup0, g_wup1]), jnp.stack([g_aup0, g_aup1])
    dp_rk, g_mup, g_mun = rk_shift_bwd(dps, proj, mu_prev, mu_next, S5W)

    def s5_out_bwd(dy, *a):
        a = [t.astype(F32) for t in a]
        _, vjp = jax.vjp(_s5_out_fn, *a)
        g = vjp(dy)
        return (g[0], g[1], g[4]), tuple(g[5:])

    (dxr, dxi, du_a), s5_pg = rowcall(
        s5_out_bwd, "s5_out_bwd", [(dycat, S5W, 0), s0r, s0i, s1r, s1i, u_view], s5_full,
        [(NS, F32), (NS, F32), (S5W, F32)], [f.shape for f in s5_full], TS)
    g_creb, g_cimb, g_s5d, g_wglu, g_bglu = s5_pg
    l0r, l0i = s5_scan(dxr, dxi, lbr[0:1], -lbi[0:1], reverse=True, name="s5_scan_b0")
    l1r, l1i = s5_scan(dxr, dxi, lbr[1:2], -lbi[1:2], reverse=False, name="s5_scan_b1")
    dl0r, dl0i = s5_dlam(s0r, s0i, l0r, l0i, reverse=False, name="s5_dlam0")
    dl1r, dl1i = s5_dlam(s1r, s1i, l1r, l1i, reverse=True, name="s5_dlam1")

    def bu_bwd(uv, g0, g1, g2, g3, wb):
        _, vjp = jax.vjp(_s5_bu_fn, uv, wb.astype(F32))
        du, dwb = vjp((g0, g1, g2, g3))
        return (du,), (dwb,)

    (du_b,), (g_wblk,) = rowcall(bu_bwd, "s5_bu_bwd", [u_view, l0r, l0i, l1r, l1i], [wblk], [(S5W, F32)],
                                 [wblk.shape], TS)
    g_bbar = jnp.einsum('cghdrkp,gk->drhcgp',
                        g_wblk.reshape(NCH, S5_CHUNK_GROUPS, S5_GROUP, N_DIR, 2, S5_CHUNK_GROUPS, S5_STATE), eye)
    g_bbar = g_bbar.reshape(N_DIR, 2, S5_GROUP, NS)
    g_lre, g_lim, g_lstep, g_bre, g_bim = s5_prep_bwd(
        lre, lim, lstep, bre, bim, jnp.concatenate([dl0r, dl1r], 0), jnp.concatenate([dl0i, dl1i], 0),
        g_bbar[:, 0], g_bbar[:, 1])

    def uncblk(gb):
        g5 = gb.reshape(NCH, S5_CHUNK_GROUPS, S5_STATE, S5_CHUNK_GROUPS, S5_GROUP)
        return jnp.einsum('cgpkh,gk->cghp', g5, eye).reshape(G, S5_GROUP, S5_STATE)

    (du_tot,), _ = rowcall(lambda a, b_: ((a + b_,), ()), "s5_du_sum", [du_a, du_b], [], [(S5W, BF16)], [], TT)
    dproj = jnp.concatenate([du_tot, dp_rk.astype(BF16)], axis=1)
    dh1 = matmul(dproj, w_in, name="dh1_mm", tb=True)
    g_win = matmul(h1, dproj, name="gwin_mm", ta=True, out_dtypes=(BF16,))

    def norm1_bwd(dx1v, dh1v, xv, g, sc, sh):
        _, vjp = jax.vjp(_normmod, xv, g, sc, sh)
        dx, dg, dsc, dsh = vjp(dh1v)
        return (dx1v + dx,), (dg, dsc, dsh)

    (grad_x,), (dgain1, dscale1, dshift1) = rowcall(
        norm1_bwd, "norm1_bwd", [dx1, dh1, x], [P['norm1_gain'], scale1, shift1], [(D, F32)], [(1, D)] * 3, TT)

    dmod = jnp.concatenate([dshift1, dscale1, dgate1, dshift2, dscale2, dgate2], axis=1)
    lstep_g = g_lstep.reshape(N_DIR, G, S5_STATE)
    small_g = {
        'ada_b': dmod, 'norm1_gain': dgain1, 'norm2_gain': dgain2, 'final_gain': dfgain.reshape(D),
        's5_lambda_re': g_lre.reshape(1, N_DIR, G, S5_STATE), 's5_lambda_im': g_lim.reshape(1, N_DIR, G, S5_STATE),
        's5_log_step': lstep_g,
        's5_b_re': g_bre.T.reshape(1, G, S5_STATE, S5_GROUP), 's5_b_im': g_bim.T.reshape(1, G, S5_STATE, S5_GROUP),
        's5_c_re': uncblk(g_creb)[None], 's5_c_im': uncblk(g_cimb)[None],
        's5_d': g_s5d, 's5_b_glu': g_bglu,
        'rk_shift_prev': g_mup[:, :RIN], 'rk_shift_next': g_mun[:, :RIN],
        'rk_k_k': g_kk, 'rk_k_a': g_ka, 'rk_r_k': g_rk_rk.reshape(1, H, RWKV_HEAD),
        'rk_ln_gain': g_lng, 'rk_ln_bias': g_lnb,
    }
    small_shapes = {n: P[n].shape for n in SMALL}
    small_shapes['s5_log_step'] = (N_DIR, G, S5_STATE)
    small_pack = _pack_flat([small_g[n] for n in SMALL])

    def lora_unpad(gp):
        return jnp.stack([gp[d, d * DECAY_LORA:(d + 1) * DECAY_LORA] for d in range(N_DIR)])

    rk_grads = {'rk_w0': g_w0, 'rk_a0': g_a0, 'rk_w_up': lora_unpad(g_wup_p), 'rk_a_up': lora_unpad(g_aup_p),
                'rk_g_up': g_gup_p[:GATE_LORA]}
    rk_gpack = jnp.stack([_pack_rows([_cols_to_slots(rk_grads[n])[j] for n in RKPACK]) for j in range(N_DEV)])
    g_win_s = _cols_to_slots(g_win[:, :PROJ])
    (small_all,) = gather_two_level([small_pack], "comm_gather_small_grads")
    slots = [g_win_s, g_wout.reshape(N_DEV, D // N_DEV, D), g_w1, g_w2.reshape(N_DEV, FF // N_DEV, D),
             g_wglu.reshape(N_DEV, S5W // N_DEV, S5W), rk_gpack]
    my_c = lax.axis_index("c")

    def core_half(a, core):
        return lax.dynamic_index_in_dim(a.reshape((N_DEV // 2, 2) + a.shape[1:]), core, axis=1, keepdims=False)

    from_sibling = swap_sibling([core_half(a, 1 - my_c) for a in slots], "comm_swap_grads")
    chip_sums = []
    for a, got, nm in zip(slots, from_sibling, ['w_in', 'w_out', 'ffn_w1', 'ffn_w2', 's5_w_glu', 'rkpack']):
        own = core_half(a, my_c)
        flat = (own.shape[0] * math.prod(own.shape[1:-1]), own.shape[-1])
        (sm,), _ = rowcall(lambda u, v_: ((u.astype(F32) + v_.astype(F32),), ()), "chip_sum_" + nm,
                           [own.reshape(flat), got.reshape(flat)], [], [(flat[1], a.dtype)], [], 512)
        chip_sums.append(sm.reshape(own.shape))
    win_parts, wout_parts, w1_parts, w2_parts, wglu_parts, rk_parts = exchange_chips(chip_sums, "comm_grads")

    res = {}

    def put(name, g, dl, m2, v2):
        shp = P[name].shape
        res[name] = tuple(t.reshape(shp) for t in (g, dl, m2, v2))

    def adam2d(name, parts):
        shp = P[name].shape
        r2 = (math.prod(shp[:-1]), shp[-1])
        put(name, *adam(parts.reshape((parts.shape[0],) + r2), P[name].reshape(r2), M[name].reshape(r2),
                        V[name].reshape(r2), "adam_" + name))

    adam2d('w_in', win_parts)
    adam2d('w_out', wout_parts)
    adam2d('ffn_w1', w1_parts)
    adam2d('ffn_w2', w2_parts)
    adam2d('s5_w_glu', wglu_parts)
    off = 0
    for n in SMALL:
        if n == 'ada_b':
            break
        off += _round_up(math.prod(small_shapes[n]), SUBLANE * LANE) // LANE
    nrow_b = N_MOD * D // LANE
    dmod_all = small_all[:, off:off + nrow_b].reshape(N_DEV, N_MOD * D)
    dmod_cols = lax.dynamic_slice(dmod_all, (0, me * cs_mod), (N_DEV, cs_mod))
    g_adaw = matmul(c_act, dmod_cols, name="gadaw_mm", ta=True, precise=True)
    adam2d('ada_w', g_adaw[None])
    small_w = dict(P)
    small_m, small_v = dict(M), dict(V)
    rk_res = adam(rk_parts, rk_pack, _pack_rows([M[n][0] for n in RKPACK]), _pack_rows([V[n][0] for n in RKPACK]),
                  "adam_rkpack")
    for name, parts4 in zip(RKPACK, zip(*[_unpack_rows(t, rk_shapes) for t in rk_res])):
        put(name, *parts4)
    return loss, grad_x, res, (small_all, small_shapes, small_w, small_m, small_v)


def _small_update(small_all, small_shapes, P, M, V, res):
    G = (D_MODEL // 2) // S5_GROUP
    names = [n for n in SMALL if n != 's5_log_step']
    shapes = [small_shapes[n] for n in SMALL]
    parts = _unpack_flat_batched(small_all, shapes)
    by = dict(zip(SMALL, parts))
    ls = by['s5_log_step']
    ls = ls.transpose(0, 3, 1, 2).reshape(N_DEV * S5_STATE, N_DIR * G)
    pk = lambda d: _pack_flat([d[n] for n in names])
    packs = jnp.stack([_pack_flat([by[n][j] for n in names]) for j in range(N_DEV)])
    out = adam(packs, pk(P), pk(M), pk(V), "adam_small")
    shp = [P[n].shape for n in names]
    for name, parts4 in zip(names, zip(*[_unpack_flat(t, shp) for t in out])):
        res[name] = parts4
    lsw = lambda d: jnp.pad(d['s5_log_step'].reshape(1, N_DIR * G), ((0, SUBLANE - 1), (0, 0)))
    ls_parts = jnp.pad(ls[:, None, :], ((0, 0), (0, SUBLANE - 1), (0, 0)))
    o = adam(ls_parts, lsw(P), lsw(M), lsw(V), "adam_log_step")
    res['s5_log_step'] = tuple(t[0:1].reshape(P['s5_log_step'].shape) for t in o)


def _unpack_flat_batched(packed, shapes):
    out, r0 = [], 0
    B = packed.shape[0]
    for s in shapes:
        n = math.prod(s)
        nr = _round_up(n, SUBLANE * LANE) // LANE
        out.append(packed[:, r0:r0 + nr].reshape(B, -1)[:, :n].reshape((B,) + tuple(s)))
        r0 += nr
    return out


def kernel(x, c, ada_w, ada_b, norm1_gain, norm2_gain, final_gain, w_in, w_out, s5_lambda_re, s5_lambda_im, s5_log_step, s5_b_re, s5_b_im, s5_c_re, s5_c_im, s5_d, s5_w_glu, s5_b_glu, rk_shift_prev, rk_shift_next, rk_w0, rk_w_up, rk_a0, rk_a_up, rk_g_up, rk_k_k, rk_k_a, rk_r_k, rk_ln_gain, rk_ln_bias, ffn_w1, ffn_w2, loss_target, m_ada_w, m_ada_b, m_norm1_gain, m_norm2_gain, m_final_gain, m_w_in, m_w_out, m_s5_lambda_re, m_s5_lambda_im, m_s5_log_step, m_s5_b_re, m_s5_b_im, m_s5_c_re, m_s5_c_im, m_s5_d, m_s5_w_glu, m_s5_b_glu, m_rk_shift_prev, m_rk_shift_next, m_rk_w0, m_rk_w_up, m_rk_a0, m_rk_a_up, m_rk_g_up, m_rk_k_k, m_rk_k_a, m_rk_r_k, m_rk_ln_gain, m_rk_ln_bias, m_ffn_w1, m_ffn_w2, v_ada_w, v_ada_b, v_norm1_gain, v_norm2_gain, v_final_gain, v_w_in, v_w_out, v_s5_lambda_re, v_s5_lambda_im, v_s5_log_step, v_s5_b_re, v_s5_b_im, v_s5_c_re, v_s5_c_im, v_s5_d, v_s5_w_glu, v_s5_b_glu, v_rk_shift_prev, v_rk_shift_next, v_rk_w0, v_rk_w_up, v_rk_a0, v_rk_a_up, v_rk_g_up, v_rk_k_k, v_rk_k_a, v_rk_r_k, v_rk_ln_gain, v_rk_ln_bias, v_ffn_w1, v_ffn_w2):
    given = dict(locals())
    P = {n: given[n] for n in ['x', 'c', 'loss_target'] + WEIGHTS}
    M = {n: given['m_' + n] for n in WEIGHTS}
    V = {n: given['v_' + n] for n in WEIGHTS}
    loss, grad_x, res, small = _step(P, M, V)
    small_all, small_shapes, _, _, _ = small
    _small_update(small_all, small_shapes, P, M, V, res)
    outs = [loss, grad_x[None]]
    for q in range(4):
        outs += [res[n][q] for n in WEIGHTS]
    return tuple(outs)
```

```python
import functools
import math

import jax
import jax.numpy as jnp
from jax import lax
from jax.experimental import pallas as pl
from jax.experimental.pallas import tpu as pltpu

F32 = jnp.float32
BF16 = jnp.bfloat16
HI = lax.Precision.HIGHEST
MXU_DTYPE = jnp.bfloat16

N_DEV = 8
MESH_AXES = ("x", "y", "c")
D_MODEL = 2048
SEQ = 2048
S5_GROUP = 16
S5_STATE = 64
RWKV_HEAD = 64
DECAY_LORA = 64
GATE_LORA = 160
N_DIR = 2
N_MOD = 6
NORM_EPS = 1e-6
GN_EPS = 64e-5
L2_EPS = 1e-12
ADAM_LR = 0.001
ADAM_B1 = 0.9
ADAM_B2 = 0.999
ADAM_EPS = 1e-08
ADAM_WD = 0.01
ADAM_STEP = 10
LANE = 128
SUBLANE = 8
S5_CHUNK_GROUPS = 8
S5_SCAN_ROWS = 256
VMEM_LIMIT = 56 * 1024 * 1024

WEIGHTS = ['ada_w', 'ada_b', 'norm1_gain', 'norm2_gain', 'final_gain', 'w_in', 'w_out', 's5_lambda_re',
           's5_lambda_im', 's5_log_step', 's5_b_re', 's5_b_im', 's5_c_re', 's5_c_im', 's5_d', 's5_w_glu',
           's5_b_glu', 'rk_shift_prev', 'rk_shift_next', 'rk_w0', 'rk_w_up', 'rk_a0', 'rk_a_up', 'rk_g_up',
           'rk_k_k', 'rk_k_a', 'rk_r_k', 'rk_ln_gain', 'rk_ln_bias', 'ffn_w1', 'ffn_w2']
SMALL = ['ada_b', 'norm1_gain', 'norm2_gain', 'final_gain', 's5_lambda_re', 's5_lambda_im', 's5_log_step',
         's5_b_re', 's5_b_im', 's5_c_re', 's5_c_im', 's5_d', 's5_b_glu', 'rk_shift_prev', 'rk_shift_next',
         'rk_k_k', 'rk_k_a', 'rk_r_k', 'rk_ln_gain', 'rk_ln_bias']
RKPACK = ['rk_w0', 'rk_a0', 'rk_w_up', 'rk_a_up', 'rk_g_up']


def _round_up(n, m):
    return (n + m - 1) // m * m


def _tile(dim, pref, unit=LANE):
    t = min(pref, dim) // unit * unit
    while t >= unit:
        if dim % t == 0:
            return t
        t -= unit
    return dim


def _cparams(sem=None):
    return pltpu.CompilerParams(dimension_semantics=sem, vmem_limit_bytes=VMEM_LIMIT)


def _full_spec(a):
    nd = a.ndim
    return pl.BlockSpec(a.shape, lambda *_: (0,) * nd)


def exchange(arrs, modes, name):
    n = len(arrs)
    out_shape = [jax.ShapeDtypeStruct((N_DEV,) + a.shape if m == 'gather' else a.shape, a.dtype)
                 for a, m in zip(arrs, modes)]

    def body(*refs):
        ins, outs = refs[:n], refs[n:2 * n]
        send_sems, recv_sems, local_sems = refs[2 * n:]
        x, y, c = (lax.axis_index(a) for a in MESH_AXES)
        me = 4 * x + 2 * y + c
        copies = []
        for i in range(n):
            gather = modes[i] == 'gather'
            mine = pltpu.make_async_copy(ins[i] if gather else ins[i].at[me], outs[i].at[me], local_sems.at[i])
            mine.start()
            copies.append(mine)
        remote = []
        for k in range(1, N_DEV):
            px = 1 - x if (k >> 2) & 1 else x
            py = 1 - y if (k >> 1) & 1 else y
            pc = 1 - c if k & 1 else c
            peer = 4 * px + 2 * py + pc
            for i in range(n):
                src = ins[i] if modes[i] == 'gather' else ins[i].at[peer]
                cp = pltpu.make_async_remote_copy(
                    src_ref=src, dst_ref=outs[i].at[me], send_sem=send_sems.at[i, k - 1],
                    recv_sem=recv_sems.at[i, k - 1], device_id=(px, py, pc), device_id_type=pl.DeviceIdType.MESH)
                cp.start()
                remote.append(cp)
        for cp in remote:
            cp.wait_recv()
        for cp in remote:
            cp.wait_send()
        for cp in copies:
            cp.wait()

    any_spec = pl.BlockSpec(memory_space=pl.ANY)
    return pl.pallas_call(
        body, name=name, out_shape=out_shape,
        in_specs=[any_spec] * n, out_specs=[any_spec] * n,
        scratch_shapes=[pltpu.SemaphoreType.DMA((n, N_DEV - 1)), pltpu.SemaphoreType.DMA((n, N_DEV - 1)),
                        pltpu.SemaphoreType.DMA((n,))],
        compiler_params=pltpu.CompilerParams(has_side_effects=True),
    )(*arrs)


def gather_two_level(arrs, name):
    n = len(arrs)
    out_shape = [jax.ShapeDtypeStruct((N_DEV,) + a.shape, a.dtype) for a in arrs]

    def body(*refs):
        ins, outs = refs[:n], refs[n:2 * n]
        send_sems, recv_sems, local_sems = refs[2 * n:]
        x, y, c = (lax.axis_index(a) for a in MESH_AXES)
        me, sibling = (x, y, c), (x, y, 1 - c)
        chips = [(1 - x, y), (x, 1 - y), (1 - x, 1 - y)]

        def slot(px, py, pc):
            return 4 * px + 2 * py + pc

        def copy(i, k, block, to, src=None):
            return pltpu.make_async_remote_copy(
                src_ref=outs[i].at[slot(*block)] if src is None else src, dst_ref=outs[i].at[slot(*block)],
                send_sem=send_sems.at[i, k], recv_sem=recv_sems.at[i, k], device_id=to,
                device_id_type=pl.DeviceIdType.MESH)

        mine = [pltpu.make_async_copy(ins[i], outs[i].at[slot(*me)], local_sems.at[i]) for i in range(n)]
        for cp in mine:
            cp.start()
        started = []
        for i in range(n):
            started.append(copy(i, 0, me, sibling, src=ins[i]))
            started += [copy(i, 1 + j, me, (*chip, c), src=ins[i]) for j, chip in enumerate(chips)]
        for cp in started:
            cp.start()
        for j, chip in enumerate(chips):
            for i in range(n):
                copy(i, 1 + j, (*chip, c), me).wait_recv()
                fwd = copy(i, 4 + j, (*chip, c), sibling)
                fwd.start()
                started.append(fwd)
        for i in range(n):
            copy(i, 0, sibling, me).wait_recv()
        for j, chip in enumerate(chips):
            for i in range(n):
                copy(i, 4 + j, (*chip, 1 - c), me).wait_recv()
        for cp in started:
            cp.wait_send()
        for cp in mine:
            cp.wait()

    any_spec = pl.BlockSpec(memory_space=pl.ANY)
    return pl.pallas_call(
        body, name=name, out_shape=out_shape,
        in_specs=[any_spec] * n, out_specs=[any_spec] * n,
        scratch_shapes=[pltpu.SemaphoreType.DMA((n, N_DEV - 1)), pltpu.SemaphoreType.DMA((n, N_DEV - 1)),
                        pltpu.SemaphoreType.DMA((n,))],
        compiler_params=pltpu.CompilerParams(has_side_effects=True),
    )(*arrs)


def swap_sibling(arrs, name):
    n = len(arrs)

    def body(*refs):
        ins, outs = refs[:n], refs[n:2 * n]
        send_sems, recv_sems = refs[2 * n:]
        x, y, c = (lax.axis_index(a) for a in MESH_AXES)
        cps = [pltpu.make_async_remote_copy(src_ref=ins[i], dst_ref=outs[i], send_sem=send_sems.at[i],
                                            recv_sem=recv_sems.at[i], device_id=(x, y, 1 - c),
                                            device_id_type=pl.DeviceIdType.MESH) for i in range(n)]
        for cp in cps:
            cp.start()
        for cp in cps:
            cp.wait()

    any_spec = pl.BlockSpec(memory_space=pl.ANY)
    return pl.pallas_call(
        body, name=name, out_shape=[jax.ShapeDtypeStruct(a.shape, a.dtype) for a in arrs],
        in_specs=[any_spec] * n, out_specs=[any_spec] * n,
        scratch_shapes=[pltpu.SemaphoreType.DMA((n,)), pltpu.SemaphoreType.DMA((n,))],
        compiler_params=pltpu.CompilerParams(has_side_effects=True),
    )(*arrs)


def exchange_chips(arrs, name):
    n = len(arrs)

    def body(*refs):
        ins, outs = refs[:n], refs[n:2 * n]
        send_sems, recv_sems, local_sems = refs[2 * n:]
        x, y, c = (lax.axis_index(a) for a in MESH_AXES)
        mine = 2 * x + y
        chips = [(1 - x, y), (x, 1 - y), (1 - x, 1 - y)]
        local = [pltpu.make_async_copy(ins[i].at[mine], outs[i].at[mine], local_sems.at[i]) for i in range(n)]
        for cp in local:
            cp.start()
        remote = []
        for j, (px, py) in enumerate(chips):
            for i in range(n):
                cp = pltpu.make_async_remote_copy(
                    src_ref=ins[i].at[2 * px + py], dst_ref=outs[i].at[mine], send_sem=send_sems.at[i, j],
                    recv_sem=recv_sems.at[i, j], device_id=(px, py, c), device_id_type=pl.DeviceIdType.MESH)
                cp.start()
                remote.append(cp)
        for cp in remote:
            cp.wait_recv()
        for cp in remote:
            cp.wait_send()
        for cp in local:
            cp.wait()

    any_spec = pl.BlockSpec(memory_space=pl.ANY)
    return pl.pallas_call(
        body, name=name, out_shape=[jax.ShapeDtypeStruct(a.shape, a.dtype) for a in arrs],
        in_specs=[any_spec] * n, out_specs=[any_spec] * n,
        scratch_shapes=[pltpu.SemaphoreType.DMA((n, 3)), pltpu.SemaphoreType.DMA((n, 3)), pltpu.SemaphoreType.DMA((n,))],
        compiler_params=pltpu.CompilerParams(has_side_effects=True),
    )(*arrs)


def matmul(a, b, *, name, ta=False, tb=False, b_slots=False, out_slots=0, out_dtypes=(F32,), epi=None,
           extras=(), precise=False, tm=1024, tn=1024, tk=2048):
    if ta:
        K, M = a.shape
    else:
        M, K = a.shape
    if b_slots:
        ns, br, bc = b.shape
        bshape = (br, ns * bc)
    else:
        bshape = b.shape
    N = bshape[0] if tb else bshape[1]
    assert (bshape[1] if tb else bshape[0]) == K, (a.shape, b.shape, ta, tb)
    tm, tn, tk = _tile(M, tm, SUBLANE), _tile(N, tn), _tile(K, tk, SUBLANE if K < LANE else LANE)
    if b_slots and tb:
        tk = _tile(b.shape[2], tk)
    elif b_slots:
        tn = _tile(b.shape[2], tn)
    if out_slots:
        tn = _tile(N // out_slots, tn)
    if b_slots:
        cs = b.shape[2]
        tcol = tk if tb else tn
        assert cs % tcol == 0
        per = cs // tcol
    if out_slots:
        ncs = N // out_slots
        assert ncs % tn == 0
        operc = ncs // tn
    nk = K // tk
    a_spec = pl.BlockSpec((tk, tm), lambda i, j, k: (k, i)) if ta else pl.BlockSpec((tm, tk), lambda i, j, k: (i, k))
    if b_slots:
        if tb:
            b_spec = pl.BlockSpec((None, tn, tk), lambda i, j, k: (k // per, j, k % per))
        else:
            b_spec = pl.BlockSpec((None, tk, tn), lambda i, j, k: (j // per, k, j % per))
    else:
        b_spec = pl.BlockSpec((tn, tk), lambda i, j, k: (j, k)) if tb else pl.BlockSpec((tk, tn), lambda i, j, k: (k, j))
    ex_specs = []
    for arr, kind in extras:
        if kind == 'mn':
            ex_specs.append(pl.BlockSpec((tm, tn), lambda i, j, k: (i, j)))
        else:
            ex_specs.append(pl.BlockSpec((1, tn), lambda i, j, k: (0, j)))
    if out_slots:
        o_spec = pl.BlockSpec((None, tm, tn), lambda i, j, k: (j // operc, i, j % operc))
        o_shape = (out_slots, M, ncs)
    else:
        o_spec = pl.BlockSpec((tm, tn), lambda i, j, k: (i, j))
        o_shape = (M, N)
    ne, no = len(extras), len(out_dtypes)
    dims = (((0 if ta else 1,), (1 if tb else 0,)), ((), ()))
    op_dtype = F32 if precise else MXU_DTYPE

    def body(a_ref, b_ref, *rest):
        ex_refs, out_refs, acc = rest[:ne], rest[ne:ne + no], rest[-1]
        k = pl.program_id(2)
        part = lax.dot_general(a_ref[...].astype(op_dtype), b_ref[...].astype(op_dtype), dims,
                               precision=HI if precise else None, preferred_element_type=F32)

        def finish(total):
            res = epi(total, *[e[...] for e in ex_refs]) if epi is not None else (total,)
            for o, r in zip(out_refs, res):
                o[...] = r.astype(o.dtype)

        if nk == 1:
            finish(part)
        else:
            @pl.when(k == 0)
            def _():
                acc[...] = part

            @pl.when(jnp.logical_and(k > 0, k < nk - 1))
            def _():
                acc[...] += part

            @pl.when(k == nk - 1)
            def _():
                finish(acc[...] + part)

    outs = pl.pallas_call(
        body, name=name, grid=(M // tm, N // tn, nk),
        in_specs=[a_spec, b_spec] + ex_specs, out_specs=[o_spec] * no,
        out_shape=[jax.ShapeDtypeStruct(o_shape, dt) for dt in out_dtypes],
        scratch_shapes=[pltpu.VMEM((tm, tn), F32)],
        compiler_params=_cparams(("parallel", "parallel", "arbitrary")),
    )(a, b, *[e[0] for e in extras])
    return outs[0] if no == 1 else outs


def rowcall(fn, name, tiled, full, tiled_out, acc_out, tt):
    views = [(t, t.shape[1], 0) if not isinstance(t, tuple) else t for t in tiled]
    T = views[0][0].shape[0]
    tt = _tile(T, tt, SUBLANE)
    nt, nf, nto, nao = len(views), len(full), len(tiled_out), len(acc_out)

    def view_spec(w, cb):
        return pl.BlockSpec((tt, w), lambda i: (i, cb))

    in_specs = [view_spec(w, cb) for _, w, cb in views] + [_full_spec(f) for f in full]
    out_specs = [pl.BlockSpec((tt, w), lambda i: (i, 0)) for w, _ in tiled_out]
    out_specs += [pl.BlockSpec(s, lambda i, nd=len(s): (0,) * nd) for s in acc_out]
    out_shape = [jax.ShapeDtypeStruct((T, w), dt) for w, dt in tiled_out]
    out_shape += [jax.ShapeDtypeStruct(s, F32) for s in acc_out]

    def body(*refs):
        tin, fin = refs[:nt], refs[nt:nt + nf]
        tout, aout = refs[nt + nf:nt + nf + nto], refs[nt + nf + nto:]
        touts, aouts = fn(*[r[...] for r in tin], *[r[...] for r in fin])
        for r, v in zip(tout, touts):
            r[...] = v.astype(r.dtype)
        if nao:
            @pl.when(pl.program_id(0) == 0)
            def _():
                for r in aout:
                    r[...] = jnp.zeros_like(r)

            for r, v in zip(aout, aouts):
                r[...] += v.astype(F32)

    outs = pl.pallas_call(
        body, name=name, grid=(T // tt,), in_specs=in_specs, out_specs=out_specs, out_shape=out_shape,
        compiler_params=_cparams(("arbitrary",) if nao else ("parallel",)),
    )(*[v[0] for v in views], *full)
    return outs[:nto], outs[nto:]


def _mm(a, b):
    return jnp.dot(a.astype(MXU_DTYPE), b.astype(MXU_DTYPE), preferred_element_type=F32)


def _rms(x, gain):
    ms = jnp.mean(x * x, axis=-1, keepdims=True)
    return x * lax.rsqrt(ms + NORM_EPS) * gain


def _normmod(x, gain, scale, shift):
    return _rms(x, gain) * (1.0 + scale) + shift


def _gelu_tanh(y):
    return 0.5 * y * (1.0 + jnp.tanh(math.sqrt(2.0 / math.pi) * (y + 0.044715 * (y * y * y))))


def _sigmoid(x):
    return 1.0 / (1.0 + jnp.exp(-x))


def _softplus(x):
    return jnp.maximum(x, 0.0) + jnp.log(1.0 + jnp.exp(-jnp.abs(x)))


def _seg_mats(width, seg):
    r = lax.broadcasted_iota(jnp.int32, (width, LANE), 0) // seg
    c = lax.broadcasted_iota(jnp.int32, (width, LANE), 1)
    s = (r == c).astype(F32)
    rt = lax.broadcasted_iota(jnp.int32, (LANE, width), 0)
    ct = lax.broadcasted_iota(jnp.int32, (LANE, width), 1) // seg
    st = (rt == ct).astype(F32)
    return s, st


def _segsum(x, s):
    return jnp.dot(x, s, precision=HI, preferred_element_type=F32)


def _s5_prep_fn(lre, lim, lstep, bre, bim):
    step = jnp.exp(lstep)
    mag = jnp.exp(lre * step)
    lbr = mag * jnp.cos(lim * step)
    lbi = mag * jnp.sin(lim * step)
    den = lre * lre + lim * lim
    nr = lbr - 1.0
    ni = lbi
    cre = (nr * lre + ni * lim) / den
    cim = (ni * lre - nr * lim) / den
    bbr = jnp.stack([cre[d:d + 1] * bre - cim[d:d + 1] * bim for d in range(N_DIR)])
    bbi = jnp.stack([cre[d:d + 1] * bim + cim[d:d + 1] * bre for d in range(N_DIR)])
    return lbr, lbi, bbr, bbi


def s5_prep(lre, lim, lstep, bre, bim):
    ns = lre.shape[1]

    def body(lre_r, lim_r, ls_r, bre_r, bim_r, lbr_r, lbi_r, bbr_r, bbi_r):
        lbr, lbi, bbr, bbi = _s5_prep_fn(lre_r[...], lim_r[...], ls_r[...], bre_r[...], bim_r[...])
        lbr_r[...] = lbr
        lbi_r[...] = lbi
        bbr_r[...] = bbr
        bbi_r[...] = bbi

    return pl.pallas_call(
        body, name="s5_prep",
        out_shape=[jax.ShapeDtypeStruct((N_DIR, ns), F32)] * 2 + [jax.ShapeDtypeStruct((N_DIR, S5_GROUP, ns), F32)] * 2,
        compiler_params=_cparams(),
    )(lre, lim, lstep, bre, bim)


def s5_prep_bwd(lre, lim, lstep, bre, bim, dlbr, dlbi, dbbr, dbbi):
    ns = lre.shape[1]

    def body(lre_r, lim_r, ls_r, bre_r, bim_r, d1, d2, d3, d4, o1, o2, o3, o4, o5):
        _, vjp = jax.vjp(_s5_prep_fn, lre_r[...], lim_r[...], ls_r[...], bre_r[...], bim_r[...])
        g = vjp((d1[...], d2[...], d3[...], d4[...]))
        for o, v in zip((o1, o2, o3, o4, o5), g):
            o[...] = v

    return pl.pallas_call(
        body, name="s5_prep_bwd",
        out_shape=[jax.ShapeDtypeStruct((N_DIR, ns), F32)] * 3 + [jax.ShapeDtypeStruct((S5_GROUP, ns), F32)] * 2,
        compiler_params=_cparams(),
    )(lre, lim, lstep, bre, bim, dlbr, dlbi, dbbr, dbbi)


def s5_scan(bre, bim, lre, lim, *, reverse, name):
    T, NS = bre.shape
    tt = _tile(T, S5_SCAN_ROWS, SUBLANE)
    wl = _tile(NS, 512)
    nT = T // tt
    ngrp = tt // SUBLANE

    def tmap(j, i):
        return ((nT - 1 - i) if reverse else i, j)

    def body(bre_r, bim_r, lre_r, lim_r, sre_r, sim_r, cre, cim):
        @pl.when(pl.program_id(1) == 0)
        def _():
            cre[...] = jnp.zeros_like(cre)
            cim[...] = jnp.zeros_like(cim)

        lr = jnp.broadcast_to(lre_r[...], (SUBLANE, wl))
        li = jnp.broadcast_to(lim_r[...], (SUBLANE, wl))
        row = lax.broadcasted_iota(jnp.int32, (SUBLANE, wl), 0)
        pows = [(lr, li)]
        for _ in range(3):
            pr, pi = pows[-1]
            pows.append((pr * pr - pi * pi, 2.0 * pr * pi))
        e = (SUBLANE - row) if reverse else (row + 1)
        Pr = jnp.ones((SUBLANE, wl), F32)
        Pi = jnp.zeros((SUBLANE, wl), F32)
        for bit, (qr, qi) in enumerate(pows):
            on = ((e >> bit) & 1) == 1
            nr, ni = Pr * qr - Pi * qi, Pr * qi + Pi * qr
            Pr, Pi = jnp.where(on, nr, Pr), jnp.where(on, ni, Pi)

        def group(g, carry):
            gg = (ngrp - 1 - g) if reverse else g
            rows = pl.ds(pl.multiple_of(gg * SUBLANE, SUBLANE), SUBLANE)
            sr, si = bre_r[rows, :], bim_r[rows, :]
            for lvl, k in enumerate((1, 2, 4)):
                qr, qi = pows[lvl]
                if reverse:
                    shr = pltpu.roll(sr, SUBLANE - k, 0)
                    shi = pltpu.roll(si, SUBLANE - k, 0)
                    keep = row < SUBLANE - k
                else:
                    shr = pltpu.roll(sr, k, 0)
                    shi = pltpu.roll(si, k, 0)
                    keep = row >= k
                shr = jnp.where(keep, shr, 0.0)
                shi = jnp.where(keep, shi, 0.0)
                sr, si = sr + qr * shr - qi * shi, si + qr * shi + qi * shr
            cr, ci = carry
            sr, si = sr + Pr * cr - Pi * ci, si + Pr * ci + Pi * cr
            sre_r[rows, :] = sr
            sim_r[rows, :] = si
            last = 0 if reverse else SUBLANE - 1
            return (jnp.broadcast_to(sr[last:last + 1, :], (SUBLANE, wl)),
                    jnp.broadcast_to(si[last:last + 1, :], (SUBLANE, wl)))

        cr, ci = lax.fori_loop(0, ngrp, group, (cre[...], cim[...]), unroll=4 if ngrp % 4 == 0 else 1)
        cre[...] = cr
        cim[...] = ci

    blk = pl.BlockSpec((tt, wl), tmap)
    row_spec = pl.BlockSpec((1, wl), lambda j, i: (0, j))
    return pl.pallas_call(
        body, name=name, grid=(NS // wl, nT),
        in_specs=[blk, blk, row_spec, row_spec], out_specs=[blk, blk],
        out_shape=[jax.ShapeDtypeStruct((T, NS), F32)] * 2,
        scratch_shapes=[pltpu.VMEM((SUBLANE, wl), F32)] * 2,
        compiler_params=_cparams(("parallel", "arbitrary")),
    )(bre, bim, lre, lim)


def s5_dlam(sre, sim, gre, gim, *, reverse, name):
    T, NS = sre.shape
    wl = _tile(NS, 256)

    def body(sr_r, si_r, gr_r, gi_r, dr_r, di_r):
        row = lax.broadcasted_iota(jnp.int32, (T, wl), 0)
        if reverse:
            keep = row < T - 1
            pr = jnp.where(keep, pltpu.roll(sr_r[...], T - 1, 0), 0.0)
            pi = jnp.where(keep, pltpu.roll(si_r[...], T - 1, 0), 0.0)
        else:
            keep = row >= 1
            pr = jnp.where(keep, pltpu.roll(sr_r[...], 1, 0), 0.0)
            pi = jnp.where(keep, pltpu.roll(si_r[...], 1, 0), 0.0)
        gr, gi = gr_r[...], gi_r[...]
        dr_r[...] = jnp.sum(pr * gr + pi * gi, axis=0, keepdims=True)
        di_r[...] = jnp.sum(pr * gi - pi * gr, axis=0, keepdims=True)

    blk = pl.BlockSpec((T, wl), lambda j: (0, j))
    o = pl.BlockSpec((1, wl), lambda j: (0, j))
    return pl.pallas_call(
        body, name=name, grid=(NS // wl,), in_specs=[blk] * 4, out_specs=[o, o],
        out_shape=[jax.ShapeDtypeStruct((1, NS), F32)] * 2,
        compiler_params=_cparams(("parallel",)),
    )(sre, sim, gre, gim)


def _s5_bu_fn(u, wblk):
    nch = wblk.shape[0]
    cw = S5_CHUNK_GROUPS * S5_GROUP
    sw = S5_CHUNK_GROUPS * S5_STATE
    parts = [[] for _ in range(4)]
    for ch in range(nch):
        res = _mm(u[:, ch * cw:(ch + 1) * cw], wblk[ch])
        for q in range(4):
            parts[q].append(res[:, q * sw:(q + 1) * sw])
    return tuple(jnp.concatenate(p, axis=1) if nch > 1 else p[0] for p in parts)


def _s5_out_fn(x0r, x0i, x1r, x1i, u, cre, cim, dsk, wglu, bglu):
    xr, xi = x0r + x1r, x0i + x1i
    nch = cre.shape[0]
    sw = S5_CHUNK_GROUPS * S5_STATE
    ys = [_mm(xr[:, ch * sw:(ch + 1) * sw], cre[ch]) - _mm(xi[:, ch * sw:(ch + 1) * sw], cim[ch]) for ch in range(nch)]
    y = jnp.concatenate(ys, axis=1) if nch > 1 else ys[0]
    z = _gelu_tanh(y + dsk * u)
    gate = _sigmoid(_mm(z, wglu) + bglu)
    return z * gate


def _rk_dims():
    C = D_MODEL // 2
    LW = N_DIR * DECAY_LORA
    GP = _round_up(GATE_LORA, LANE)
    return C, LW, GP


def _rk_pre_fn(ps, w0, wup0, wup1, a0, aup0, aup1, gup, k_k, k_a):
    C, LW, GP = _rk_dims()
    r, k, v = ps[:, 0:C], ps[:, C:2 * C], ps[:, 2 * C:3 * C]
    wdn = ps[:, 3 * C:3 * C + LW]
    adn = ps[:, 3 * C + LW:3 * C + 2 * LW]
    gdn = ps[:, 3 * C + 2 * LW:3 * C + 2 * LW + GP]
    s, st = _seg_mats(C, RWKV_HEAD)
    kk = k * k_k
    n2 = _segsum(kk * kk, s)
    n2 = jnp.where(n2 > 0.0, n2, 1.0)
    inv = 1.0 / jnp.maximum(jnp.sqrt(n2), L2_EPS)
    kkn = kk * _segsum(inv, st)
    tw = jnp.tanh(wdn)
    wup, aup = (wup0, wup1), (aup0, aup1)
    ws, ks, bs = [], [], []
    for d in range(N_DIR):
        wraw = w0[d:d + 1] + _mm(tw, wup[d])
        w = -_softplus(-wraw) - 0.5
        ws.append(jnp.exp(-jnp.exp(w)))
        a = _sigmoid(a0[d:d + 1] + _mm(adn, aup[d]))
        ks.append(k * (1.0 + (a - 1.0) * k_a))
        bs.append(kkn * a)
    g = _mm(_sigmoid(gdn), gup)
    return r, v, kkn, ws[0], ws[1], ks[0], ks[1], bs[0], bs[1], g


def _rk_post_fn(y0, y1, r, v, k0, k1, g, r_k, lng, lnb):
    C = r.shape[1]
    s, st = _seg_mats(C, RWKV_HEAD)
    y = y0 + y1
    mu = _segsum(_segsum(y, s) * (1.0 / RWKV_HEAD), st)
    yc = y - mu
    var = _segsum(_segsum(yc * yc, s) * (1.0 / RWKV_HEAD), st)
    yn = yc * lax.rsqrt(var + GN_EPS) * lng + lnb
    bonus = _segsum(_segsum(r * (k0 + k1) * r_k, s), st)
    return (yn + bonus * v) * g


def rk_shift(proj, mp, mn, col0):
    T = proj.shape[0]
    W = mp.shape[1]
    wl = _tile(math.gcd(W, col0), 256)
    cb0 = col0 // wl

    def body(p_r, mp_r, mn_r, o_r):
        p = p_r[...]
        row = lax.broadcasted_iota(jnp.int32, (T, wl), 0)
        prev = jnp.where(row >= 1, pltpu.roll(p, 1, 0), 0.0)
        nxt = jnp.where(row < T - 1, pltpu.roll(p, T - 1, 0), 0.0)
        o_r[...] = p + mp_r[...] * (prev - p) + mn_r[...] * (nxt - p)

    rs = pl.BlockSpec((1, wl), lambda j: (0, j))
    return pl.pallas_call(
        body, name="rk_shift", grid=(W // wl,),
        in_specs=[pl.BlockSpec((T, wl), lambda j: (0, cb0 + j)), rs, rs],
        out_specs=pl.BlockSpec((T, wl), lambda j: (0, j)),
        out_shape=jax.ShapeDtypeStruct((T, W), F32),
        compiler_params=_cparams(("parallel",)),
    )(proj, mp, mn)


def rk_shift_bwd(dps, proj, mp, mn, col0):
    T, W = dps.shape
    wl = _tile(math.gcd(W, col0), 256)
    cb0 = col0 // wl

    def body(d_r, p_r, mp_r, mn_r, dp_r, dmp_r, dmn_r):
        d, p = d_r[...], p_r[...]
        mpv, mnv = mp_r[...], mn_r[...]
        row = lax.broadcasted_iota(jnp.int32, (T, wl), 0)
        first, last = row >= 1, row < T - 1
        prev = jnp.where(first, pltpu.roll(p, 1, 0), 0.0)
        nxt = jnp.where(last, pltpu.roll(p, T - 1, 0), 0.0)
        dmp_r[...] = jnp.sum(d * (prev - p), axis=0, keepdims=True)
        dmn_r[...] = jnp.sum(d * (nxt - p), axis=0, keepdims=True)
        dp_r[...] = (d * (1.0 - mpv - mnv) + jnp.where(last, pltpu.roll(d * mpv, T - 1, 0), 0.0)
                     + jnp.where(first, pltpu.roll(d * mnv, 1, 0), 0.0))

    rs = pl.BlockSpec((1, wl), lambda j: (0, j))
    blk = pl.BlockSpec((T, wl), lambda j: (0, j))
    return pl.pallas_call(
        body, name="rk_shift_bwd", grid=(W // wl,),
        in_specs=[blk, pl.BlockSpec((T, wl), lambda j: (0, cb0 + j)), rs, rs],
        out_specs=[blk, rs, rs],
        out_shape=[jax.ShapeDtypeStruct((T, W), F32), jax.ShapeDtypeStruct((1, W), F32), jax.ShapeDtypeStruct((1, W), F32)],
        compiler_params=_cparams(("parallel",)),
    )(dps, proj, mp, mn)


RK_FWD_PAIRS = 8
RK_BWD_PAIRS = 4
RK_SPREAD_PAIRS = 2
RK_TIME_BLOCK = 32
RK_LANE_BLOCK = 128


def _rk_blocks(T, C, N, order_reversed, pairs):
    pw = 2 * N
    pp = min(pairs, C // pw)
    tb = min(RK_TIME_BLOCK, T)
    lb = min(RK_LANE_BLOCK, T)
    nb, per = T // tb, lb // tb

    def tix(i):
        return (nb - 1 - i) if order_reversed else i

    rows = pl.BlockSpec((tb, pp * pw), lambda g, i: (tix(i), g))
    cols = pl.BlockSpec((2 * pp, N, lb), lambda g, i: (g, 0, tix(i) // per))
    hist = pl.BlockSpec((tb, pp, N, pw), lambda g, i: (tix(i), g, 0, 0))
    return pp, pw, tb, lb, nb, per, tix, rows, cols, hist


def rk_spread(xT, name):
    H, N, T = xT.shape
    pw = 2 * N
    lb = min(RK_LANE_BLOCK, T)
    pp = min(RK_SPREAD_PAIRS, H // 2)

    def body(x_r, o_r):
        lane = lax.broadcasted_iota(jnp.int32, (N, lb), 1)
        first = lax.broadcasted_iota(jnp.int32, (N, pw), 1) < N
        tiles = [x_r[h] for h in range(2 * pp)]

        def step(t, carry):
            for p in range(pp):
                c = [jnp.sum(jnp.where(lane == t, tiles[2 * p + q], 0.0), axis=1, keepdims=True) for q in range(2)]
                o_r[t, p] = jnp.where(first, c[0], c[1])
            return carry

        lax.fori_loop(0, lb, step, 0, unroll=8)

    return pl.pallas_call(
        body, name=name, grid=(H // (2 * pp), T // lb),
        in_specs=[pl.BlockSpec((2 * pp, N, lb), lambda g, i: (g, 0, i))],
        out_specs=pl.BlockSpec((lb, pp, N, pw), lambda g, i: (i, g, 0, 0)),
        out_shape=jax.ShapeDtypeStruct((T, H // 2, N, pw), F32),
        compiler_params=_cparams(("parallel", "parallel")),
    )(xT)


def _half_sums(x, first):
    return (jnp.sum(jnp.where(first, x, 0.0), axis=1, keepdims=True),
            jnp.sum(jnp.where(first, 0.0, x), axis=1, keepdims=True))


def rk_scan(r, kk, w, k, b, vc, *, reverse, name):
    T, C = r.shape
    N = vc.shape[2]
    H = C // N
    pp, pw, tb, lb, nb, per, tix, rows, cols, hist_spec = _rk_blocks(T, C, N, reverse, RK_FWD_PAIRS)

    def body(r_r, kk_r, w_r, k_r, b_r, VC, yT_r, hist_r, S, YA):
        i = pl.program_id(1)

        @pl.when(i == 0)
        def _():
            S[...] = jnp.zeros_like(S)

        @pl.when(i % per == 0)
        def _():
            yT_r[...] = jnp.zeros_like(yT_r)

        off = (tix(i) % per) * tb
        first = lax.broadcasted_iota(jnp.int32, (N, pw), 1) < N
        YA[...] = jnp.zeros_like(YA)
        st = [S[p] for p in range(pp)]
        seg = [slice(p * pw, (p + 1) * pw) for p in range(pp)]
        for s in range(tb):
            t = (tb - 1 - s) if reverse else s
            row = slice(t, t + 1)
            sk = [_half_sums(st[p] * kk_r[row, seg[p]], first) for p in range(pp)]
            for p in range(pp):
                hist_r[t, p] = st[p]
                skp = jnp.where(first, sk[p][0], sk[p][1])
                st[p] = st[p] * w_r[row, seg[p]] - skp * b_r[row, seg[p]] + VC[t, p] * k_r[row, seg[p]]
            ys = [_half_sums(st[p] * r_r[row, seg[p]], first) for p in range(pp)]
            for p in range(pp):
                for q in range(2):
                    YA[2 * p + q, :, row] = ys[p][q]
        for p in range(pp):
            S[p] = st[p]
        for h in range(2 * pp):
            yT_r[h] = yT_r[h] + (pltpu.roll(YA[h], off, 1) if per > 1 else YA[h])

    return pl.pallas_call(
        body, name=name, grid=(C // (pp * pw), nb),
        in_specs=[rows] * 5 + [hist_spec], out_specs=[cols, hist_spec],
        out_shape=[jax.ShapeDtypeStruct((H, N, T), F32), jax.ShapeDtypeStruct((T, H // 2, N, pw), F32)],
        scratch_shapes=[pltpu.VMEM((pp, N, pw), F32), pltpu.VMEM((2 * pp, N, lb), F32)],
        compiler_params=_cparams(("parallel", "arbitrary")),
    )(r, kk, w, k, b, vc)


def rk_scan_bwd(r, kk, w, k, b, vc, dc, hist, *, reverse, name):
    T, C = r.shape
    N = vc.shape[2]
    H = C // N
    pp, pw, tb, lb, nb, per, tix, rows, cols, hist_spec = _rk_blocks(T, C, N, not reverse, RK_BWD_PAIRS)

    def body(r_r, kk_r, w_r, k_r, b_r, VC, DC, hist_r, dr_r, dkk_r, dw_r, dk_r, db_r, dvT_r, G, YA):
        i = pl.program_id(1)

        @pl.when(i == 0)
        def _():
            G[...] = jnp.zeros_like(G)

        @pl.when(i % per == 0)
        def _():
            dvT_r[...] = jnp.zeros_like(dvT_r)

        off = (tix(i) % per) * tb
        first = lax.broadcasted_iota(jnp.int32, (N, pw), 1) < N
        YA[...] = jnp.zeros_like(YA)
        gs = [G[p] for p in range(pp)]
        seg = [slice(p * pw, (p + 1) * pw) for p in range(pp)]
        for s in range(tb):
            t = s if reverse else (tb - 1 - s)
            row = slice(t, t + 1)
            g = [gs[p] + DC[t, p] * r_r[row, seg[p]] for p in range(pp)]
            sk = [_half_sums(hist_r[t, p] * kk_r[row, seg[p]], first) for p in range(pp)]
            gb = [_half_sums(g[p] * b_r[row, seg[p]], first) for p in range(pp)]
            gk = [_half_sums(g[p] * k_r[row, seg[p]], first) for p in range(pp)]
            for p in range(pp):
                sp = hist_r[t, p]
                kkv, wv, kv, bv = kk_r[row, seg[p]], w_r[row, seg[p]], k_r[row, seg[p]], b_r[row, seg[p]]
                vcol, dycol = VC[t, p], DC[t, p]
                sa = -jnp.where(first, sk[p][0], sk[p][1])
                dsa = jnp.where(first, gb[p][0], gb[p][1])
                sn = sp * wv + sa * bv + vcol * kv
                dr_r[row, seg[p]] = jnp.sum(sn * dycol, axis=0, keepdims=True)
                dw_r[row, seg[p]] = jnp.sum(g[p] * sp, axis=0, keepdims=True)
                db_r[row, seg[p]] = jnp.sum(g[p] * sa, axis=0, keepdims=True)
                dk_r[row, seg[p]] = jnp.sum(g[p] * vcol, axis=0, keepdims=True)
                dkk_r[row, seg[p]] = -jnp.sum(sp * dsa, axis=0, keepdims=True)
                gs[p] = g[p] * wv - dsa * kkv
                for q in range(2):
                    YA[2 * p + q, :, row] = gk[p][q]
        for p in range(pp):
            G[p] = gs[p]
        for h in range(2 * pp):
            dvT_r[h] = dvT_r[h] + (pltpu.roll(YA[h], off, 1) if per > 1 else YA[h])

    return pl.pallas_call(
        body, name=name, grid=(C // (pp * pw), nb),
        in_specs=[rows] * 5 + [hist_spec] * 3, out_specs=[rows] * 5 + [cols],
        out_shape=[jax.ShapeDtypeStruct((T, C), F32)] * 5 + [jax.ShapeDtypeStruct((H, N, T), F32)],
        scratch_shapes=[pltpu.VMEM((pp, N, pw), F32), pltpu.VMEM((2 * pp, N, lb), F32)],
        compiler_params=_cparams(("parallel", "arbitrary")),
    )(r, kk, w, k, b, vc, dc, hist)


def adam(parts, w, m, v, name):
    P, R, C = parts.shape
    tr = _tile(R, max(SUBLANE, (1 << 19) // max(C, 1) // SUBLANE * SUBLANE), SUBLANE)
    c1 = 1.0 / (1.0 - ADAM_B1 ** ADAM_STEP)
    c2 = 1.0 / (1.0 - ADAM_B2 ** ADAM_STEP)

    def body(p_r, w_r, m_r, v_r, g_o, d_o, m_o, v_o):
        g = p_r[0].astype(F32)
        for q in range(1, P):
            g = g + p_r[q].astype(F32)
        m2 = ADAM_B1 * m_r[...] + (1.0 - ADAM_B1) * g
        v2 = ADAM_B2 * v_r[...] + (1.0 - ADAM_B2) * (g * g)
        g_o[...] = g
        m_o[...] = m2
        v_o[...] = v2
        d_o[...] = -ADAM_LR * ((m2 * c1) / (jnp.sqrt(v2 * c2) + ADAM_EPS) + ADAM_WD * w_r[...])

    blk = pl.BlockSpec((tr, C), lambda i: (i, 0))
    return pl.pallas_call(
        body, name=name, grid=(R // tr,),
        in_specs=[pl.BlockSpec((P, tr, C), lambda i: (0, i, 0)), blk, blk, blk], out_specs=[blk] * 4,
        out_shape=[jax.ShapeDtypeStruct((R, C), F32)] * 4,
        compiler_params=_cparams(("parallel",)),
    )(parts, w, m, v)


def _pack_flat(arrs):
    rows = []
    for a in arrs:
        f = a.reshape(-1).astype(F32)
        n = _round_up(f.shape[0], SUBLANE * LANE)
        rows.append(jnp.pad(f, (0, n - f.shape[0])).reshape(-1, LANE))
    return jnp.concatenate(rows, axis=0)


def _unpack_flat(packed, shapes):
    out, r0 = [], 0
    for s in shapes:
        n = math.prod(s)
        nr = _round_up(n, SUBLANE * LANE) // LANE
        out.append(packed[r0:r0 + nr].reshape(-1)[:n].reshape(s))
        r0 += nr
    return out


def _pack_rows(arrs):
    rows = []
    for a in arrs:
        f = a.reshape(-1, a.shape[-1]).astype(F32)
        n = _round_up(f.shape[0], SUBLANE)
        rows.append(jnp.pad(f, ((0, n - f.shape[0]), (0, 0))))
    return jnp.concatenate(rows, axis=0)


def _unpack_rows(packed, shapes):
    out, r0 = [], 0
    for s in shapes:
        nr = math.prod(s[:-1])
        out.append(packed[r0:r0 + nr].reshape(s))
        r0 += _round_up(nr, SUBLANE)
    return out


def _cols_from_slots(g):
    return jnp.moveaxis(g, 0, -2).reshape(g.shape[1:-1] + (N_DEV * g.shape[-1],))


def _cols_to_slots(a):
    cs = a.shape[-1] // N_DEV
    return jnp.moveaxis(a.reshape(a.shape[:-1] + (N_DEV, cs)), -2, 0)


def _step(P, M, V):
    D, T = D_MODEL, SEQ
    S5W = D // 2
    C, LW, GP = _rk_dims()
    H = C // RWKV_HEAD
    G = S5W // S5_GROUP
    NS = G * S5_STATE
    NCH = G // S5_CHUNK_GROUPS
    SW = S5_CHUNK_GROUPS * S5_STATE
    RIN = 3 * C + 2 * LW + GATE_LORA
    RINP = 3 * C + 2 * LW + GP
    PROJ = S5W + RIN
    PROJP = S5W + RINP
    FF = 4 * D
    me = 4 * lax.axis_index("x") + 2 * lax.axis_index("y") + lax.axis_index("c")
    cs_mod = N_MOD * D // N_DEV
    eye = jnp.eye(S5_CHUNK_GROUPS, dtype=F32)

    x = P['x'][0]
    target = P['loss_target'][0]

    (c_all,) = exchange([P['c']], ['gather'], "comm_gather_c")
    c_all = c_all.reshape(N_DEV, D)
    (c_act,), _ = rowcall(lambda cv: ((cv * _sigmoid(cv),), ()), "silu_c", [c_all], [], [(D, F32)], [], N_DEV)
    ada_b_loc = lax.dynamic_slice(P['ada_b'], (0, me * cs_mod), (1, cs_mod))
    mod_loc = matmul(c_act, P['ada_w'][0], name="mod_mm", precise=True, extras=[(ada_b_loc, 'n')],
                     epi=lambda acc, bias: (acc + bias,))

    rk_shapes = [P[n][0].shape for n in RKPACK]
    rk_pack = _pack_rows([P[n][0] for n in RKPACK])
    gathered = gather_two_level(
        [mod_loc, P['w_in'][0].astype(BF16), P['w_out'][0].astype(BF16), P['ffn_w1'][0].astype(BF16),
         P['ffn_w2'][0].astype(BF16), P['s5_w_glu'][0].astype(BF16), rk_pack], "comm_gather_weights")
    mod_all, w_in_g, w_out_g, w1_g, w2_g, wglu_g, rk_g = gathered
    mod_me = lax.dynamic_index_in_dim(mod_all, me, axis=1, keepdims=False).reshape(N_MOD, 1, D)
    shift1, scale1, gate1, shift2, scale2, gate2 = (mod_me[i] for i in range(N_MOD))
    w_in = jnp.pad(_cols_from_slots(w_in_g), ((0, 0), (0, PROJP - PROJ)))
    w_out = w_out_g.reshape(D, D)
    w2 = w2_g.reshape(FF, D)
    wglu = wglu_g.reshape(S5W, S5W)
    rk_full = _unpack_rows(_cols_from_slots(rk_g), [s[:-1] + (C,) for s in rk_shapes])
    rk_w0, rk_a0, rk_wup, rk_aup, rk_gup = rk_full

    def lora_pad(up):
        z = jnp.zeros((N_DIR, LW, C), F32)
        for d in range(N_DIR):
            z = z.at[d, d * DECAY_LORA:(d + 1) * DECAY_LORA].set(up[d])
        return z

    wup_p, aup_p = lora_pad(rk_wup), lora_pad(rk_aup)
    gup_p = jnp.pad(rk_gup, ((0, GP - GATE_LORA), (0, 0)))
    mu_prev = jnp.pad(P['rk_shift_prev'], ((0, 0), (0, RINP - RIN)))
    mu_next = jnp.pad(P['rk_shift_next'], ((0, 0), (0, RINP - RIN)))
    r_k = P['rk_r_k'].reshape(1, C)
    fgain = P['final_gain'].reshape(1, D)

    TT = 256
    (h1,), _ = rowcall(lambda xv, g, sc, sh: ((_normmod(xv, g, sc, sh),), ()), "norm1",
                       [x], [P['norm1_gain'], scale1, shift1], [(D, BF16)], [], TT)
    proj = matmul(h1, w_in, name="proj_mm")

    lre = P['s5_lambda_re'][0].reshape(N_DIR, NS)
    lim = P['s5_lambda_im'][0].reshape(N_DIR, NS)
    lstep = jnp.broadcast_to(P['s5_log_step'][0][:, :, None], (N_DIR, G, S5_STATE)).reshape(N_DIR, NS)
    bre = P['s5_b_re'][0].reshape(NS, S5_GROUP).T
    bim = P['s5_b_im'][0].reshape(NS, S5_GROUP).T
    lbr, lbi, bbr, bbi = s5_prep(lre, lim, lstep, bre, bim)
    bbar = jnp.stack([bbr, bbi], axis=1).reshape(N_DIR, 2, S5_GROUP, NCH, S5_CHUNK_GROUPS, S5_STATE)
    wblk = jnp.einsum('drhcgp,gk->cghdrkp', bbar, eye).reshape(NCH, S5_CHUNK_GROUPS * S5_GROUP, 4 * SW)
    wblk = wblk.astype(MXU_DTYPE)
    u_view = (proj, S5W, 0)
    bus, _ = rowcall(lambda uv, wb: (_s5_bu_fn(uv, wb), ()), "s5_bu", [u_view], [wblk], [(NS, F32)] * 4, [], TT)
    s0r, s0i = s5_scan(bus[0], bus[1], lbr[0:1], lbi[0:1], reverse=False, name="s5_scan_f0")
    s1r, s1i = s5_scan(bus[2], bus[3], lbr[1:2], lbi[1:2], reverse=True, name="s5_scan_f1")

    def cblk(cm):
        c4 = cm.reshape(NCH, S5_CHUNK_GROUPS, S5_GROUP, S5_STATE)
        return jnp.einsum('cghp,gk->cgpkh', c4, eye).reshape(NCH, SW, S5_CHUNK_GROUPS * S5_GROUP)

    cre_b = cblk(P['s5_c_re'][0]).astype(MXU_DTYPE)
    cim_b = cblk(P['s5_c_im'][0]).astype(MXU_DTYPE)
    s5_full = [cre_b, cim_b, P['s5_d'], wglu, P['s5_b_glu']]
    TS = 128
    (y_s5,), _ = rowcall(lambda *a: ((_s5_out_fn(*a),), ()), "s5_out", [s0r, s0i, s1r, s1i, u_view], s5_full,
                         [(S5W, BF16)], [], TS)

    ps = rk_shift(proj, mu_prev, mu_next, S5W)
    pre_full = [rk_w0, wup_p[0], wup_p[1], rk_a0, aup_p[0], aup_p[1], gup_p, P['rk_k_k'], P['rk_k_a']]
    pre_out, _ = rowcall(lambda *a: (_rk_pre_fn(*a)[2:], ()), "rk_pre", [ps], pre_full, [(C, F32)] * 8, [], TS)
    kkn, w_0, w_1, k_0, k_1, b_0, b_1, g_gate = pre_out
    r_t, v_t = ps[:, 0:C], ps[:, 2 * C:3 * C]

    def hmT(a):
        return a.reshape(T, H, RWKV_HEAD).transpose(1, 2, 0)

    def unT(a):
        return a.transpose(2, 0, 1).reshape(T, C)

    v_cols = rk_spread(hmT(v_t), "rk_spread_v")
    dir_rows = [(w_0, k_0, b_0), (w_1, k_1, b_1)]
    yT, hist = [], []
    for d in range(N_DIR):
        wd, kd, bd = dir_rows[d]
        yd, hd = rk_scan(r_t, kkn, wd, kd, bd, v_cols, reverse=(d == 1), name=f"rk_scan_f{d}")
        yT.append(yd)
        hist.append(hd)
    y_0, y_1 = unT(yT[0]), unT(yT[1])
    post_full = [r_k, P['rk_ln_gain'], P['rk_ln_bias']]
    post_tiled = [y_0, y_1, (ps, C, 0), (ps, C, 2), k_0, k_1, g_gate]
    (y_rk,), _ = rowcall(lambda *a: ((_rk_post_fn(*a),), ()), "rk_post", post_tiled, post_full, [(C, BF16)], [], TS)

    ycat = jnp.concatenate([y_s5, y_rk], axis=1)
    mixed = matmul(ycat, w_out, name="out_mm")

    def res_norm(xv, mv, gate, g, sc, sh):
        x1v = xv + gate * mv
        return x1v, _normmod(x1v, g, sc, sh)

    (x1, h2), _ = rowcall(lambda *a: (res_norm(*a), ()), "norm2", [x, mixed], [gate1, P['norm2_gain'], scale2, shift2],
                          [(D, F32), (D, BF16)], [], TT)
    a_ff, hh_ff = matmul(h2, w1_g, name="ffn1_mm", b_slots=True, out_dtypes=(F32, BF16),
                         epi=lambda acc: (acc, jnp.square(jnp.maximum(acc, 0.0))))
    ffn = matmul(hh_ff, w2, name="ffn2_mm")

    def loss_fn(x1v, fv, tg, gate, fg):
        def f(x1_, f_, gate_, fg_):
            out = _rms(x1_ + gate_ * f_, fg_)
            err = out - tg
            return 0.5 * jnp.sum(jnp.sum(err * err, axis=1, keepdims=True), axis=0, keepdims=True) * (1.0 / D)
        lv, vjp = jax.vjp(f, x1v, fv, gate, fg)
        dx1, dff, dgate, dfg = vjp(jnp.ones((1, 1), F32))
        return (dx1, dff), (jnp.broadcast_to(lv, (SUBLANE, LANE)), dgate, dfg)

    (dx2, dffn), (loss_t, dgate2, dfgain) = rowcall(
        loss_fn, "loss", [x1, ffn, target], [gate2, fgain], [(D, F32), (D, BF16)], [(SUBLANE, LANE), (1, D), (1, D)], TT)
    loss = lax.psum(loss_t[0, 0], MESH_AXES)

    da = matmul(dffn, w2, name="dffn2_mm", tb=True, out_dtypes=(BF16,), extras=[(a_ff, 'mn')],
                epi=lambda acc, av: (acc * (2.0 * jnp.maximum(av, 0.0)),))
    g_w2 = matmul(hh_ff, dffn, name="gw2_mm", ta=True, out_dtypes=(BF16,))
    dh2 = matmul(da, w1_g, name="dh2_mm", tb=True, b_slots=True)
    g_w1 = matmul(h2, da, name="gw1_mm", ta=True, out_slots=N_DEV, out_dtypes=(BF16,))

    def res_norm_bwd(dx2v, dh2v, xv, mv, gate, g, sc, sh):
        _, vjp = jax.vjp(res_norm, xv, mv, gate, g, sc, sh)
        dx, dm, dgate, dg, dsc, dsh = vjp((dx2v, dh2v))
        return (dx, dm), (dgate, dg, dsc, dsh)

    (dx1, dmixed), (dgate1, dgain2, dscale2, dshift2) = rowcall(
        res_norm_bwd, "norm2_bwd", [dx2, dh2, x, mixed], [gate1, P['norm2_gain'], scale2, shift2],
        [(D, F32), (D, BF16)], [(1, D)] * 4, TT)

    dycat = matmul(dmixed, w_out, name="dycat_mm", tb=True)
    g_wout = matmul(ycat, dmixed, name="gwout_mm", ta=True, out_dtypes=(BF16,))

    def post_bwd(dy, *a):
        _, vjp = jax.vjp(_rk_post_fn, *a)
        gy0, gy1, gr, gv, gk0, gk1, gg, grk, glg, glb = vjp(dy)
        return (gy0, gr, gv, gk0, gk1, gg), (grk, glg, glb)

    cb_rk = S5W // C if C else 0
    (dy_rk, dr_p, dv_p, dk0_p, dk1_p, dg_p), (g_rk_rk, g_lng, g_lnb) = rowcall(
        post_bwd, "rk_post_bwd", [(dycat, C, cb_rk)] + post_tiled, post_full, [(C, F32)] * 6, [(1, C)] * 3, TS)
    dy_cols = rk_spread(hmT(dy_rk), "rk_spread_dy")
    scan_g = []
    for d in range(N_DIR):
        wd, kd, bd = dir_rows[d]
        scan_g.append(rk_scan_bwd(r_t, kkn, wd, kd, bd, v_cols, dy_cols, hist[d], reverse=(d == 1), name=f"rk_scan_b{d}"))
    cot = [dr_p, scan_g[0][0], scan_g[1][0],
           dv_p, unT(scan_g[0][5]), unT(scan_g[1][5]),
           scan_g[0][1], scan_g[1][1],
           scan_g[0][2], scan_g[1][2],
           dk0_p, scan_g[0][3], dk1_p, scan_g[1][3],
           scan_g[0][4], scan_g[1][4],
           dg_p]

    def pre_bwd(psv, r0, r1, r2, v0, v1, v2, q0, q1, dw0, dw1, k0a, k0b, k1a, k1b, db0, db1, dgv, *params):
        _, vjp = jax.vjp(_rk_pre_fn, psv, *params)
        grads = vjp((r0 + r1 + r2, v0 + v1 + v2, q0 + q1, dw0, dw1, k0a + k0b, k1a + k1b, db0, db1, dgv))
        return (grads[0],), tuple(grads[1:])

    (dps,), pre_g = rowcall(pre_bwd, "rk_pre_bwd", [ps] + cot, pre_full, [(RINP, F32)],
                            [f.shape for f in pre_full], TS)
    g_w0, g_wup0, g_wup1, g_a0, g_aup0, g_aup1, g_gup_p, g_kk, g_ka = pre_g
    g_wup_p, g_aup_p = jnp.stack([g_wup0, g_wup1]), jnp.stack([g_aup0, g_aup1])
    dp_rk, g_mup, g_mun = rk_shift_bwd(dps, proj, mu_prev, mu_next, S5W)

    def s5_out_bwd(dy, *a):
        a = [t.astype(F32) for t in a]
        _, vjp = jax.vjp(_s5_out_fn, *a)
        g = vjp(dy)
        return (g[0], g[1], g[4]), tuple(g[5:])

    (dxr, dxi, du_a), s5_pg = rowcall(
        s5_out_bwd, "s5_out_bwd", [(dycat, S5W, 0), s0r, s0i, s1r, s1i, u_view], s5_full,
        [(NS, F32), (NS, F32), (S5W, F32)], [f.shape for f in s5_full], TS)
    g_creb, g_cimb, g_s5d, g_wglu, g_bglu = s5_pg
    l0r, l0i = s5_scan(dxr, dxi, lbr[0:1], -lbi[0:1], reverse=True, name="s5_scan_b0")
    l1r, l1i = s5_scan(dxr, dxi, lbr[1:2], -lbi[1:2], reverse=False, name="s5_scan_b1")
    dl0r, dl0i = s5_dlam(s0r, s0i, l0r, l0i, reverse=False, name="s5_dlam0")
    dl1r, dl1i = s5_dlam(s1r, s1i, l1r, l1i, reverse=True, name="s5_dlam1")

    def bu_bwd(uv, g0, g1, g2, g3, wb):
        _, vjp = jax.vjp(_s5_bu_fn, uv, wb.astype(F32))
        du, dwb = vjp((g0, g1, g2, g3))
        return (du,), (dwb,)

    (du_b,), (g_wblk,) = rowcall(bu_bwd, "s5_bu_bwd", [u_view, l0r, l0i, l1r, l1i], [wblk], [(S5W, F32)],
                                 [wblk.shape], TS)
    g_bbar = jnp.einsum('cghdrkp,gk->drhcgp',
                        g_wblk.reshape(NCH, S5_CHUNK_GROUPS, S5_GROUP, N_DIR, 2, S5_CHUNK_GROUPS, S5_STATE), eye)
    g_bbar = g_bbar.reshape(N_DIR, 2, S5_GROUP, NS)
    g_lre, g_lim, g_lstep, g_bre, g_bim = s5_prep_bwd(
        lre, lim, lstep, bre, bim, jnp.concatenate([dl0r, dl1r], 0), jnp.concatenate([dl0i, dl1i], 0),
        g_bbar[:, 0], g_bbar[:, 1])

    def uncblk(gb):
        g5 = gb.reshape(NCH, S5_CHUNK_GROUPS, S5_STATE, S5_CHUNK_GROUPS, S5_GROUP)
        return jnp.einsum('cgpkh,gk->cghp', g5, eye).reshape(G, S5_GROUP, S5_STATE)

    (du_tot,), _ = rowcall(lambda a, b_: ((a + b_,), ()), "s5_du_sum", [du_a, du_b], [], [(S5W, BF16)], [], TT)
    dproj = jnp.concatenate([du_tot, dp_rk.astype(BF16)], axis=1)
    dh1 = matmul(dproj, w_in, name="dh1_mm", tb=True)
    g_win = matmul(h1, dproj, name="gwin_mm", ta=True, out_dtypes=(BF16,))

    def norm1_bwd(dx1v, dh1v, xv, g, sc, sh):
        _, vjp = jax.vjp(_normmod, xv, g, sc, sh)
        dx, dg, dsc, dsh = vjp(dh1v)
        return (dx1v + dx,), (dg, dsc, dsh)

    (grad_x,), (dgain1, dscale1, dshift1) = rowcall(
        norm1_bwd, "norm1_bwd", [dx1, dh1, x], [P['norm1_gain'], scale1, shift1], [(D, F32)], [(1, D)] * 3, TT)

    dmod = jnp.concatenate([dshift1, dscale1, dgate1, dshift2, dscale2, dgate2], axis=1)
    lstep_g = g_lstep.reshape(N_DIR, G, S5_STATE)
    small_g = {
        'ada_b': dmod, 'norm1_gain': dgain1, 'norm2_gain': dgain2, 'final_gain': dfgain.reshape(D),
        's5_lambda_re': g_lre.reshape(1, N_DIR, G, S5_STATE), 's5_lambda_im': g_lim.reshape(1, N_DIR, G, S5_STATE),
        's5_log_step': lstep_g,
        's5_b_re': g_bre.T.reshape(1, G, S5_STATE, S5_GROUP), 's5_b_im': g_bim.T.reshape(1, G, S5_STATE, S5_GROUP),
        's5_c_re': uncblk(g_creb)[None], 's5_c_im': uncblk(g_cimb)[None],
        's5_d': g_s5d, 's5_b_glu': g_bglu,
        'rk_shift_prev': g_mup[:, :RIN], 'rk_shift_next': g_mun[:, :RIN],
        'rk_k_k': g_kk, 'rk_k_a': g_ka, 'rk_r_k': g_rk_rk.reshape(1, H, RWKV_HEAD),
        'rk_ln_gain': g_lng, 'rk_ln_bias': g_lnb,
    }
    small_shapes = {n: P[n].shape for n in SMALL}
    small_shapes['s5_log_step'] = (N_DIR, G, S5_STATE)
    small_pack = _pack_flat([small_g[n] for n in SMALL])

    def lora_unpad(gp):
        return jnp.stack([gp[d, d * DECAY_LORA:(d + 1) * DECAY_LORA] for d in range(N_DIR)])

    rk_grads = {'rk_w0': g_w0, 'rk_a0': g_a0, 'rk_w_up': lora_unpad(g_wup_p), 'rk_a_up': lora_unpad(g_aup_p),
                'rk_g_up': g_gup_p[:GATE_LORA]}
    rk_gpack = jnp.stack([_pack_rows([_cols_to_slots(rk_grads[n])[j] for n in RKPACK]) for j in range(N_DEV)])
    g_win_s = _cols_to_slots(g_win[:, :PROJ])
    (small_all,) = gather_two_level([small_pack], "comm_gather_small_grads")
    slots = [g_win_s, g_wout.reshape(N_DEV, D // N_DEV, D), g_w1, g_w2.reshape(N_DEV, FF // N_DEV, D),
             g_wglu.reshape(N_DEV, S5W // N_DEV, S5W), rk_gpack]
    my_c = lax.axis_index("c")

    def core_half(a, core):
        return lax.dynamic_index_in_dim(a.reshape((N_DEV // 2, 2) + a.shape[1:]), core, axis=1, keepdims=False)

    from_sibling = swap_sibling([core_half(a, 1 - my_c) for a in slots], "comm_swap_grads")
    chip_sums = []
    for a, got, nm in zip(slots, from_sibling, ['w_in', 'w_out', 'ffn_w1', 'ffn_w2', 's5_w_glu', 'rkpack']):
        own = core_half(a, my_c)
        flat = (own.shape[0] * math.prod(own.shape[1:-1]), own.shape[-1])
        (sm,), _ = rowcall(lambda u, v_: ((u.astype(F32) + v_.astype(F32),), ()), "chip_sum_" + nm,
                           [own.reshape(flat), got.reshape(flat)], [], [(flat[1], a.dtype)], [], 512)
        chip_sums.append(sm.reshape(own.shape))
    win_parts, wout_parts, w1_parts, w2_parts, wglu_parts, rk_parts = exchange_chips(chip_sums, "comm_grads")

    res = {}

    def put(name, g, dl, m2, v2):
        shp = P[name].shape
        res[name] = tuple(t.reshape(shp) for t in (g, dl, m2, v2))

    def adam2d(name, parts):
        shp = P[name].shape
        r2 = (math.prod(shp[:-1]), shp[-1])
        put(name, *adam(parts.reshape((parts.shape[0],) + r2), P[name].reshape(r2), M[name].reshape(r2),
                        V[name].reshape(r2), "adam_" + name))

    adam2d('w_in', win_parts)
    adam2d('w_out', wout_parts)
    adam2d('ffn_w1', w1_parts)
    adam2d('ffn_w2', w2_parts)
    adam2d('s5_w_glu', wglu_parts)
    off = 0
    for n in SMALL:
        if n == 'ada_b':
            break
        off += _round_up(math.prod(small_shapes[n]), SUBLANE * LANE) // LANE
    nrow_b = N_MOD * D // LANE
    dmod_all = small_all[:, off:off + nrow_b].reshape(N_DEV, N_MOD * D)
    dmod_cols = lax.dynamic_slice(dmod_all, (0, me * cs_mod), (N_DEV, cs_mod))
    g_adaw = matmul(c_act, dmod_cols, name="gadaw_mm", ta=True, precise=True)
    adam2d('ada_w', g_adaw[None])
    small_w = dict(P)
    small_m, small_v = dict(M), dict(V)
    rk_res = adam(rk_parts, rk_pack, _pack_rows([M[n][0] for n in RKPACK]), _pack_rows([V[n][0] for n in RKPACK]),
                  "adam_rkpack")
    for name, parts4 in zip(RKPACK, zip(*[_unpack_rows(t, rk_shapes) for t in rk_res])):
        put(name, *parts4)
    return loss, grad_x, res, (small_all, small_shapes, small_w, small_m, small_v)


def _small_update(small_all, small_shapes, P, M, V, res):
    G = (D_MODEL // 2) // S5_GROUP
    names = [n for n in SMALL if n != 's5_log_step']
    shapes = [small_shapes[n] for n in SMALL]
    parts = _unpack_flat_batched(small_all, shapes)
    by = dict(zip(SMALL, parts))
    ls = by['s5_log_step']
    ls = ls.transpose(0, 3, 1, 2).reshape(N_DEV * S5_STATE, N_DIR * G)
    pk = lambda d: _pack_flat([d[n] for n in names])
    packs = jnp.stack([_pack_flat([by[n][j] for n in names]) for j in range(N_DEV)])
    out = adam(packs, pk(P), pk(M), pk(V), "adam_small")
    shp = [P[n].shape for n in names]
    for name, parts4 in zip(names, zip(*[_unpack_flat(t, shp) for t in out])):
        res[name] = parts4
    lsw = lambda d: jnp.pad(d['s5_log_step'].reshape(1, N_DIR * G), ((0, SUBLANE - 1), (0, 0)))
    ls_parts = jnp.pad(ls[:, None, :], ((0, 0), (0, SUBLANE - 1), (0, 0)))
    o = adam(ls_parts, lsw(P), lsw(M), lsw(V), "adam_log_step")
    res['s5_log_step'] = tuple(t[0:1].reshape(P['s5_log_step'].shape) for t in o)


def _unpack_flat_batched(packed, shapes):
    out, r0 = [], 0
    B = packed.shape[0]
    for s in shapes:
        n = math.prod(s)
        nr = _round_up(n, SUBLANE * LANE) // LANE
        out.append(packed[:, r0:r0 + nr].reshape(B, -1)[:, :n].reshape((B,) + tuple(s)))
        r0 += nr
    return out


def kernel(x, c, ada_w, ada_b, norm1_gain, norm2_gain, final_gain, w_in, w_out, s5_lambda_re, s5_lambda_im, s5_log_step, s5_b_re, s5_b_im, s5_c_re, s5_c_im, s5_d, s5_w_glu, s5_b_glu, rk_shift_prev, rk_shift_next, rk_w0, rk_w_up, rk_a0, rk_a_up, rk_g_up, rk_k_k, rk_k_a, rk_r_k, rk_ln_gain, rk_ln_bias, ffn_w1, ffn_w2, loss_target, m_ada_w, m_ada_b, m_norm1_gain, m_norm2_gain, m_final_gain, m_w_in, m_w_out, m_s5_lambda_re, m_s5_lambda_im, m_s5_log_step, m_s5_b_re, m_s5_b_im, m_s5_c_re, m_s5_c_im, m_s5_d, m_s5_w_glu, m_s5_b_glu, m_rk_shift_prev, m_rk_shift_next, m_rk_w0, m_rk_w_up, m_rk_a0, m_rk_a_up, m_rk_g_up, m_rk_k_k, m_rk_k_a, m_rk_r_k, m_rk_ln_gain, m_rk_ln_bias, m_ffn_w1, m_ffn_w2, v_ada_w, v_ada_b, v_norm1_gain, v_norm2_gain, v_final_gain, v_w_in, v_w_out, v_s5_lambda_re, v_s5_lambda_im, v_s5_log_step, v_s5_b_re, v_s5_b_im, v_s5_c_re, v_s5_c_im, v_s5_d, v_s5_w_glu, v_s5_b_glu, v_rk_shift_prev, v_rk_shift_next, v_rk_w0, v_rk_w_up, v_rk_a0, v_rk_a_up, v_rk_g_up, v_rk_k_k, v_rk_k_a, v_rk_r_k, v_rk_ln_gain, v_rk_ln_bias, v_ffn_w1, v_ffn_w2):
    given = dict(locals())
    P = {n: given[n] for n in ['x', 'c', 'loss_target'] + WEIGHTS}
    M = {n: given['m_' + n] for n in WEIGHTS}
    V = {n: given['v_' + n] for n in WEIGHTS}
    loss, grad_x, res, small = _step(P, M, V)
    small_all, small_shapes, _, _, _ = small
    _small_update(small_all, small_shapes, P, M, V, res)
    outs = [loss, grad_x[None]]
    for q in range(4):
        outs += [res[n][q] for n in WEIGHTS]
    return tuple(outs)
```
